```python
import jax
import jax.numpy as jnp
from jax import lax
import numpy as np

D_MODEL = 2048
BATCH = 4
SEQ = 2048
DEPTH = 1

EPS = 1e-6
D_MIX = D_MODEL

HG_HEADS = 8
HG_DK = 128
HG_DV = 128
HG_WIDTH = HG_HEADS * HG_DV
HG_CHUNK = 64

MLA_HEADS = 8
MLA_Q_RANK = 512
MLA_KV_RANK = 256
MLA_NOPE = 128
MLA_ROPE = 64
MLA_QK = MLA_NOPE + MLA_ROPE
MLA_V = 128
MLA_WIDTH = MLA_HEADS * MLA_V
ATTN_BLOCK = 128
ROPE_THETA = 10000.0

IN_SIZES = (HG_HEADS * HG_DK, HG_HEADS * HG_DK, HG_HEADS * HG_DV, HG_HEADS * HG_DV,
            MLA_Q_RANK, MLA_KV_RANK, MLA_ROPE)
IN_COLS = sum(IN_SIZES)

N_EXPERTS = 64
TOP_K = 8
N_GROUPS = 8
TOPK_GROUPS = 4
EXPERTS_PER_GROUP = N_EXPERTS // N_GROUPS
D_EXPERT = 512
ROUTED_SCALE = 2.5
MOE_BLOCK = 128

kernel_name = 'hymba_hgrn2_mla_moe_adaln'


def rms_norm(x, g):
    xf = x.astype(jnp.float32)
    y = xf * lax.rsqrt(jnp.mean(xf * xf, axis=-1, keepdims=True) + EPS)
    return (y * g.astype(jnp.float32)).astype(x.dtype)


def apply_rope(x, pos):
    half = MLA_ROPE // 2
    inv_freq = ROPE_THETA ** (-jnp.arange(half, dtype=jnp.float32) / half)
    ang = pos.astype(jnp.float32)[:, :, None, None] * inv_freq
    cos, sin = jnp.cos(ang), jnp.sin(ang)
    xf = x.astype(jnp.float32)
    x1, x2 = xf[..., :half], xf[..., half:]
    return jnp.concatenate([x1 * cos - x2 * sin, x2 * cos + x1 * sin], axis=-1).astype(x.dtype)


def hgrn2_group(q, f_logit, i_in, g, lb, out_g):
    B, S, _ = q.shape
    n_chunks = S // HG_CHUNK
    f32 = jnp.float32
    f = lb + (1.0 - lb) * jax.nn.sigmoid(f_logit.astype(f32))
    log_f = jnp.log(f)
    k = 1.0 - f

    def to_chunks(t, d):
        return t.astype(f32).reshape(B, n_chunks, HG_CHUNK, HG_HEADS, d).transpose(1, 0, 3, 2, 4)

    qc, lfc, kc, vc = to_chunks(q, HG_DK), to_chunks(log_f, HG_DK), to_chunks(k, HG_DK), to_chunks(i_in, HG_DV)
    causal = jnp.tril(jnp.ones((HG_CHUNK, HG_CHUNK), dtype=bool))[:, :, None]

    def chunk_step(state, inp):
        q_c, lf_c, k_c, v_c = inp
        b = jnp.cumsum(lf_c, axis=2)
        o_inter = jnp.einsum('bhtk,bhkv->bhtv', q_c * jnp.exp(b), state)
        rel = b[:, :, :, None, :] - b[:, :, None, :, :]
        decay = jnp.exp(jnp.where(causal, rel, -jnp.inf))
        attn = jnp.einsum('bhtk,bhtsk,bhsk->bhts', q_c, decay, k_c)
        o_intra = jnp.einsum('bhts,bhsv->bhtv', attn, v_c)
        b_last = b[:, :, -1, :]
        state = (jnp.exp(b_last)[..., None] * state
                 + jnp.einsum('bhsk,bhsv->bhkv', k_c * jnp.exp(b_last[:, :, None, :] - b), v_c))
        return state, o_inter + o_intra

    s0 = jnp.zeros((B, HG_HEADS, HG_DK, HG_DV), f32)
    _, o = lax.scan(chunk_step, s0, (qc, lfc, kc, vc))
    o = o.transpose(1, 0, 3, 2, 4).reshape(B, S, HG_HEADS, HG_DV)
    gate = jax.nn.silu(g.astype(f32)).reshape(B, S, HG_HEADS, HG_DV)
    o = rms_norm(o, out_g) * gate
    return o.reshape(B, S, HG_WIDTH).astype(q.dtype)


def mla_group(c_q, c_kv, k_rope, pos, q_a_g, w_uq, kv_a_g, w_ukv, q_norm_g, k_norm_g):
    B, S, _ = c_q.shape
    q = (rms_norm(c_q, q_a_g) @ w_uq).reshape(B, S, MLA_HEADS, MLA_QK)
    kv = (rms_norm(c_kv, kv_a_g) @ w_ukv).reshape(B, S, MLA_HEADS, MLA_NOPE + MLA_V)
    k_nope, v = kv[..., :MLA_NOPE], kv[..., MLA_NOPE:]
    k = jnp.concatenate([k_nope, jnp.broadcast_to(k_rope[:, :, None, :], (B, S, MLA_HEADS, MLA_ROPE))], axis=-1)
    q = rms_norm(q, q_norm_g)
    k = rms_norm(k, k_norm_g)
    q = jnp.concatenate([q[..., :MLA_NOPE], apply_rope(q[..., MLA_NOPE:], pos)], axis=-1)
    k = jnp.concatenate([k[..., :MLA_NOPE], apply_rope(k[..., MLA_NOPE:], pos)], axis=-1)

    n_blocks = S // ATTN_BLOCK
    q_blocks = q.reshape(B, n_blocks, ATTN_BLOCK, MLA_HEADS, MLA_QK).transpose(1, 0, 2, 3, 4)
    key_idx = jnp.arange(S)
    scale = MLA_QK ** -0.5

    def attend(args):
        q_blk, blk = args
        q_idx = blk * ATTN_BLOCK + jnp.arange(ATTN_BLOCK)
        s = jnp.einsum('bqhd,bkhd->bhqk', q_blk, k, preferred_element_type=jnp.float32) * scale
        s = jnp.where(key_idx[None, :] <= q_idx[:, None], s, -jnp.inf)
        p = jax.nn.softmax(s, axis=-1).astype(v.dtype)
        return jnp.einsum('bhqk,bkhd->bqhd', p, v)

    o = lax.map(attend, (q_blocks, jnp.arange(n_blocks)))
    return o.transpose(1, 0, 2, 3, 4).reshape(B, S, MLA_WIDTH)


def moe(h, w_router, router_bias, w_gate, w_up, w_down, ws_gate, ws_up, ws_down):
    B, S, D = h.shape
    T = B * S
    xt = h.reshape(T, D)
    scores = jax.nn.sigmoid(jnp.einsum('td,de->te', xt, w_router, preferred_element_type=jnp.float32))
    sel = scores + router_bias.astype(jnp.float32)
    grp_score = lax.top_k(sel.reshape(T, N_GROUPS, EXPERTS_PER_GROUP), 2)[0].sum(-1)
    _, grp_idx = lax.top_k(grp_score, TOPK_GROUPS)
    grp_mask = jnp.any(grp_idx[:, :, None] == jnp.arange(N_GROUPS)[None, None, :], axis=1)
    sel = jnp.where(jnp.repeat(grp_mask, EXPERTS_PER_GROUP, axis=1), sel, -jnp.inf)
    _, top_idx = lax.top_k(sel, TOP_K)
    top_w = jnp.take_along_axis(scores, top_idx, axis=1)
    top_w = top_w / jnp.sum(top_w, axis=-1, keepdims=True) * ROUTED_SCALE

    A = T * TOP_K
    e_flat = top_idx.reshape(A)
    tok_flat = jnp.repeat(jnp.arange(T, dtype=jnp.int32), TOP_K)
    w_flat = top_w.reshape(A)
    order = jnp.argsort(e_flat)
    e_sorted, tok_sorted, w_sorted = e_flat[order], tok_flat[order], w_flat[order]
    counts = jnp.bincount(e_flat, length=N_EXPERTS)
    padded = (counts + MOE_BLOCK - 1) // MOE_BLOCK * MOE_BLOCK
    start = jnp.cumsum(counts) - counts
    pad_end = jnp.cumsum(padded)
    pad_start = pad_end - padded
    dest = pad_start[e_sorted] + (jnp.arange(A, dtype=jnp.int32) - start[e_sorted])
    P = A + N_EXPERTS * MOE_BLOCK
    n_blk = P // MOE_BLOCK
    row_tok = jnp.full((P,), T, dtype=jnp.int32).at[dest].set(tok_sorted)
    row_w = jnp.zeros((P,), jnp.float32).at[dest].set(w_sorted)
    blk_expert = jnp.minimum(jnp.searchsorted(pad_end, jnp.arange(n_blk) * MOE_BLOCK, side='right'), N_EXPERTS - 1)
    x_pad = jnp.concatenate([xt, jnp.zeros((1, D), xt.dtype)], axis=0)

    def block_step(acc, args):
        rows, wts, e = args
        xb = x_pad[rows]
        hid = jax.nn.silu(xb @ w_gate[e]) * (xb @ w_up[e])
        yb = (hid @ w_down[e]) * wts[:, None].astype(xb.dtype)
        return acc.at[rows].add(yb), None

    acc0 = jnp.zeros((T + 1, D), xt.dtype)
    acc, _ = lax.scan(block_step, acc0,
                      (row_tok.reshape(n_blk, MOE_BLOCK), row_w.reshape(n_blk, MOE_BLOCK), blk_expert))
    routed = acc[:T]
    shared = (jax.nn.silu(xt @ ws_gate) * (xt @ ws_up)) @ ws_down
    return (routed + shared).reshape(B, S, D)


def setup_inputs(seed: int = 0) -> dict:
    key = jax.random.key(seed)
    ks = jax.random.split(key, 27)
    f32 = jnp.float32
    nrm = lambda k, shape, s: jax.random.normal(k, shape, f32) * s
    gain = lambda k, shape: 1.0 + 0.02 * jax.random.normal(k, shape, f32)
    L = DEPTH
    return {
        'x': nrm(ks[0], (BATCH, SEQ, D_MODEL), 1.0),
        'c': nrm(ks[1], (BATCH, D_MODEL), 1.0),
        'positions': (jnp.arange(SEQ, dtype=jnp.int32)[None, :]
                      + jax.random.randint(ks[2], (BATCH, 1), 0, 4096, dtype=jnp.int32)),
        'w_ada': nrm(ks[3], (L, D_MODEL, 6 * D_MODEL), 0.5 * D_MODEL ** -0.5),
        'b_ada': nrm(ks[4], (L, 6 * D_MODEL), 0.02),
        'norm_mix_g': gain(ks[5], (L, D_MODEL)),
        'norm_ffn_g': gain(ks[6], (L, D_MODEL)),
        'w_in': nrm(ks[7], (L, D_MODEL, IN_COLS), D_MODEL ** -0.5),
        'hg_lb_logits': nrm(ks[8], (DEPTH + 1, HG_HEADS * HG_DK), 0.1),
        'hg_out_g': gain(ks[9], (L, HG_DV)),
        'mla_q_a_g': gain(ks[10], (L, MLA_Q_RANK)),
        'mla_w_uq': nrm(ks[11], (L, MLA_Q_RANK, MLA_HEADS * MLA_QK), MLA_Q_RANK ** -0.5),
        'mla_kv_a_g': gain(ks[12], (L, MLA_KV_RANK)),
        'mla_w_ukv': nrm(ks[13], (L, MLA_KV_RANK, MLA_HEADS * (MLA_NOPE + MLA_V)), MLA_KV_RANK ** -0.5),
        'mla_q_norm_g': gain(ks[14], (L, MLA_QK)),
        'mla_k_norm_g': gain(ks[15], (L, MLA_QK)),
        'mla_out_g': gain(ks[16], (L, MLA_WIDTH)),
        'w_out': nrm(ks[17], (L, D_MIX, D_MODEL), D_MIX ** -0.5),
        'w_router': nrm(ks[18], (L, D_MODEL, N_EXPERTS), D_MODEL ** -0.5),
        'router_bias': nrm(ks[19], (L, N_EXPERTS), 0.01),
        'w_gate': nrm(ks[20], (L, N_EXPERTS, D_MODEL, D_EXPERT), D_MODEL ** -0.5),
        'w_up': nrm(ks[21], (L, N_EXPERTS, D_MODEL, D_EXPERT), D_MODEL ** -0.5),
        'w_down': nrm(ks[22], (L, N_EXPERTS, D_EXPERT, D_MODEL), D_EXPERT ** -0.5),
        'ws_gate': nrm(ks[23], (L, D_MODEL, D_EXPERT), D_MODEL ** -0.5),
        'ws_up': nrm(ks[24], (L, D_MODEL, D_EXPERT), D_MODEL ** -0.5),
        'ws_down': nrm(ks[25], (L, D_EXPERT, D_MODEL), D_EXPERT ** -0.5),
    }


def reference(x, c, positions, w_ada, b_ada, norm_mix_g, norm_ffn_g, w_in, hg_lb_logits, hg_out_g,
              mla_q_a_g, mla_w_uq, mla_kv_a_g, mla_w_ukv, mla_q_norm_g, mla_k_norm_g, mla_out_g,
              w_out, w_router, router_bias, w_gate, w_up, w_down, ws_gate, ws_up, ws_down):
    lb_all = jnp.cumsum(jax.nn.softmax(hg_lb_logits.astype(jnp.float32), axis=0), axis=0)
    cond = jax.nn.silu(c)
    offs = [int(o) for o in np.cumsum(IN_SIZES)[:-1]]
    h = x
    for l in range(DEPTH):
        mod = cond @ w_ada[l] + b_ada[l]
        sh_m, sc_m, gt_m, sh_f, sc_f, gt_f = jnp.split(mod, 6, axis=-1)

        u = rms_norm(h, norm_mix_g[l]) * (1.0 + sc_m[:, None, :]) + sh_m[:, None, :]
        proj = u @ w_in[l]
        hq, hf, hi, hgate, cq, ckv, krope = jnp.split(proj, offs, axis=-1)
        o_hg = hgrn2_group(hq, hf, hi, hgate, lb_all[l], hg_out_g[l])
        o_mla = mla_group(cq, ckv, krope, positions, mla_q_a_g[l], mla_w_uq[l], mla_kv_a_g[l],
                          mla_w_ukv[l], mla_q_norm_g[l], mla_k_norm_g[l])
        merged = jnp.concatenate([o_hg, rms_norm(o_mla, mla_out_g[l])], axis=-1)
        h = h + gt_m[:, None, :] * (merged @ w_out[l])

        u = rms_norm(h, norm_ffn_g[l]) * (1.0 + sc_f[:, None, :]) + sh_f[:, None, :]
        h = h + gt_f[:, None, :] * moe(u, w_router[l], router_bias[l], w_gate[l], w_up[l], w_down[l],
                                       ws_gate[l], ws_up[l], ws_down[l])
    return h
```

```python
import functools

import numpy as np
import jax
import jax.numpy as jnp
from jax import lax
from jax.experimental import pallas as pl
from jax.experimental.pallas import tpu as pltpu

F32 = jnp.float32
BF16 = jnp.bfloat16

D_MODEL = 2048
EPS = 1e-6

HG_HEADS = 8
HG_DK = 128
HG_DV = 128
HG_WIDTH = HG_HEADS * HG_DV
HG_CHUNK = 128

MLA_HEADS = 8
MLA_Q_RANK = 512
MLA_KV_RANK = 256
MLA_NOPE = 128
MLA_ROPE = 64
MLA_QK = MLA_NOPE + MLA_ROPE
MLA_QK_PAD = 256
MLA_V = 128
MLA_WIDTH = MLA_HEADS * MLA_V
ROPE_THETA = 10000.0

IN_COLS = 4928
IN_PAD = 5120
COL_CQ = 4096
COL_CKV = 4608
COL_KR = 4864

N_EXPERTS = 64
TOP_K = 8
N_GROUPS = 8
TOPK_GROUPS = 4
EXPERTS_PER_GROUP = N_EXPERTS // N_GROUPS
D_EXPERT = 512
ROUTED_SCALE = 2.5
MOE_BLK = 256

VMEM_LIMIT = 56 * 1024 * 1024

NT_DIMS = (((1,), (1,)), ((), ()))
TN_DIMS = (((0,), (0,)), ((), ()))


def _cparams(sem):
    return pltpu.CompilerParams(dimension_semantics=sem, vmem_limit_bytes=VMEM_LIMIT)


def _sigmoid(x):
    return 1.0 / (1.0 + jnp.exp(-x))


def _split_bf16(x):
    hi = x.astype(BF16)
    lo = (x - hi.astype(F32)).astype(BF16)
    return hi, lo


def _adaln_kernel(c_ref, w_ref, b_ref, o_ref):
    c = c_ref[...]
    cond = c * _sigmoid(c)
    hi, lo = _split_bf16(cond)
    lhs = jnp.concatenate([hi, lo], axis=0)
    r = jnp.dot(lhs, w_ref[...].astype(BF16), preferred_element_type=F32)
    o_ref[...] = r[:8] + r[8:] + b_ref[...]


def _adaln(c8, w, b):
    d, n = w.shape
    tn = 1024
    return pl.pallas_call(
        _adaln_kernel,
        out_shape=jax.ShapeDtypeStruct((8, n), F32),
        grid=(n // tn,),
        in_specs=[
            pl.BlockSpec((8, d), lambda j: (0, 0)),
            pl.BlockSpec((d, tn), lambda j: (0, j)),
            pl.BlockSpec((1, tn), lambda j: (0, j)),
        ],
        out_specs=pl.BlockSpec((8, tn), lambda j: (0, j)),
        compiler_params=_cparams(("arbitrary",)),
        name="adaln",
    )(c8, w, b)


def _inproj_kernel(x_ref, g_ref, sc_ref, sh_ref, w_ref, o_ref, u_scr, *, tm):
    @pl.when(pl.program_id(1) == 0)
    def _():
        g = g_ref[...]
        sc = 1.0 + sc_ref[0]
        sh = sh_ref[0]

        def body(r, carry):
            rows = pl.ds(pl.multiple_of(r * 128, 128), 128)
            x = x_ref[rows, :]
            ms = jnp.mean(x * x, axis=-1, keepdims=True)
            u = x * lax.rsqrt(ms + EPS) * g * sc + sh
            u_scr[rows, :] = u.astype(BF16)
            return carry

        lax.fori_loop(0, tm // 128, body, 0)

    o_ref[...] = jnp.dot(u_scr[...], w_ref[...], preferred_element_type=F32)


def _inproj(x2, g, mod3, w_bf, seq):
    t, d = x2.shape
    n = w_bf.shape[1]
    tm = min(1024, seq)
    tn = 1280
    per_b = seq // tm
    return pl.pallas_call(
        functools.partial(_inproj_kernel, tm=tm),
        out_shape=jax.ShapeDtypeStruct((t, n), F32),
        grid=(t // tm, n // tn),
        in_specs=[
            pl.BlockSpec((tm, d), lambda i, j: (i, 0)),
            pl.BlockSpec((1, d), lambda i, j: (0, 0)),
            pl.BlockSpec((1, 1, d), lambda i, j: (i // per_b, 0, 1)),
            pl.BlockSpec((1, 1, d), lambda i, j: (i // per_b, 0, 0)),
            pl.BlockSpec((d, tn), lambda i, j: (0, j)),
        ],
        out_specs=pl.BlockSpec((tm, tn), lambda i, j: (i, j)),
        scratch_shapes=[pltpu.VMEM((tm, d), BF16)],
        compiler_params=_cparams(("arbitrary", "arbitrary")),
        name="inproj",
    )(x2, g, mod3, mod3, w_bf)


def _hgrn_consts():
    c = HG_CHUNK
    t = np.arange(c)[:, None]
    j = np.arange(c)[None, :]
    tril = (j <= t).astype(np.float32)
    mats = [tril]
    masks = [np.eye(c, dtype=np.float32)]
    s = c // 2
    while s >= 1:
        ref = (t // (2 * s)) * (2 * s) + s - 1
        mats.append(tril - (j <= ref).astype(np.float32))
        masks.append((((t // s) % 2 == 1) & ((j // s) == (t // s) - 1)).astype(np.float32))
        s //= 2
    mats.append((j > t).astype(np.float32))
    return np.concatenate(mats, 0), np.stack(masks, 0)


def _hgrn_kernel(q_ref, f_ref, i_ref, g_ref, lb_ref, og_ref, w_ref, m_ref, o_ref, st_scr, *, seq):
    c = HG_CHUNK
    n_lvl = m_ref.shape[0]
    st_scr[...] = jnp.zeros_like(st_scr)
    lb = lb_ref[...]
    og = og_ref[...]

    def chunk(ci, carry):
        rows = pl.ds(pl.multiple_of(ci * c, c), c)
        q = q_ref[rows, :]
        v = i_ref[rows, :]
        g = g_ref[rows, :]
        f = lb + (1.0 - lb) * _sigmoid(f_ref[rows, :])
        lf = jnp.log(f)
        kk = 1.0 - f
        hi, lo = _split_bf16(lf)
        xs = jnp.dot(w_ref[...], jnp.concatenate([hi, lo], axis=1), preferred_element_type=F32)
        xs = xs[:, :c] + xs[:, c:]
        b = xs[0:c]
        bl = xs[n_lvl * c:(n_lvl + 1) * c]
        st = st_scr[...]
        vb = v.astype(BF16)
        o = lax.dot_general((q * jnp.exp(b)).astype(BF16), st.astype(BF16), NT_DIMS,
                            preferred_element_type=F32)
        attn = m_ref[0] * lax.dot_general(q.astype(BF16), kk.astype(BF16), NT_DIMS,
                                          preferred_element_type=F32)
        for l in range(1, n_lvl):
            e = jnp.exp(-jnp.abs(xs[l * c:(l + 1) * c]))
            attn = attn + m_ref[l] * lax.dot_general(
                (q * e).astype(BF16), (kk * e).astype(BF16), NT_DIMS, preferred_element_type=F32)
        o = o + jnp.dot(attn.astype(BF16), vb, preferred_element_type=F32)
        khat = (kk * jnp.exp(bl)).astype(BF16)
        st_scr[...] = st * jnp.exp(b[c - 1:c, :]) + lax.dot_general(
            vb, khat, TN_DIMS, preferred_element_type=F32)
        ms = jnp.mean(o * o, axis=-1, keepdims=True)
        y = o * lax.rsqrt(ms + EPS) * og * (g * _sigmoid(g))
        o_ref[rows, :] = y.astype(BF16)
        return carry

    lax.fori_loop(0, seq // c, chunk, 0)


def _hgrn(proj, lb, og, batch, seq):
    t = proj.shape[0]
    wm, mm = _hgrn_consts()
    wm = jnp.asarray(wm, BF16)
    mm = jnp.asarray(mm, F32)
    h = HG_HEADS
    return pl.pallas_call(
        functools.partial(_hgrn_kernel, seq=seq),
        out_shape=jax.ShapeDtypeStruct((t, HG_WIDTH), BF16),
        grid=(batch, h),
        in_specs=[
            pl.BlockSpec((seq, 128), lambda b, i: (b, i)),
            pl.BlockSpec((seq, 128), lambda b, i: (b, h + i)),
            pl.BlockSpec((seq, 128), lambda b, i: (b, 2 * h + i)),
            pl.BlockSpec((seq, 128), lambda b, i: (b, 3 * h + i)),
            pl.BlockSpec((1, 128), lambda b, i: (0, i)),
            pl.BlockSpec((1, 128), lambda b, i: (0, 0)),
            pl.BlockSpec(wm.shape, lambda b, i: (0, 0)),
            pl.BlockSpec(mm.shape, lambda b, i: (0, 0, 0)),
        ],
        out_specs=pl.BlockSpec((seq, 128), lambda b, i: (b, i)),
        scratch_shapes=[pltpu.VMEM((HG_DV, HG_DK), F32)],
        compiler_params=_cparams(("arbitrary", "arbitrary")),
        name="hgrn2",
    )(proj, proj, proj, proj, lb, og, wm, mm)


def _mla_prep_kernel(cq_ref, ckv_ref, kr_ref, pos_ref, qag_ref, kvag_ref, wuq_ref, wukv_ref,
                     gq_ref, gkn_ref, gkr_ref, invf_ref, sgn_ref, cm_ref,
                     q_out, k_out, v_out):
    cq = cq_ref[...]
    cqn = cq * lax.rsqrt(jnp.mean(cq * cq, axis=-1, keepdims=True) + EPS) * qag_ref[...]
    qf = jnp.dot(cqn.astype(BF16), wuq_ref[...], preferred_element_type=F32)
    ckv = ckv_ref[...]
    ckvn = ckv * lax.rsqrt(jnp.mean(ckv * ckv, axis=-1, keepdims=True) + EPS) * kvag_ref[...]
    kvf = jnp.dot(ckvn.astype(BF16), wukv_ref[...], preferred_element_type=F32)

    ang = pos_ref[...].astype(F32) * invf_ref[...]
    cos_t = jnp.cos(ang) * cm_ref[...]
    sin_t = jnp.sin(ang) * sgn_ref[...]

    def rope(r):
        return r * cos_t + (pltpu.roll(r, 96, 1) + pltpu.roll(r, 32, 1)) * sin_t

    kr = kr_ref[...]
    kr_ss = jnp.sum(kr * kr, axis=-1, keepdims=True)
    krr = rope(kr * gkr_ref[...])
    gq = gq_ref[...]
    gkn = gkn_ref[...]
    scale = MLA_QK ** -0.5
    for h in range(MLA_HEADS):
        qh = qf[:, h * MLA_QK_PAD:(h + 1) * MLA_QK_PAD]
        inv = lax.rsqrt(jnp.sum(qh * qh, axis=-1, keepdims=True) / MLA_QK + EPS)
        qn = qh * inv * gq
        qo = jnp.concatenate([qn[:, :MLA_NOPE], rope(qn[:, MLA_NOPE:])], axis=-1) * scale
        q_out[0, h] = qo.astype(BF16)
        kn = kvf[:, h * 256:h * 256 + MLA_NOPE]
        inv = lax.rsqrt((jnp.sum(kn * kn, axis=-1, keepdims=True) + kr_ss) / MLA_QK + EPS)
        k_out[0, h] = jnp.concatenate([kn * inv * gkn, krr * inv], axis=-1).astype(BF16)
        v_out[0, h] = kvf[:, h * 256 + MLA_NOPE:(h + 1) * 256].astype(BF16)


def _mla_prep(proj, pos2, qag, kvag, wuq_p, wukv, gq, gkn, gkr, invf, sgn, cm, batch, seq):
    tm = min(256, seq)
    per_b = seq // tm
    hh = MLA_HEADS
    full = lambda shape: pl.BlockSpec(shape, lambda b, i: tuple(0 for _ in shape))
    return pl.pallas_call(
        _mla_prep_kernel,
        out_shape=(
            jax.ShapeDtypeStruct((batch, hh, seq, MLA_QK_PAD), BF16),
            jax.ShapeDtypeStruct((batch, hh, seq, MLA_QK_PAD), BF16),
            jax.ShapeDtypeStruct((batch, hh, seq, MLA_V), BF16),
        ),
        grid=(batch, per_b),
        in_specs=[
            pl.BlockSpec((tm, MLA_Q_RANK), lambda b, i: (b * per_b + i, COL_CQ // MLA_Q_RANK)),
            pl.BlockSpec((tm, MLA_KV_RANK), lambda b, i: (b * per_b + i, COL_CKV // MLA_KV_RANK)),
            pl.BlockSpec((tm, 128), lambda b, i: (b * per_b + i, COL_KR // 128)),
            pl.BlockSpec((tm, 1), lambda b, i: (b * per_b + i, 0)),
            full((1, MLA_Q_RANK)),
            full((1, MLA_KV_RANK)),
            full(wuq_p.shape),
            full(wukv.shape),
            full((1, MLA_QK_PAD)),
            full((1, 128)),
            full((1, 128)),
            full((1, 128)),
            full((1, 128)),
            full((1, 128)),
        ],
        out_specs=(
            pl.BlockSpec((1, hh, tm, MLA_QK_PAD), lambda b, i: (b, 0, i, 0)),
            pl.BlockSpec((1, hh, tm, MLA_QK_PAD), lambda b, i: (b, 0, i, 0)),
            pl.BlockSpec((1, hh, tm, MLA_V), lambda b, i: (b, 0, i, 0)),
        ),
        compiler_params=_cparams(("arbitrary", "arbitrary")),
        name="mla_prep",
    )(proj, proj, proj, pos2, qag, kvag, wuq_p, wukv, gq, gkn, gkr, invf, sgn, cm)


def _attn_kernel(q_ref, k_ref, v_ref, o_ref, m_scr, l_scr, acc_scr, *, tq):
    i = pl.program_id(2)
    j = pl.program_id(3)

    @pl.when(j == 0)
    def _():
        m_scr[...] = jnp.full_like(m_scr, -jnp.inf)
        l_scr[...] = jnp.zeros_like(l_scr)
        acc_scr[...] = jnp.zeros_like(acc_scr)

    @pl.when(j <= i)
    def _():
        s = lax.dot_general(q_ref[0, 0], k_ref[0, 0], NT_DIMS, preferred_element_type=F32)
        row = lax.broadcasted_iota(jnp.int32, (tq, tq), 0)
        col = lax.broadcasted_iota(jnp.int32, (tq, tq), 1)
        s = jnp.where((col <= row) | (j < i), s, -jnp.inf)
        m_prev = m_scr[...]
        m_new = jnp.maximum(m_prev, jnp.max(s, axis=-1, keepdims=True))
        alpha = jnp.exp(m_prev - m_new)
        p = jnp.exp(s - m_new)
        l_scr[...] = alpha * l_scr[...] + jnp.sum(p, axis=-1, keepdims=True)
        acc_scr[...] = alpha * acc_scr[...] + jnp.dot(
            p.astype(BF16), v_ref[0, 0], preferred_element_type=F32)
        m_scr[...] = m_new

    @pl.when(j == i)
    def _():
        o_ref[...] = acc_scr[...] / l_scr[...]


def _attention(q, k, v):
    batch, hh, seq, dq = q.shape
    tq = min(512, seq)
    nq = seq // tq
    return pl.pallas_call(
        functools.partial(_attn_kernel, tq=tq),
        out_shape=jax.ShapeDtypeStruct((batch * seq, hh * MLA_V), F32),
        grid=(batch, hh, nq, nq),
        in_specs=[
            pl.BlockSpec((1, 1, tq, dq), lambda b, h, i, j: (b, h, i, 0)),
            pl.BlockSpec((1, 1, tq, dq), lambda b, h, i, j: (b, h, jnp.minimum(i, j), 0)),
            pl.BlockSpec((1, 1, tq, MLA_V), lambda b, h, i, j: (b, h, jnp.minimum(i, j), 0)),
        ],
        out_specs=pl.BlockSpec((tq, MLA_V), lambda b, h, i, j: (b * nq + i, h)),
        scratch_shapes=[
            pltpu.VMEM((tq, 1), F32),
            pltpu.VMEM((tq, 1), F32),
            pltpu.VMEM((tq, MLA_V), F32),
        ],
        compiler_params=_cparams(("arbitrary", "arbitrary", "arbitrary", "arbitrary")),
        name="mla_attn",
    )(q, k, v)


def _outproj_kernel(ohg_ref, omla_ref, x_ref, gt_ref, sc_ref, sh_ref, og_ref, g2_ref,
                    wo_ref, wrh_ref, wrl_ref, h_out, u_out, lg_out):
    om = omla_ref[...]
    omn = om * lax.rsqrt(jnp.mean(om * om, axis=-1, keepdims=True) + EPS) * og_ref[...]
    y = jnp.dot(ohg_ref[...], wo_ref[:HG_WIDTH, :], preferred_element_type=F32)
    y = y + jnp.dot(omn.astype(BF16), wo_ref[HG_WIDTH:, :], preferred_element_type=F32)
    h1 = x_ref[...] + gt_ref[0] * y
    h_out[...] = h1
    u = h1 * lax.rsqrt(jnp.mean(h1 * h1, axis=-1, keepdims=True) + EPS) * g2_ref[...]
    u = u * (1.0 + sc_ref[0]) + sh_ref[0]
    uh, ul = _split_bf16(u)
    u_out[...] = uh
    lg = lax.dot_general(wrh_ref[...], uh, NT_DIMS, preferred_element_type=F32)
    lg = lg + lax.dot_general(wrl_ref[...], uh, NT_DIMS, preferred_element_type=F32)
    lg = lg + lax.dot_general(wrh_ref[...], ul, NT_DIMS, preferred_element_type=F32)
    lg_out[...] = lg


def _outproj(ohg, omla, x2, mod3, og, g2, wo_bf, wr_hi, wr_lo, seq):
    t, d = x2.shape
    tm = min(256, seq)
    per_b = seq // tm
    full = lambda shape: pl.BlockSpec(shape, lambda i: tuple(0 for _ in shape))
    modspec = lambda col: pl.BlockSpec((1, 1, d), lambda i: (i // per_b, 0, col))
    return pl.pallas_call(
        _outproj_kernel,
        out_shape=(
            jax.ShapeDtypeStruct((t, d), F32),
            jax.ShapeDtypeStruct((t, d), BF16),
            jax.ShapeDtypeStruct((N_EXPERTS, t), F32),
        ),
        grid=(t // tm,),
        in_specs=[
            pl.BlockSpec((tm, HG_WIDTH), lambda i: (i, 0)),
            pl.BlockSpec((tm, MLA_WIDTH), lambda i: (i, 0)),
            pl.BlockSpec((tm, d), lambda i: (i, 0)),
            modspec(2),
            modspec(4),
            modspec(3),
            full((1, MLA_WIDTH)),
            full((1, d)),
            full(wo_bf.shape),
            full(wr_hi.shape),
            full(wr_lo.shape),
        ],
        out_specs=(
            pl.BlockSpec((tm, d), lambda i: (i, 0)),
            pl.BlockSpec((tm, d), lambda i: (i, 0)),
            pl.BlockSpec((N_EXPERTS, tm), lambda i: (0, i)),
        ),
        compiler_params=_cparams(("arbitrary",)),
        name="outproj",
    )(ohg, omla, x2, mod3, mod3, mod3, og, g2, wo_bf, wr_hi, wr_lo)


def _route_kernel(lg_ref, bias_ref, tri_ref, idx_out, w_out, rank_out, cnt_out, carry, *, tt):
    @pl.when(pl.program_id(0) == 0)
    def _():
        carry[...] = jnp.zeros_like(carry)

    ne, ng, eg = N_EXPERTS, N_GROUPS, EXPERTS_PER_GROUP
    neg = -jnp.inf
    sc = _sigmoid(lg_ref[...])
    sel = sc + bias_ref[...]
    sel3 = sel.reshape(ng, eg, tt)
    sub = lax.broadcasted_iota(jnp.int32, (ng, eg, tt), 1)
    m1 = jnp.max(sel3, axis=1, keepdims=True)
    first = jnp.min(jnp.where(sel3 == m1, sub, eg), axis=1, keepdims=True)
    m2 = jnp.max(jnp.where(sub == first, neg, sel3), axis=1, keepdims=True)
    gs = (m1 + m2).reshape(ng, tt)

    gio = lax.broadcasted_iota(jnp.int32, (ng, tt), 0)
    gsel = jnp.zeros((ng, tt), F32)
    for _ in range(TOPK_GROUPS):
        m = jnp.max(gs, axis=0, keepdims=True)
        gi = jnp.min(jnp.where(gs == m, gio, ng), axis=0, keepdims=True)
        hit = gio == gi
        gsel = jnp.where(hit, 1.0, gsel)
        gs = jnp.where(hit, neg, gs)
    gfull = jnp.broadcast_to(gsel.reshape(ng, 1, tt), (ng, eg, tt)).reshape(ne, tt)
    selm = jnp.where(gfull > 0.0, sel, neg)

    eio = lax.broadcasted_iota(jnp.int32, (ne, tt), 0)
    chosen = jnp.zeros((ne, tt), F32)
    idxs, ws = [], []
    for _ in range(TOP_K):
        m = jnp.max(selm, axis=0, keepdims=True)
        ei = jnp.min(jnp.where(selm == m, eio, ne), axis=0, keepdims=True)
        hit = eio == ei
        ws.append(jnp.sum(jnp.where(hit, sc, 0.0), axis=0, keepdims=True))
        idxs.append(ei)
        chosen = jnp.where(hit, 1.0, chosen)
        selm = jnp.where(hit, neg, selm)
    wsum = ws[0]
    for k in range(1, TOP_K):
        wsum = wsum + ws[k]

    excl = jnp.dot(chosen.astype(BF16), tri_ref[...], preferred_element_type=F32) + carry[...]
    for k in range(TOP_K):
        idx_out[k:k + 1, :] = idxs[k]
        w_out[k:k + 1, :] = ws[k] / wsum * ROUTED_SCALE
        rk = jnp.sum(jnp.where(eio == idxs[k], excl, 0.0), axis=0, keepdims=True)
        rank_out[k:k + 1, :] = rk.astype(jnp.int32)
    carry[...] = carry[...] + jnp.sum(chosen, axis=1, keepdims=True)
    cnt_out[...] = carry[...]


def _route(logits_t, bias_col):
    ne, t = logits_t.shape
    tt = min(512, t)
    tri = jnp.asarray(np.triu(np.ones((tt, tt), np.float32), 1), BF16)
    return pl.pallas_call(
        functools.partial(_route_kernel, tt=tt),
        out_shape=(
            jax.ShapeDtypeStruct((TOP_K, t), jnp.int32),
            jax.ShapeDtypeStruct((TOP_K, t), F32),
            jax.ShapeDtypeStruct((TOP_K, t), jnp.int32),
            jax.ShapeDtypeStruct((ne, 1), F32),
        ),
        grid=(t // tt,),
        in_specs=[
            pl.BlockSpec((ne, tt), lambda i: (0, i)),
            pl.BlockSpec((ne, 1), lambda i: (0, 0)),
            pl.BlockSpec((tt, tt), lambda i: (0, 0)),
        ],
        out_specs=(
            pl.BlockSpec((TOP_K, tt), lambda i: (0, i)),
            pl.BlockSpec((TOP_K, tt), lambda i: (0, i)),
            pl.BlockSpec((TOP_K, tt), lambda i: (0, i)),
            pl.BlockSpec((ne, 1), lambda i: (0, 0)),
        ),
        scratch_shapes=[pltpu.VMEM((ne, 1), F32)],
        compiler_params=_cparams(("arbitrary",)),
        name="route",
    )(logits_t, bias_col, tri)


def _gmm_kernel(be_ref, meta_ref, xs_ref, wg_ref, wu_ref, wd_ref, y_ref, wg_s, wu_s, wd_s):
    i = pl.program_id(0)
    prev = be_ref[jnp.maximum(i - 1, 0)]
    fresh = (i == 0) | (be_ref[i] != prev)

    @pl.when(fresh & (i < meta_ref[0]))
    def _():
        wg_s[...] = wg_ref[...].astype(BF16)
        wu_s[...] = wu_ref[...].astype(BF16)
        wd_s[...] = wd_ref[...].astype(BF16)

    @pl.when(i < meta_ref[0])
    def _():
        xb = xs_ref[...]
        a = jnp.dot(xb, wg_s[...], preferred_element_type=F32)
        u = jnp.dot(xb, wu_s[...], preferred_element_type=F32)
        hid = (a * _sigmoid(a) * u).astype(BF16)
        y_ref[...] = jnp.dot(hid, wd_s[...], preferred_element_type=F32).astype(BF16)


def _gmm(blk_expert, meta, xs, w_gate, w_up, w_down):
    p, d = xs.shape
    n_blk = p // MOE_BLK
    de = w_gate.shape[-1]
    row = lambda i, be, meta: (jnp.minimum(i, meta[0] - 1), 0)
    wsel = lambda i, be, meta: (be[jnp.minimum(i, meta[0] - 1)], 0, 0)
    return pl.pallas_call(
        _gmm_kernel,
        out_shape=jax.ShapeDtypeStruct((p, d), BF16),
        grid_spec=pltpu.PrefetchScalarGridSpec(
            num_scalar_prefetch=2,
            grid=(n_blk,),
            in_specs=[
                pl.BlockSpec((MOE_BLK, d), row),
                pl.BlockSpec((None, d, de), wsel),
                pl.BlockSpec((None, d, de), wsel),
                pl.BlockSpec((None, de, d), wsel),
            ],
            out_specs=pl.BlockSpec((MOE_BLK, d), row),
            scratch_shapes=[
                pltpu.VMEM((d, de), BF16),
                pltpu.VMEM((d, de), BF16),
                pltpu.VMEM((de, d), BF16),
            ],
        ),
        compiler_params=_cparams(("arbitrary",)),
        name="moe_gmm",
    )(blk_expert, meta, xs, w_gate, w_up, w_down)


def _final_kernel(u_ref, h_ref, gt_ref, ytm_ref, wt_ref, wsg_ref, wsu_ref, wsd_ref, o_ref):
    u = u_ref[...]
    a = jnp.dot(u, wsg_ref[...], preferred_element_type=F32)
    b = jnp.dot(u, wsu_ref[...], preferred_element_type=F32)
    hid = (a * _sigmoid(a) * b).astype(BF16)
    acc = jnp.dot(hid, wsd_ref[...], preferred_element_type=F32)
    wt = wt_ref[...]
    for k in range(TOP_K):
        acc = acc + ytm_ref[k].astype(F32) * wt[:, k:k + 1]
    o_ref[...] = h_ref[...] + gt_ref[0] * acc


def _final(u2, h1, mod3, ytm, wt, wsg, wsu, wsd, seq):
    t, d = h1.shape
    tm = min(256, seq)
    per_b = seq // tm
    full = lambda shape: pl.BlockSpec(shape, lambda i: tuple(0 for _ in shape))
    return pl.pallas_call(
        _final_kernel,
        out_shape=jax.ShapeDtypeStruct((t, d), F32),
        grid=(t // tm,),
        in_specs=[
            pl.BlockSpec((tm, d), lambda i: (i, 0)),
            pl.BlockSpec((tm, d), lambda i: (i, 0)),
            pl.BlockSpec((1, 1, d), lambda i: (i // per_b, 0, 5)),
            pl.BlockSpec((TOP_K, tm, d), lambda i: (0, i, 0)),
            pl.BlockSpec((tm, TOP_K), lambda i: (i, 0)),
            full(wsg.shape),
            full(wsu.shape),
            full(wsd.shape),
        ],
        out_specs=pl.BlockSpec((tm, d), lambda i: (i, 0)),
        compiler_params=_cparams(("arbitrary",)),
        name="final",
    )(u2, h1, mod3, ytm, wt, wsg, wsu, wsd)


def _layer(h2, mod3, positions, lb, p, batch, seq):
    t, d = h2.shape

    w_in_bf = jnp.pad(p["w_in"], ((0, 0), (0, IN_PAD - IN_COLS))).astype(BF16)
    proj = _inproj(h2, p["norm_mix_g"][None, :], mod3, w_in_bf, seq)

    o_hg = _hgrn(proj, lb[None, :], p["hg_out_g"][None, :], batch, seq)

    wuq = p["mla_w_uq"].reshape(MLA_Q_RANK, MLA_HEADS, MLA_QK)
    wuq_p = jnp.pad(wuq, ((0, 0), (0, 0), (0, MLA_QK_PAD - MLA_QK))).reshape(
        MLA_Q_RANK, MLA_HEADS * MLA_QK_PAD).astype(BF16)
    wukv = p["mla_w_ukv"].astype(BF16)
    gq = jnp.pad(p["mla_q_norm_g"], (0, MLA_QK_PAD - MLA_QK))[None, :]
    gkn = p["mla_k_norm_g"][None, :MLA_NOPE]
    gkr = jnp.pad(p["mla_k_norm_g"][MLA_NOPE:], (0, 128 - MLA_ROPE))[None, :]
    half = MLA_ROPE // 2
    inv_freq = ROPE_THETA ** (-jnp.arange(half, dtype=F32) / half)
    zeros64 = jnp.zeros((128 - MLA_ROPE,), F32)
    invf = jnp.concatenate([inv_freq, inv_freq, zeros64])[None, :]
    sgn = jnp.concatenate([-jnp.ones((half,), F32), jnp.ones((half,), F32), zeros64])[None, :]
    cm = jnp.concatenate([jnp.ones((MLA_ROPE,), F32), zeros64])[None, :]
    q, k, v = _mla_prep(proj, positions.reshape(t, 1), p["mla_q_a_g"][None, :],
                        p["mla_kv_a_g"][None, :], wuq_p, wukv, gq, gkn, gkr, invf, sgn, cm,
                        batch, seq)
    o_mla = _attention(q, k, v)

    wr_t = p["w_router"].T
    wr_hi = wr_t.astype(BF16)
    wr_lo = (wr_t - wr_hi.astype(F32)).astype(BF16)
    h1, u2, logits_t = _outproj(o_hg, o_mla, h2, mod3, p["mla_out_g"][None, :],
                                p["norm_ffn_g"][None, :], p["w_out"].astype(BF16),
                                wr_hi, wr_lo, seq)

    top_idx, top_w, rank, cnt = _route(logits_t, p["router_bias"][:, None])

    counts = cnt[:, 0].astype(jnp.int32)
    padded = (counts + MOE_BLK - 1) // MOE_BLK * MOE_BLK
    pad_end = jnp.cumsum(padded)
    pad_start = pad_end - padded
    n_blk = (t * TOP_K) // MOE_BLK + N_EXPERTS
    blk_expert = jnp.minimum(
        jnp.searchsorted(pad_end, jnp.arange(n_blk, dtype=jnp.int32) * MOE_BLK, side="right"),
        N_EXPERTS - 1).astype(jnp.int32)
    meta = (pad_end[-1:] // MOE_BLK).astype(jnp.int32)
    dest = pad_start[top_idx] + rank
    tok = jnp.broadcast_to(jnp.arange(t, dtype=jnp.int32)[None, :], (TOP_K, t))
    row_tok = jnp.zeros((n_blk * MOE_BLK,), jnp.int32).at[dest.reshape(-1)].set(tok.reshape(-1))
    xs = u2[row_tok]

    y = _gmm(blk_expert, meta, xs, p["w_gate"], p["w_up"], p["w_down"])
    ytm = y[dest]

    return _final(u2, h1, mod3, ytm, top_w.T, p["ws_gate"].astype(BF16),
                  p["ws_up"].astype(BF16), p["ws_down"].astype(BF16), seq)


def kernel(x, c, positions, w_ada, b_ada, norm_mix_g, norm_ffn_g, w_in, hg_lb_logits, hg_out_g,
           mla_q_a_g, mla_w_uq, mla_kv_a_g, mla_w_ukv, mla_q_norm_g, mla_k_norm_g, mla_out_g,
           w_out, w_router, router_bias, w_gate, w_up, w_down, ws_gate, ws_up, ws_down):
    batch, seq, d = x.shape
    depth = w_ada.shape[0]
    lb_all = jnp.cumsum(jax.nn.softmax(hg_lb_logits.astype(F32), axis=0), axis=0)
    c8 = jnp.pad(c, ((0, 8 - batch), (0, 0)))
    h2 = x.reshape(batch * seq, d)
    for l in range(depth):
        mod = _adaln(c8, w_ada[l], b_ada[l][None, :])[:batch]
        mod3 = mod.reshape(batch, 1, 6 * d)
        p = dict(norm_mix_g=norm_mix_g[l], norm_ffn_g=norm_ffn_g[l], w_in=w_in[l],
                 hg_out_g=hg_out_g[l], mla_q_a_g=mla_q_a_g[l], mla_w_uq=mla_w_uq[l],
                 mla_kv_a_g=mla_kv_a_g[l], mla_w_ukv=mla_w_ukv[l], mla_q_norm_g=mla_q_norm_g[l],
                 mla_k_norm_g=mla_k_norm_g[l], mla_out_g=mla_out_g[l], w_out=w_out[l],
                 w_router=w_router[l], router_bias=router_bias[l], w_gate=w_gate[l],
                 w_up=w_up[l], w_down=w_down[l], ws_gate=ws_gate[l], ws_up=ws_up[l],
                 ws_down=ws_down[l])
        h2 = _layer(h2, mod3, positions, lb_all[l], p, batch, seq)
    return h2.reshape(batch, seq, d)
```

```python
import functools

import numpy as np
import jax
import jax.numpy as jnp
from jax import lax
from jax.experimental import pallas as pl
from jax.experimental.pallas import tpu as pltpu

F32 = jnp.float32
BF16 = jnp.bfloat16

D_MODEL = 2048
EPS = 1e-6

HG_HEADS = 8
HG_DK = 128
HG_DV = 128
HG_WIDTH = HG_HEADS * HG_DV
HG_CHUNK = 128

MLA_HEADS = 8
MLA_Q_RANK = 512
MLA_KV_RANK = 256
MLA_NOPE = 128
MLA_ROPE = 64
MLA_QK = MLA_NOPE + MLA_ROPE
MLA_QK_PAD = 256
MLA_V = 128
MLA_WIDTH = MLA_HEADS * MLA_V
ROPE_THETA = 10000.0

IN_COLS = 4928
IN_PAD = 5120
COL_CQ = 4096
COL_CKV = 4608
COL_KR = 4864

N_EXPERTS = 64
TOP_K = 8
N_GROUPS = 8
TOPK_GROUPS = 4
EXPERTS_PER_GROUP = N_EXPERTS // N_GROUPS
D_EXPERT = 512
ROUTED_SCALE = 2.5
MOE_BLK = 256

VMEM_LIMIT = 56 * 1024 * 1024

NT_DIMS = (((1,), (1,)), ((), ()))
TN_DIMS = (((0,), (0,)), ((), ()))


def _cparams(sem):
    return pltpu.CompilerParams(dimension_semantics=sem, vmem_limit_bytes=VMEM_LIMIT)


def _sigmoid(x):
    return 1.0 / (1.0 + jnp.exp(-x))


def _split_bf16(x):
    hi = x.astype(BF16)
    lo = (x - hi.astype(F32)).astype(BF16)
    return hi, lo


def _adaln_kernel(c_ref, w_ref, b_ref, o_ref):
    c = c_ref[...]
    cond = c * _sigmoid(c)
    hi, lo = _split_bf16(cond)
    lhs = jnp.concatenate([hi, lo], axis=0)
    r = jnp.dot(lhs, w_ref[...].astype(BF16), preferred_element_type=F32)
    o_ref[...] = r[:8] + r[8:] + b_ref[...]


def _adaln(c8, w, b):
    d, n = w.shape
    tn = 1024
    return pl.pallas_call(
        _adaln_kernel,
        out_shape=jax.ShapeDtypeStruct((8, n), F32),
        grid=(n // tn,),
        in_specs=[
            pl.BlockSpec((8, d), lambda j: (0, 0)),
            pl.BlockSpec((d, tn), lambda j: (0, j)),
            pl.BlockSpec((1, tn), lambda j: (0, j)),
        ],
        out_specs=pl.BlockSpec((8, tn), lambda j: (0, j)),
        compiler_params=_cparams(("arbitrary",)),
        name="adaln",
    )(c8, w, b)


def _inproj_kernel(x_ref, g_ref, sc_ref, sh_ref, w_ref, o_ref, u_scr, *, tm):
    @pl.when(pl.program_id(1) == 0)
    def _():
        g = g_ref[...]
        sc = 1.0 + sc_ref[0]
        sh = sh_ref[0]

        def body(r, carry):
            rows = pl.ds(pl.multiple_of(r * 128, 128), 128)
            x = x_ref[rows, :]
            ms = jnp.mean(x * x, axis=-1, keepdims=True)
            u = x * lax.rsqrt(ms + EPS) * g * sc + sh
            u_scr[rows, :] = u.astype(BF16)
            return carry

        lax.fori_loop(0, tm // 128, body, 0)

    o_ref[...] = jnp.dot(u_scr[...], w_ref[...], preferred_element_type=F32)


def _inproj(x2, g, mod3, w_bf, seq):
    t, d = x2.shape
    n = w_bf.shape[1]
    tm = min(1024, seq)
    tn = 1280
    per_b = seq // tm
    return pl.pallas_call(
        functools.partial(_inproj_kernel, tm=tm),
        out_shape=jax.ShapeDtypeStruct((t, n), F32),
        grid=(t // tm, n // tn),
        in_specs=[
            pl.BlockSpec((tm, d), lambda i, j: (i, 0)),
            pl.BlockSpec((1, d), lambda i, j: (0, 0)),
            pl.BlockSpec((1, 1, d), lambda i, j: (i // per_b, 0, 1)),
            pl.BlockSpec((1, 1, d), lambda i, j: (i // per_b, 0, 0)),
            pl.BlockSpec((d, tn), lambda i, j: (0, j)),
        ],
        out_specs=pl.BlockSpec((tm, tn), lambda i, j: (i, j)),
        scratch_shapes=[pltpu.VMEM((tm, d), BF16)],
        compiler_params=_cparams(("arbitrary", "arbitrary")),
        name="inproj",
    )(x2, g, mod3, mod3, w_bf)


def _hgrn_consts():
    c = HG_CHUNK
    t = np.arange(c)[:, None]
    j = np.arange(c)[None, :]
    tril = (j <= t).astype(np.float32)
    mats = [tril]
    masks = [np.eye(c, dtype=np.float32)]
    s = c // 2
    while s >= 1:
        ref = (t // (2 * s)) * (2 * s) + s - 1
        mats.append(tril - (j <= ref).astype(np.float32))
        masks.append((((t // s) % 2 == 1) & ((j // s) == (t // s) - 1)).astype(np.float32))
        s //= 2
    mats.append((j > t).astype(np.float32))
    return np.concatenate(mats, 0), np.stack(masks, 0)


def _hgrn_kernel(q_ref, f_ref, i_ref, g_ref, lb_ref, og_ref, w_ref, m_ref, o_ref, st_scr, *, seq):
    c = HG_CHUNK
    n_lvl = m_ref.shape[0]
    st_scr[...] = jnp.zeros_like(st_scr)
    lb = lb_ref[...]
    og = og_ref[...]

    def chunk(ci, carry):
        rows = pl.ds(pl.multiple_of(ci * c, c), c)
        q = q_ref[rows, :]
        v = i_ref[rows, :]
        g = g_ref[rows, :]
        f = lb + (1.0 - lb) * _sigmoid(f_ref[rows, :])
        lf = jnp.log(f)
        kk = 1.0 - f
        hi, lo = _split_bf16(lf)
        xs = jnp.dot(w_ref[...], jnp.concatenate([hi, lo], axis=1), preferred_element_type=F32)
        xs = xs[:, :c] + xs[:, c:]
        b = xs[0:c]
        bl = xs[n_lvl * c:(n_lvl + 1) * c]
        st = st_scr[...]
        vb = v.astype(BF16)
        o = lax.dot_general((q * jnp.exp(b)).astype(BF16), st.astype(BF16), NT_DIMS,
                            preferred_element_type=F32)
        attn = m_ref[0] * lax.dot_general(q.astype(BF16), kk.astype(BF16), NT_DIMS,
                                          preferred_element_type=F32)
        for l in range(1, n_lvl):
            e = jnp.exp(-jnp.abs(xs[l * c:(l + 1) * c]))
            attn = attn + m_ref[l] * lax.dot_general(
                (q * e).astype(BF16), (kk * e).astype(BF16), NT_DIMS, preferred_element_type=F32)
        o = o + jnp.dot(attn.astype(BF16), vb, preferred_element_type=F32)
        khat = (kk * jnp.exp(bl)).astype(BF16)
        st_scr[...] = st * jnp.exp(b[c - 1:c, :]) + lax.dot_general(
            vb, khat, TN_DIMS, preferred_element_type=F32)
        ms = jnp.mean(o * o, axis=-1, keepdims=True)
        y = o * lax.rsqrt(ms + EPS) * og * (g * _sigmoid(g))
        o_ref[rows, :] = y.astype(BF16)
        return carry

    lax.fori_loop(0, seq // c, chunk, 0)


def _hgrn(proj, lb, og, batch, seq):
    t = proj.shape[0]
    wm, mm = _hgrn_consts()
    wm = jnp.asarray(wm, BF16)
    mm = jnp.asarray(mm, F32)
    h = HG_HEADS
    return pl.pallas_call(
        functools.partial(_hgrn_kernel, seq=seq),
        out_shape=jax.ShapeDtypeStruct((t, HG_WIDTH), BF16),
        grid=(batch, h),
        in_specs=[
            pl.BlockSpec((seq, 128), lambda b, i: (b, i)),
            pl.BlockSpec((seq, 128), lambda b, i: (b, h + i)),
            pl.BlockSpec((seq, 128), lambda b, i: (b, 2 * h + i)),
            pl.BlockSpec((seq, 128), lambda b, i: (b, 3 * h + i)),
            pl.BlockSpec((1, 128), lambda b, i: (0, i)),
            pl.BlockSpec((1, 128), lambda b, i: (0, 0)),
            pl.BlockSpec(wm.shape, lambda b, i: (0, 0)),
            pl.BlockSpec(mm.shape, lambda b, i: (0, 0, 0)),
        ],
        out_specs=pl.BlockSpec((seq, 128), lambda b, i: (b, i)),
        scratch_shapes=[pltpu.VMEM((HG_DV, HG_DK), F32)],
        compiler_params=_cparams(("arbitrary", "arbitrary")),
        name="hgrn2",
    )(proj, proj, proj, proj, lb, og, wm, mm)


def _mla_prep_kernel(cq_ref, ckv_ref, kr_ref, pos_ref, qag_ref, kvag_ref, wuq_ref, wukv_ref,
                     gq_ref, gkn_ref, gkr_ref, invf_ref, sgn_ref, cm_ref,
                     q_out, k_out, v_out):
    cq = cq_ref[...]
    cqn = cq * lax.rsqrt(jnp.mean(cq * cq, axis=-1, keepdims=True) + EPS) * qag_ref[...]
    qf = jnp.dot(cqn.astype(BF16), wuq_ref[...], preferred_element_type=F32)
    ckv = ckv_ref[...]
    ckvn = ckv * lax.rsqrt(jnp.mean(ckv * ckv, axis=-1, keepdims=True) + EPS) * kvag_ref[...]
    kvf = jnp.dot(ckvn.astype(BF16), wukv_ref[...], preferred_element_type=F32)

    ang = pos_ref[...].astype(F32) * invf_ref[...]
    cos_t = jnp.cos(ang) * cm_ref[...]
    sin_t = jnp.sin(ang) * sgn_ref[...]

    def rope(r):
        return r * cos_t + (pltpu.roll(r, 96, 1) + pltpu.roll(r, 32, 1)) * sin_t

    kr = kr_ref[...]
    kr_ss = jnp.sum(kr * kr, axis=-1, keepdims=True)
    krr = rope(kr * gkr_ref[...])
    gq = gq_ref[...]
    gkn = gkn_ref[...]
    scale = MLA_QK ** -0.5
    for h in range(MLA_HEADS):
        qh = qf[:, h * MLA_QK_PAD:(h + 1) * MLA_QK_PAD]
        inv = lax.rsqrt(jnp.sum(qh * qh, axis=-1, keepdims=True) / MLA_QK + EPS)
        qn = qh * inv * gq
        qo = jnp.concatenate([qn[:, :MLA_NOPE], rope(qn[:, MLA_NOPE:])], axis=-1) * scale
        q_out[0, h] = qo.astype(BF16)
        kn = kvf[:, h * 256:h * 256 + MLA_NOPE]
        inv = lax.rsqrt((jnp.sum(kn * kn, axis=-1, keepdims=True) + kr_ss) / MLA_QK + EPS)
        k_out[0, h] = jnp.concatenate([kn * inv * gkn, krr * inv], axis=-1).astype(BF16)
        v_out[0, h] = kvf[:, h * 256 + MLA_NOPE:(h + 1) * 256].astype(BF16)


def _mla_prep(proj, pos2, qag, kvag, wuq_p, wukv, gq, gkn, gkr, invf, sgn, cm, batch, seq):
    tm = min(256, seq)
    per_b = seq // tm
    hh = MLA_HEADS
    full = lambda shape: pl.BlockSpec(shape, lambda b, i: tuple(0 for _ in shape))
    return pl.pallas_call(
        _mla_prep_kernel,
        out_shape=(
            jax.ShapeDtypeStruct((batch, hh, seq, MLA_QK_PAD), BF16),
            jax.ShapeDtypeStruct((batch, hh, seq, MLA_QK_PAD), BF16),
            jax.ShapeDtypeStruct((batch, hh, seq, MLA_V), BF16),
        ),
        grid=(batch, per_b),
        in_specs=[
            pl.BlockSpec((tm, MLA_Q_RANK), lambda b, i: (b * per_b + i, COL_CQ // MLA_Q_RANK)),
            pl.BlockSpec((tm, MLA_KV_RANK), lambda b, i: (b * per_b + i, COL_CKV // MLA_KV_RANK)),
            pl.BlockSpec((tm, 128), lambda b, i: (b * per_b + i, COL_KR // 128)),
            pl.BlockSpec((tm, 1), lambda b, i: (b * per_b + i, 0)),
            full((1, MLA_Q_RANK)),
            full((1, MLA_KV_RANK)),
            full(wuq_p.shape),
            full(wukv.shape),
            full((1, MLA_QK_PAD)),
            full((1, 128)),
            full((1, 128)),
            full((1, 128)),
            full((1, 128)),
            full((1, 128)),
        ],
        out_specs=(
            pl.BlockSpec((1, hh, tm, MLA_QK_PAD), lambda b, i: (b, 0, i, 0)),
            pl.BlockSpec((1, hh, tm, MLA_QK_PAD), lambda b, i: (b, 0, i, 0)),
            pl.BlockSpec((1, hh, tm, MLA_V), lambda b, i: (b, 0, i, 0)),
        ),
        compiler_params=_cparams(("arbitrary", "arbitrary")),
        name="mla_prep",
    )(proj, proj, proj, pos2, qag, kvag, wuq_p, wukv, gq, gkn, gkr, invf, sgn, cm)


def _attn_kernel(q_ref, k_ref, v_ref, o_ref, m_scr, l_scr, acc_scr, *, tq):
    i = pl.program_id(2)
    j = pl.program_id(3)

    @pl.when(j == 0)
    def _():
        m_scr[...] = jnp.full_like(m_scr, -jnp.inf)
        l_scr[...] = jnp.zeros_like(l_scr)
        acc_scr[...] = jnp.zeros_like(acc_scr)

    @pl.when(j <= i)
    def _():
        s = lax.dot_general(q_ref[0, 0], k_ref[0, 0], NT_DIMS, preferred_element_type=F32)
        row = lax.broadcasted_iota(jnp.int32, (tq, tq), 0)
        col = lax.broadcasted_iota(jnp.int32, (tq, tq), 1)
        s = jnp.where((col <= row) | (j < i), s, -jnp.inf)
        m_prev = m_scr[...]
        m_new = jnp.maximum(m_prev, jnp.max(s, axis=-1, keepdims=True))
        alpha = jnp.exp(m_prev - m_new)
        p = jnp.exp(s - m_new)
        l_scr[...] = alpha * l_scr[...] + jnp.sum(p, axis=-1, keepdims=True)
        acc_scr[...] = alpha * acc_scr[...] + jnp.dot(
            p.astype(BF16), v_ref[0, 0], preferred_element_type=F32)
        m_scr[...] = m_new

    @pl.when(j == i)
    def _():
        o_ref[...] = acc_scr[...] / l_scr[...]


def _attention(q, k, v):
    batch, hh, seq, dq = q.shape
    tq = min(512, seq)
    nq = seq // tq
    return pl.pallas_call(
        functools.partial(_attn_kernel, tq=tq),
        out_shape=jax.ShapeDtypeStruct((batch * seq, hh * MLA_V), F32),
        grid=(batch, hh, nq, nq),
        in_specs=[
            pl.BlockSpec((1, 1, tq, dq), lambda b, h, i, j: (b, h, i, 0)),
            pl.BlockSpec((1, 1, tq, dq), lambda b, h, i, j: (b, h, jnp.minimum(i, j), 0)),
            pl.BlockSpec((1, 1, tq, MLA_V), lambda b, h, i, j: (b, h, jnp.minimum(i, j), 0)),
        ],
        out_specs=pl.BlockSpec((tq, MLA_V), lambda b, h, i, j: (b * nq + i, h)),
        scratch_shapes=[
            pltpu.VMEM((tq, 1), F32),
            pltpu.VMEM((tq, 1), F32),
            pltpu.VMEM((tq, MLA_V), F32),
        ],
        compiler_params=_cparams(("arbitrary", "arbitrary", "arbitrary", "arbitrary")),
        name="mla_attn",
    )(q, k, v)


def _outproj_kernel(ohg_ref, omla_ref, x_ref, gt_ref, sc_ref, sh_ref, og_ref, g2_ref,
                    wo_ref, wrh_ref, wrl_ref, h_out, u_out, lg_out):
    om = omla_ref[...]
    omn = om * lax.rsqrt(jnp.mean(om * om, axis=-1, keepdims=True) + EPS) * og_ref[...]
    y = jnp.dot(ohg_ref[...], wo_ref[:HG_WIDTH, :], preferred_element_type=F32)
    y = y + jnp.dot(omn.astype(BF16), wo_ref[HG_WIDTH:, :], preferred_element_type=F32)
    h1 = x_ref[...] + gt_ref[0] * y
    h_out[...] = h1
    u = h1 * lax.rsqrt(jnp.mean(h1 * h1, axis=-1, keepdims=True) + EPS) * g2_ref[...]
    u = u * (1.0 + sc_ref[0]) + sh_ref[0]
    uh, ul = _split_bf16(u)
    u_out[...] = u
    lg = lax.dot_general(wrh_ref[...], uh, NT_DIMS, preferred_element_type=F32)
    lg = lg + lax.dot_general(wrl_ref[...], uh, NT_DIMS, preferred_element_type=F32)
    lg = lg + lax.dot_general(wrh_ref[...], ul, NT_DIMS, preferred_element_type=F32)
    lg_out[...] = lg


def _outproj(ohg, omla, x2, mod3, og, g2, wo_bf, wr_hi, wr_lo, seq):
    t, d = x2.shape
    tm = min(256, seq)
    per_b = seq // tm
    full = lambda shape: pl.BlockSpec(shape, lambda i: tuple(0 for _ in shape))
    modspec = lambda col: pl.BlockSpec((1, 1, d), lambda i: (i // per_b, 0, col))
    return pl.pallas_call(
        _outproj_kernel,
        out_shape=(
            jax.ShapeDtypeStruct((t, d), F32),
            jax.ShapeDtypeStruct((t, d), F32),
            jax.ShapeDtypeStruct((N_EXPERTS, t), F32),
        ),
        grid=(t // tm,),
        in_specs=[
            pl.BlockSpec((tm, HG_WIDTH), lambda i: (i, 0)),
            pl.BlockSpec((tm, MLA_WIDTH), lambda i: (i, 0)),
            pl.BlockSpec((tm, d), lambda i: (i, 0)),
            modspec(2),
            modspec(4),
            modspec(3),
            full((1, MLA_WIDTH)),
            full((1, d)),
            full(wo_bf.shape),
            full(wr_hi.shape),
            full(wr_lo.shape),
        ],
        out_specs=(
            pl.BlockSpec((tm, d), lambda i: (i, 0)),
            pl.BlockSpec((tm, d), lambda i: (i, 0)),
            pl.BlockSpec((N_EXPERTS, tm), lambda i: (0, i)),
        ),
        compiler_params=_cparams(("arbitrary",)),
        name="outproj",
    )(ohg, omla, x2, mod3, mod3, mod3, og, g2, wo_bf, wr_hi, wr_lo)


def _route_kernel(lg_ref, bias_ref, tri_ref, ltri_ref, idx_out, w_out, dest_out, info_out,
                  carry, base, *, tt, n_info):
    phase = pl.program_id(0)
    step = pl.program_id(1)

    @pl.when((phase == 0) & (step == 0))
    def _():
        carry[...] = jnp.zeros_like(carry)

    @pl.when((phase == 1) & (step == 0))
    def _():
        cnt = carry[...]
        padded = jnp.floor((cnt + (MOE_BLK - 1)) / MOE_BLK) * MOE_BLK
        hi, lo = _split_bf16(padded)
        ps = jnp.dot(ltri_ref[...], jnp.concatenate([hi, lo], axis=1), preferred_element_type=F32)
        pad_start = ps[:, :128] + ps[:, 128:]
        base[...] = pad_start
        carry[...] = jnp.zeros_like(carry)
        pad_end = pad_start + padded
        reps = n_info // 128
        pe = jnp.concatenate([pad_end] * reps, axis=1)
        starts = (lax.broadcasted_iota(jnp.int32, (N_EXPERTS, n_info), 1) * MOE_BLK).astype(F32)
        blk_e = jnp.sum(jnp.where(pe <= starts, 1.0, 0.0), axis=0, keepdims=True)
        blk_e = jnp.minimum(blk_e, N_EXPERTS - 1.0)
        n_used = pe[N_EXPERTS - 1:N_EXPERTS, :] / MOE_BLK
        info_out[...] = jnp.concatenate(
            [blk_e, n_used, jnp.zeros((6, n_info), F32)], axis=0).astype(jnp.int32)

    ne, ng, eg = N_EXPERTS, N_GROUPS, EXPERTS_PER_GROUP
    neg = -jnp.inf
    sc = _sigmoid(lg_ref[...])
    sel = sc + bias_ref[...]
    sel3 = sel.reshape(ng, eg, tt)
    sub = lax.broadcasted_iota(jnp.int32, (ng, eg, tt), 1)
    m1 = jnp.max(sel3, axis=1, keepdims=True)
    first = jnp.min(jnp.where(sel3 == m1, sub, eg), axis=1, keepdims=True)
    m2 = jnp.max(jnp.where(sub == first, neg, sel3), axis=1, keepdims=True)
    gs = (m1 + m2).reshape(ng, tt)

    gio = lax.broadcasted_iota(jnp.int32, (ng, tt), 0)
    gsel = jnp.zeros((ng, tt), F32)
    for _ in range(TOPK_GROUPS):
        m = jnp.max(gs, axis=0, keepdims=True)
        gi = jnp.min(jnp.where(gs == m, gio, ng), axis=0, keepdims=True)
        hit = gio == gi
        gsel = jnp.where(hit, 1.0, gsel)
        gs = jnp.where(hit, neg, gs)
    gfull = jnp.broadcast_to(gsel.reshape(ng, 1, tt), (ng, eg, tt)).reshape(ne, tt)
    selm = jnp.where(gfull > 0.0, sel, neg)

    eio = lax.broadcasted_iota(jnp.int32, (ne, tt), 0)
    chosen = jnp.zeros((ne, tt), F32)
    idxs, ws = [], []
    for _ in range(TOP_K):
        m = jnp.max(selm, axis=0, keepdims=True)
        ei = jnp.min(jnp.where(selm == m, eio, ne), axis=0, keepdims=True)
        hit = eio == ei
        ws.append(jnp.sum(jnp.where(hit, sc, 0.0), axis=0, keepdims=True))
        idxs.append(ei)
        chosen = jnp.where(hit, 1.0, chosen)
        selm = jnp.where(hit, neg, selm)
    wsum = ws[0]
    for k in range(1, TOP_K):
        wsum = wsum + ws[k]

    tot = jnp.sum(chosen, axis=1, keepdims=True)

    @pl.when(phase == 1)
    def _():
        off = base[...][:, :1] + carry[...][:, :1]
        excl = jnp.dot(chosen.astype(BF16), tri_ref[...], preferred_element_type=F32) + off
        for k in range(TOP_K):
            idx_out[k:k + 1, :] = idxs[k]
            w_out[k:k + 1, :] = ws[k] / wsum * ROUTED_SCALE
            dk = jnp.sum(jnp.where(eio == idxs[k], excl, 0.0), axis=0, keepdims=True)
            dest_out[k:k + 1, :] = dk.astype(jnp.int32)

    carry[...] = carry[...] + tot


def _route(logits_t, bias_col, n_blk):
    ne, t = logits_t.shape
    tt = min(512, t)
    n_info = -(-n_blk // 128) * 128
    tri = jnp.asarray(np.triu(np.ones((tt, tt), np.float32), 1), BF16)
    ltri = jnp.asarray(np.tril(np.ones((ne, ne), np.float32), -1), BF16)
    return pl.pallas_call(
        functools.partial(_route_kernel, tt=tt, n_info=n_info),
        out_shape=(
            jax.ShapeDtypeStruct((TOP_K, t), jnp.int32),
            jax.ShapeDtypeStruct((TOP_K, t), F32),
            jax.ShapeDtypeStruct((TOP_K, t), jnp.int32),
            jax.ShapeDtypeStruct((8, n_info), jnp.int32),
        ),
        grid=(2, t // tt),
        in_specs=[
            pl.BlockSpec((ne, tt), lambda p, i: (0, i)),
            pl.BlockSpec((ne, 1), lambda p, i: (0, 0)),
            pl.BlockSpec((tt, tt), lambda p, i: (0, 0)),
            pl.BlockSpec((ne, ne), lambda p, i: (0, 0)),
        ],
        out_specs=(
            pl.BlockSpec((TOP_K, tt), lambda p, i: (0, i * p)),
            pl.BlockSpec((TOP_K, tt), lambda p, i: (0, i * p)),
            pl.BlockSpec((TOP_K, tt), lambda p, i: (0, i * p)),
            pl.BlockSpec((8, n_info), lambda p, i: (0, 0)),
        ),
        scratch_shapes=[pltpu.VMEM((ne, 128), F32), pltpu.VMEM((ne, 128), F32)],
        compiler_params=_cparams(("arbitrary", "arbitrary")),
        name="route",
    )(logits_t, bias_col, tri, ltri)


def _gmm_kernel(be_ref, meta_ref, cur_ref, nxt_ref, u_hbm, wg_ref, wu_ref, wd_ref, y_ref,
                xbuf, sem, wg_s, wu_s, wd_s):
    i = pl.program_id(0)
    n_used = meta_ref[0]
    slot = i % 2

    def issue(idx_ref, dst_slot):
        def body(r, carry):
            tok = idx_ref[0, 0, r]
            pltpu.make_async_copy(u_hbm.at[pl.ds(tok, 1)], xbuf.at[dst_slot, pl.ds(r, 1)],
                                  sem.at[dst_slot]).start()
            return carry
        lax.fori_loop(0, MOE_BLK, body, 0, unroll=8)

    @pl.when(i == 0)
    def _():
        issue(cur_ref, 0)

    @pl.when(i + 1 < n_used)
    def _():
        issue(nxt_ref, 1 - slot)

    prev = be_ref[jnp.maximum(i - 1, 0)]
    fresh = (i == 0) | (be_ref[i] != prev)

    @pl.when(fresh & (i < n_used))
    def _():
        wg_s[...] = wg_ref[...].astype(BF16)
        wu_s[...] = wu_ref[...].astype(BF16)
        wd_s[...] = wd_ref[...].astype(BF16)

    @pl.when(i < n_used)
    def _():
        pltpu.make_async_copy(u_hbm.at[pl.ds(0, MOE_BLK)], xbuf.at[slot], sem.at[slot]).wait()
        xb = xbuf[slot].astype(BF16)
        a = jnp.dot(xb, wg_s[...], preferred_element_type=F32)
        u = jnp.dot(xb, wu_s[...], preferred_element_type=F32)
        hid = (a * _sigmoid(a) * u).astype(BF16)
        y_ref[...] = jnp.dot(hid, wd_s[...], preferred_element_type=F32)

    @pl.when(i >= n_used)
    def _():
        y_ref[...] = jnp.zeros_like(y_ref)


def _gmm(blk_expert, meta, row_tok3, u2, w_gate, w_up, w_down):
    n_blk = row_tok3.shape[0]
    d = u2.shape[1]
    de = w_gate.shape[-1]
    last = lambda i, meta: jnp.minimum(i, meta[0] - 1)
    wsel = lambda i, be, meta: (be[last(i, meta)], 0, 0)
    return pl.pallas_call(
        _gmm_kernel,
        out_shape=jax.ShapeDtypeStruct((n_blk * MOE_BLK, d), F32),
        grid_spec=pltpu.PrefetchScalarGridSpec(
            num_scalar_prefetch=2,
            grid=(n_blk,),
            in_specs=[
                pl.BlockSpec((1, 1, MOE_BLK), lambda i, be, meta: (last(i, meta), 0, 0),
                             memory_space=pltpu.SMEM),
                pl.BlockSpec((1, 1, MOE_BLK), lambda i, be, meta: (last(i + 1, meta), 0, 0),
                             memory_space=pltpu.SMEM),
                pl.BlockSpec(memory_space=pl.ANY),
                pl.BlockSpec((None, d, de), wsel),
                pl.BlockSpec((None, d, de), wsel),
                pl.BlockSpec((None, de, d), wsel),
            ],
            out_specs=pl.BlockSpec((MOE_BLK, d), lambda i, be, meta: (i, 0)),
            scratch_shapes=[
                pltpu.VMEM((2, MOE_BLK, d), F32),
                pltpu.SemaphoreType.DMA((2,)),
                pltpu.VMEM((d, de), BF16),
                pltpu.VMEM((d, de), BF16),
                pltpu.VMEM((de, d), BF16),
            ],
        ),
        compiler_params=_cparams(("arbitrary",)),
        name="moe_gmm",
    )(blk_expert, meta, row_tok3, row_tok3, u2, w_gate, w_up, w_down)


def _final_kernel(cur_ref, nxt_ref, u_ref, h_ref, gt_ref, wt_ref, wsg_ref, wsu_ref, wsd_ref, y_hbm,
                  o_ref, ybuf, sem, *, tm):
    i = pl.program_id(0)
    n = pl.num_programs(0)
    slot = i % 2
    n_rows = TOP_K * tm

    def issue(idx_ref, dst_slot):
        for k in range(TOP_K):
            def body(r, carry):
                src = idx_ref[0, k, r]
                pltpu.make_async_copy(y_hbm.at[pl.ds(src, 1)],
                                      ybuf.at[dst_slot, pl.ds(k * tm + r, 1)],
                                      sem.at[dst_slot]).start()
                return carry
            lax.fori_loop(0, tm, body, 0, unroll=8)

    @pl.when(i == 0)
    def _():
        issue(cur_ref, 0)

    @pl.when(i + 1 < n)
    def _():
        issue(nxt_ref, 1 - slot)

    ub = u_ref[...].astype(BF16)
    a = jnp.dot(ub, wsg_ref[...], preferred_element_type=F32)
    b = jnp.dot(ub, wsu_ref[...], preferred_element_type=F32)
    hid = (a * _sigmoid(a) * b).astype(BF16)
    acc = jnp.dot(hid, wsd_ref[...], preferred_element_type=F32)

    pltpu.make_async_copy(y_hbm.at[pl.ds(0, n_rows)], ybuf.at[slot], sem.at[slot]).wait()
    wt = wt_ref[...]
    for k in range(TOP_K):
        acc = acc + ybuf[slot, k * tm:(k + 1) * tm, :] * wt[:, k:k + 1]
    o_ref[...] = h_ref[...] + gt_ref[0] * acc


def _final(dest3, u2, h1, mod3, wt, wsg, wsu, wsd, y, seq):
    t, d = h1.shape
    tm = dest3.shape[2]
    n = t // tm
    per_b = seq // tm
    full = lambda shape: pl.BlockSpec(shape, lambda i: tuple(0 for _ in shape))
    return pl.pallas_call(
        functools.partial(_final_kernel, tm=tm),
        out_shape=jax.ShapeDtypeStruct((t, d), F32),
        grid=(n,),
        in_specs=[
            pl.BlockSpec((1, TOP_K, tm), lambda i: (i, 0, 0), memory_space=pltpu.SMEM),
            pl.BlockSpec((1, TOP_K, tm), lambda i: (jnp.minimum(i + 1, n - 1), 0, 0),
                         memory_space=pltpu.SMEM),
            pl.BlockSpec((tm, d), lambda i: (i, 0)),
            pl.BlockSpec((tm, d), lambda i: (i, 0)),
            pl.BlockSpec((1, 1, d), lambda i: (i // per_b, 0, 5)),
            pl.BlockSpec((tm, TOP_K), lambda i: (i, 0)),
            full(wsg.shape),
            full(wsu.shape),
            full(wsd.shape),
            pl.BlockSpec(memory_space=pl.ANY),
        ],
        out_specs=pl.BlockSpec((tm, d), lambda i: (i, 0)),
        scratch_shapes=[
            pltpu.VMEM((2, TOP_K * tm, d), F32),
            pltpu.SemaphoreType.DMA((2,)),
        ],
        compiler_params=_cparams(("arbitrary",)),
        name="final",
    )(dest3, dest3, u2, h1, mod3, wt, wsg, wsu, wsd, y)


def _layer(h2, mod3, positions, lb, p, batch, seq):
    t, d = h2.shape

    w_in_bf = jnp.pad(p["w_in"], ((0, 0), (0, IN_PAD - IN_COLS))).astype(BF16)
    proj = _inproj(h2, p["norm_mix_g"][None, :], mod3, w_in_bf, seq)

    o_hg = _hgrn(proj, lb[None, :], p["hg_out_g"][None, :], batch, seq)

    wuq = p["mla_w_uq"].reshape(MLA_Q_RANK, MLA_HEADS, MLA_QK)
    wuq_p = jnp.pad(wuq, ((0, 0), (0, 0), (0, MLA_QK_PAD - MLA_QK))).reshape(
        MLA_Q_RANK, MLA_HEADS * MLA_QK_PAD).astype(BF16)
    wukv = p["mla_w_ukv"].astype(BF16)
    gq = jnp.pad(p["mla_q_norm_g"], (0, MLA_QK_PAD - MLA_QK))[None, :]
    gkn = p["mla_k_norm_g"][None, :MLA_NOPE]
    gkr = jnp.pad(p["mla_k_norm_g"][MLA_NOPE:], (0, 128 - MLA_ROPE))[None, :]
    half = MLA_ROPE // 2
    inv_freq = ROPE_THETA ** (-jnp.arange(half, dtype=F32) / half)
    zeros64 = jnp.zeros((128 - MLA_ROPE,), F32)
    invf = jnp.concatenate([inv_freq, inv_freq, zeros64])[None, :]
    sgn = jnp.concatenate([-jnp.ones((half,), F32), jnp.ones((half,), F32), zeros64])[None, :]
    cm = jnp.concatenate([jnp.ones((MLA_ROPE,), F32), zeros64])[None, :]
    q, k, v = _mla_prep(proj, positions.reshape(t, 1), p["mla_q_a_g"][None, :],
                        p["mla_kv_a_g"][None, :], wuq_p, wukv, gq, gkn, gkr, invf, sgn, cm,
                        batch, seq)
    o_mla = _attention(q, k, v)

    wr_t = p["w_router"].T
    wr_hi = wr_t.astype(BF16)
    wr_lo = (wr_t - wr_hi.astype(F32)).astype(BF16)
    h1, u2, logits_t = _outproj(o_hg, o_mla, h2, mod3, p["mla_out_g"][None, :],
                                p["norm_ffn_g"][None, :], p["w_out"].astype(BF16),
                                wr_hi, wr_lo, seq)

    n_blk = (t * TOP_K) // MOE_BLK + N_EXPERTS
    _, top_w, dest, info = _route(logits_t, p["router_bias"][:, None], n_blk)
    blk_expert = info[0, :n_blk]
    meta = info[1, :1]

    tok = jnp.broadcast_to(jnp.arange(t, dtype=jnp.int32)[None, :], (TOP_K, t))
    row_tok = jnp.zeros((n_blk * MOE_BLK,), jnp.int32).at[dest.reshape(-1)].set(tok.reshape(-1))

    y = _gmm(blk_expert, meta, row_tok.reshape(n_blk, 1, MOE_BLK), u2, p["w_gate"], p["w_up"],
             p["w_down"])

    tm = min(128, seq)
    dest3 = dest.reshape(TOP_K, t // tm, tm).transpose(1, 0, 2)
    return _final(dest3, u2, h1, mod3, top_w.T, p["ws_gate"].astype(BF16),
                  p["ws_up"].astype(BF16), p["ws_down"].astype(BF16), y, seq)


def kernel(x, c, positions, w_ada, b_ada, norm_mix_g, norm_ffn_g, w_in, hg_lb_logits, hg_out_g,
           mla_q_a_g, mla_w_uq, mla_kv_a_g, mla_w_ukv, mla_q_norm_g, mla_k_norm_g, mla_out_g,
           w_out, w_router, router_bias, w_gate, w_up, w_down, ws_gate, ws_up, ws_down):
    batch, seq, d = x.shape
    depth = w_ada.shape[0]
    lb_all = jnp.cumsum(jax.nn.softmax(hg_lb_logits.astype(F32), axis=0), axis=0)
    c8 = jnp.pad(c, ((0, 8 - batch), (0, 0)))
    h2 = x.reshape(batch * seq, d)
    for l in range(depth):
        mod = _adaln(c8, w_ada[l], b_ada[l][None, :])[:batch]
        mod3 = mod.reshape(batch, 1, 6 * d)
        p = dict(norm_mix_g=norm_mix_g[l], norm_ffn_g=norm_ffn_g[l], w_in=w_in[l],
                 hg_out_g=hg_out_g[l], mla_q_a_g=mla_q_a_g[l], mla_w_uq=mla_w_uq[l],
                 mla_kv_a_g=mla_kv_a_g[l], mla_w_ukv=mla_w_ukv[l], mla_q_norm_g=mla_q_norm_g[l],
                 mla_k_norm_g=mla_k_norm_g[l], mla_out_g=mla_out_g[l], w_out=w_out[l],
                 w_router=w_router[l], router_bias=router_bias[l], w_gate=w_gate[l],
                 w_up=w_up[l], w_down=w_down[l], ws_gate=ws_gate[l], ws_up=ws_up[l],
                 ws_down=ws_down[l])
        h2 = _layer(h2, mod3, positions, lb_all[l], p, batch, seq)
    return h2.reshape(batch, seq, d)
```

```python
import functools

import numpy as np
import jax
import jax.numpy as jnp
from jax import lax
from jax.experimental import pallas as pl
from jax.experimental.pallas import tpu as pltpu

F32 = jnp.float32
BF16 = jnp.bfloat16

D_MODEL = 2048
EPS = 1e-6

HG_HEADS = 8
HG_DK = 128
HG_DV = 128
HG_WIDTH = HG_HEADS * HG_DV
HG_CHUNK = 128

MLA_HEADS = 8
MLA_Q_RANK = 512
MLA_KV_RANK = 256
MLA_NOPE = 128
MLA_ROPE = 64
MLA_QK = MLA_NOPE + MLA_ROPE
MLA_QK_PAD = 256
MLA_V = 128
MLA_WIDTH = MLA_HEADS * MLA_V
ROPE_THETA = 10000.0

IN_COLS = 4928
IN_PAD = 5120
COL_CQ = 4096
COL_CKV = 4608
COL_KR = 4864

N_EXPERTS = 64
TOP_K = 8
N_GROUPS = 8
TOPK_GROUPS = 4
EXPERTS_PER_GROUP = N_EXPERTS // N_GROUPS
D_EXPERT = 512
ROUTED_SCALE = 2.5
MOE_BLK = 256

VMEM_LIMIT = 56 * 1024 * 1024

NT_DIMS = (((1,), (1,)), ((), ()))
TN_DIMS = (((0,), (0,)), ((), ()))


def _cparams(sem):
    return pltpu.CompilerParams(dimension_semantics=sem, vmem_limit_bytes=VMEM_LIMIT)


def _sigmoid(x):
    return 1.0 / (1.0 + jnp.exp(-x))


def _split_bf16(x):
    hi = x.astype(BF16)
    lo = (x - hi.astype(F32)).astype(BF16)
    return hi, lo


def _adaln_kernel(c_ref, w_ref, b_ref, o_ref):
    c = c_ref[...]
    cond = c * _sigmoid(c)
    hi, lo = _split_bf16(cond)
    lhs = jnp.concatenate([hi, lo], axis=0)
    r = jnp.dot(lhs, w_ref[...].astype(BF16), preferred_element_type=F32)
    o_ref[...] = r[:8] + r[8:] + b_ref[...]


def _adaln(c8, w, b):
    d, n = w.shape
    tn = 1024
    return pl.pallas_call(
        _adaln_kernel,
        out_shape=jax.ShapeDtypeStruct((8, n), F32),
        grid=(n // tn,),
        in_specs=[
            pl.BlockSpec((8, d), lambda j: (0, 0)),
            pl.BlockSpec((d, tn), lambda j: (0, j)),
            pl.BlockSpec((1, tn), lambda j: (0, j)),
        ],
        out_specs=pl.BlockSpec((8, tn), lambda j: (0, j)),
        compiler_params=_cparams(("arbitrary",)),
        name="adaln",
    )(c8, w, b)


def _inproj_kernel(x_ref, g_ref, sc_ref, sh_ref, w_ref, o_ref, u_scr, *, tm):
    @pl.when(pl.program_id(1) == 0)
    def _():
        g = g_ref[...]
        sc = 1.0 + sc_ref[0]
        sh = sh_ref[0]

        def body(r, carry):
            rows = pl.ds(pl.multiple_of(r * 128, 128), 128)
            x = x_ref[rows, :]
            ms = jnp.mean(x * x, axis=-1, keepdims=True)
            u = x * lax.rsqrt(ms + EPS) * g * sc + sh
            u_scr[rows, :] = u.astype(BF16)
            return carry

        lax.fori_loop(0, tm // 128, body, 0)

    o_ref[...] = jnp.dot(u_scr[...], w_ref[...], preferred_element_type=F32)


def _inproj(x2, g, mod3, w_bf, seq):
    t, d = x2.shape
    n = w_bf.shape[1]
    tm = min(1024, seq)
    tn = 1280
    per_b = seq // tm
    return pl.pallas_call(
        functools.partial(_inproj_kernel, tm=tm),
        out_shape=jax.ShapeDtypeStruct((t, n), F32),
        grid=(t // tm, n // tn),
        in_specs=[
            pl.BlockSpec((tm, d), lambda i, j: (i, 0)),
            pl.BlockSpec((1, d), lambda i, j: (0, 0)),
            pl.BlockSpec((1, 1, d), lambda i, j: (i // per_b, 0, 1)),
            pl.BlockSpec((1, 1, d), lambda i, j: (i // per_b, 0, 0)),
            pl.BlockSpec((d, tn), lambda i, j: (0, j)),
        ],
        out_specs=pl.BlockSpec((tm, tn), lambda i, j: (i, j)),
        scratch_shapes=[pltpu.VMEM((tm, d), BF16)],
        compiler_params=_cparams(("arbitrary", "arbitrary")),
        name="inproj",
    )(x2, g, mod3, mod3, w_bf)


def _hgrn_consts():
    c = HG_CHUNK
    t = np.arange(c)[:, None]
    j = np.arange(c)[None, :]
    tril = (j <= t).astype(np.float32)
    mats = [tril]
    masks = [np.eye(c, dtype=np.float32)]
    s = c // 2
    while s >= 1:
        ref = (t // (2 * s)) * (2 * s) + s - 1
        mats.append(tril - (j <= ref).astype(np.float32))
        masks.append((((t // s) % 2 == 1) & ((j // s) == (t // s) - 1)).astype(np.float32))
        s //= 2
    mats.append((j > t).astype(np.float32))
    return np.concatenate(mats, 0), np.stack(masks, 0)


def _hgrn_kernel(q_ref, f_ref, i_ref, g_ref, lb_ref, og_ref, w_ref, m_ref, o_ref, st_scr, *, seq):
    c = HG_CHUNK
    n_lvl = m_ref.shape[0]
    st_scr[...] = jnp.zeros_like(st_scr)
    lb = lb_ref[...]
    og = og_ref[...]

    def chunk(ci, carry):
        rows = pl.ds(pl.multiple_of(ci * c, c), c)
        q = q_ref[rows, :]
        v = i_ref[rows, :]
        g = g_ref[rows, :]
        f = lb + (1.0 - lb) * _sigmoid(f_ref[rows, :])
        lf = jnp.log(f)
        kk = 1.0 - f
        hi, lo = _split_bf16(lf)
        xs = jnp.dot(w_ref[...], jnp.concatenate([hi, lo], axis=1), preferred_element_type=F32)
        xs = xs[:, :c] + xs[:, c:]
        b = xs[0:c]
        bl = xs[n_lvl * c:(n_lvl + 1) * c]
        st = st_scr[...]
        vb = v.astype(BF16)
        o = lax.dot_general((q * jnp.exp(b)).astype(BF16), st.astype(BF16), NT_DIMS,
                            preferred_element_type=F32)
        attn = m_ref[0] * lax.dot_general(q.astype(BF16), kk.astype(BF16), NT_DIMS,
                                          preferred_element_type=F32)
        for l in range(1, n_lvl):
            e = jnp.exp(-jnp.abs(xs[l * c:(l + 1) * c]))
            attn = attn + m_ref[l] * lax.dot_general(
                (q * e).astype(BF16), (kk * e).astype(BF16), NT_DIMS, preferred_element_type=F32)
        o = o + jnp.dot(attn.astype(BF16), vb, preferred_element_type=F32)
        khat = (kk * jnp.exp(bl)).astype(BF16)
        st_scr[...] = st * jnp.exp(b[c - 1:c, :]) + lax.dot_general(
            vb, khat, TN_DIMS, preferred_element_type=F32)
        ms = jnp.mean(o * o, axis=-1, keepdims=True)
        y = o * lax.rsqrt(ms + EPS) * og * (g * _sigmoid(g))
        o_ref[rows, :] = y.astype(BF16)
        return carry

    lax.fori_loop(0, seq // c, chunk, 0, unroll=4)


def _hgrn(proj, lb, og, batch, seq):
    t = proj.shape[0]
    wm, mm = _hgrn_consts()
    wm = jnp.asarray(wm, BF16)
    mm = jnp.asarray(mm, F32)
    h = HG_HEADS
    return pl.pallas_call(
        functools.partial(_hgrn_kernel, seq=seq),
        out_shape=jax.ShapeDtypeStruct((t, HG_WIDTH), BF16),
        grid=(batch, h),
        in_specs=[
            pl.BlockSpec((seq, 128), lambda b, i: (b, i)),
            pl.BlockSpec((seq, 128), lambda b, i: (b, h + i)),
            pl.BlockSpec((seq, 128), lambda b, i: (b, 2 * h + i)),
            pl.BlockSpec((seq, 128), lambda b, i: (b, 3 * h + i)),
            pl.BlockSpec((1, 128), lambda b, i: (0, i)),
            pl.BlockSpec((1, 128), lambda b, i: (0, 0)),
            pl.BlockSpec(wm.shape, lambda b, i: (0, 0)),
            pl.BlockSpec(mm.shape, lambda b, i: (0, 0, 0)),
        ],
        out_specs=pl.BlockSpec((seq, 128), lambda b, i: (b, i)),
        scratch_shapes=[pltpu.VMEM((HG_DV, HG_DK), F32)],
        compiler_params=_cparams(("arbitrary", "arbitrary")),
        name="hgrn2",
    )(proj, proj, proj, proj, lb, og, wm, mm)


def _mla_prep_kernel(cq_ref, ckv_ref, kr_ref, pos_ref, qag_ref, kvag_ref, wuq_ref, wukv_ref,
                     gq_ref, gkn_ref, gkr_ref, invf_ref, sgn_ref, cm_ref,
                     q_out, k_out, v_out):
    cq = cq_ref[...]
    cqn = cq * lax.rsqrt(jnp.mean(cq * cq, axis=-1, keepdims=True) + EPS) * qag_ref[...]
    qf = jnp.dot(cqn.astype(BF16), wuq_ref[...], preferred_element_type=F32)
    ckv = ckv_ref[...]
    ckvn = ckv * lax.rsqrt(jnp.mean(ckv * ckv, axis=-1, keepdims=True) + EPS) * kvag_ref[...]
    kvf = jnp.dot(ckvn.astype(BF16), wukv_ref[...], preferred_element_type=F32)

    ang = pos_ref[...].astype(F32) * invf_ref[...]
    cos_t = jnp.cos(ang) * cm_ref[...]
    sin_t = jnp.sin(ang) * sgn_ref[...]

    def rope(r):
        return r * cos_t + (pltpu.roll(r, 96, 1) + pltpu.roll(r, 32, 1)) * sin_t

    kr = kr_ref[...]
    kr_ss = jnp.sum(kr * kr, axis=-1, keepdims=True)
    krr = rope(kr * gkr_ref[...])
    gq = gq_ref[...]
    gkn = gkn_ref[...]
    scale = MLA_QK ** -0.5
    for h in range(MLA_HEADS):
        qh = qf[:, h * MLA_QK_PAD:(h + 1) * MLA_QK_PAD]
        inv = lax.rsqrt(jnp.sum(qh * qh, axis=-1, keepdims=True) / MLA_QK + EPS)
        qn = qh * inv * gq
        qo = jnp.concatenate([qn[:, :MLA_NOPE], rope(qn[:, MLA_NOPE:])], axis=-1) * scale
        q_out[0, h] = qo.astype(BF16)
        kn = kvf[:, h * 256:h * 256 + MLA_NOPE]
        inv = lax.rsqrt((jnp.sum(kn * kn, axis=-1, keepdims=True) + kr_ss) / MLA_QK + EPS)
        k_out[0, h] = jnp.concatenate([kn * inv * gkn, krr * inv], axis=-1).astype(BF16)
        v_out[0, h] = kvf[:, h * 256 + MLA_NOPE:(h + 1) * 256].astype(BF16)


def _mla_prep(proj, pos2, qag, kvag, wuq_p, wukv, gq, gkn, gkr, invf, sgn, cm, batch, seq):
    tm = min(256, seq)
    per_b = seq // tm
    hh = MLA_HEADS
    full = lambda shape: pl.BlockSpec(shape, lambda b, i: tuple(0 for _ in shape))
    return pl.pallas_call(
        _mla_prep_kernel,
        out_shape=(
            jax.ShapeDtypeStruct((batch, hh, seq, MLA_QK_PAD), BF16),
            jax.ShapeDtypeStruct((batch, hh, seq, MLA_QK_PAD), BF16),
            jax.ShapeDtypeStruct((batch, hh, seq, MLA_V), BF16),
        ),
        grid=(batch, per_b),
        in_specs=[
            pl.BlockSpec((tm, MLA_Q_RANK), lambda b, i: (b * per_b + i, COL_CQ // MLA_Q_RANK)),
            pl.BlockSpec((tm, MLA_KV_RANK), lambda b, i: (b * per_b + i, COL_CKV // MLA_KV_RANK)),
            pl.BlockSpec((tm, 128), lambda b, i: (b * per_b + i, COL_KR // 128)),
            pl.BlockSpec((tm, 1), lambda b, i: (b * per_b + i, 0)),
            full((1, MLA_Q_RANK)),
            full((1, MLA_KV_RANK)),
            full(wuq_p.shape),
            full(wukv.shape),
            full((1, MLA_QK_PAD)),
            full((1, 128)),
            full((1, 128)),
            full((1, 128)),
            full((1, 128)),
            full((1, 128)),
        ],
        out_specs=(
            pl.BlockSpec((1, hh, tm, MLA_QK_PAD), lambda b, i: (b, 0, i, 0)),
            pl.BlockSpec((1, hh, tm, MLA_QK_PAD), lambda b, i: (b, 0, i, 0)),
            pl.BlockSpec((1, hh, tm, MLA_V), lambda b, i: (b, 0, i, 0)),
        ),
        compiler_params=_cparams(("arbitrary", "arbitrary")),
        name="mla_prep",
    )(proj, proj, proj, pos2, qag, kvag, wuq_p, wukv, gq, gkn, gkr, invf, sgn, cm)


def _attn_kernel(q_ref, k_ref, v_ref, o_ref, *, tq, seq):
    row = lax.broadcasted_iota(jnp.int32, (tq, tq), 0)
    col = lax.broadcasted_iota(jnp.int32, (tq, tq), 1)
    causal = col <= row
    for i in range(seq // tq):
        lo = i * tq
        q = q_ref[0, 0, lo:lo + tq, :]
        sd = lax.dot_general(q, k_ref[0, 0, lo:lo + tq, :], NT_DIMS, preferred_element_type=F32)
        sd = jnp.where(causal, sd, -jnp.inf)
        m = jnp.max(sd, axis=-1, keepdims=True)
        if i > 0:
            so = lax.dot_general(q, k_ref[0, 0, :lo, :], NT_DIMS, preferred_element_type=F32)
            m = jnp.maximum(m, jnp.max(so, axis=-1, keepdims=True))
        pd = jnp.exp(sd - m)
        l = jnp.sum(pd, axis=-1, keepdims=True)
        o = jnp.dot(pd.astype(BF16), v_ref[0, 0, lo:lo + tq, :], preferred_element_type=F32)
        if i > 0:
            po = jnp.exp(so - m)
            l = l + jnp.sum(po, axis=-1, keepdims=True)
            o = o + jnp.dot(po.astype(BF16), v_ref[0, 0, :lo, :], preferred_element_type=F32)
        o_ref[lo:lo + tq, :] = o / l


def _attention(q, k, v):
    batch, hh, seq, dq = q.shape
    tq = min(256, seq)
    return pl.pallas_call(
        functools.partial(_attn_kernel, tq=tq, seq=seq),
        out_shape=jax.ShapeDtypeStruct((batch * seq, hh * MLA_V), F32),
        grid=(batch, hh),
        in_specs=[
            pl.BlockSpec((1, 1, seq, dq), lambda b, h: (b, h, 0, 0)),
            pl.BlockSpec((1, 1, seq, dq), lambda b, h: (b, h, 0, 0)),
            pl.BlockSpec((1, 1, seq, MLA_V), lambda b, h: (b, h, 0, 0)),
        ],
        out_specs=pl.BlockSpec((seq, MLA_V), lambda b, h: (b, h)),
        compiler_params=_cparams(("arbitrary", "arbitrary")),
        name="mla_attn",
    )(q, k, v)


def _outproj_kernel(ohg_ref, omla_ref, x_ref, gt_ref, sc_ref, sh_ref, og_ref, g2_ref,
                    wo_ref, wrh_ref, wrl_ref, h_out, u_out, lg_out):
    om = omla_ref[...]
    omn = om * lax.rsqrt(jnp.mean(om * om, axis=-1, keepdims=True) + EPS) * og_ref[...]
    y = jnp.dot(ohg_ref[...], wo_ref[:HG_WIDTH, :], preferred_element_type=F32)
    y = y + jnp.dot(omn.astype(BF16), wo_ref[HG_WIDTH:, :], preferred_element_type=F32)
    h1 = x_ref[...] + gt_ref[0] * y
    h_out[...] = h1
    u = h1 * lax.rsqrt(jnp.mean(h1 * h1, axis=-1, keepdims=True) + EPS) * g2_ref[...]
    u = u * (1.0 + sc_ref[0]) + sh_ref[0]
    uh, ul = _split_bf16(u)
    u_out[...] = u
    lg = lax.dot_general(wrh_ref[...], uh, NT_DIMS, preferred_element_type=F32)
    lg = lg + lax.dot_general(wrl_ref[...], uh, NT_DIMS, preferred_element_type=F32)
    lg = lg + lax.dot_general(wrh_ref[...], ul, NT_DIMS, preferred_element_type=F32)
    lg_out[...] = lg


def _outproj(ohg, omla, x2, mod3, og, g2, wo_bf, wr_hi, wr_lo, seq):
    t, d = x2.shape
    tm = min(256, seq)
    per_b = seq // tm
    full = lambda shape: pl.BlockSpec(shape, lambda i: tuple(0 for _ in shape))
    modspec = lambda col: pl.BlockSpec((1, 1, d), lambda i: (i // per_b, 0, col))
    return pl.pallas_call(
        _outproj_kernel,
        out_shape=(
            jax.ShapeDtypeStruct((t, d), F32),
            jax.ShapeDtypeStruct((t, d), F32),
            jax.ShapeDtypeStruct((N_EXPERTS, t), F32),
        ),
        grid=(t // tm,),
        in_specs=[
            pl.BlockSpec((tm, HG_WIDTH), lambda i: (i, 0)),
            pl.BlockSpec((tm, MLA_WIDTH), lambda i: (i, 0)),
            pl.BlockSpec((tm, d), lambda i: (i, 0)),
            modspec(2),
            modspec(4),
            modspec(3),
            full((1, MLA_WIDTH)),
            full((1, d)),
            full(wo_bf.shape),
            full(wr_hi.shape),
            full(wr_lo.shape),
        ],
        out_specs=(
            pl.BlockSpec((tm, d), lambda i: (i, 0)),
            pl.BlockSpec((tm, d), lambda i: (i, 0)),
            pl.BlockSpec((N_EXPERTS, tm), lambda i: (0, i)),
        ),
        compiler_params=_cparams(("arbitrary",)),
        name="outproj",
    )(ohg, omla, x2, mod3, mod3, mod3, og, g2, wo_bf, wr_hi, wr_lo)


def _route_kernel(lg_ref, bias_ref, tri_ref, ltri_ref, idx_out, w_out, dest_out, info_out,
                  carry, base, *, tt, n_info):
    phase = pl.program_id(0)
    step = pl.program_id(1)

    @pl.when((phase == 0) & (step == 0))
    def _():
        carry[...] = jnp.zeros_like(carry)

    @pl.when((phase == 1) & (step == 0))
    def _():
        cnt = carry[...]
        padded = jnp.floor((cnt + (MOE_BLK - 1)) / MOE_BLK) * MOE_BLK
        hi, lo = _split_bf16(padded)
        ps = jnp.dot(ltri_ref[...], jnp.concatenate([hi, lo], axis=1), preferred_element_type=F32)
        pad_start = ps[:, :128] + ps[:, 128:]
        base[...] = pad_start
        carry[...] = jnp.zeros_like(carry)
        pad_end = pad_start + padded
        reps = n_info // 128
        pe = jnp.concatenate([pad_end] * reps, axis=1)
        starts = (lax.broadcasted_iota(jnp.int32, (N_EXPERTS, n_info), 1) * MOE_BLK).astype(F32)
        blk_e = jnp.sum(jnp.where(pe <= starts, 1.0, 0.0), axis=0, keepdims=True)
        blk_e = jnp.minimum(blk_e, N_EXPERTS - 1.0)
        n_used = pe[N_EXPERTS - 1:N_EXPERTS, :] / MOE_BLK
        diag = (lax.broadcasted_iota(jnp.int32, (N_EXPERTS, n_info), 0)
                == lax.broadcasted_iota(jnp.int32, (N_EXPERTS, n_info), 1))
        end_blk = jnp.sum(jnp.where(diag, pe, 0.0), axis=0, keepdims=True) / MOE_BLK
        pd = jnp.concatenate([padded] * reps, axis=1)
        num_blk = jnp.sum(jnp.where(diag, pd, 0.0), axis=0, keepdims=True) / MOE_BLK
        info_out[...] = jnp.concatenate(
            [blk_e, n_used, end_blk, num_blk, jnp.zeros((4, n_info), F32)],
            axis=0).astype(jnp.int32)

    ne, ng, eg = N_EXPERTS, N_GROUPS, EXPERTS_PER_GROUP
    neg = -jnp.inf
    sc = _sigmoid(lg_ref[...])
    sel = sc + bias_ref[...]
    sel3 = sel.reshape(ng, eg, tt)
    sub = lax.broadcasted_iota(jnp.int32, (ng, eg, tt), 1)
    m1 = jnp.max(sel3, axis=1, keepdims=True)
    first = jnp.min(jnp.where(sel3 == m1, sub, eg), axis=1, keepdims=True)
    m2 = jnp.max(jnp.where(sub == first, neg, sel3), axis=1, keepdims=True)
    gs = (m1 + m2).reshape(ng, tt)

    gio = lax.broadcasted_iota(jnp.int32, (ng, tt), 0)
    gsel = jnp.zeros((ng, tt), F32)
    for _ in range(TOPK_GROUPS):
        m = jnp.max(gs, axis=0, keepdims=True)
        gi = jnp.min(jnp.where(gs == m, gio, ng), axis=0, keepdims=True)
        hit = gio == gi
        gsel = jnp.where(hit, 1.0, gsel)
        gs = jnp.where(hit, neg, gs)
    gfull = jnp.broadcast_to(gsel.reshape(ng, 1, tt), (ng, eg, tt)).reshape(ne, tt)
    selm = jnp.where(gfull > 0.0, sel, neg)

    eio = lax.broadcasted_iota(jnp.int32, (ne, tt), 0)
    chosen = jnp.zeros((ne, tt), F32)
    idxs, ws = [], []
    for _ in range(TOP_K):
        m = jnp.max(selm, axis=0, keepdims=True)
        ei = jnp.min(jnp.where(selm == m, eio, ne), axis=0, keepdims=True)
        hit = eio == ei
        ws.append(jnp.sum(jnp.where(hit, sc, 0.0), axis=0, keepdims=True))
        idxs.append(ei)
        chosen = jnp.where(hit, 1.0, chosen)
        selm = jnp.where(hit, neg, selm)
    wsum = ws[0]
    for k in range(1, TOP_K):
        wsum = wsum + ws[k]

    tot = jnp.sum(chosen, axis=1, keepdims=True)

    @pl.when(phase == 1)
    def _():
        off = base[...][:, :1] + carry[...][:, :1]
        excl = jnp.dot(chosen.astype(BF16), tri_ref[...], preferred_element_type=F32) + off
        for k in range(TOP_K):
            idx_out[k:k + 1, :] = idxs[k]
            w_out[k:k + 1, :] = ws[k] / wsum * ROUTED_SCALE
            dk = jnp.sum(jnp.where(eio == idxs[k], excl, 0.0), axis=0, keepdims=True)
            dest_out[k:k + 1, :] = dk.astype(jnp.int32)

    carry[...] = carry[...] + tot


def _route(logits_t, bias_col, n_blk):
    ne, t = logits_t.shape
    tt = min(512, t)
    n_info = -(-n_blk // 128) * 128
    tri = jnp.asarray(np.triu(np.ones((tt, tt), np.float32), 1), BF16)
    ltri = jnp.asarray(np.tril(np.ones((ne, ne), np.float32), -1), BF16)
    return pl.pallas_call(
        functools.partial(_route_kernel, tt=tt, n_info=n_info),
        out_shape=(
            jax.ShapeDtypeStruct((TOP_K, t), jnp.int32),
            jax.ShapeDtypeStruct((TOP_K, t), F32),
            jax.ShapeDtypeStruct((TOP_K, t), jnp.int32),
            jax.ShapeDtypeStruct((8, n_info), jnp.int32),
        ),
        grid=(2, t // tt),
        in_specs=[
            pl.BlockSpec((ne, tt), lambda p, i: (0, i)),
            pl.BlockSpec((ne, 1), lambda p, i: (0, 0)),
            pl.BlockSpec((tt, tt), lambda p, i: (0, 0)),
            pl.BlockSpec((ne, ne), lambda p, i: (0, 0)),
        ],
        out_specs=(
            pl.BlockSpec((TOP_K, tt), lambda p, i: (0, i * p)),
            pl.BlockSpec((TOP_K, tt), lambda p, i: (0, i * p)),
            pl.BlockSpec((TOP_K, tt), lambda p, i: (0, i * p)),
            pl.BlockSpec((8, n_info), lambda p, i: (0, 0)),
        ),
        scratch_shapes=[pltpu.VMEM((ne, 128), F32), pltpu.VMEM((ne, 128), F32)],
        compiler_params=_cparams(("arbitrary", "arbitrary")),
        name="route",
    )(logits_t, bias_col, tri, ltri)


def _gmm_kernel(be_ref, meta_ref, cur_ref, nxt_ref, u_hbm, wg_ref, wu_ref, wd_ref, y_ref,
                xbuf, sem, wg_s, wu_s, wd_s):
    i = pl.program_id(0)
    n_used = meta_ref[0]
    slot = i % 2

    def issue(idx_ref, dst_slot):
        def body(r, carry):
            tok = idx_ref[0, 0, r]
            pltpu.make_async_copy(u_hbm.at[pl.ds(tok, 1)], xbuf.at[dst_slot, pl.ds(r, 1)],
                                  sem.at[dst_slot]).start()
            return carry
        lax.fori_loop(0, MOE_BLK, body, 0, unroll=8)

    @pl.when(i == 0)
    def _():
        issue(cur_ref, 0)

    @pl.when(i + 1 < n_used)
    def _():
        issue(nxt_ref, 1 - slot)

    prev = be_ref[jnp.maximum(i - 1, 0)]
    fresh = (i == 0) | (be_ref[i] != prev)

    @pl.when(fresh & (i < n_used))
    def _():
        wg_s[...] = wg_ref[...].astype(BF16)
        wu_s[...] = wu_ref[...].astype(BF16)
        wd_s[...] = wd_ref[...].astype(BF16)

    @pl.when(i < n_used)
    def _():
        pltpu.make_async_copy(u_hbm.at[pl.ds(0, MOE_BLK)], xbuf.at[slot], sem.at[slot]).wait()
        xb = xbuf[slot].astype(BF16)
        a = jnp.dot(xb, wg_s[...], preferred_element_type=F32)
        u = jnp.dot(xb, wu_s[...], preferred_element_type=F32)
        hid = (a * _sigmoid(a) * u).astype(BF16)
        y_ref[...] = jnp.dot(hid, wd_s[...], preferred_element_type=F32)

    @pl.when(i >= n_used)
    def _():
        y_ref[...] = jnp.zeros_like(y_ref)


def _gmm(blk_expert, meta, row_tok3, u2, w_gate, w_up, w_down):
    n_blk = row_tok3.shape[0]
    d = u2.shape[1]
    de = w_gate.shape[-1]
    last = lambda i, meta: jnp.minimum(i, meta[0] - 1)
    wsel = lambda i, be, meta: (be[last(i, meta)], 0, 0)
    return pl.pallas_call(
        _gmm_kernel,
        out_shape=jax.ShapeDtypeStruct((n_blk * MOE_BLK, d), F32),
        grid_spec=pltpu.PrefetchScalarGridSpec(
            num_scalar_prefetch=2,
            grid=(n_blk,),
            in_specs=[
                pl.BlockSpec((1, 1, MOE_BLK), lambda i, be, meta: (last(i, meta), 0, 0),
                             memory_space=pltpu.SMEM),
                pl.BlockSpec((1, 1, MOE_BLK), lambda i, be, meta: (last(i + 1, meta), 0, 0),
                             memory_space=pltpu.SMEM),
                pl.BlockSpec(memory_space=pl.ANY),
                pl.BlockSpec((None, d, de), wsel),
                pl.BlockSpec((None, d, de), wsel),
                pl.BlockSpec((None, de, d), wsel),
            ],
            out_specs=pl.BlockSpec((MOE_BLK, d), lambda i, be, meta: (i, 0)),
            scratch_shapes=[
                pltpu.VMEM((2, MOE_BLK, d), F32),
                pltpu.SemaphoreType.DMA((2,)),
                pltpu.VMEM((d, de), BF16),
                pltpu.VMEM((d, de), BF16),
                pltpu.VMEM((de, d), BF16),
            ],
        ),
        compiler_params=_cparams(("arbitrary",)),
        name="moe_gmm",
    )(blk_expert, meta, row_tok3, row_tok3, u2, w_gate, w_up, w_down)


def _final_kernel(cur_ref, nxt_ref, u_ref, h_ref, gt_ref, wt_ref, wsg_ref, wsu_ref, wsd_ref, y_hbm,
                  o_ref, ybuf, sem, *, tm):
    i = pl.program_id(0)
    n = pl.num_programs(0)
    slot = i % 2
    n_rows = TOP_K * tm

    def issue(idx_ref, dst_slot):
        for k in range(TOP_K):
            def body(r, carry):
                src = idx_ref[0, k, r]
                pltpu.make_async_copy(y_hbm.at[pl.ds(src, 1)],
                                      ybuf.at[dst_slot, pl.ds(k * tm + r, 1)],
                                      sem.at[dst_slot]).start()
                return carry
            lax.fori_loop(0, tm, body, 0, unroll=8)

    @pl.when(i == 0)
    def _():
        issue(cur_ref, 0)

    @pl.when(i + 1 < n)
    def _():
        issue(nxt_ref, 1 - slot)

    ub = u_ref[...].astype(BF16)
    a = jnp.dot(ub, wsg_ref[...], preferred_element_type=F32)
    b = jnp.dot(ub, wsu_ref[...], preferred_element_type=F32)
    hid = (a * _sigmoid(a) * b).astype(BF16)
    acc = jnp.dot(hid, wsd_ref[...], preferred_element_type=F32)

    pltpu.make_async_copy(y_hbm.at[pl.ds(0, n_rows)], ybuf.at[slot], sem.at[slot]).wait()
    wt = wt_ref[...]
    for k in range(TOP_K):
        acc = acc + ybuf[slot, k * tm:(k + 1) * tm, :] * wt[:, k:k + 1]
    o_ref[...] = h_ref[...] + gt_ref[0] * acc


def _final(dest3, u2, h1, mod3, wt, wsg, wsu, wsd, y, seq):
    t, d = h1.shape
    tm = dest3.shape[2]
    n = t // tm
    per_b = seq // tm
    full = lambda shape: pl.BlockSpec(shape, lambda i: tuple(0 for _ in shape))
    return pl.pallas_call(
        functools.partial(_final_kernel, tm=tm),
        out_shape=jax.ShapeDtypeStruct((t, d), F32),
        grid=(n,),
        in_specs=[
            pl.BlockSpec((1, TOP_K, tm), lambda i: (i, 0, 0), memory_space=pltpu.SMEM),
            pl.BlockSpec((1, TOP_K, tm), lambda i: (jnp.minimum(i + 1, n - 1), 0, 0),
                         memory_space=pltpu.SMEM),
            pl.BlockSpec((tm, d), lambda i: (i, 0)),
            pl.BlockSpec((tm, d), lambda i: (i, 0)),
            pl.BlockSpec((1, 1, d), lambda i: (i // per_b, 0, 5)),
            pl.BlockSpec((tm, TOP_K), lambda i: (i, 0)),
            full(wsg.shape),
            full(wsu.shape),
            full(wsd.shape),
            pl.BlockSpec(memory_space=pl.ANY),
        ],
        out_specs=pl.BlockSpec((tm, d), lambda i: (i, 0)),
        scratch_shapes=[
            pltpu.VMEM((2, TOP_K * tm, d), F32),
            pltpu.SemaphoreType.DMA((2,)),
        ],
        compiler_params=_cparams(("arbitrary",)),
        name="final",
    )(dest3, dest3, u2, h1, mod3, wt, wsg, wsu, wsd, y)


def _layer(h2, mod3, positions, lb, p, batch, seq):
    t, d = h2.shape

    w_in_bf = jnp.pad(p["w_in"], ((0, 0), (0, IN_PAD - IN_COLS))).astype(BF16)
    proj = _inproj(h2, p["norm_mix_g"][None, :], mod3, w_in_bf, seq)

    o_hg = _hgrn(proj, lb[None, :], p["hg_out_g"][None, :], batch, seq)

    wuq = p["mla_w_uq"].reshape(MLA_Q_RANK, MLA_HEADS, MLA_QK)
    wuq_p = jnp.pad(wuq, ((0, 0), (0, 0), (0, MLA_QK_PAD - MLA_QK))).reshape(
        MLA_Q_RANK, MLA_HEADS * MLA_QK_PAD).astype(BF16)
    wukv = p["mla_w_ukv"].astype(BF16)
    gq = jnp.pad(p["mla_q_norm_g"], (0, MLA_QK_PAD - MLA_QK))[None, :]
    gkn = p["mla_k_norm_g"][None, :MLA_NOPE]
    gkr = jnp.pad(p["mla_k_norm_g"][MLA_NOPE:], (0, 128 - MLA_ROPE))[None, :]
    half = MLA_ROPE // 2
    inv_freq = ROPE_THETA ** (-jnp.arange(half, dtype=F32) / half)
    zeros64 = jnp.zeros((128 - MLA_ROPE,), F32)
    invf = jnp.concatenate([inv_freq, inv_freq, zeros64])[None, :]
    sgn = jnp.concatenate([-jnp.ones((half,), F32), jnp.ones((half,), F32), zeros64])[None, :]
    cm = jnp.concatenate([jnp.ones((MLA_ROPE,), F32), zeros64])[None, :]
    q, k, v = _mla_prep(proj, positions.reshape(t, 1), p["mla_q_a_g"][None, :],
                        p["mla_kv_a_g"][None, :], wuq_p, wukv, gq, gkn, gkr, invf, sgn, cm,
                        batch, seq)
    o_mla = _attention(q, k, v)

    wr_t = p["w_router"].T
    wr_hi = wr_t.astype(BF16)
    wr_lo = (wr_t - wr_hi.astype(F32)).astype(BF16)
    h1, u2, logits_t = _outproj(o_hg, o_mla, h2, mod3, p["mla_out_g"][None, :],
                                p["norm_ffn_g"][None, :], p["w_out"].astype(BF16),
                                wr_hi, wr_lo, seq)

    n_blk = (t * TOP_K) // MOE_BLK + N_EXPERTS
    _, top_w, dest, info = _route(logits_t, p["router_bias"][:, None], n_blk)
    blk_expert = info[0, :n_blk]
    meta = info[1, :1]

    tok = jnp.broadcast_to(jnp.arange(t, dtype=jnp.int32)[None, :], (TOP_K, t))
    row_tok = jnp.zeros((n_blk * MOE_BLK,), jnp.int32).at[dest.reshape(-1)].set(tok.reshape(-1))

    y = _gmm(blk_expert, meta, row_tok.reshape(n_blk, 1, MOE_BLK), u2, p["w_gate"], p["w_up"],
             p["w_down"])

    tm = min(128, seq)
    dest3 = dest.reshape(TOP_K, t // tm, tm).transpose(1, 0, 2)
    return _final(dest3, u2, h1, mod3, top_w.T, p["ws_gate"].astype(BF16),
                  p["ws_up"].astype(BF16), p["ws_down"].astype(BF16), y, seq)


def kernel(x, c, positions, w_ada, b_ada, norm_mix_g, norm_ffn_g, w_in, hg_lb_logits, hg_out_g,
           mla_q_a_g, mla_w_uq, mla_kv_a_g, mla_w_ukv, mla_q_norm_g, mla_k_norm_g, mla_out_g,
           w_out, w_router, router_bias, w_gate, w_up, w_down, ws_gate, ws_up, ws_down):
    batch, seq, d = x.shape
    depth = w_ada.shape[0]
    lb_all = jnp.cumsum(jax.nn.softmax(hg_lb_logits.astype(F32), axis=0), axis=0)
    c8 = jnp.pad(c, ((0, 8 - batch), (0, 0)))
    h2 = x.reshape(batch * seq, d)
    for l in range(depth):
        mod = _adaln(c8, w_ada[l], b_ada[l][None, :])[:batch]
        mod3 = mod.reshape(batch, 1, 6 * d)
        p = dict(norm_mix_g=norm_mix_g[l], norm_ffn_g=norm_ffn_g[l], w_in=w_in[l],
                 hg_out_g=hg_out_g[l], mla_q_a_g=mla_q_a_g[l], mla_w_uq=mla_w_uq[l],
                 mla_kv_a_g=mla_kv_a_g[l], mla_w_ukv=mla_w_ukv[l], mla_q_norm_g=mla_q_norm_g[l],
                 mla_k_norm_g=mla_k_norm_g[l], mla_out_g=mla_out_g[l], w_out=w_out[l],
                 w_router=w_router[l], router_bias=router_bias[l], w_gate=w_gate[l],
                 w_up=w_up[l], w_down=w_down[l], ws_gate=ws_gate[l], ws_up=ws_up[l],
                 ws_down=ws_down[l])
        h2 = _layer(h2, mod3, positions, lb_all[l], p, batch, seq)
    return h2.reshape(batch, seq, d)
```

```python
import functools

import numpy as np
import jax
import jax.numpy as jnp
from jax import lax
from jax.experimental import pallas as pl
from jax.experimental.pallas import tpu as pltpu

F32 = jnp.float32
BF16 = jnp.bfloat16

D_MODEL = 2048
EPS = 1e-6

HG_HEADS = 8
HG_DK = 128
HG_DV = 128
HG_WIDTH = HG_HEADS * HG_DV
HG_CHUNK = 128

MLA_HEADS = 8
MLA_Q_RANK = 512
MLA_KV_RANK = 256
MLA_NOPE = 128
MLA_ROPE = 64
MLA_QK = MLA_NOPE + MLA_ROPE
MLA_QK_PAD = 256
MLA_V = 128
MLA_WIDTH = MLA_HEADS * MLA_V
ROPE_THETA = 10000.0

IN_COLS = 4928
IN_PAD = 5120
COL_CQ = 4096
COL_CKV = 4608
COL_KR = 4864

N_EXPERTS = 64
TOP_K = 8
N_GROUPS = 8
TOPK_GROUPS = 4
EXPERTS_PER_GROUP = N_EXPERTS // N_GROUPS
D_EXPERT = 512
ROUTED_SCALE = 2.5
MOE_BLK = 256

VMEM_LIMIT = 56 * 1024 * 1024

NT_DIMS = (((1,), (1,)), ((), ()))
TN_DIMS = (((0,), (0,)), ((), ()))


def _cparams(sem):
    return pltpu.CompilerParams(dimension_semantics=sem, vmem_limit_bytes=VMEM_LIMIT)


def _sigmoid(x):
    return 1.0 / (1.0 + jnp.exp(-x))


def _split_bf16(x):
    hi = x.astype(BF16)
    lo = (x - hi.astype(F32)).astype(BF16)
    return hi, lo


def _adaln_kernel(c_ref, w_ref, b_ref, o_ref):
    c = c_ref[...]
    cond = c * _sigmoid(c)
    hi, lo = _split_bf16(cond)
    lhs = jnp.concatenate([hi, lo], axis=0)
    r = jnp.dot(lhs, w_ref[...].astype(BF16), preferred_element_type=F32)
    o_ref[...] = r[:8] + r[8:] + b_ref[...]


def _adaln(c8, w, b):
    d, n = w.shape
    tn = 1024
    return pl.pallas_call(
        _adaln_kernel,
        out_shape=jax.ShapeDtypeStruct((8, n), F32),
        grid=(n // tn,),
        in_specs=[
            pl.BlockSpec((8, d), lambda j: (0, 0)),
            pl.BlockSpec((d, tn), lambda j: (0, j)),
            pl.BlockSpec((1, tn), lambda j: (0, j)),
        ],
        out_specs=pl.BlockSpec((8, tn), lambda j: (0, j)),
        compiler_params=_cparams(("arbitrary",)),
        name="adaln",
    )(c8, w, b)


def _inproj_kernel(x_ref, g_ref, sc_ref, sh_ref, w_ref, o_ref, u_scr, *, tm):
    @pl.when(pl.program_id(1) == 0)
    def _():
        g = g_ref[...]
        sc = 1.0 + sc_ref[0]
        sh = sh_ref[0]

        def body(r, carry):
            rows = pl.ds(pl.multiple_of(r * 128, 128), 128)
            x = x_ref[rows, :]
            ms = jnp.mean(x * x, axis=-1, keepdims=True)
            u = x * lax.rsqrt(ms + EPS) * g * sc + sh
            u_scr[rows, :] = u.astype(BF16)
            return carry

        lax.fori_loop(0, tm // 128, body, 0)

    o_ref[...] = jnp.dot(u_scr[...], w_ref[...], preferred_element_type=F32)


def _inproj(x2, g, mod3, w_bf, seq):
    t, d = x2.shape
    n = w_bf.shape[1]
    tm = min(1024, seq)
    tn = 1280
    per_b = seq // tm
    return pl.pallas_call(
        functools.partial(_inproj_kernel, tm=tm),
        out_shape=jax.ShapeDtypeStruct((t, n), F32),
        grid=(t // tm, n // tn),
        in_specs=[
            pl.BlockSpec((tm, d), lambda i, j: (i, 0)),
            pl.BlockSpec((1, d), lambda i, j: (0, 0)),
            pl.BlockSpec((1, 1, d), lambda i, j: (i // per_b, 0, 1)),
            pl.BlockSpec((1, 1, d), lambda i, j: (i // per_b, 0, 0)),
            pl.BlockSpec((d, tn), lambda i, j: (0, j)),
        ],
        out_specs=pl.BlockSpec((tm, tn), lambda i, j: (i, j)),
        scratch_shapes=[pltpu.VMEM((tm, d), BF16)],
        compiler_params=_cparams(("arbitrary", "arbitrary")),
        name="inproj",
    )(x2, g, mod3, mod3, w_bf)


def _hgrn_consts():
    c = HG_CHUNK
    t = np.arange(c)[:, None]
    j = np.arange(c)[None, :]
    tril = (j <= t).astype(np.float32)
    mats = [tril]
    masks = [np.eye(c, dtype=np.float32)]
    s = c // 2
    while s >= 1:
        ref = (t // (2 * s)) * (2 * s) + s - 1
        mats.append(tril - (j <= ref).astype(np.float32))
        masks.append((((t // s) % 2 == 1) & ((j // s) == (t // s) - 1)).astype(np.float32))
        s //= 2
    mats.append((j > t).astype(np.float32))
    return np.concatenate(mats, 0), np.stack(masks, 0)


def _hgrn_kernel(q_ref, f_ref, i_ref, g_ref, lb_ref, og_ref, w_ref, m_ref, o_ref, st_scr, *, seq):
    c = HG_CHUNK
    n_lvl = m_ref.shape[0]
    st_scr[...] = jnp.zeros_like(st_scr)
    lb = lb_ref[...]
    og = og_ref[...]

    def chunk(ci, carry):
        rows = pl.ds(pl.multiple_of(ci * c, c), c)
        q = q_ref[rows, :]
        v = i_ref[rows, :]
        g = g_ref[rows, :]
        f = lb + (1.0 - lb) * _sigmoid(f_ref[rows, :])
        lf = jnp.log(f)
        kk = 1.0 - f
        hi, lo = _split_bf16(lf)
        xs = jnp.dot(w_ref[...], jnp.concatenate([hi, lo], axis=1), preferred_element_type=F32)
        xs = xs[:, :c] + xs[:, c:]
        b = xs[0:c]
        bl = xs[n_lvl * c:(n_lvl + 1) * c]
        st = st_scr[...]
        vb = v.astype(BF16)
        o = lax.dot_general((q * jnp.exp(b)).astype(BF16), st.astype(BF16), NT_DIMS,
                            preferred_element_type=F32)
        attn = m_ref[0] * lax.dot_general(q.astype(BF16), kk.astype(BF16), NT_DIMS,
                                          preferred_element_type=F32)
        for l in range(1, n_lvl):
            e = jnp.exp(-jnp.abs(xs[l * c:(l + 1) * c]))
            attn = attn + m_ref[l] * lax.dot_general(
                (q * e).astype(BF16), (kk * e).astype(BF16), NT_DIMS, preferred_element_type=F32)
        o = o + jnp.dot(attn.astype(BF16), vb, preferred_element_type=F32)
        khat = (kk * jnp.exp(bl)).astype(BF16)
        st_scr[...] = st * jnp.exp(b[c - 1:c, :]) + lax.dot_general(
            vb, khat, TN_DIMS, preferred_element_type=F32)
        ms = jnp.mean(o * o, axis=-1, keepdims=True)
        y = o * lax.rsqrt(ms + EPS) * og * (g * _sigmoid(g))
        o_ref[rows, :] = y.astype(BF16)
        return carry

    lax.fori_loop(0, seq // c, chunk, 0, unroll=4)


def _hgrn(proj, lb, og, batch, seq):
    t = proj.shape[0]
    wm, mm = _hgrn_consts()
    wm = jnp.asarray(wm, BF16)
    mm = jnp.asarray(mm, F32)
    h = HG_HEADS
    return pl.pallas_call(
        functools.partial(_hgrn_kernel, seq=seq),
        out_shape=jax.ShapeDtypeStruct((t, HG_WIDTH), BF16),
        grid=(batch, h),
        in_specs=[
            pl.BlockSpec((seq, 128), lambda b, i: (b, i)),
            pl.BlockSpec((seq, 128), lambda b, i: (b, h + i)),
            pl.BlockSpec((seq, 128), lambda b, i: (b, 2 * h + i)),
            pl.BlockSpec((seq, 128), lambda b, i: (b, 3 * h + i)),
            pl.BlockSpec((1, 128), lambda b, i: (0, i)),
            pl.BlockSpec((1, 128), lambda b, i: (0, 0)),
            pl.BlockSpec(wm.shape, lambda b, i: (0, 0)),
            pl.BlockSpec(mm.shape, lambda b, i: (0, 0, 0)),
        ],
        out_specs=pl.BlockSpec((seq, 128), lambda b, i: (b, i)),
        scratch_shapes=[pltpu.VMEM((HG_DV, HG_DK), F32)],
        compiler_params=_cparams(("arbitrary", "arbitrary")),
        name="hgrn2",
    )(proj, proj, proj, proj, lb, og, wm, mm)


def _mla_prep_kernel(cq_ref, ckv_ref, kr_ref, pos_ref, qag_ref, kvag_ref, wuq_ref, wukv_ref,
                     gq_ref, gkn_ref, gkr_ref, invf_ref, sgn_ref, cm_ref,
                     q_out, k_out, v_out):
    cq = cq_ref[...]
    cqn = cq * lax.rsqrt(jnp.mean(cq * cq, axis=-1, keepdims=True) + EPS) * qag_ref[...]
    qf = jnp.dot(cqn.astype(BF16), wuq_ref[...], preferred_element_type=F32)
    ckv = ckv_ref[...]
    ckvn = ckv * lax.rsqrt(jnp.mean(ckv * ckv, axis=-1, keepdims=True) + EPS) * kvag_ref[...]
    kvf = jnp.dot(ckvn.astype(BF16), wukv_ref[...], preferred_element_type=F32)

    ang = pos_ref[...].astype(F32) * invf_ref[...]
    cos_t = jnp.cos(ang) * cm_ref[...]
    sin_t = jnp.sin(ang) * sgn_ref[...]

    def rope(r):
        return r * cos_t + (pltpu.roll(r, 96, 1) + pltpu.roll(r, 32, 1)) * sin_t

    kr = kr_ref[...]
    kr_ss = jnp.sum(kr * kr, axis=-1, keepdims=True)
    krr = rope(kr * gkr_ref[...])
    gq = gq_ref[...]
    gkn = gkn_ref[...]
    scale = MLA_QK ** -0.5
    for h in range(MLA_HEADS):
        qh = qf[:, h * MLA_QK_PAD:(h + 1) * MLA_QK_PAD]
        inv = lax.rsqrt(jnp.sum(qh * qh, axis=-1, keepdims=True) / MLA_QK + EPS)
        qn = qh * inv * gq
        qo = jnp.concatenate([qn[:, :MLA_NOPE], rope(qn[:, MLA_NOPE:])], axis=-1) * scale
        q_out[0, h] = qo.astype(BF16)
        kn = kvf[:, h * 256:h * 256 + MLA_NOPE]
        inv = lax.rsqrt((jnp.sum(kn * kn, axis=-1, keepdims=True) + kr_ss) / MLA_QK + EPS)
        k_out[0, h] = jnp.concatenate([kn * inv * gkn, krr * inv], axis=-1).astype(BF16)
        v_out[0, h] = kvf[:, h * 256 + MLA_NOPE:(h + 1) * 256].astype(BF16)


def _mla_prep(proj, pos2, qag, kvag, wuq_p, wukv, gq, gkn, gkr, invf, sgn, cm, batch, seq):
    tm = min(256, seq)
    per_b = seq // tm
    hh = MLA_HEADS
    full = lambda shape: pl.BlockSpec(shape, lambda b, i: tuple(0 for _ in shape))
    return pl.pallas_call(
        _mla_prep_kernel,
        out_shape=(
            jax.ShapeDtypeStruct((batch, hh, seq, MLA_QK_PAD), BF16),
            jax.ShapeDtypeStruct((batch, hh, seq, MLA_QK_PAD), BF16),
            jax.ShapeDtypeStruct((batch, hh, seq, MLA_V), BF16),
        ),
        grid=(batch, per_b),
        in_specs=[
            pl.BlockSpec((tm, MLA_Q_RANK), lambda b, i: (b * per_b + i, COL_CQ // MLA_Q_RANK)),
            pl.BlockSpec((tm, MLA_KV_RANK), lambda b, i: (b * per_b + i, COL_CKV // MLA_KV_RANK)),
            pl.BlockSpec((tm, 128), lambda b, i: (b * per_b + i, COL_KR // 128)),
            pl.BlockSpec((tm, 1), lambda b, i: (b * per_b + i, 0)),
            full((1, MLA_Q_RANK)),
            full((1, MLA_KV_RANK)),
            full(wuq_p.shape),
            full(wukv.shape),
            full((1, MLA_QK_PAD)),
            full((1, 128)),
            full((1, 128)),
            full((1, 128)),
            full((1, 128)),
            full((1, 128)),
        ],
        out_specs=(
            pl.BlockSpec((1, hh, tm, MLA_QK_PAD), lambda b, i: (b, 0, i, 0)),
            pl.BlockSpec((1, hh, tm, MLA_QK_PAD), lambda b, i: (b, 0, i, 0)),
            pl.BlockSpec((1, hh, tm, MLA_V), lambda b, i: (b, 0, i, 0)),
        ),
        compiler_params=_cparams(("arbitrary", "arbitrary")),
        name="mla_prep",
    )(proj, proj, proj, pos2, qag, kvag, wuq_p, wukv, gq, gkn, gkr, invf, sgn, cm)


def _attn_kernel(q_ref, k_ref, v_ref, o_ref, *, tq, seq):
    row = lax.broadcasted_iota(jnp.int32, (tq, tq), 0)
    col = lax.broadcasted_iota(jnp.int32, (tq, tq), 1)
    causal = col <= row
    for i in range(seq // tq):
        lo = i * tq
        q = q_ref[0, 0, lo:lo + tq, :]
        sd = lax.dot_general(q, k_ref[0, 0, lo:lo + tq, :], NT_DIMS, preferred_element_type=F32)
        sd = jnp.where(causal, sd, -jnp.inf)
        m = jnp.max(sd, axis=-1, keepdims=True)
        if i > 0:
            so = lax.dot_general(q, k_ref[0, 0, :lo, :], NT_DIMS, preferred_element_type=F32)
            m = jnp.maximum(m, jnp.max(so, axis=-1, keepdims=True))
        pd = jnp.exp(sd - m)
        l = jnp.sum(pd, axis=-1, keepdims=True)
        o = jnp.dot(pd.astype(BF16), v_ref[0, 0, lo:lo + tq, :], preferred_element_type=F32)
        if i > 0:
            po = jnp.exp(so - m)
            l = l + jnp.sum(po, axis=-1, keepdims=True)
            o = o + jnp.dot(po.astype(BF16), v_ref[0, 0, :lo, :], preferred_element_type=F32)
        o_ref[lo:lo + tq, :] = o / l


def _attention(q, k, v):
    batch, hh, seq, dq = q.shape
    tq = min(256, seq)
    return pl.pallas_call(
        functools.partial(_attn_kernel, tq=tq, seq=seq),
        out_shape=jax.ShapeDtypeStruct((batch * seq, hh * MLA_V), F32),
        grid=(batch, hh),
        in_specs=[
            pl.BlockSpec((1, 1, seq, dq), lambda b, h: (b, h, 0, 0)),
            pl.BlockSpec((1, 1, seq, dq), lambda b, h: (b, h, 0, 0)),
            pl.BlockSpec((1, 1, seq, MLA_V), lambda b, h: (b, h, 0, 0)),
        ],
        out_specs=pl.BlockSpec((seq, MLA_V), lambda b, h: (b, h)),
        compiler_params=_cparams(("arbitrary", "arbitrary")),
        name="mla_attn",
    )(q, k, v)


def _outproj_kernel(ohg_ref, omla_ref, x_ref, gt_ref, sc_ref, sh_ref, og_ref, g2_ref,
                    wo_ref, wrh_ref, wrl_ref, h_out, u_out, lg_out):
    om = omla_ref[...]
    omn = om * lax.rsqrt(jnp.mean(om * om, axis=-1, keepdims=True) + EPS) * og_ref[...]
    y = jnp.dot(ohg_ref[...], wo_ref[:HG_WIDTH, :], preferred_element_type=F32)
    y = y + jnp.dot(omn.astype(BF16), wo_ref[HG_WIDTH:, :], preferred_element_type=F32)
    h1 = x_ref[...] + gt_ref[0] * y
    h_out[...] = h1
    u = h1 * lax.rsqrt(jnp.mean(h1 * h1, axis=-1, keepdims=True) + EPS) * g2_ref[...]
    u = u * (1.0 + sc_ref[0]) + sh_ref[0]
    uh, ul = _split_bf16(u)
    u_out[...] = u
    lg = lax.dot_general(wrh_ref[...], uh, NT_DIMS, preferred_element_type=F32)
    lg = lg + lax.dot_general(wrl_ref[...], uh, NT_DIMS, preferred_element_type=F32)
    lg = lg + lax.dot_general(wrh_ref[...], ul, NT_DIMS, preferred_element_type=F32)
    lg_out[...] = lg


def _outproj(ohg, omla, x2, mod3, og, g2, wo_bf, wr_hi, wr_lo, seq):
    t, d = x2.shape
    tm = min(256, seq)
    per_b = seq // tm
    full = lambda shape: pl.BlockSpec(shape, lambda i: tuple(0 for _ in shape))
    modspec = lambda col: pl.BlockSpec((1, 1, d), lambda i: (i // per_b, 0, col))
    return pl.pallas_call(
        _outproj_kernel,
        out_shape=(
            jax.ShapeDtypeStruct((t, d), F32),
            jax.ShapeDtypeStruct((t, d), F32),
            jax.ShapeDtypeStruct((N_EXPERTS, t), F32),
        ),
        grid=(t // tm,),
        in_specs=[
            pl.BlockSpec((tm, HG_WIDTH), lambda i: (i, 0)),
            pl.BlockSpec((tm, MLA_WIDTH), lambda i: (i, 0)),
            pl.BlockSpec((tm, d), lambda i: (i, 0)),
            modspec(2),
            modspec(4),
            modspec(3),
            full((1, MLA_WIDTH)),
            full((1, d)),
            full(wo_bf.shape),
            full(wr_hi.shape),
            full(wr_lo.shape),
        ],
        out_specs=(
            pl.BlockSpec((tm, d), lambda i: (i, 0)),
            pl.BlockSpec((tm, d), lambda i: (i, 0)),
            pl.BlockSpec((N_EXPERTS, tm), lambda i: (0, i)),
        ),
        compiler_params=_cparams(("arbitrary",)),
        name="outproj",
    )(ohg, omla, x2, mod3, mod3, mod3, og, g2, wo_bf, wr_hi, wr_lo)


def _route_kernel(lg_ref, bias_ref, tri_ref, ltri_ref, idx_out, w_out, dest_out, info_out,
                  carry, base, *, tt, n_info):
    phase = pl.program_id(0)
    step = pl.program_id(1)

    @pl.when((phase == 0) & (step == 0))
    def _():
        carry[...] = jnp.zeros_like(carry)

    @pl.when((phase == 1) & (step == 0))
    def _():
        cnt = carry[...]
        padded = jnp.floor((cnt + (MOE_BLK - 1)) / MOE_BLK) * MOE_BLK
        hi, lo = _split_bf16(padded)
        ps = jnp.dot(ltri_ref[...], jnp.concatenate([hi, lo], axis=1), preferred_element_type=F32)
        pad_start = ps[:, :128] + ps[:, 128:]
        base[...] = pad_start
        carry[...] = jnp.zeros_like(carry)
        pad_end = pad_start + padded
        reps = n_info // 128
        pe = jnp.concatenate([pad_end] * reps, axis=1)
        starts = (lax.broadcasted_iota(jnp.int32, (N_EXPERTS, n_info), 1) * MOE_BLK).astype(F32)
        blk_e = jnp.sum(jnp.where(pe <= starts, 1.0, 0.0), axis=0, keepdims=True)
        blk_e = jnp.minimum(blk_e, N_EXPERTS - 1.0)
        n_used = pe[N_EXPERTS - 1:N_EXPERTS, :] / MOE_BLK
        diag = (lax.broadcasted_iota(jnp.int32, (N_EXPERTS, n_info), 0)
                == lax.broadcasted_iota(jnp.int32, (N_EXPERTS, n_info), 1))
        end_blk = jnp.sum(jnp.where(diag, pe, 0.0), axis=0, keepdims=True) / MOE_BLK
        pd = jnp.concatenate([padded] * reps, axis=1)
        num_blk = jnp.sum(jnp.where(diag, pd, 0.0), axis=0, keepdims=True) / MOE_BLK
        info_out[...] = jnp.concatenate(
            [blk_e, n_used, end_blk, num_blk, jnp.zeros((4, n_info), F32)],
            axis=0).astype(jnp.int32)

    ne, ng, eg = N_EXPERTS, N_GROUPS, EXPERTS_PER_GROUP
    neg = -jnp.inf
    sc = _sigmoid(lg_ref[...])
    sel = sc + bias_ref[...]
    sel3 = sel.reshape(ng, eg, tt)
    sub = lax.broadcasted_iota(jnp.int32, (ng, eg, tt), 1)
    m1 = jnp.max(sel3, axis=1, keepdims=True)
    first = jnp.min(jnp.where(sel3 == m1, sub, eg), axis=1, keepdims=True)
    m2 = jnp.max(jnp.where(sub == first, neg, sel3), axis=1, keepdims=True)
    gs = (m1 + m2).reshape(ng, tt)

    gio = lax.broadcasted_iota(jnp.int32, (ng, tt), 0)
    gsel = jnp.zeros((ng, tt), F32)
    for _ in range(TOPK_GROUPS):
        m = jnp.max(gs, axis=0, keepdims=True)
        gi = jnp.min(jnp.where(gs == m, gio, ng), axis=0, keepdims=True)
        hit = gio == gi
        gsel = jnp.where(hit, 1.0, gsel)
        gs = jnp.where(hit, neg, gs)
    gfull = jnp.broadcast_to(gsel.reshape(ng, 1, tt), (ng, eg, tt)).reshape(ne, tt)
    selm = jnp.where(gfull > 0.0, sel, neg)

    eio = lax.broadcasted_iota(jnp.int32, (ne, tt), 0)
    chosen = jnp.zeros((ne, tt), F32)
    idxs, ws = [], []
    for _ in range(TOP_K):
        m = jnp.max(selm, axis=0, keepdims=True)
        ei = jnp.min(jnp.where(selm == m, eio, ne), axis=0, keepdims=True)
        hit = eio == ei
        ws.append(jnp.sum(jnp.where(hit, sc, 0.0), axis=0, keepdims=True))
        idxs.append(ei)
        chosen = jnp.where(hit, 1.0, chosen)
        selm = jnp.where(hit, neg, selm)
    wsum = ws[0]
    for k in range(1, TOP_K):
        wsum = wsum + ws[k]

    tot = jnp.sum(chosen, axis=1, keepdims=True)

    @pl.when(phase == 1)
    def _():
        off = base[...][:, :1] + carry[...][:, :1]
        excl = jnp.dot(chosen.astype(BF16), tri_ref[...], preferred_element_type=F32) + off
        for k in range(TOP_K):
            idx_out[k:k + 1, :] = idxs[k]
            w_out[k:k + 1, :] = ws[k] / wsum * ROUTED_SCALE
            dk = jnp.sum(jnp.where(eio == idxs[k], excl, 0.0), axis=0, keepdims=True)
            dest_out[k:k + 1, :] = dk.astype(jnp.int32)

    carry[...] = carry[...] + tot


def _route(logits_t, bias_col, n_blk):
    ne, t = logits_t.shape
    tt = min(512, t)
    n_info = -(-n_blk // 128) * 128
    tri = jnp.asarray(np.triu(np.ones((tt, tt), np.float32), 1), BF16)
    ltri = jnp.asarray(np.tril(np.ones((ne, ne), np.float32), -1), BF16)
    return pl.pallas_call(
        functools.partial(_route_kernel, tt=tt, n_info=n_info),
        out_shape=(
            jax.ShapeDtypeStruct((TOP_K, t), jnp.int32),
            jax.ShapeDtypeStruct((TOP_K, t), F32),
            jax.ShapeDtypeStruct((TOP_K, t), jnp.int32),
            jax.ShapeDtypeStruct((8, n_info), jnp.int32),
        ),
        grid=(2, t // tt),
        in_specs=[
            pl.BlockSpec((ne, tt), lambda p, i: (0, i)),
            pl.BlockSpec((ne, 1), lambda p, i: (0, 0)),
            pl.BlockSpec((tt, tt), lambda p, i: (0, 0)),
            pl.BlockSpec((ne, ne), lambda p, i: (0, 0)),
        ],
        out_specs=(
            pl.BlockSpec((TOP_K, tt), lambda p, i: (0, i * p)),
            pl.BlockSpec((TOP_K, tt), lambda p, i: (0, i * p)),
            pl.BlockSpec((TOP_K, tt), lambda p, i: (0, i * p)),
            pl.BlockSpec((8, n_info), lambda p, i: (0, 0)),
        ),
        scratch_shapes=[pltpu.VMEM((ne, 128), F32), pltpu.VMEM((ne, 128), F32)],
        compiler_params=_cparams(("arbitrary", "arbitrary")),
        name="route",
    )(logits_t, bias_col, tri, ltri)


def _dispatch_kernel(info_ref, dest_ref, u_hbm, xs_hbm, stage, zbuf, lsem, ssem, zsem, *, tm, n_blk):
    i = pl.program_id(0)
    n = pl.num_programs(0)
    slot = i % 2
    n_used = info_ref[1, 0]

    def load(tile, dst_slot):
        return pltpu.make_async_copy(u_hbm.at[pl.ds(tile * tm, tm)], stage.at[dst_slot],
                                     lsem.at[dst_slot])

    def zero_block(blk):
        return pltpu.make_async_copy(zbuf, xs_hbm.at[pl.ds(blk * MOE_BLK, MOE_BLK)], zsem)

    @pl.when(i == 0)
    def _():
        load(0, 0).start()
        zbuf[...] = jnp.zeros_like(zbuf)
        for e in range(N_EXPERTS):
            @pl.when(info_ref[3, e] > 0)
            def _():
                zero_block(info_ref[2, e] - 1).start()

        def zero_unused(j, carry):
            zero_block(j).start()
            return carry
        lax.fori_loop(n_used, n_blk, zero_unused, 0)
        for e in range(N_EXPERTS):
            @pl.when(info_ref[3, e] > 0)
            def _():
                zero_block(0).wait()

        def wait_unused(j, carry):
            zero_block(0).wait()
            return carry
        lax.fori_loop(n_used, n_blk, wait_unused, 0)

    def scatter_wait(src_slot):
        pltpu.make_async_copy(stage.at[src_slot], xs_hbm.at[pl.ds(0, tm)], ssem.at[src_slot]).wait()

    @pl.when(i >= 1)
    def _():
        for _ in range(TOP_K):
            scatter_wait(1 - slot)

    @pl.when(i + 1 < n)
    def _():
        load(i + 1, 1 - slot).start()

    load(i, slot).wait()
    for k in range(TOP_K):
        for r in range(tm):
            pltpu.make_async_copy(stage.at[slot, pl.ds(r, 1)],
                                  xs_hbm.at[pl.ds(dest_ref[0, k, r], 1)], ssem.at[slot]).start()

    @pl.when(i == n - 1)
    def _():
        for _ in range(TOP_K):
            scatter_wait(slot)


def _dispatch(info, dest3, u2, n_blk):
    t, d = u2.shape
    tm = dest3.shape[2]
    return pl.pallas_call(
        functools.partial(_dispatch_kernel, tm=tm, n_blk=n_blk),
        out_shape=jax.ShapeDtypeStruct((n_blk * MOE_BLK, d), F32),
        grid_spec=pltpu.PrefetchScalarGridSpec(
            num_scalar_prefetch=1,
            grid=(t // tm,),
            in_specs=[
                pl.BlockSpec((1, TOP_K, tm), lambda i, info: (i, 0, 0), memory_space=pltpu.SMEM),
                pl.BlockSpec(memory_space=pl.ANY),
            ],
            out_specs=pl.BlockSpec(memory_space=pl.ANY),
            scratch_shapes=[
                pltpu.VMEM((2, tm, d), F32),
                pltpu.VMEM((MOE_BLK, d), F32),
                pltpu.SemaphoreType.DMA((2,)),
                pltpu.SemaphoreType.DMA((2,)),
                pltpu.SemaphoreType.DMA(()),
            ],
        ),
        compiler_params=_cparams(("arbitrary",)),
        name="moe_dispatch",
    )(info, dest3, u2)


def _gmm_kernel(be_ref, meta_ref, xs_ref, wg_ref, wu_ref, wd_ref, y_ref, wg_s, wu_s, wd_s):
    i = pl.program_id(0)
    n_used = meta_ref[0]
    prev = be_ref[jnp.maximum(i - 1, 0)]
    fresh = (i == 0) | (be_ref[i] != prev)

    @pl.when(fresh & (i < n_used))
    def _():
        wg_s[...] = wg_ref[...].astype(BF16)
        wu_s[...] = wu_ref[...].astype(BF16)
        wd_s[...] = wd_ref[...].astype(BF16)

    @pl.when(i < n_used)
    def _():
        xb = xs_ref[...].astype(BF16)
        a = jnp.dot(xb, wg_s[...], preferred_element_type=F32)
        u = jnp.dot(xb, wu_s[...], preferred_element_type=F32)
        hid = (a * _sigmoid(a) * u).astype(BF16)
        y_ref[...] = jnp.dot(hid, wd_s[...], preferred_element_type=F32)

    @pl.when(i >= n_used)
    def _():
        y_ref[...] = jnp.zeros_like(y_ref)


def _gmm(blk_expert, meta, xs, w_gate, w_up, w_down):
    p, d = xs.shape
    n_blk = p // MOE_BLK
    de = w_gate.shape[-1]
    last = lambda i, meta: jnp.minimum(i, meta[0] - 1)
    wsel = lambda i, be, meta: (be[last(i, meta)], 0, 0)
    return pl.pallas_call(
        _gmm_kernel,
        out_shape=jax.ShapeDtypeStruct((p, d), F32),
        grid_spec=pltpu.PrefetchScalarGridSpec(
            num_scalar_prefetch=2,
            grid=(n_blk,),
            in_specs=[
                pl.BlockSpec((MOE_BLK, d), lambda i, be, meta: (last(i, meta), 0)),
                pl.BlockSpec((None, d, de), wsel),
                pl.BlockSpec((None, d, de), wsel),
                pl.BlockSpec((None, de, d), wsel),
            ],
            out_specs=pl.BlockSpec((MOE_BLK, d), lambda i, be, meta: (i, 0)),
            scratch_shapes=[
                pltpu.VMEM((d, de), BF16),
                pltpu.VMEM((d, de), BF16),
                pltpu.VMEM((de, d), BF16),
            ],
        ),
        compiler_params=_cparams(("arbitrary",)),
        name="moe_gmm",
    )(blk_expert, meta, xs, w_gate, w_up, w_down)


def _final_kernel(cur_ref, nxt_ref, u_ref, h_ref, gt_ref, wt_ref, wsg_ref, wsu_ref, wsd_ref, y_hbm,
                  o_ref, ybuf, sem, *, tm):
    i = pl.program_id(0)
    n = pl.num_programs(0)
    slot = i % 2
    n_rows = TOP_K * tm

    def issue(idx_ref, dst_slot):
        for k in range(TOP_K):
            for r in range(tm):
                pltpu.make_async_copy(y_hbm.at[pl.ds(idx_ref[0, k, r], 1)],
                                      ybuf.at[dst_slot, pl.ds(k * tm + r, 1)],
                                      sem.at[dst_slot]).start()

    def gather_wait(dst_slot):
        pltpu.make_async_copy(y_hbm.at[pl.ds(0, n_rows)], ybuf.at[dst_slot], sem.at[dst_slot]).wait()

    @pl.when(i == 0)
    def _():
        issue(cur_ref, 0)

    issue(nxt_ref, 1 - slot)
    ub = u_ref[...].astype(BF16)
    a = jnp.dot(ub, wsg_ref[...], preferred_element_type=F32)
    b = jnp.dot(ub, wsu_ref[...], preferred_element_type=F32)
    hid = (a * _sigmoid(a) * b).astype(BF16)
    acc = jnp.dot(hid, wsd_ref[...], preferred_element_type=F32)

    gather_wait(slot)
    wt = wt_ref[...]
    for k in range(TOP_K):
        acc = acc + ybuf[slot, k * tm:(k + 1) * tm, :] * wt[:, k:k + 1]
    o_ref[...] = h_ref[...] + gt_ref[0] * acc

    @pl.when(i == n - 1)
    def _():
        gather_wait(1 - slot)


def _final(dest3, u2, h1, mod3, wt, wsg, wsu, wsd, y, seq):
    t, d = h1.shape
    tm = dest3.shape[2]
    n = t // tm
    per_b = seq // tm
    full = lambda shape: pl.BlockSpec(shape, lambda i: tuple(0 for _ in shape))
    return pl.pallas_call(
        functools.partial(_final_kernel, tm=tm),
        out_shape=jax.ShapeDtypeStruct((t, d), F32),
        grid=(n,),
        in_specs=[
            pl.BlockSpec((1, TOP_K, tm), lambda i: (i, 0, 0), memory_space=pltpu.SMEM),
            pl.BlockSpec((1, TOP_K, tm), lambda i: (jnp.minimum(i + 1, n - 1), 0, 0),
                         memory_space=pltpu.SMEM),
            pl.BlockSpec((tm, d), lambda i: (i, 0)),
            pl.BlockSpec((tm, d), lambda i: (i, 0)),
            pl.BlockSpec((1, 1, d), lambda i: (i // per_b, 0, 5)),
            pl.BlockSpec((tm, TOP_K), lambda i: (i, 0)),
            full(wsg.shape),
            full(wsu.shape),
            full(wsd.shape),
            pl.BlockSpec(memory_space=pl.ANY),
        ],
        out_specs=pl.BlockSpec((tm, d), lambda i: (i, 0)),
        scratch_shapes=[
            pltpu.VMEM((2, TOP_K * tm, d), F32),
            pltpu.SemaphoreType.DMA((2,)),
        ],
        compiler_params=_cparams(("arbitrary",)),
        name="final",
    )(dest3, dest3, u2, h1, mod3, wt, wsg, wsu, wsd, y)


def _layer(h2, mod3, positions, lb, p, batch, seq):
    t, d = h2.shape

    w_in_bf = jnp.pad(p["w_in"], ((0, 0), (0, IN_PAD - IN_COLS))).astype(BF16)
    proj = _inproj(h2, p["norm_mix_g"][None, :], mod3, w_in_bf, seq)

    o_hg = _hgrn(proj, lb[None, :], p["hg_out_g"][None, :], batch, seq)

    wuq = p["mla_w_uq"].reshape(MLA_Q_RANK, MLA_HEADS, MLA_QK)
    wuq_p = jnp.pad(wuq, ((0, 0), (0, 0), (0, MLA_QK_PAD - MLA_QK))).reshape(
        MLA_Q_RANK, MLA_HEADS * MLA_QK_PAD).astype(BF16)
    wukv = p["mla_w_ukv"].astype(BF16)
    gq = jnp.pad(p["mla_q_norm_g"], (0, MLA_QK_PAD - MLA_QK))[None, :]
    gkn = p["mla_k_norm_g"][None, :MLA_NOPE]
    gkr = jnp.pad(p["mla_k_norm_g"][MLA_NOPE:], (0, 128 - MLA_ROPE))[None, :]
    half = MLA_ROPE // 2
    inv_freq = ROPE_THETA ** (-jnp.arange(half, dtype=F32) / half)
    zeros64 = jnp.zeros((128 - MLA_ROPE,), F32)
    invf = jnp.concatenate([inv_freq, inv_freq, zeros64])[None, :]
    sgn = jnp.concatenate([-jnp.ones((half,), F32), jnp.ones((half,), F32), zeros64])[None, :]
    cm = jnp.concatenate([jnp.ones((MLA_ROPE,), F32), zeros64])[None, :]
    q, k, v = _mla_prep(proj, positions.reshape(t, 1), p["mla_q_a_g"][None, :],
                        p["mla_kv_a_g"][None, :], wuq_p, wukv, gq, gkn, gkr, invf, sgn, cm,
                        batch, seq)
    o_mla = _attention(q, k, v)

    wr_t = p["w_router"].T
    wr_hi = wr_t.astype(BF16)
    wr_lo = (wr_t - wr_hi.astype(F32)).astype(BF16)
    h1, u2, logits_t = _outproj(o_hg, o_mla, h2, mod3, p["mla_out_g"][None, :],
                                p["norm_ffn_g"][None, :], p["w_out"].astype(BF16),
                                wr_hi, wr_lo, seq)

    n_blk = (t * TOP_K) // MOE_BLK + N_EXPERTS
    _, top_w, dest, info = _route(logits_t, p["router_bias"][:, None], n_blk)
    blk_expert = info[0, :n_blk]
    meta = info[1, :1]

    tm = min(128, seq)
    dest3 = dest.reshape(TOP_K, t // tm, tm).transpose(1, 0, 2)
    xs = _dispatch(info, dest3, u2, n_blk)
    y = _gmm(blk_expert, meta, xs, p["w_gate"], p["w_up"], p["w_down"])
    return _final(dest3, u2, h1, mod3, top_w.T, p["ws_gate"].astype(BF16),
                  p["ws_up"].astype(BF16), p["ws_down"].astype(BF16), y, seq)


def kernel(x, c, positions, w_ada, b_ada, norm_mix_g, norm_ffn_g, w_in, hg_lb_logits, hg_out_g,
           mla_q_a_g, mla_w_uq, mla_kv_a_g, mla_w_ukv, mla_q_norm_g, mla_k_norm_g, mla_out_g,
           w_out, w_router, router_bias, w_gate, w_up, w_down, ws_gate, ws_up, ws_down):
    batch, seq, d = x.shape
    depth = w_ada.shape[0]
    lb_all = jnp.cumsum(jax.nn.softmax(hg_lb_logits.astype(F32), axis=0), axis=0)
    c8 = jnp.pad(c, ((0, 8 - batch), (0, 0)))
    h2 = x.reshape(batch * seq, d)
    for l in range(depth):
        mod = _adaln(c8, w_ada[l], b_ada[l][None, :])[:batch]
        mod3 = mod.reshape(batch, 1, 6 * d)
        p = dict(norm_mix_g=norm_mix_g[l], norm_ffn_g=norm_ffn_g[l], w_in=w_in[l],
                 hg_out_g=hg_out_g[l], mla_q_a_g=mla_q_a_g[l], mla_w_uq=mla_w_uq[l],
                 mla_kv_a_g=mla_kv_a_g[l], mla_w_ukv=mla_w_ukv[l], mla_q_norm_g=mla_q_norm_g[l],
                 mla_k_norm_g=mla_k_norm_g[l], mla_out_g=mla_out_g[l], w_out=w_out[l],
                 w_router=w_router[l], router_bias=router_bias[l], w_gate=w_gate[l],
                 w_up=w_up[l], w_down=w_down[l], ws_gate=ws_gate[l], ws_up=ws_up[l],
                 ws_down=ws_down[l])
        h2 = _layer(h2, mod3, positions, lb_all[l], p, batch, seq)
    return h2.reshape(batch, seq, d)
```

```python
import functools

import numpy as np
import jax
import jax.numpy as jnp
from jax import lax
from jax.experimental import pallas as pl
from jax.experimental.pallas import tpu as pltpu

F32 = jnp.float32
BF16 = jnp.bfloat16

D_MODEL = 2048
EPS = 1e-6

HG_HEADS = 8
HG_DK = 128
HG_DV = 128
HG_WIDTH = HG_HEADS * HG_DV
HG_CHUNK = 128
HG_ROW_LEVEL = 8

MLA_HEADS = 8
MLA_Q_RANK = 512
MLA_KV_RANK = 256
MLA_NOPE = 128
MLA_ROPE = 64
MLA_QK = MLA_NOPE + MLA_ROPE
MLA_QK_PAD = 256
MLA_V = 128
MLA_WIDTH = MLA_HEADS * MLA_V
ROPE_THETA = 10000.0

IN_COLS = 4928
IN_PAD = 5120
COL_CQ = 4096
COL_CKV = 4608
COL_KR = 4864

N_EXPERTS = 64
TOP_K = 8
N_GROUPS = 8
TOPK_GROUPS = 4
EXPERTS_PER_GROUP = N_EXPERTS // N_GROUPS
D_EXPERT = 512
ROUTED_SCALE = 2.5
MOE_BLK = 256

VMEM_LIMIT = 56 * 1024 * 1024

NT_DIMS = (((1,), (1,)), ((), ()))
TN_DIMS = (((0,), (0,)), ((), ()))


def _cparams(sem):
    return pltpu.CompilerParams(dimension_semantics=sem, vmem_limit_bytes=VMEM_LIMIT)


def _sigmoid(x):
    return 1.0 / (1.0 + jnp.exp(-x))


def _split_bf16(x):
    hi = x.astype(BF16)
    lo = (x - hi.astype(F32)).astype(BF16)
    return hi, lo


def _adaln_kernel(c_ref, w_ref, b_ref, o_ref):
    c = c_ref[...]
    cond = c * _sigmoid(c)
    hi, lo = _split_bf16(cond)
    lhs = jnp.concatenate([hi, lo], axis=0)
    r = jnp.dot(lhs, w_ref[...].astype(BF16), preferred_element_type=F32)
    o_ref[...] = r[:8] + r[8:] + b_ref[...]


def _adaln(c8, w, b):
    d, n = w.shape
    tn = 1024
    return pl.pallas_call(
        _adaln_kernel,
        out_shape=jax.ShapeDtypeStruct((8, n), F32),
        grid=(n // tn,),
        in_specs=[
            pl.BlockSpec((8, d), lambda j: (0, 0)),
            pl.BlockSpec((d, tn), lambda j: (0, j)),
            pl.BlockSpec((1, tn), lambda j: (0, j)),
        ],
        out_specs=pl.BlockSpec((8, tn), lambda j: (0, j)),
        compiler_params=_cparams(("arbitrary",)),
        name="adaln",
    )(c8, w, b)


def _inproj_kernel(x_ref, g_ref, sc_ref, sh_ref, w_ref, o_ref, u_scr, *, tm):
    @pl.when(pl.program_id(1) == 0)
    def _():
        g = g_ref[...]
        sc = 1.0 + sc_ref[0]
        sh = sh_ref[0]

        def body(r, carry):
            rows = pl.ds(pl.multiple_of(r * 128, 128), 128)
            x = x_ref[rows, :]
            ms = jnp.mean(x * x, axis=-1, keepdims=True)
            u = x * lax.rsqrt(ms + EPS) * g * sc + sh
            u_scr[rows, :] = u.astype(BF16)
            return carry

        lax.fori_loop(0, tm // 128, body, 0)

    o_ref[...] = jnp.dot(u_scr[...], w_ref[...], preferred_element_type=F32)


def _inproj(x2, g, mod3, w_bf, seq):
    t, d = x2.shape
    n = w_bf.shape[1]
    tm = min(1024, seq)
    tn = 1280
    per_b = seq // tm
    return pl.pallas_call(
        functools.partial(_inproj_kernel, tm=tm),
        out_shape=jax.ShapeDtypeStruct((t, n), F32),
        grid=(t // tm, n // tn),
        in_specs=[
            pl.BlockSpec((tm, d), lambda i, j: (i, 0)),
            pl.BlockSpec((1, d), lambda i, j: (0, 0)),
            pl.BlockSpec((1, 1, d), lambda i, j: (i // per_b, 0, 1)),
            pl.BlockSpec((1, 1, d), lambda i, j: (i // per_b, 0, 0)),
            pl.BlockSpec((d, tn), lambda i, j: (0, j)),
        ],
        out_specs=pl.BlockSpec((tm, tn), lambda i, j: (i, j)),
        scratch_shapes=[pltpu.VMEM((tm, d), BF16)],
        compiler_params=_cparams(("arbitrary", "arbitrary")),
        name="inproj",
    )(x2, g, mod3, mod3, w_bf)


def _hgrn_consts():
    c = HG_CHUNK
    t = np.arange(c)[:, None]
    j = np.arange(c)[None, :]
    tril = (j <= t).astype(np.float32)
    mats = [tril]
    masks = [np.eye(c, dtype=np.float32)]
    s = c // 2
    while s >= 1:
        ref = (t // (2 * s)) * (2 * s) + s - 1
        if s < HG_ROW_LEVEL:
            mats.append(tril - (j <= ref).astype(np.float32))
        masks.append((((t // s) % 2 == 1) & ((j // s) == (t // s) - 1)).astype(np.float32))
        s //= 2
    w = np.concatenate(mats, 0)
    return np.concatenate([w, w], 1), np.stack(masks, 0)


def _hgrn_kernel(q_ref, f_ref, i_ref, g_ref, lb_ref, og_ref, w_ref, m_ref, o_ref, st_scr, *, seq):
    c = HG_CHUNK
    n_lvl = m_ref.shape[0]
    st_scr[...] = jnp.zeros_like(st_scr)
    lb = lb_ref[...]
    og = og_ref[...]

    def chunk(ci, carry):
        rows = pl.ds(pl.multiple_of(ci * c, c), c)
        q = q_ref[rows, :]
        v = i_ref[rows, :]
        g = g_ref[rows, :]
        f = lb + (1.0 - lb) * _sigmoid(f_ref[rows, :])
        lf = jnp.log(f)
        kk = 1.0 - f
        hi, lo = _split_bf16(lf)
        xs = jnp.dot(w_ref[...], jnp.concatenate([hi, lo], axis=0), preferred_element_type=F32)
        b = xs[0:c]
        bl = b[c - 1:c, :] - b
        st = st_scr[...]
        vb = v.astype(BF16)
        o = lax.dot_general((q * jnp.exp(b)).astype(BF16), st.astype(BF16), NT_DIMS,
                            preferred_element_type=F32)
        attn = m_ref[0] * lax.dot_general(q.astype(BF16), kk.astype(BF16), NT_DIMS,
                                          preferred_element_type=F32)
        s = c // 2
        fine = 1
        for l in range(1, n_lvl):
            if s >= HG_ROW_LEVEL:
                b3 = b.reshape(c // (2 * s), 2 * s, HG_DK)
                x = (b3 - b3[:, s - 1:s, :]).reshape(c, HG_DK)
            else:
                x = xs[fine * c:(fine + 1) * c]
                fine += 1
            s //= 2
            e = jnp.exp(-jnp.abs(x))
            attn = attn + m_ref[l] * lax.dot_general(
                (q * e).astype(BF16), (kk * e).astype(BF16), NT_DIMS, preferred_element_type=F32)
        o = o + jnp.dot(attn.astype(BF16), vb, preferred_element_type=F32)
        khat = (kk * jnp.exp(bl)).astype(BF16)
        st_scr[...] = st * jnp.exp(b[c - 1:c, :]) + lax.dot_general(
            vb, khat, TN_DIMS, preferred_element_type=F32)
        ms = jnp.mean(o * o, axis=-1, keepdims=True)
        y = o * lax.rsqrt(ms + EPS) * og * (g * _sigmoid(g))
        o_ref[rows, :] = y.astype(BF16)
        return carry

    lax.fori_loop(0, seq // c, chunk, 0, unroll=4)


def _hgrn(proj, lb, og, batch, seq):
    t = proj.shape[0]
    wm, mm = _hgrn_consts()
    wm = jnp.asarray(wm, BF16)
    mm = jnp.asarray(mm, F32)
    h = HG_HEADS
    return pl.pallas_call(
        functools.partial(_hgrn_kernel, seq=seq),
        out_shape=jax.ShapeDtypeStruct((t, HG_WIDTH), BF16),
        grid=(batch, h),
        in_specs=[
            pl.BlockSpec((seq, 128), lambda b, i: (b, i)),
            pl.BlockSpec((seq, 128), lambda b, i: (b, h + i)),
            pl.BlockSpec((seq, 128), lambda b, i: (b, 2 * h + i)),
            pl.BlockSpec((seq, 128), lambda b, i: (b, 3 * h + i)),
            pl.BlockSpec((1, 128), lambda b, i: (0, i)),
            pl.BlockSpec((1, 128), lambda b, i: (0, 0)),
            pl.BlockSpec(wm.shape, lambda b, i: (0, 0)),
            pl.BlockSpec(mm.shape, lambda b, i: (0, 0, 0)),
        ],
        out_specs=pl.BlockSpec((seq, 128), lambda b, i: (b, i)),
        scratch_shapes=[pltpu.VMEM((HG_DV, HG_DK), F32)],
        compiler_params=_cparams(("arbitrary", "arbitrary")),
        name="hgrn2",
    )(proj, proj, proj, proj, lb, og, wm, mm)


def _mla_prep_kernel(cq_ref, ckv_ref, kr_ref, pos_ref, qag_ref, kvag_ref, wuq_ref, wukv_ref,
                     gq_ref, gkn_ref, gkr_ref, invf_ref, sgn_ref, cm_ref,
                     q_out, k_out, v_out):
    cq = cq_ref[...]
    cqn = cq * lax.rsqrt(jnp.mean(cq * cq, axis=-1, keepdims=True) + EPS) * qag_ref[...]
    qf = jnp.dot(cqn.astype(BF16), wuq_ref[...], preferred_element_type=F32)
    ckv = ckv_ref[...]
    ckvn = ckv * lax.rsqrt(jnp.mean(ckv * ckv, axis=-1, keepdims=True) + EPS) * kvag_ref[...]
    kvf = jnp.dot(ckvn.astype(BF16), wukv_ref[...], preferred_element_type=F32)

    ang = pos_ref[...].astype(F32) * invf_ref[...]
    cos_t = jnp.cos(ang) * cm_ref[...]
    sin_t = jnp.sin(ang) * sgn_ref[...]

    def rope(r):
        return r * cos_t + (pltpu.roll(r, 96, 1) + pltpu.roll(r, 32, 1)) * sin_t

    kr = kr_ref[...]
    kr_ss = jnp.sum(kr * kr, axis=-1, keepdims=True)
    krr = rope(kr * gkr_ref[...])
    gq = gq_ref[...]
    gkn = gkn_ref[...]
    scale = MLA_QK ** -0.5
    for h in range(MLA_HEADS):
        qh = qf[:, h * MLA_QK_PAD:(h + 1) * MLA_QK_PAD]
        inv = lax.rsqrt(jnp.sum(qh * qh, axis=-1, keepdims=True) / MLA_QK + EPS)
        qn = qh * inv * gq
        qo = jnp.concatenate([qn[:, :MLA_NOPE], rope(qn[:, MLA_NOPE:])], axis=-1) * scale
        q_out[0, h] = qo.astype(BF16)
        kn = kvf[:, h * 256:h * 256 + MLA_NOPE]
        inv = lax.rsqrt((jnp.sum(kn * kn, axis=-1, keepdims=True) + kr_ss) / MLA_QK + EPS)
        k_out[0, h] = jnp.concatenate([kn * inv * gkn, krr * inv], axis=-1).astype(BF16)
        v_out[0, h] = kvf[:, h * 256 + MLA_NOPE:(h + 1) * 256].astype(BF16)


def _mla_prep(proj, pos2, qag, kvag, wuq_p, wukv, gq, gkn, gkr, invf, sgn, cm, batch, seq):
    tm = min(256, seq)
    per_b = seq // tm
    hh = MLA_HEADS
    full = lambda shape: pl.BlockSpec(shape, lambda b, i: tuple(0 for _ in shape))
    return pl.pallas_call(
        _mla_prep_kernel,
        out_shape=(
            jax.ShapeDtypeStruct((batch, hh, seq, MLA_QK_PAD), BF16),
            jax.ShapeDtypeStruct((batch, hh, seq, MLA_QK_PAD), BF16),
            jax.ShapeDtypeStruct((batch, hh, seq, MLA_V), BF16),
        ),
        grid=(batch, per_b),
        in_specs=[
            pl.BlockSpec((tm, MLA_Q_RANK), lambda b, i: (b * per_b + i, COL_CQ // MLA_Q_RANK)),
            pl.BlockSpec((tm, MLA_KV_RANK), lambda b, i: (b * per_b + i, COL_CKV // MLA_KV_RANK)),
            pl.BlockSpec((tm, 128), lambda b, i: (b * per_b + i, COL_KR // 128)),
            pl.BlockSpec((tm, 1), lambda b, i: (b * per_b + i, 0)),
            full((1, MLA_Q_RANK)),
            full((1, MLA_KV_RANK)),
            full(wuq_p.shape),
            full(wukv.shape),
            full((1, MLA_QK_PAD)),
            full((1, 128)),
            full((1, 128)),
            full((1, 128)),
            full((1, 128)),
            full((1, 128)),
        ],
        out_specs=(
            pl.BlockSpec((1, hh, tm, MLA_QK_PAD), lambda b, i: (b, 0, i, 0)),
            pl.BlockSpec((1, hh, tm, MLA_QK_PAD), lambda b, i: (b, 0, i, 0)),
            pl.BlockSpec((1, hh, tm, MLA_V), lambda b, i: (b, 0, i, 0)),
        ),
        compiler_params=_cparams(("arbitrary", "arbitrary")),
        name="mla_prep",
    )(proj, proj, proj, pos2, qag, kvag, wuq_p, wukv, gq, gkn, gkr, invf, sgn, cm)


def _attn_kernel(q_ref, k_ref, v_ref, o_ref, *, tq, seq):
    row = lax.broadcasted_iota(jnp.int32, (tq, tq), 0)
    col = lax.broadcasted_iota(jnp.int32, (tq, tq), 1)
    causal = col <= row
    for i in range(seq // tq):
        lo = i * tq
        q = q_ref[0, 0, lo:lo + tq, :]
        sd = lax.dot_general(q, k_ref[0, 0, lo:lo + tq, :], NT_DIMS, preferred_element_type=F32)
        sd = jnp.where(causal, sd, -jnp.inf)
        m = jnp.max(sd, axis=-1, keepdims=True)
        if i > 0:
            so = lax.dot_general(q, k_ref[0, 0, :lo, :], NT_DIMS, preferred_element_type=F32)
            m = jnp.maximum(m, jnp.max(so, axis=-1, keepdims=True))
        pd = jnp.exp(sd - m)
        l = jnp.sum(pd, axis=-1, keepdims=True)
        o = jnp.dot(pd.astype(BF16), v_ref[0, 0, lo:lo + tq, :], preferred_element_type=F32)
        if i > 0:
            po = jnp.exp(so - m)
            l = l + jnp.sum(po, axis=-1, keepdims=True)
            o = o + jnp.dot(po.astype(BF16), v_ref[0, 0, :lo, :], preferred_element_type=F32)
        o_ref[lo:lo + tq, :] = o / l


def _attention(q, k, v):
    batch, hh, seq, dq = q.shape
    tq = min(256, seq)
    return pl.pallas_call(
        functools.partial(_attn_kernel, tq=tq, seq=seq),
        out_shape=jax.ShapeDtypeStruct((batch * seq, hh * MLA_V), F32),
        grid=(batch, hh),
        in_specs=[
            pl.BlockSpec((1, 1, seq, dq), lambda b, h: (b, h, 0, 0)),
            pl.BlockSpec((1, 1, seq, dq), lambda b, h: (b, h, 0, 0)),
            pl.BlockSpec((1, 1, seq, MLA_V), lambda b, h: (b, h, 0, 0)),
        ],
        out_specs=pl.BlockSpec((seq, MLA_V), lambda b, h: (b, h)),
        compiler_params=_cparams(("arbitrary", "arbitrary")),
        name="mla_attn",
    )(q, k, v)


def _outproj_kernel(ohg_ref, omla_ref, x_ref, gt_ref, sc_ref, sh_ref, og_ref, g2_ref,
                    wo_ref, wrh_ref, wrl_ref, h_out, u_out, lg_out):
    om = omla_ref[...]
    omn = om * lax.rsqrt(jnp.mean(om * om, axis=-1, keepdims=True) + EPS) * og_ref[...]
    y = jnp.dot(ohg_ref[...], wo_ref[:HG_WIDTH, :], preferred_element_type=F32)
    y = y + jnp.dot(omn.astype(BF16), wo_ref[HG_WIDTH:, :], preferred_element_type=F32)
    h1 = x_ref[...] + gt_ref[0] * y
    h_out[...] = h1
    u = h1 * lax.rsqrt(jnp.mean(h1 * h1, axis=-1, keepdims=True) + EPS) * g2_ref[...]
    u = u * (1.0 + sc_ref[0]) + sh_ref[0]
    uh, ul = _split_bf16(u)
    u_out[...] = u
    lg = lax.dot_general(wrh_ref[...], uh, NT_DIMS, preferred_element_type=F32)
    lg = lg + lax.dot_general(wrl_ref[...], uh, NT_DIMS, preferred_element_type=F32)
    lg = lg + lax.dot_general(wrh_ref[...], ul, NT_DIMS, preferred_element_type=F32)
    lg_out[...] = lg


def _outproj(ohg, omla, x2, mod3, og, g2, wo_bf, wr_hi, wr_lo, seq):
    t, d = x2.shape
    tm = min(256, seq)
    per_b = seq // tm
    full = lambda shape: pl.BlockSpec(shape, lambda i: tuple(0 for _ in shape))
    modspec = lambda col: pl.BlockSpec((1, 1, d), lambda i: (i // per_b, 0, col))
    return pl.pallas_call(
        _outproj_kernel,
        out_shape=(
            jax.ShapeDtypeStruct((t, d), F32),
            jax.ShapeDtypeStruct((t, d), F32),
            jax.ShapeDtypeStruct((N_EXPERTS, t), F32),
        ),
        grid=(t // tm,),
        in_specs=[
            pl.BlockSpec((tm, HG_WIDTH), lambda i: (i, 0)),
            pl.BlockSpec((tm, MLA_WIDTH), lambda i: (i, 0)),
            pl.BlockSpec((tm, d), lambda i: (i, 0)),
            modspec(2),
            modspec(4),
            modspec(3),
            full((1, MLA_WIDTH)),
            full((1, d)),
            full(wo_bf.shape),
            full(wr_hi.shape),
            full(wr_lo.shape),
        ],
        out_specs=(
            pl.BlockSpec((tm, d), lambda i: (i, 0)),
            pl.BlockSpec((tm, d), lambda i: (i, 0)),
            pl.BlockSpec((N_EXPERTS, tm), lambda i: (0, i)),
        ),
        compiler_params=_cparams(("arbitrary",)),
        name="outproj",
    )(ohg, omla, x2, mod3, mod3, mod3, og, g2, wo_bf, wr_hi, wr_lo)


def _route_kernel(lg_ref, bias_ref, tri_ref, ltri_ref, idx_out, w_out, dest_out, info_out,
                  carry, base, *, tt, n_info):
    phase = pl.program_id(0)
    step = pl.program_id(1)

    @pl.when((phase == 0) & (step == 0))
    def _():
        carry[...] = jnp.zeros_like(carry)

    @pl.when((phase == 1) & (step == 0))
    def _():
        cnt = carry[...]
        padded = jnp.floor((cnt + (MOE_BLK - 1)) / MOE_BLK) * MOE_BLK
        hi, lo = _split_bf16(padded)
        ps = jnp.dot(ltri_ref[...], jnp.concatenate([hi, lo], axis=1), preferred_element_type=F32)
        pad_start = ps[:, :128] + ps[:, 128:]
        base[...] = pad_start
        carry[...] = jnp.zeros_like(carry)
        pad_end = pad_start + padded
        reps = n_info // 128
        pe = jnp.concatenate([pad_end] * reps, axis=1)
        starts = (lax.broadcasted_iota(jnp.int32, (N_EXPERTS, n_info), 1) * MOE_BLK).astype(F32)
        blk_e = jnp.sum(jnp.where(pe <= starts, 1.0, 0.0), axis=0, keepdims=True)
        blk_e = jnp.minimum(blk_e, N_EXPERTS - 1.0)
        n_used = pe[N_EXPERTS - 1:N_EXPERTS, :] / MOE_BLK
        diag = (lax.broadcasted_iota(jnp.int32, (N_EXPERTS, n_info), 0)
                == lax.broadcasted_iota(jnp.int32, (N_EXPERTS, n_info), 1))
        end_blk = jnp.sum(jnp.where(diag, pe, 0.0), axis=0, keepdims=True) / MOE_BLK
        pd = jnp.concatenate([padded] * reps, axis=1)
        num_blk = jnp.sum(jnp.where(diag, pd, 0.0), axis=0, keepdims=True) / MOE_BLK
        info_out[...] = jnp.concatenate(
            [blk_e, n_used, end_blk, num_blk, jnp.zeros((4, n_info), F32)],
            axis=0).astype(jnp.int32)

    ne, ng, eg = N_EXPERTS, N_GROUPS, EXPERTS_PER_GROUP
    neg = -jnp.inf
    sc = _sigmoid(lg_ref[...])
    sel = sc + bias_ref[...]
    sel3 = sel.reshape(ng, eg, tt)
    sub = lax.broadcasted_iota(jnp.int32, (ng, eg, tt), 1)
    m1 = jnp.max(sel3, axis=1, keepdims=True)
    first = jnp.min(jnp.where(sel3 == m1, sub, eg), axis=1, keepdims=True)
    m2 = jnp.max(jnp.where(sub == first, neg, sel3), axis=1, keepdims=True)
    gs = (m1 + m2).reshape(ng, tt)

    gio = lax.broadcasted_iota(jnp.int32, (ng, tt), 0)
    gsel = jnp.zeros((ng, tt), F32)
    for _ in range(TOPK_GROUPS):
        m = jnp.max(gs, axis=0, keepdims=True)
        gi = jnp.min(jnp.where(gs == m, gio, ng), axis=0, keepdims=True)
        hit = gio == gi
        gsel = jnp.where(hit, 1.0, gsel)
        gs = jnp.where(hit, neg, gs)
    gfull = jnp.broadcast_to(gsel.reshape(ng, 1, tt), (ng, eg, tt)).reshape(ne, tt)
    selm = jnp.where(gfull > 0.0, sel, neg)

    eio = lax.broadcasted_iota(jnp.int32, (ne, tt), 0)
    chosen = jnp.zeros((ne, tt), F32)
    idxs, ws = [], []
    for _ in range(TOP_K):
        m = jnp.max(selm, axis=0, keepdims=True)
        ei = jnp.min(jnp.where(selm == m, eio, ne), axis=0, keepdims=True)
        hit = eio == ei
        ws.append(jnp.sum(jnp.where(hit, sc, 0.0), axis=0, keepdims=True))
        idxs.append(ei)
        chosen = jnp.where(hit, 1.0, chosen)
        selm = jnp.where(hit, neg, selm)
    wsum = ws[0]
    for k in range(1, TOP_K):
        wsum = wsum + ws[k]

    tot = jnp.sum(chosen, axis=1, keepdims=True)

    @pl.when(phase == 1)
    def _():
        off = base[...][:, :1] + carry[...][:, :1]
        excl = jnp.dot(chosen.astype(BF16), tri_ref[...], preferred_element_type=F32) + off
        for k in range(TOP_K):
            idx_out[k:k + 1, :] = idxs[k]
            w_out[k:k + 1, :] = ws[k] / wsum * ROUTED_SCALE
            dk = jnp.sum(jnp.where(eio == idxs[k], excl, 0.0), axis=0, keepdims=True)
            dest_out[k:k + 1, :] = dk.astype(jnp.int32)

    carry[...] = carry[...] + tot


def _route(logits_t, bias_col, n_blk):
    ne, t = logits_t.shape
    tt = min(512, t)
    n_info = -(-n_blk // 128) * 128
    tri = jnp.asarray(np.triu(np.ones((tt, tt), np.float32), 1), BF16)
    ltri = jnp.asarray(np.tril(np.ones((ne, ne), np.float32), -1), BF16)
    return pl.pallas_call(
        functools.partial(_route_kernel, tt=tt, n_info=n_info),
        out_shape=(
            jax.ShapeDtypeStruct((TOP_K, t), jnp.int32),
            jax.ShapeDtypeStruct((TOP_K, t), F32),
            jax.ShapeDtypeStruct((TOP_K, t), jnp.int32),
            jax.ShapeDtypeStruct((8, n_info), jnp.int32),
        ),
        grid=(2, t // tt),
        in_specs=[
            pl.BlockSpec((ne, tt), lambda p, i: (0, i)),
            pl.BlockSpec((ne, 1), lambda p, i: (0, 0)),
            pl.BlockSpec((tt, tt), lambda p, i: (0, 0)),
            pl.BlockSpec((ne, ne), lambda p, i: (0, 0)),
        ],
        out_specs=(
            pl.BlockSpec((TOP_K, tt), lambda p, i: (0, i * p)),
            pl.BlockSpec((TOP_K, tt), lambda p, i: (0, i * p)),
            pl.BlockSpec((TOP_K, tt), lambda p, i: (0, i * p)),
            pl.BlockSpec((8, n_info), lambda p, i: (0, 0)),
        ),
        scratch_shapes=[pltpu.VMEM((ne, 128), F32), pltpu.VMEM((ne, 128), F32)],
        compiler_params=_cparams(("arbitrary", "arbitrary")),
        name="route",
    )(logits_t, bias_col, tri, ltri)


def _dispatch_kernel(info_ref, dest_ref, u_hbm, xs_hbm, stage, zbuf, lsem, ssem, zsem, *, tm, n_blk):
    i = pl.program_id(0)
    n = pl.num_programs(0)
    slot = i % 2
    n_used = info_ref[1, 0]

    def load(tile, dst_slot):
        return pltpu.make_async_copy(u_hbm.at[pl.ds(tile * tm, tm)], stage.at[dst_slot],
                                     lsem.at[dst_slot])

    def zero_block(blk):
        return pltpu.make_async_copy(zbuf, xs_hbm.at[pl.ds(blk * MOE_BLK, MOE_BLK)], zsem)

    @pl.when(i == 0)
    def _():
        load(0, 0).start()
        zbuf[...] = jnp.zeros_like(zbuf)
        for e in range(N_EXPERTS):
            @pl.when(info_ref[3, e] > 0)
            def _():
                zero_block(info_ref[2, e] - 1).start()

        def zero_unused(j, carry):
            zero_block(j).start()
            return carry
        lax.fori_loop(n_used, n_blk, zero_unused, 0)
        for e in range(N_EXPERTS):
            @pl.when(info_ref[3, e] > 0)
            def _():
                zero_block(0).wait()

        def wait_unused(j, carry):
            zero_block(0).wait()
            return carry
        lax.fori_loop(n_used, n_blk, wait_unused, 0)

    def scatter_wait(src_slot):
        pltpu.make_async_copy(stage.at[src_slot], xs_hbm.at[pl.ds(0, tm)], ssem.at[src_slot]).wait()

    @pl.when(i >= 1)
    def _():
        for _ in range(TOP_K):
            scatter_wait(1 - slot)

    @pl.when(i + 1 < n)
    def _():
        load(i + 1, 1 - slot).start()

    load(i, slot).wait()
    for k in range(TOP_K):
        for r in range(tm):
            pltpu.make_async_copy(stage.at[slot, pl.ds(r, 1)],
                                  xs_hbm.at[pl.ds(dest_ref[0, k, r], 1)], ssem.at[slot]).start()

    @pl.when(i == n - 1)
    def _():
        for _ in range(TOP_K):
            scatter_wait(slot)


def _dispatch(info, dest3, u2, n_blk):
    t, d = u2.shape
    tm = dest3.shape[2]
    return pl.pallas_call(
        functools.partial(_dispatch_kernel, tm=tm, n_blk=n_blk),
        out_shape=jax.ShapeDtypeStruct((n_blk * MOE_BLK, d), F32),
        grid_spec=pltpu.PrefetchScalarGridSpec(
            num_scalar_prefetch=1,
            grid=(t // tm,),
            in_specs=[
                pl.BlockSpec((1, TOP_K, tm), lambda i, info: (i, 0, 0), memory_space=pltpu.SMEM),
                pl.BlockSpec(memory_space=pl.ANY),
            ],
            out_specs=pl.BlockSpec(memory_space=pl.ANY),
            scratch_shapes=[
                pltpu.VMEM((2, tm, d), F32),
                pltpu.VMEM((MOE_BLK, d), F32),
                pltpu.SemaphoreType.DMA((2,)),
                pltpu.SemaphoreType.DMA((2,)),
                pltpu.SemaphoreType.DMA(()),
            ],
        ),
        compiler_params=_cparams(("arbitrary",)),
        name="moe_dispatch",
    )(info, dest3, u2)


def _gmm_kernel(info_ref, xs_ref, wg_hbm, wu_hbm, wd_hbm, y_ref,
                wg_f, wu_f, wd_f, wsem, wg_s, wu_s, wd_s, nswap):
    i = pl.program_id(0)
    n_used = info_ref[1, 0]
    e = info_ref[0, i]
    prev = info_ref[0, jnp.maximum(i - 1, 0)]
    fresh = (i == 0) | (e != prev)

    def fetch(expert, slot):
        return (pltpu.make_async_copy(wg_hbm.at[expert], wg_f.at[slot], wsem.at[slot]),
                pltpu.make_async_copy(wu_hbm.at[expert], wu_f.at[slot], wsem.at[slot]),
                pltpu.make_async_copy(wd_hbm.at[expert], wd_f.at[slot], wsem.at[slot]))

    @pl.when(i == 0)
    def _():
        nswap[0] = 0
        for c in fetch(e, 0):
            c.start()

    @pl.when(fresh & (i < n_used))
    def _():
        slot = nswap[0] % 2
        nswap[0] = nswap[0] + 1
        for c in fetch(e, slot):
            c.wait()
        nxt = info_ref[2, e]

        @pl.when(nxt < n_used)
        def _():
            for c in fetch(info_ref[0, nxt], 1 - slot):
                c.start()

        wg_s[...] = wg_f[slot].astype(BF16)
        wu_s[...] = wu_f[slot].astype(BF16)
        wd_s[...] = wd_f[slot].astype(BF16)

    @pl.when(i < n_used)
    def _():
        xb = xs_ref[...].astype(BF16)
        a = jnp.dot(xb, wg_s[...], preferred_element_type=F32)
        u = jnp.dot(xb, wu_s[...], preferred_element_type=F32)
        hid = (a * _sigmoid(a) * u).astype(BF16)
        y_ref[...] = jnp.dot(hid, wd_s[...], preferred_element_type=F32)

    @pl.when(i >= n_used)
    def _():
        y_ref[...] = jnp.zeros_like(y_ref)


def _gmm(info, xs, w_gate, w_up, w_down):
    p, d = xs.shape
    n_blk = p // MOE_BLK
    de = w_gate.shape[-1]
    return pl.pallas_call(
        _gmm_kernel,
        out_shape=jax.ShapeDtypeStruct((p, d), F32),
        grid_spec=pltpu.PrefetchScalarGridSpec(
            num_scalar_prefetch=1,
            grid=(n_blk,),
            in_specs=[
                pl.BlockSpec((MOE_BLK, d), lambda i, info: (jnp.minimum(i, info[1, 0] - 1), 0)),
                pl.BlockSpec(memory_space=pl.ANY),
                pl.BlockSpec(memory_space=pl.ANY),
                pl.BlockSpec(memory_space=pl.ANY),
            ],
            out_specs=pl.BlockSpec((MOE_BLK, d), lambda i, info: (i, 0)),
            scratch_shapes=[
                pltpu.VMEM((2, d, de), F32),
                pltpu.VMEM((2, d, de), F32),
                pltpu.VMEM((2, de, d), F32),
                pltpu.SemaphoreType.DMA((2,)),
                pltpu.VMEM((d, de), BF16),
                pltpu.VMEM((d, de), BF16),
                pltpu.VMEM((de, d), BF16),
                pltpu.SMEM((1,), jnp.int32),
            ],
        ),
        compiler_params=_cparams(("arbitrary",)),
        name="moe_gmm",
    )(info, xs, w_gate, w_up, w_down)


def _final_kernel(cur_ref, nxt_ref, u_ref, h_ref, gt_ref, wt_ref, wsg_ref, wsu_ref, wsd_ref, y_hbm,
                  o_ref, ybuf, sem, *, tm):
    i = pl.program_id(0)
    n = pl.num_programs(0)
    slot = i % 2
    n_rows = TOP_K * tm

    def issue(idx_ref, dst_slot):
        for k in range(TOP_K):
            for r in range(tm):
                pltpu.make_async_copy(y_hbm.at[pl.ds(idx_ref[0, k, r], 1)],
                                      ybuf.at[dst_slot, pl.ds(k * tm + r, 1)],
                                      sem.at[dst_slot]).start()

    def gather_wait(dst_slot):
        pltpu.make_async_copy(y_hbm.at[pl.ds(0, n_rows)], ybuf.at[dst_slot], sem.at[dst_slot]).wait()

    @pl.when(i == 0)
    def _():
        issue(cur_ref, 0)

    issue(nxt_ref, 1 - slot)
    ub = u_ref[...].astype(BF16)
    a = jnp.dot(ub, wsg_ref[...], preferred_element_type=F32)
    b = jnp.dot(ub, wsu_ref[...], preferred_element_type=F32)
    hid = (a * _sigmoid(a) * b).astype(BF16)
    acc = jnp.dot(hid, wsd_ref[...], preferred_element_type=F32)

    gather_wait(slot)
    wt = wt_ref[...]
    for k in range(TOP_K):
        acc = acc + ybuf[slot, k * tm:(k + 1) * tm, :] * wt[:, k:k + 1]
    o_ref[...] = h_ref[...] + gt_ref[0] * acc

    @pl.when(i == n - 1)
    def _():
        gather_wait(1 - slot)


def _final(dest3, u2, h1, mod3, wt, wsg, wsu, wsd, y, seq):
    t, d = h1.shape
    tm = dest3.shape[2]
    n = t // tm
    per_b = seq // tm
    full = lambda shape: pl.BlockSpec(shape, lambda i: tuple(0 for _ in shape))
    return pl.pallas_call(
        functools.partial(_final_kernel, tm=tm),
        out_shape=jax.ShapeDtypeStruct((t, d), F32),
        grid=(n,),
        in_specs=[
            pl.BlockSpec((1, TOP_K, tm), lambda i: (i, 0, 0), memory_space=pltpu.SMEM),
            pl.BlockSpec((1, TOP_K, tm), lambda i: (jnp.minimum(i + 1, n - 1), 0, 0),
                         memory_space=pltpu.SMEM),
            pl.BlockSpec((tm, d), lambda i: (i, 0)),
            pl.BlockSpec((tm, d), lambda i: (i, 0)),
            pl.BlockSpec((1, 1, d), lambda i: (i // per_b, 0, 5)),
            pl.BlockSpec((tm, TOP_K), lambda i: (i, 0)),
            full(wsg.shape),
            full(wsu.shape),
            full(wsd.shape),
            pl.BlockSpec(memory_space=pl.ANY),
        ],
        out_specs=pl.BlockSpec((tm, d), lambda i: (i, 0)),
        scratch_shapes=[
            pltpu.VMEM((2, TOP_K * tm, d), F32),
            pltpu.SemaphoreType.DMA((2,)),
        ],
        compiler_params=_cparams(("arbitrary",)),
        name="final",
    )(dest3, dest3, u2, h1, mod3, wt, wsg, wsu, wsd, y)


def _layer(h2, mod3, positions, lb, p, batch, seq):
    t, d = h2.shape

    w_in_bf = jnp.pad(p["w_in"], ((0, 0), (0, IN_PAD - IN_COLS))).astype(BF16)
    proj = _inproj(h2, p["norm_mix_g"][None, :], mod3, w_in_bf, seq)

    o_hg = _hgrn(proj, lb[None, :], p["hg_out_g"][None, :], batch, seq)

    wuq = p["mla_w_uq"].reshape(MLA_Q_RANK, MLA_HEADS, MLA_QK)
    wuq_p = jnp.pad(wuq, ((0, 0), (0, 0), (0, MLA_QK_PAD - MLA_QK))).reshape(
        MLA_Q_RANK, MLA_HEADS * MLA_QK_PAD).astype(BF16)
    wukv = p["mla_w_ukv"].astype(BF16)
    gq = jnp.pad(p["mla_q_norm_g"], (0, MLA_QK_PAD - MLA_QK))[None, :]
    gkn = p["mla_k_norm_g"][None, :MLA_NOPE]
    gkr = jnp.pad(p["mla_k_norm_g"][MLA_NOPE:], (0, 128 - MLA_ROPE))[None, :]
    half = MLA_ROPE // 2
    inv_freq = ROPE_THETA ** (-jnp.arange(half, dtype=F32) / half)
    zeros64 = jnp.zeros((128 - MLA_ROPE,), F32)
    invf = jnp.concatenate([inv_freq, inv_freq, zeros64])[None, :]
    sgn = jnp.concatenate([-jnp.ones((half,), F32), jnp.ones((half,), F32), zeros64])[None, :]
    cm = jnp.concatenate([jnp.ones((MLA_ROPE,), F32), zeros64])[None, :]
    q, k, v = _mla_prep(proj, positions.reshape(t, 1), p["mla_q_a_g"][None, :],
                        p["mla_kv_a_g"][None, :], wuq_p, wukv, gq, gkn, gkr, invf, sgn, cm,
                        batch, seq)
    o_mla = _attention(q, k, v)

    wr_t = p["w_router"].T
    wr_hi = wr_t.astype(BF16)
    wr_lo = (wr_t - wr_hi.astype(F32)).astype(BF16)
    h1, u2, logits_t = _outproj(o_hg, o_mla, h2, mod3, p["mla_out_g"][None, :],
                                p["norm_ffn_g"][None, :], p["w_out"].astype(BF16),
                                wr_hi, wr_lo, seq)

    n_blk = (t * TOP_K) // MOE_BLK + N_EXPERTS
    _, top_w, dest, info = _route(logits_t, p["router_bias"][:, None], n_blk)

    tm = min(128, seq)
    dest3 = dest.reshape(TOP_K, t // tm, tm).transpose(1, 0, 2)
    xs = _dispatch(info, dest3, u2, n_blk)
    y = _gmm(info, xs, p["w_gate"], p["w_up"], p["w_down"])
    return _final(dest3, u2, h1, mod3, top_w.T, p["ws_gate"].astype(BF16),
                  p["ws_up"].astype(BF16), p["ws_down"].astype(BF16), y, seq)


def kernel(x, c, positions, w_ada, b_ada, norm_mix_g, norm_ffn_g, w_in, hg_lb_logits, hg_out_g,
           mla_q_a_g, mla_w_uq, mla_kv_a_g, mla_w_ukv, mla_q_norm_g, mla_k_norm_g, mla_out_g,
           w_out, w_router, router_bias, w_gate, w_up, w_down, ws_gate, ws_up, ws_down):
    batch, seq, d = x.shape
    depth = w_ada.shape[0]
    lb_all = jnp.cumsum(jax.nn.softmax(hg_lb_logits.astype(F32), axis=0), axis=0)
    c8 = jnp.pad(c, ((0, 8 - batch), (0, 0)))
    h2 = x.reshape(batch * seq, d)
    for l in range(depth):
        mod = _adaln(c8, w_ada[l], b_ada[l][None, :])[:batch]
        mod3 = mod.reshape(batch, 1, 6 * d)
        p = dict(norm_mix_g=norm_mix_g[l], norm_ffn_g=norm_ffn_g[l], w_in=w_in[l],
                 hg_out_g=hg_out_g[l], mla_q_a_g=mla_q_a_g[l], mla_w_uq=mla_w_uq[l],
                 mla_kv_a_g=mla_kv_a_g[l], mla_w_ukv=mla_w_ukv[l], mla_q_norm_g=mla_q_norm_g[l],
                 mla_k_norm_g=mla_k_norm_g[l], mla_out_g=mla_out_g[l], w_out=w_out[l],
                 w_router=w_router[l], router_bias=router_bias[l], w_gate=w_gate[l],
                 w_up=w_up[l], w_down=w_down[l], ws_gate=ws_gate[l], ws_up=ws_up[l],
                 ws_down=ws_down[l])
        h2 = _layer(h2, mod3, positions, lb_all[l], p, batch, seq)
    return h2.reshape(batch, seq, d)
```

```python
import functools

import numpy as np
import jax
import jax.numpy as jnp
from jax import lax
from jax.experimental import pallas as pl
from jax.experimental.pallas import tpu as pltpu

F32 = jnp.float32
BF16 = jnp.bfloat16

D_MODEL = 2048
EPS = 1e-6

HG_HEADS = 8
HG_DK = 128
HG_DV = 128
HG_WIDTH = HG_HEADS * HG_DV
HG_CHUNK = 128
HG_ROW_LEVEL = 8

MLA_HEADS = 8
MLA_Q_RANK = 512
MLA_KV_RANK = 256
MLA_NOPE = 128
MLA_ROPE = 64
MLA_QK = MLA_NOPE + MLA_ROPE
MLA_QK_PAD = 256
MLA_V = 128
MLA_WIDTH = MLA_HEADS * MLA_V
ROPE_THETA = 10000.0

IN_COLS = 4928
IN_PAD = 5120
COL_CQ = 4096
COL_CKV = 4608
COL_KR = 4864

N_EXPERTS = 64
TOP_K = 8
N_GROUPS = 8
TOPK_GROUPS = 4
EXPERTS_PER_GROUP = N_EXPERTS // N_GROUPS
D_EXPERT = 512
ROUTED_SCALE = 2.5
MOE_BLK = 256

VMEM_LIMIT = 56 * 1024 * 1024

NT_DIMS = (((1,), (1,)), ((), ()))
TN_DIMS = (((0,), (0,)), ((), ()))


def _cparams(sem):
    return pltpu.CompilerParams(dimension_semantics=sem, vmem_limit_bytes=VMEM_LIMIT)


def _sigmoid(x):
    return 1.0 / (1.0 + jnp.exp(-x))


def _split_bf16(x):
    hi = x.astype(BF16)
    lo = (x - hi.astype(F32)).astype(BF16)
    return hi, lo


def _rows_to_tiles(x):
    n = x.shape[1] // 128
    return jnp.swapaxes(jnp.stack([x[:, s * 128:(s + 1) * 128] for s in range(n)], axis=0), 0, 1)


def _tiles_to_rows(x3):
    xt = jnp.swapaxes(x3, 0, 1)
    return jnp.concatenate([xt[s] for s in range(x3.shape[1])], axis=1)


def _adaln_kernel(c_ref, w_ref, b_ref, o_ref):
    c = c_ref[...]
    cond = c * _sigmoid(c)
    hi, lo = _split_bf16(cond)
    lhs = jnp.concatenate([hi, lo], axis=0)
    r = jnp.dot(lhs, w_ref[...].astype(BF16), preferred_element_type=F32)
    o_ref[...] = r[:8] + r[8:] + b_ref[...]


def _adaln(c8, w, b):
    d, n = w.shape
    tn = 1024
    return pl.pallas_call(
        _adaln_kernel,
        out_shape=jax.ShapeDtypeStruct((8, n), F32),
        grid=(n // tn,),
        in_specs=[
            pl.BlockSpec((8, d), lambda j: (0, 0)),
            pl.BlockSpec((d, tn), lambda j: (0, j)),
            pl.BlockSpec((1, tn), lambda j: (0, j)),
        ],
        out_specs=pl.BlockSpec((8, tn), lambda j: (0, j)),
        compiler_params=_cparams(("arbitrary",)),
        name="adaln",
    )(c8, w, b)


def _inproj_kernel(x_ref, g_ref, sc_ref, sh_ref, w_ref, o_ref, u_scr, *, tm):
    @pl.when(pl.program_id(1) == 0)
    def _():
        g = g_ref[...]
        sc = 1.0 + sc_ref[0]
        sh = sh_ref[0]

        def body(r, carry):
            rows = pl.ds(pl.multiple_of(r * 128, 128), 128)
            x = x_ref[rows, :]
            ms = jnp.mean(x * x, axis=-1, keepdims=True)
            u = x * lax.rsqrt(ms + EPS) * g * sc + sh
            u_scr[rows, :] = u.astype(BF16)
            return carry

        lax.fori_loop(0, tm // 128, body, 0)

    o_ref[...] = jnp.dot(u_scr[...], w_ref[...], preferred_element_type=F32)


def _inproj(x2, g, mod3, w_bf, seq):
    t, d = x2.shape
    n = w_bf.shape[1]
    tm = min(1024, seq)
    tn = 1280
    per_b = seq // tm
    return pl.pallas_call(
        functools.partial(_inproj_kernel, tm=tm),
        out_shape=jax.ShapeDtypeStruct((t, n), F32),
        grid=(t // tm, n // tn),
        in_specs=[
            pl.BlockSpec((tm, d), lambda i, j: (i, 0)),
            pl.BlockSpec((1, d), lambda i, j: (0, 0)),
            pl.BlockSpec((1, 1, d), lambda i, j: (i // per_b, 0, 1)),
            pl.BlockSpec((1, 1, d), lambda i, j: (i // per_b, 0, 0)),
            pl.BlockSpec((d, tn), lambda i, j: (0, j)),
        ],
        out_specs=pl.BlockSpec((tm, tn), lambda i, j: (i, j)),
        scratch_shapes=[pltpu.VMEM((tm, d), BF16)],
        compiler_params=_cparams(("arbitrary", "arbitrary")),
        name="inproj",
    )(x2, g, mod3, mod3, w_bf)


def _hgrn_consts():
    c = HG_CHUNK
    t = np.arange(c)[:, None]
    j = np.arange(c)[None, :]
    tril = (j <= t).astype(np.float32)
    mats = [tril]
    masks = [np.eye(c, dtype=np.float32)]
    s = c // 2
    while s >= 1:
        ref = (t // (2 * s)) * (2 * s) + s - 1
        if s < HG_ROW_LEVEL:
            mats.append(tril - (j <= ref).astype(np.float32))
        masks.append((((t // s) % 2 == 1) & ((j // s) == (t // s) - 1)).astype(np.float32))
        s //= 2
    w = np.concatenate(mats, 0)
    return np.concatenate([w, w], 1), np.stack(masks, 0)


def _hgrn_kernel(q_ref, f_ref, i_ref, g_ref, lb_ref, og_ref, w_ref, m_ref, o_ref, st_scr, *, seq):
    c = HG_CHUNK
    n_lvl = m_ref.shape[0]
    st_scr[...] = jnp.zeros_like(st_scr)
    lb = lb_ref[...]
    og = og_ref[...]

    def chunk(ci, carry):
        rows = pl.ds(pl.multiple_of(ci * c, c), c)
        q = q_ref[rows, :]
        v = i_ref[rows, :]
        g = g_ref[rows, :]
        f = lb + (1.0 - lb) * _sigmoid(f_ref[rows, :])
        lf = jnp.log(f)
        kk = 1.0 - f
        hi, lo = _split_bf16(lf)
        xs = jnp.dot(w_ref[...], jnp.concatenate([hi, lo], axis=0), preferred_element_type=F32)
        b = xs[0:c]
        bl = b[c - 1:c, :] - b
        st = st_scr[...]
        vb = v.astype(BF16)
        o = lax.dot_general((q * jnp.exp(b)).astype(BF16), st.astype(BF16), NT_DIMS,
                            preferred_element_type=F32)
        attn = m_ref[0] * lax.dot_general(q.astype(BF16), kk.astype(BF16), NT_DIMS,
                                          preferred_element_type=F32)
        s = c // 2
        fine = 1
        for l in range(1, n_lvl):
            if s >= HG_ROW_LEVEL:
                b3 = b.reshape(c // (2 * s), 2 * s, HG_DK)
                x = (b3 - b3[:, s - 1:s, :]).reshape(c, HG_DK)
            else:
                x = xs[fine * c:(fine + 1) * c]
                fine += 1
            s //= 2
            e = jnp.exp(-jnp.abs(x))
            attn = attn + m_ref[l] * lax.dot_general(
                (q * e).astype(BF16), (kk * e).astype(BF16), NT_DIMS, preferred_element_type=F32)
        o = o + jnp.dot(attn.astype(BF16), vb, preferred_element_type=F32)
        khat = (kk * jnp.exp(bl)).astype(BF16)
        st_scr[...] = st * jnp.exp(b[c - 1:c, :]) + lax.dot_general(
            vb, khat, TN_DIMS, preferred_element_type=F32)
        ms = jnp.mean(o * o, axis=-1, keepdims=True)
        y = o * lax.rsqrt(ms + EPS) * og * (g * _sigmoid(g))
        o_ref[rows, :] = y.astype(BF16)
        return carry

    lax.fori_loop(0, seq // c, chunk, 0, unroll=4)


def _hgrn(proj, lb, og, batch, seq):
    t = proj.shape[0]
    wm, mm = _hgrn_consts()
    wm = jnp.asarray(wm, BF16)
    mm = jnp.asarray(mm, F32)
    h = HG_HEADS
    return pl.pallas_call(
        functools.partial(_hgrn_kernel, seq=seq),
        out_shape=jax.ShapeDtypeStruct((t, HG_WIDTH), BF16),
        grid=(batch, h),
        in_specs=[
            pl.BlockSpec((seq, 128), lambda b, i: (b, i)),
            pl.BlockSpec((seq, 128), lambda b, i: (b, h + i)),
            pl.BlockSpec((seq, 128), lambda b, i: (b, 2 * h + i)),
            pl.BlockSpec((seq, 128), lambda b, i: (b, 3 * h + i)),
            pl.BlockSpec((1, 128), lambda b, i: (0, i)),
            pl.BlockSpec((1, 128), lambda b, i: (0, 0)),
            pl.BlockSpec(wm.shape, lambda b, i: (0, 0)),
            pl.BlockSpec(mm.shape, lambda b, i: (0, 0, 0)),
        ],
        out_specs=pl.BlockSpec((seq, 128), lambda b, i: (b, i)),
        scratch_shapes=[pltpu.VMEM((HG_DV, HG_DK), F32)],
        compiler_params=_cparams(("arbitrary", "arbitrary")),
        name="hgrn2",
    )(proj, proj, proj, proj, lb, og, wm, mm)


def _mla_prep_kernel(cq_ref, ckv_ref, kr_ref, pos_ref, qag_ref, kvag_ref, wuq_ref, wukv_ref,
                     gq_ref, gkn_ref, gkr_ref, invf_ref, sgn_ref, cm_ref,
                     q_out, k_out, v_out):
    cq = cq_ref[...]
    cqn = cq * lax.rsqrt(jnp.mean(cq * cq, axis=-1, keepdims=True) + EPS) * qag_ref[...]
    qf = jnp.dot(cqn.astype(BF16), wuq_ref[...], preferred_element_type=F32)
    ckv = ckv_ref[...]
    ckvn = ckv * lax.rsqrt(jnp.mean(ckv * ckv, axis=-1, keepdims=True) + EPS) * kvag_ref[...]
    kvf = jnp.dot(ckvn.astype(BF16), wukv_ref[...], preferred_element_type=F32)

    ang = pos_ref[...].astype(F32) * invf_ref[...]
    cos_t = jnp.cos(ang) * cm_ref[...]
    sin_t = jnp.sin(ang) * sgn_ref[...]

    def rope(r):
        return r * cos_t + (pltpu.roll(r, 96, 1) + pltpu.roll(r, 32, 1)) * sin_t

    kr = kr_ref[...]
    kr_ss = jnp.sum(kr * kr, axis=-1, keepdims=True)
    krr = rope(kr * gkr_ref[...])
    gq = gq_ref[...]
    gkn = gkn_ref[...]
    scale = MLA_QK ** -0.5
    for h in range(MLA_HEADS):
        qh = qf[:, h * MLA_QK_PAD:(h + 1) * MLA_QK_PAD]
        inv = lax.rsqrt(jnp.sum(qh * qh, axis=-1, keepdims=True) / MLA_QK + EPS)
        qn = qh * inv * gq
        qo = jnp.concatenate([qn[:, :MLA_NOPE], rope(qn[:, MLA_NOPE:])], axis=-1) * scale
        q_out[0, h] = qo.astype(BF16)
        kn = kvf[:, h * 256:h * 256 + MLA_NOPE]
        inv = lax.rsqrt((jnp.sum(kn * kn, axis=-1, keepdims=True) + kr_ss) / MLA_QK + EPS)
        k_out[0, h] = jnp.concatenate([kn * inv * gkn, krr * inv], axis=-1).astype(BF16)
        v_out[0, h] = kvf[:, h * 256 + MLA_NOPE:(h + 1) * 256].astype(BF16)


def _mla_prep(proj, pos2, qag, kvag, wuq_p, wukv, gq, gkn, gkr, invf, sgn, cm, batch, seq):
    tm = min(256, seq)
    per_b = seq // tm
    hh = MLA_HEADS
    full = lambda shape: pl.BlockSpec(shape, lambda b, i: tuple(0 for _ in shape))
    return pl.pallas_call(
        _mla_prep_kernel,
        out_shape=(
            jax.ShapeDtypeStruct((batch, hh, seq, MLA_QK_PAD), BF16),
            jax.ShapeDtypeStruct((batch, hh, seq, MLA_QK_PAD), BF16),
            jax.ShapeDtypeStruct((batch, hh, seq, MLA_V), BF16),
        ),
        grid=(batch, per_b),
        in_specs=[
            pl.BlockSpec((tm, MLA_Q_RANK), lambda b, i: (b * per_b + i, COL_CQ // MLA_Q_RANK)),
            pl.BlockSpec((tm, MLA_KV_RANK), lambda b, i: (b * per_b + i, COL_CKV // MLA_KV_RANK)),
            pl.BlockSpec((tm, 128), lambda b, i: (b * per_b + i, COL_KR // 128)),
            pl.BlockSpec((tm, 1), lambda b, i: (b * per_b + i, 0)),
            full((1, MLA_Q_RANK)),
            full((1, MLA_KV_RANK)),
            full(wuq_p.shape),
            full(wukv.shape),
            full((1, MLA_QK_PAD)),
            full((1, 128)),
            full((1, 128)),
            full((1, 128)),
            full((1, 128)),
            full((1, 128)),
        ],
        out_specs=(
            pl.BlockSpec((1, hh, tm, MLA_QK_PAD), lambda b, i: (b, 0, i, 0)),
            pl.BlockSpec((1, hh, tm, MLA_QK_PAD), lambda b, i: (b, 0, i, 0)),
            pl.BlockSpec((1, hh, tm, MLA_V), lambda b, i: (b, 0, i, 0)),
        ),
        compiler_params=_cparams(("arbitrary", "arbitrary")),
        name="mla_prep",
    )(proj, proj, proj, pos2, qag, kvag, wuq_p, wukv, gq, gkn, gkr, invf, sgn, cm)


def _attn_kernel(q_ref, k_ref, v_ref, o_ref, *, tq, seq):
    row = lax.broadcasted_iota(jnp.int32, (tq, tq), 0)
    col = lax.broadcasted_iota(jnp.int32, (tq, tq), 1)
    causal = col <= row
    for i in range(seq // tq):
        lo = i * tq
        q = q_ref[0, 0, lo:lo + tq, :]
        sd = lax.dot_general(q, k_ref[0, 0, lo:lo + tq, :], NT_DIMS, preferred_element_type=F32)
        sd = jnp.where(causal, sd, -jnp.inf)
        m = jnp.max(sd, axis=-1, keepdims=True)
        if i > 0:
            so = lax.dot_general(q, k_ref[0, 0, :lo, :], NT_DIMS, preferred_element_type=F32)
            m = jnp.maximum(m, jnp.max(so, axis=-1, keepdims=True))
        pd = jnp.exp(sd - m)
        l = jnp.sum(pd, axis=-1, keepdims=True)
        o = jnp.dot(pd.astype(BF16), v_ref[0, 0, lo:lo + tq, :], preferred_element_type=F32)
        if i > 0:
            po = jnp.exp(so - m)
            l = l + jnp.sum(po, axis=-1, keepdims=True)
            o = o + jnp.dot(po.astype(BF16), v_ref[0, 0, :lo, :], preferred_element_type=F32)
        o_ref[lo:lo + tq, :] = o / l


def _attention(q, k, v):
    batch, hh, seq, dq = q.shape
    tq = min(256, seq)
    return pl.pallas_call(
        functools.partial(_attn_kernel, tq=tq, seq=seq),
        out_shape=jax.ShapeDtypeStruct((batch * seq, hh * MLA_V), F32),
        grid=(batch, hh),
        in_specs=[
            pl.BlockSpec((1, 1, seq, dq), lambda b, h: (b, h, 0, 0)),
            pl.BlockSpec((1, 1, seq, dq), lambda b, h: (b, h, 0, 0)),
            pl.BlockSpec((1, 1, seq, MLA_V), lambda b, h: (b, h, 0, 0)),
        ],
        out_specs=pl.BlockSpec((seq, MLA_V), lambda b, h: (b, h)),
        compiler_params=_cparams(("arbitrary", "arbitrary")),
        name="mla_attn",
    )(q, k, v)


def _outproj_kernel(ohg_ref, omla_ref, x_ref, gt_ref, sc_ref, sh_ref, og_ref, g2_ref,
                    wo_ref, wrh_ref, wrl_ref, h_out, u_out, lg_out):
    om = omla_ref[...]
    omn = om * lax.rsqrt(jnp.mean(om * om, axis=-1, keepdims=True) + EPS) * og_ref[...]
    y = jnp.dot(ohg_ref[...], wo_ref[:HG_WIDTH, :], preferred_element_type=F32)
    y = y + jnp.dot(omn.astype(BF16), wo_ref[HG_WIDTH:, :], preferred_element_type=F32)
    h1 = x_ref[...] + gt_ref[0] * y
    h_out[...] = h1
    u = h1 * lax.rsqrt(jnp.mean(h1 * h1, axis=-1, keepdims=True) + EPS) * g2_ref[...]
    u = u * (1.0 + sc_ref[0]) + sh_ref[0]
    uh, ul = _split_bf16(u)
    u_out[...] = _rows_to_tiles(uh)
    lg = lax.dot_general(wrh_ref[...], uh, NT_DIMS, preferred_element_type=F32)
    lg = lg + lax.dot_general(wrl_ref[...], uh, NT_DIMS, preferred_element_type=F32)
    lg = lg + lax.dot_general(wrh_ref[...], ul, NT_DIMS, preferred_element_type=F32)
    lg_out[...] = lg


def _outproj(ohg, omla, x2, mod3, og, g2, wo_bf, wr_hi, wr_lo, seq):
    t, d = x2.shape
    tm = min(256, seq)
    per_b = seq // tm
    full = lambda shape: pl.BlockSpec(shape, lambda i: tuple(0 for _ in shape))
    modspec = lambda col: pl.BlockSpec((1, 1, d), lambda i: (i // per_b, 0, col))
    return pl.pallas_call(
        _outproj_kernel,
        out_shape=(
            jax.ShapeDtypeStruct((t, d), F32),
            jax.ShapeDtypeStruct((t, d // 128, 128), BF16),
            jax.ShapeDtypeStruct((N_EXPERTS, t), F32),
        ),
        grid=(t // tm,),
        in_specs=[
            pl.BlockSpec((tm, HG_WIDTH), lambda i: (i, 0)),
            pl.BlockSpec((tm, MLA_WIDTH), lambda i: (i, 0)),
            pl.BlockSpec((tm, d), lambda i: (i, 0)),
            modspec(2),
            modspec(4),
            modspec(3),
            full((1, MLA_WIDTH)),
            full((1, d)),
            full(wo_bf.shape),
            full(wr_hi.shape),
            full(wr_lo.shape),
        ],
        out_specs=(
            pl.BlockSpec((tm, d), lambda i: (i, 0)),
            pl.BlockSpec((tm, d // 128, 128), lambda i: (i, 0, 0)),
            pl.BlockSpec((N_EXPERTS, tm), lambda i: (0, i)),
        ),
        compiler_params=_cparams(("arbitrary",)),
        name="outproj",
    )(ohg, omla, x2, mod3, mod3, mod3, og, g2, wo_bf, wr_hi, wr_lo)


def _route_kernel(lg_ref, bias_ref, tri_ref, ltri_ref, idx_out, w_out, dest_out, info_out,
                  carry, base, *, tt, n_info):
    phase = pl.program_id(0)
    step = pl.program_id(1)

    @pl.when((phase == 0) & (step == 0))
    def _():
        carry[...] = jnp.zeros_like(carry)

    @pl.when((phase == 1) & (step == 0))
    def _():
        cnt = carry[...]
        padded = jnp.floor((cnt + (MOE_BLK - 1)) / MOE_BLK) * MOE_BLK
        hi, lo = _split_bf16(padded)
        ps = jnp.dot(ltri_ref[...], jnp.concatenate([hi, lo], axis=1), preferred_element_type=F32)
        pad_start = ps[:, :128] + ps[:, 128:]
        base[...] = pad_start
        carry[...] = jnp.zeros_like(carry)
        pad_end = pad_start + padded
        reps = n_info // 128
        pe = jnp.concatenate([pad_end] * reps, axis=1)
        starts = (lax.broadcasted_iota(jnp.int32, (N_EXPERTS, n_info), 1) * MOE_BLK).astype(F32)
        blk_e = jnp.sum(jnp.where(pe <= starts, 1.0, 0.0), axis=0, keepdims=True)
        blk_e = jnp.minimum(blk_e, N_EXPERTS - 1.0)
        n_used = pe[N_EXPERTS - 1:N_EXPERTS, :] / MOE_BLK
        diag = (lax.broadcasted_iota(jnp.int32, (N_EXPERTS, n_info), 0)
                == lax.broadcasted_iota(jnp.int32, (N_EXPERTS, n_info), 1))
        end_blk = jnp.sum(jnp.where(diag, pe, 0.0), axis=0, keepdims=True) / MOE_BLK
        pd = jnp.concatenate([padded] * reps, axis=1)
        num_blk = jnp.sum(jnp.where(diag, pd, 0.0), axis=0, keepdims=True) / MOE_BLK
        info_out[...] = jnp.concatenate(
            [blk_e, n_used, end_blk, num_blk, jnp.zeros((4, n_info), F32)],
            axis=0).astype(jnp.int32)

    ne, ng, eg = N_EXPERTS, N_GROUPS, EXPERTS_PER_GROUP
    neg = -jnp.inf
    sc = _sigmoid(lg_ref[...])
    sel = sc + bias_ref[...]
    sel3 = sel.reshape(ng, eg, tt)
    sub = lax.broadcasted_iota(jnp.int32, (ng, eg, tt), 1)
    m1 = jnp.max(sel3, axis=1, keepdims=True)
    first = jnp.min(jnp.where(sel3 == m1, sub, eg), axis=1, keepdims=True)
    m2 = jnp.max(jnp.where(sub == first, neg, sel3), axis=1, keepdims=True)
    gs = (m1 + m2).reshape(ng, tt)

    gio = lax.broadcasted_iota(jnp.int32, (ng, tt), 0)
    gsel = jnp.zeros((ng, tt), F32)
    for _ in range(TOPK_GROUPS):
        m = jnp.max(gs, axis=0, keepdims=True)
        gi = jnp.min(jnp.where(gs == m, gio, ng), axis=0, keepdims=True)
        hit = gio == gi
        gsel = jnp.where(hit, 1.0, gsel)
        gs = jnp.where(hit, neg, gs)
    gfull = jnp.broadcast_to(gsel.reshape(ng, 1, tt), (ng, eg, tt)).reshape(ne, tt)
    selm = jnp.where(gfull > 0.0, sel, neg)

    eio = lax.broadcasted_iota(jnp.int32, (ne, tt), 0)
    chosen = jnp.zeros((ne, tt), F32)
    idxs, ws = [], []
    for _ in range(TOP_K):
        m = jnp.max(selm, axis=0, keepdims=True)
        ei = jnp.min(jnp.where(selm == m, eio, ne), axis=0, keepdims=True)
        hit = eio == ei
        ws.append(jnp.sum(jnp.where(hit, sc, 0.0), axis=0, keepdims=True))
        idxs.append(ei)
        chosen = jnp.where(hit, 1.0, chosen)
        selm = jnp.where(hit, neg, selm)
    wsum = ws[0]
    for k in range(1, TOP_K):
        wsum = wsum + ws[k]

    tot = jnp.sum(chosen, axis=1, keepdims=True)

    @pl.when(phase == 1)
    def _():
        off = base[...][:, :1] + carry[...][:, :1]
        excl = jnp.dot(chosen.astype(BF16), tri_ref[...], preferred_element_type=F32) + off
        for k in range(TOP_K):
            idx_out[k:k + 1, :] = idxs[k]
            w_out[k:k + 1, :] = ws[k] / wsum * ROUTED_SCALE
            dk = jnp.sum(jnp.where(eio == idxs[k], excl, 0.0), axis=0, keepdims=True)
            dest_out[k:k + 1, :] = dk.astype(jnp.int32)

    carry[...] = carry[...] + tot


def _route(logits_t, bias_col, n_blk):
    ne, t = logits_t.shape
    tt = min(512, t)
    n_info = -(-n_blk // 128) * 128
    tri = jnp.asarray(np.triu(np.ones((tt, tt), np.float32), 1), BF16)
    ltri = jnp.asarray(np.tril(np.ones((ne, ne), np.float32), -1), BF16)
    return pl.pallas_call(
        functools.partial(_route_kernel, tt=tt, n_info=n_info),
        out_shape=(
            jax.ShapeDtypeStruct((TOP_K, t), jnp.int32),
            jax.ShapeDtypeStruct((TOP_K, t), F32),
            jax.ShapeDtypeStruct((TOP_K, t), jnp.int32),
            jax.ShapeDtypeStruct((8, n_info), jnp.int32),
        ),
        grid=(2, t // tt),
        in_specs=[
            pl.BlockSpec((ne, tt), lambda p, i: (0, i)),
            pl.BlockSpec((ne, 1), lambda p, i: (0, 0)),
            pl.BlockSpec((tt, tt), lambda p, i: (0, 0)),
            pl.BlockSpec((ne, ne), lambda p, i: (0, 0)),
        ],
        out_specs=(
            pl.BlockSpec((TOP_K, tt), lambda p, i: (0, i * p)),
            pl.BlockSpec((TOP_K, tt), lambda p, i: (0, i * p)),
            pl.BlockSpec((TOP_K, tt), lambda p, i: (0, i * p)),
            pl.BlockSpec((8, n_info), lambda p, i: (0, 0)),
        ),
        scratch_shapes=[pltpu.VMEM((ne, 128), F32), pltpu.VMEM((ne, 128), F32)],
        compiler_params=_cparams(("arbitrary", "arbitrary")),
        name="route",
    )(logits_t, bias_col, tri, ltri)


def _dispatch_kernel(info_ref, dest_ref, u_hbm, xs_hbm, stage, zbuf, lsem, ssem, zsem, *, tm, n_blk):
    i = pl.program_id(0)
    n = pl.num_programs(0)
    slot = i % 2
    n_used = info_ref[1, 0]

    def load(tile, dst_slot):
        return pltpu.make_async_copy(u_hbm.at[pl.ds(tile * tm, tm)], stage.at[dst_slot],
                                     lsem.at[dst_slot])

    def zero_block(blk):
        return pltpu.make_async_copy(zbuf, xs_hbm.at[pl.ds(blk * MOE_BLK, MOE_BLK)], zsem)

    @pl.when(i == 0)
    def _():
        load(0, 0).start()
        zbuf[...] = jnp.zeros_like(zbuf)
        for e in range(N_EXPERTS):
            @pl.when(info_ref[3, e] > 0)
            def _():
                zero_block(info_ref[2, e] - 1).start()

        def zero_unused(j, carry):
            zero_block(j).start()
            return carry
        lax.fori_loop(n_used, n_blk, zero_unused, 0)
        for e in range(N_EXPERTS):
            @pl.when(info_ref[3, e] > 0)
            def _():
                zero_block(0).wait()

        def wait_unused(j, carry):
            zero_block(0).wait()
            return carry
        lax.fori_loop(n_used, n_blk, wait_unused, 0)

    def scatter_wait(src_slot):
        pltpu.make_async_copy(stage.at[src_slot], xs_hbm.at[pl.ds(0, tm)], ssem.at[src_slot]).wait()

    @pl.when(i >= 1)
    def _():
        for _ in range(TOP_K):
            scatter_wait(1 - slot)

    @pl.when(i + 1 < n)
    def _():
        load(i + 1, 1 - slot).start()

    load(i, slot).wait()
    for k in range(TOP_K):
        for r in range(tm):
            pltpu.make_async_copy(stage.at[slot, r], xs_hbm.at[dest_ref[0, k, r]],
                                  ssem.at[slot]).start()

    @pl.when(i == n - 1)
    def _():
        for _ in range(TOP_K):
            scatter_wait(slot)


def _dispatch(info, dest3, u3, n_blk):
    t, nt, _ = u3.shape
    tm = dest3.shape[2]
    return pl.pallas_call(
        functools.partial(_dispatch_kernel, tm=tm, n_blk=n_blk),
        out_shape=jax.ShapeDtypeStruct((n_blk * MOE_BLK, nt, 128), BF16),
        grid_spec=pltpu.PrefetchScalarGridSpec(
            num_scalar_prefetch=1,
            grid=(t // tm,),
            in_specs=[
                pl.BlockSpec((1, TOP_K, tm), lambda i, info: (i, 0, 0), memory_space=pltpu.SMEM),
                pl.BlockSpec(memory_space=pl.ANY),
            ],
            out_specs=pl.BlockSpec(memory_space=pl.ANY),
            scratch_shapes=[
                pltpu.VMEM((2, tm, nt, 128), BF16),
                pltpu.VMEM((MOE_BLK, nt, 128), BF16),
                pltpu.SemaphoreType.DMA((2,)),
                pltpu.SemaphoreType.DMA((2,)),
                pltpu.SemaphoreType.DMA(()),
            ],
        ),
        compiler_params=_cparams(("arbitrary",)),
        name="moe_dispatch",
    )(info, dest3, u3)


def _gmm_kernel(info_ref, xs_ref, wg_hbm, wu_hbm, wd_hbm, y_ref,
                wg_f, wu_f, wd_f, wsem, wg_s, wu_s, wd_s, nswap):
    i = pl.program_id(0)
    n_used = info_ref[1, 0]
    e = info_ref[0, i]
    prev = info_ref[0, jnp.maximum(i - 1, 0)]
    fresh = (i == 0) | (e != prev)

    def fetch(expert, slot):
        return (pltpu.make_async_copy(wg_hbm.at[expert], wg_f.at[slot], wsem.at[slot]),
                pltpu.make_async_copy(wu_hbm.at[expert], wu_f.at[slot], wsem.at[slot]),
                pltpu.make_async_copy(wd_hbm.at[expert], wd_f.at[slot], wsem.at[slot]))

    @pl.when(i == 0)
    def _():
        nswap[0] = 0
        for c in fetch(e, 0):
            c.start()

    @pl.when(fresh & (i < n_used))
    def _():
        slot = nswap[0] % 2
        nswap[0] = nswap[0] + 1
        for c in fetch(e, slot):
            c.wait()
        nxt = info_ref[2, e]

        @pl.when(nxt < n_used)
        def _():
            for c in fetch(info_ref[0, nxt], 1 - slot):
                c.start()

        wg_s[...] = wg_f[slot].astype(BF16)
        wu_s[...] = wu_f[slot].astype(BF16)
        wd_s[...] = wd_f[slot].astype(BF16)

    @pl.when(i < n_used)
    def _():
        xb = _tiles_to_rows(xs_ref[...])
        a = jnp.dot(xb, wg_s[...], preferred_element_type=F32)
        u = jnp.dot(xb, wu_s[...], preferred_element_type=F32)
        hid = (a * _sigmoid(a) * u).astype(BF16)
        y = jnp.dot(hid, wd_s[...], preferred_element_type=F32)
        y_ref[...] = _rows_to_tiles(y.astype(BF16))

    @pl.when(i >= n_used)
    def _():
        y_ref[...] = jnp.zeros_like(y_ref)


def _gmm(info, xs3, w_gate, w_up, w_down):
    p, nt, _ = xs3.shape
    d = nt * 128
    n_blk = p // MOE_BLK
    de = w_gate.shape[-1]
    return pl.pallas_call(
        _gmm_kernel,
        out_shape=jax.ShapeDtypeStruct((p, nt, 128), BF16),
        grid_spec=pltpu.PrefetchScalarGridSpec(
            num_scalar_prefetch=1,
            grid=(n_blk,),
            in_specs=[
                pl.BlockSpec((MOE_BLK, nt, 128),
                             lambda i, info: (jnp.minimum(i, info[1, 0] - 1), 0, 0)),
                pl.BlockSpec(memory_space=pl.ANY),
                pl.BlockSpec(memory_space=pl.ANY),
                pl.BlockSpec(memory_space=pl.ANY),
            ],
            out_specs=pl.BlockSpec((MOE_BLK, nt, 128), lambda i, info: (i, 0, 0)),
            scratch_shapes=[
                pltpu.VMEM((2, d, de), F32),
                pltpu.VMEM((2, d, de), F32),
                pltpu.VMEM((2, de, d), F32),
                pltpu.SemaphoreType.DMA((2,)),
                pltpu.VMEM((d, de), BF16),
                pltpu.VMEM((d, de), BF16),
                pltpu.VMEM((de, d), BF16),
                pltpu.SMEM((1,), jnp.int32),
            ],
        ),
        compiler_params=_cparams(("arbitrary",)),
        name="moe_gmm",
    )(info, xs3, w_gate, w_up, w_down)


def _final_kernel(cur_ref, nxt_ref, u_ref, h_ref, gt_ref, wt_ref, wsg_ref, wsu_ref, wsd_ref, y_hbm,
                  o_ref, ybuf, sem, *, tm):
    i = pl.program_id(0)
    n = pl.num_programs(0)
    slot = i % 2
    n_rows = TOP_K * tm

    def issue(idx_ref, dst_slot):
        for k in range(TOP_K):
            for r in range(tm):
                pltpu.make_async_copy(y_hbm.at[idx_ref[0, k, r]], ybuf.at[dst_slot, k * tm + r],
                                      sem.at[dst_slot]).start()

    def gather_wait(dst_slot):
        pltpu.make_async_copy(y_hbm.at[pl.ds(0, n_rows)], ybuf.at[dst_slot], sem.at[dst_slot]).wait()

    @pl.when(i == 0)
    def _():
        issue(cur_ref, 0)

    issue(nxt_ref, 1 - slot)
    ub = _tiles_to_rows(u_ref[...])
    a = jnp.dot(ub, wsg_ref[...], preferred_element_type=F32)
    b = jnp.dot(ub, wsu_ref[...], preferred_element_type=F32)
    hid = (a * _sigmoid(a) * b).astype(BF16)
    acc = jnp.dot(hid, wsd_ref[...], preferred_element_type=F32)

    gather_wait(slot)
    wt = wt_ref[...]
    for k in range(TOP_K):
        yk = _tiles_to_rows(ybuf[slot, k * tm:(k + 1) * tm])
        acc = acc + yk.astype(F32) * wt[:, k:k + 1]
    o_ref[...] = h_ref[...] + gt_ref[0] * acc

    @pl.when(i == n - 1)
    def _():
        gather_wait(1 - slot)


def _final(dest3, u2, h1, mod3, wt, wsg, wsu, wsd, y, seq):
    t, d = h1.shape
    tm = dest3.shape[2]
    n = t // tm
    per_b = seq // tm
    full = lambda shape: pl.BlockSpec(shape, lambda i: tuple(0 for _ in shape))
    return pl.pallas_call(
        functools.partial(_final_kernel, tm=tm),
        out_shape=jax.ShapeDtypeStruct((t, d), F32),
        grid=(n,),
        in_specs=[
            pl.BlockSpec((1, TOP_K, tm), lambda i: (i, 0, 0), memory_space=pltpu.SMEM),
            pl.BlockSpec((1, TOP_K, tm), lambda i: (jnp.minimum(i + 1, n - 1), 0, 0),
                         memory_space=pltpu.SMEM),
            pl.BlockSpec((tm, d // 128, 128), lambda i: (i, 0, 0)),
            pl.BlockSpec((tm, d), lambda i: (i, 0)),
            pl.BlockSpec((1, 1, d), lambda i: (i // per_b, 0, 5)),
            pl.BlockSpec((tm, TOP_K), lambda i: (i, 0)),
            full(wsg.shape),
            full(wsu.shape),
            full(wsd.shape),
            pl.BlockSpec(memory_space=pl.ANY),
        ],
        out_specs=pl.BlockSpec((tm, d), lambda i: (i, 0)),
        scratch_shapes=[
            pltpu.VMEM((2, TOP_K * tm, d // 128, 128), BF16),
            pltpu.SemaphoreType.DMA((2,)),
        ],
        compiler_params=_cparams(("arbitrary",)),
        name="final",
    )(dest3, dest3, u2, h1, mod3, wt, wsg, wsu, wsd, y)


def _layer(h2, mod3, positions, lb, p, batch, seq):
    t, d = h2.shape

    w_in_bf = jnp.pad(p["w_in"], ((0, 0), (0, IN_PAD - IN_COLS))).astype(BF16)
    proj = _inproj(h2, p["norm_mix_g"][None, :], mod3, w_in_bf, seq)

    o_hg = _hgrn(proj, lb[None, :], p["hg_out_g"][None, :], batch, seq)

    wuq = p["mla_w_uq"].reshape(MLA_Q_RANK, MLA_HEADS, MLA_QK)
    wuq_p = jnp.pad(wuq, ((0, 0), (0, 0), (0, MLA_QK_PAD - MLA_QK))).reshape(
        MLA_Q_RANK, MLA_HEADS * MLA_QK_PAD).astype(BF16)
    wukv = p["mla_w_ukv"].astype(BF16)
    gq = jnp.pad(p["mla_q_norm_g"], (0, MLA_QK_PAD - MLA_QK))[None, :]
    gkn = p["mla_k_norm_g"][None, :MLA_NOPE]
    gkr = jnp.pad(p["mla_k_norm_g"][MLA_NOPE:], (0, 128 - MLA_ROPE))[None, :]
    half = MLA_ROPE // 2
    inv_freq = ROPE_THETA ** (-jnp.arange(half, dtype=F32) / half)
    zeros64 = jnp.zeros((128 - MLA_ROPE,), F32)
    invf = jnp.concatenate([inv_freq, inv_freq, zeros64])[None, :]
    sgn = jnp.concatenate([-jnp.ones((half,), F32), jnp.ones((half,), F32), zeros64])[None, :]
    cm = jnp.concatenate([jnp.ones((MLA_ROPE,), F32), zeros64])[None, :]
    q, k, v = _mla_prep(proj, positions.reshape(t, 1), p["mla_q_a_g"][None, :],
                        p["mla_kv_a_g"][None, :], wuq_p, wukv, gq, gkn, gkr, invf, sgn, cm,
                        batch, seq)
    o_mla = _attention(q, k, v)

    wr_t = p["w_router"].T
    wr_hi = wr_t.astype(BF16)
    wr_lo = (wr_t - wr_hi.astype(F32)).astype(BF16)
    h1, u2, logits_t = _outproj(o_hg, o_mla, h2, mod3, p["mla_out_g"][None, :],
                                p["norm_ffn_g"][None, :], p["w_out"].astype(BF16),
                                wr_hi, wr_lo, seq)

    n_blk = (t * TOP_K) // MOE_BLK + N_EXPERTS
    _, top_w, dest, info = _route(logits_t, p["router_bias"][:, None], n_blk)

    tm = min(128, seq)
    dest3 = dest.reshape(TOP_K, t // tm, tm).transpose(1, 0, 2)
    xs = _dispatch(info, dest3, u2, n_blk)
    y = _gmm(info, xs, p["w_gate"], p["w_up"], p["w_down"])
    return _final(dest3, u2, h1, mod3, top_w.T, p["ws_gate"].astype(BF16),
                  p["ws_up"].astype(BF16), p["ws_down"].astype(BF16), y, seq)


def kernel(x, c, positions, w_ada, b_ada, norm_mix_g, norm_ffn_g, w_in, hg_lb_logits, hg_out_g,
           mla_q_a_g, mla_w_uq, mla_kv_a_g, mla_w_ukv, mla_q_norm_g, mla_k_norm_g, mla_out_g,
           w_out, w_router, router_bias, w_gate, w_up, w_down, ws_gate, ws_up, ws_down):
    batch, seq, d = x.shape
    depth = w_ada.shape[0]
    lb_all = jnp.cumsum(jax.nn.softmax(hg_lb_logits.astype(F32), axis=0), axis=0)
    c8 = jnp.pad(c, ((0, 8 - batch), (0, 0)))
    h2 = x.reshape(batch * seq, d)
    for l in range(depth):
        mod = _adaln(c8, w_ada[l], b_ada[l][None, :])[:batch]
        mod3 = mod.reshape(batch, 1, 6 * d)
        p = dict(norm_mix_g=norm_mix_g[l], norm_ffn_g=norm_ffn_g[l], w_in=w_in[l],
                 hg_out_g=hg_out_g[l], mla_q_a_g=mla_q_a_g[l], mla_w_uq=mla_w_uq[l],
                 mla_kv_a_g=mla_kv_a_g[l], mla_w_ukv=mla_w_ukv[l], mla_q_norm_g=mla_q_norm_g[l],
                 mla_k_norm_g=mla_k_norm_g[l], mla_out_g=mla_out_g[l], w_out=w_out[l],
                 w_router=w_router[l], router_bias=router_bias[l], w_gate=w_gate[l],
                 w_up=w_up[l], w_down=w_down[l], ws_gate=ws_gate[l], ws_up=ws_up[l],
                 ws_down=ws_down[l])
        h2 = _layer(h2, mod3, positions, lb_all[l], p, batch, seq)
    return h2.reshape(batch, seq, d)
```

```python
import functools

import numpy as np
import jax
import jax.numpy as jnp
from jax import lax
from jax.experimental import pallas as pl
from jax.experimental.pallas import tpu as pltpu

F32 = jnp.float32
BF16 = jnp.bfloat16

D_MODEL = 2048
EPS = 1e-6

HG_HEADS = 8
HG_DK = 128
HG_DV = 128
HG_WIDTH = HG_HEADS * HG_DV
HG_CHUNK = 128
HG_ROW_LEVEL = 8

MLA_HEADS = 8
MLA_Q_RANK = 512
MLA_KV_RANK = 256
MLA_NOPE = 128
MLA_ROPE = 64
MLA_QK = MLA_NOPE + MLA_ROPE
MLA_QK_PAD = 256
MLA_V = 128
MLA_WIDTH = MLA_HEADS * MLA_V
ROPE_THETA = 10000.0

IN_COLS = 4928
IN_PAD = 5120
COL_CQ = 4096
COL_CKV = 4608
COL_KR = 4864

N_EXPERTS = 64
TOP_K = 8
N_GROUPS = 8
TOPK_GROUPS = 4
EXPERTS_PER_GROUP = N_EXPERTS // N_GROUPS
D_EXPERT = 512
ROUTED_SCALE = 2.5
MOE_BLK = 256

VMEM_LIMIT = 56 * 1024 * 1024

NT_DIMS = (((1,), (1,)), ((), ()))
TN_DIMS = (((0,), (0,)), ((), ()))


def _cparams(sem):
    return pltpu.CompilerParams(dimension_semantics=sem, vmem_limit_bytes=VMEM_LIMIT)


def _sigmoid(x):
    return 1.0 / (1.0 + jnp.exp(-x))


def _split_bf16(x):
    hi = x.astype(BF16)
    lo = (x - hi.astype(F32)).astype(BF16)
    return hi, lo


def _rows_to_tiles(x):
    n = x.shape[1] // 128
    return jnp.swapaxes(jnp.stack([x[:, s * 128:(s + 1) * 128] for s in range(n)], axis=0), 0, 1)


def _tiles_to_rows(x3):
    xt = jnp.swapaxes(x3, 0, 1)
    return jnp.concatenate([xt[s] for s in range(x3.shape[1])], axis=1)


def _adaln_kernel(c_ref, w_ref, b_ref, o_ref):
    c = c_ref[...]
    cond = c * _sigmoid(c)
    hi, lo = _split_bf16(cond)
    lhs = jnp.concatenate([hi, lo], axis=0)
    r = jnp.dot(lhs, w_ref[...].astype(BF16), preferred_element_type=F32)
    o_ref[...] = r[:8] + r[8:] + b_ref[...]


def _adaln(c8, w, b):
    d, n = w.shape
    tn = 1024
    return pl.pallas_call(
        _adaln_kernel,
        out_shape=jax.ShapeDtypeStruct((8, n), F32),
        grid=(n // tn,),
        in_specs=[
            pl.BlockSpec((8, d), lambda j: (0, 0)),
            pl.BlockSpec((d, tn), lambda j: (0, j)),
            pl.BlockSpec((1, tn), lambda j: (0, j)),
        ],
        out_specs=pl.BlockSpec((8, tn), lambda j: (0, j)),
        compiler_params=_cparams(("arbitrary",)),
        name="adaln",
    )(c8, w, b)


def _inproj_kernel(x_ref, g_ref, sc_ref, sh_ref, w_ref, o_ref, u_scr, *, tm):
    @pl.when(pl.program_id(1) == 0)
    def _():
        g = g_ref[...]
        sc = 1.0 + sc_ref[0]
        sh = sh_ref[0]

        def body(r, carry):
            rows = pl.ds(pl.multiple_of(r * 128, 128), 128)
            x = x_ref[rows, :]
            ms = jnp.mean(x * x, axis=-1, keepdims=True)
            u = x * lax.rsqrt(ms + EPS) * g * sc + sh
            u_scr[rows, :] = u.astype(BF16)
            return carry

        lax.fori_loop(0, tm // 128, body, 0)

    o_ref[...] = jnp.dot(u_scr[...], w_ref[...], preferred_element_type=F32)


def _inproj(x2, g, mod3, w_bf, seq):
    t, d = x2.shape
    n = w_bf.shape[1]
    tm = min(1024, seq)
    tn = 1280
    per_b = seq // tm
    return pl.pallas_call(
        functools.partial(_inproj_kernel, tm=tm),
        out_shape=jax.ShapeDtypeStruct((t, n), F32),
        grid=(t // tm, n // tn),
        in_specs=[
            pl.BlockSpec((tm, d), lambda i, j: (i, 0)),
            pl.BlockSpec((1, d), lambda i, j: (0, 0)),
            pl.BlockSpec((1, 1, d), lambda i, j: (i // per_b, 0, 1)),
            pl.BlockSpec((1, 1, d), lambda i, j: (i // per_b, 0, 0)),
            pl.BlockSpec((d, tn), lambda i, j: (0, j)),
        ],
        out_specs=pl.BlockSpec((tm, tn), lambda i, j: (i, j)),
        scratch_shapes=[pltpu.VMEM((tm, d), BF16)],
        compiler_params=_cparams(("arbitrary", "arbitrary")),
        name="inproj",
    )(x2, g, mod3, mod3, w_bf)


def _hgrn_consts():
    c = HG_CHUNK
    t = np.arange(c)[:, None]
    j = np.arange(c)[None, :]
    tril = (j <= t).astype(np.float32)
    mats = [tril]
    masks = [np.eye(c, dtype=np.float32)]
    s = c // 2
    while s >= 1:
        ref = (t // (2 * s)) * (2 * s) + s - 1
        if s < HG_ROW_LEVEL:
            mats.append(tril - (j <= ref).astype(np.float32))
        masks.append((((t // s) % 2 == 1) & ((j // s) == (t // s) - 1)).astype(np.float32))
        s //= 2
    w = np.concatenate(mats, 0)
    return np.concatenate([w, w], 1), np.stack(masks, 0)


def _hgrn_kernel(q_ref, f_ref, i_ref, g_ref, lb_ref, og_ref, w_ref, m_ref, o_ref, st_scr, *, seq):
    c = HG_CHUNK
    n_lvl = m_ref.shape[0]
    st_scr[...] = jnp.zeros_like(st_scr)
    lb = lb_ref[...]
    og = og_ref[...]

    def chunk(ci, carry):
        rows = pl.ds(pl.multiple_of(ci * c, c), c)
        q = q_ref[rows, :]
        v = i_ref[rows, :]
        g = g_ref[rows, :]
        f = lb + (1.0 - lb) * _sigmoid(f_ref[rows, :])
        lf = jnp.log(f)
        kk = 1.0 - f
        hi, lo = _split_bf16(lf)
        xs = jnp.dot(w_ref[...], jnp.concatenate([hi, lo], axis=0), preferred_element_type=F32)
        b = xs[0:c]
        bl = b[c - 1:c, :] - b
        st = st_scr[...]
        vb = v.astype(BF16)
        o = lax.dot_general((q * jnp.exp(b)).astype(BF16), st.astype(BF16), NT_DIMS,
                            preferred_element_type=F32)
        attn = m_ref[0] * lax.dot_general(q.astype(BF16), kk.astype(BF16), NT_DIMS,
                                          preferred_element_type=F32)
        s = c // 2
        fine = 1
        for l in range(1, n_lvl):
            if s >= HG_ROW_LEVEL:
                b3 = b.reshape(c // (2 * s), 2 * s, HG_DK)
                x = (b3 - b3[:, s - 1:s, :]).reshape(c, HG_DK)
            else:
                x = xs[fine * c:(fine + 1) * c]
                fine += 1
            s //= 2
            e = jnp.exp(-jnp.abs(x))
            attn = attn + m_ref[l] * lax.dot_general(
                (q * e).astype(BF16), (kk * e).astype(BF16), NT_DIMS, preferred_element_type=F32)
        o = o + jnp.dot(attn.astype(BF16), vb, preferred_element_type=F32)
        khat = (kk * jnp.exp(bl)).astype(BF16)
        st_scr[...] = st * jnp.exp(b[c - 1:c, :]) + lax.dot_general(
            vb, khat, TN_DIMS, preferred_element_type=F32)
        ms = jnp.mean(o * o, axis=-1, keepdims=True)
        y = o * lax.rsqrt(ms + EPS) * og * (g * _sigmoid(g))
        o_ref[rows, :] = y.astype(BF16)
        return carry

    lax.fori_loop(0, seq // c, chunk, 0, unroll=4)


def _hgrn(proj, lb, og, batch, seq):
    t = proj.shape[0]
    wm, mm = _hgrn_consts()
    wm = jnp.asarray(wm, BF16)
    mm = jnp.asarray(mm, F32)
    h = HG_HEADS
    return pl.pallas_call(
        functools.partial(_hgrn_kernel, seq=seq),
        out_shape=jax.ShapeDtypeStruct((t, HG_WIDTH), BF16),
        grid=(batch, h),
        in_specs=[
            pl.BlockSpec((seq, 128), lambda b, i: (b, i)),
            pl.BlockSpec((seq, 128), lambda b, i: (b, h + i)),
            pl.BlockSpec((seq, 128), lambda b, i: (b, 2 * h + i)),
            pl.BlockSpec((seq, 128), lambda b, i: (b, 3 * h + i)),
            pl.BlockSpec((1, 128), lambda b, i: (0, i)),
            pl.BlockSpec((1, 128), lambda b, i: (0, 0)),
            pl.BlockSpec(wm.shape, lambda b, i: (0, 0)),
            pl.BlockSpec(mm.shape, lambda b, i: (0, 0, 0)),
        ],
        out_specs=pl.BlockSpec((seq, 128), lambda b, i: (b, i)),
        scratch_shapes=[pltpu.VMEM((HG_DV, HG_DK), F32)],
        compiler_params=_cparams(("arbitrary", "arbitrary")),
        name="hgrn2",
    )(proj, proj, proj, proj, lb, og, wm, mm)


def _mla_prep_kernel(cq_ref, ckv_ref, kr_ref, pos_ref, qag_ref, kvag_ref, wuq_ref, wukv_ref,
                     gq_ref, gkn_ref, gkr_ref, invf_ref, sgn_ref, cm_ref,
                     q_out, k_out, v_out):
    cq = cq_ref[...]
    cqn = cq * lax.rsqrt(jnp.mean(cq * cq, axis=-1, keepdims=True) + EPS) * qag_ref[...]
    qf = jnp.dot(cqn.astype(BF16), wuq_ref[...], preferred_element_type=F32)
    ckv = ckv_ref[...]
    ckvn = ckv * lax.rsqrt(jnp.mean(ckv * ckv, axis=-1, keepdims=True) + EPS) * kvag_ref[...]
    kvf = jnp.dot(ckvn.astype(BF16), wukv_ref[...], preferred_element_type=F32)

    ang = pos_ref[...].astype(F32) * invf_ref[...]
    cos_t = jnp.cos(ang) * cm_ref[...]
    sin_t = jnp.sin(ang) * sgn_ref[...]

    def rope(r):
        return r * cos_t + (pltpu.roll(r, 96, 1) + pltpu.roll(r, 32, 1)) * sin_t

    kr = kr_ref[...]
    kr_ss = jnp.sum(kr * kr, axis=-1, keepdims=True)
    krr = rope(kr * gkr_ref[...])
    gq = gq_ref[...]
    gkn = gkn_ref[...]
    scale = MLA_QK ** -0.5
    for h in range(MLA_HEADS):
        qh = qf[:, h * MLA_QK_PAD:(h + 1) * MLA_QK_PAD]
        inv = lax.rsqrt(jnp.sum(qh * qh, axis=-1, keepdims=True) / MLA_QK + EPS)
        qn = qh * inv * gq
        qo = jnp.concatenate([qn[:, :MLA_NOPE], rope(qn[:, MLA_NOPE:])], axis=-1) * scale
        q_out[0, h] = qo.astype(BF16)
        kn = kvf[:, h * 256:h * 256 + MLA_NOPE]
        inv = lax.rsqrt((jnp.sum(kn * kn, axis=-1, keepdims=True) + kr_ss) / MLA_QK + EPS)
        k_out[0, h] = jnp.concatenate([kn * inv * gkn, krr * inv], axis=-1).astype(BF16)
        v_out[0, h] = kvf[:, h * 256 + MLA_NOPE:(h + 1) * 256].astype(BF16)


def _mla_prep(proj, pos2, qag, kvag, wuq_p, wukv, gq, gkn, gkr, invf, sgn, cm, batch, seq):
    tm = min(256, seq)
    per_b = seq // tm
    hh = MLA_HEADS
    full = lambda shape: pl.BlockSpec(shape, lambda b, i: tuple(0 for _ in shape))
    return pl.pallas_call(
        _mla_prep_kernel,
        out_shape=(
            jax.ShapeDtypeStruct((batch, hh, seq, MLA_QK_PAD), BF16),
            jax.ShapeDtypeStruct((batch, hh, seq, MLA_QK_PAD), BF16),
            jax.ShapeDtypeStruct((batch, hh, seq, MLA_V), BF16),
        ),
        grid=(batch, per_b),
        in_specs=[
            pl.BlockSpec((tm, MLA_Q_RANK), lambda b, i: (b * per_b + i, COL_CQ // MLA_Q_RANK)),
            pl.BlockSpec((tm, MLA_KV_RANK), lambda b, i: (b * per_b + i, COL_CKV // MLA_KV_RANK)),
            pl.BlockSpec((tm, 128), lambda b, i: (b * per_b + i, COL_KR // 128)),
            pl.BlockSpec((tm, 1), lambda b, i: (b * per_b + i, 0)),
            full((1, MLA_Q_RANK)),
            full((1, MLA_KV_RANK)),
            full(wuq_p.shape),
            full(wukv.shape),
            full((1, MLA_QK_PAD)),
            full((1, 128)),
            full((1, 128)),
            full((1, 128)),
            full((1, 128)),
            full((1, 128)),
        ],
        out_specs=(
            pl.BlockSpec((1, hh, tm, MLA_QK_PAD), lambda b, i: (b, 0, i, 0)),
            pl.BlockSpec((1, hh, tm, MLA_QK_PAD), lambda b, i: (b, 0, i, 0)),
            pl.BlockSpec((1, hh, tm, MLA_V), lambda b, i: (b, 0, i, 0)),
        ),
        compiler_params=_cparams(("arbitrary", "arbitrary")),
        name="mla_prep",
    )(proj, proj, proj, pos2, qag, kvag, wuq_p, wukv, gq, gkn, gkr, invf, sgn, cm)


def _attn_kernel(q_ref, k_ref, v_ref, o_ref, *, tq, seq):
    row = lax.broadcasted_iota(jnp.int32, (tq, tq), 0)
    col = lax.broadcasted_iota(jnp.int32, (tq, tq), 1)
    causal = col <= row
    for i in range(seq // tq):
        lo = i * tq
        q = q_ref[0, 0, lo:lo + tq, :]
        sd = lax.dot_general(q, k_ref[0, 0, lo:lo + tq, :], NT_DIMS, preferred_element_type=F32)
        sd = jnp.where(causal, sd, -jnp.inf)
        m = jnp.max(sd, axis=-1, keepdims=True)
        if i > 0:
            so = lax.dot_general(q, k_ref[0, 0, :lo, :], NT_DIMS, preferred_element_type=F32)
            m = jnp.maximum(m, jnp.max(so, axis=-1, keepdims=True))
        pd = jnp.exp(sd - m)
        l = jnp.sum(pd, axis=-1, keepdims=True)
        o = jnp.dot(pd.astype(BF16), v_ref[0, 0, lo:lo + tq, :], preferred_element_type=F32)
        if i > 0:
            po = jnp.exp(so - m)
            l = l + jnp.sum(po, axis=-1, keepdims=True)
            o = o + jnp.dot(po.astype(BF16), v_ref[0, 0, :lo, :], preferred_element_type=F32)
        o_ref[lo:lo + tq, :] = o / l


def _attention(q, k, v):
    batch, hh, seq, dq = q.shape
    tq = min(256, seq)
    return pl.pallas_call(
        functools.partial(_attn_kernel, tq=tq, seq=seq),
        out_shape=jax.ShapeDtypeStruct((batch * seq, hh * MLA_V), F32),
        grid=(batch, hh),
        in_specs=[
            pl.BlockSpec((1, 1, seq, dq), lambda b, h: (b, h, 0, 0)),
            pl.BlockSpec((1, 1, seq, dq), lambda b, h: (b, h, 0, 0)),
            pl.BlockSpec((1, 1, seq, MLA_V), lambda b, h: (b, h, 0, 0)),
        ],
        out_specs=pl.BlockSpec((seq, MLA_V), lambda b, h: (b, h)),
        compiler_params=_cparams(("arbitrary", "arbitrary")),
        name="mla_attn",
    )(q, k, v)


def _outproj_kernel(ohg_ref, omla_ref, x_ref, gt_ref, sc_ref, sh_ref, og_ref, g2_ref,
                    wo_ref, wrh_ref, wrl_ref, h_out, u_out, lg_out):
    om = omla_ref[...]
    omn = om * lax.rsqrt(jnp.mean(om * om, axis=-1, keepdims=True) + EPS) * og_ref[...]
    y = jnp.dot(ohg_ref[...], wo_ref[:HG_WIDTH, :], preferred_element_type=F32)
    y = y + jnp.dot(omn.astype(BF16), wo_ref[HG_WIDTH:, :], preferred_element_type=F32)
    h1 = x_ref[...] + gt_ref[0] * y
    h_out[...] = h1
    u = h1 * lax.rsqrt(jnp.mean(h1 * h1, axis=-1, keepdims=True) + EPS) * g2_ref[...]
    u = u * (1.0 + sc_ref[0]) + sh_ref[0]
    uh, ul = _split_bf16(u)
    u_out[...] = _rows_to_tiles(uh)
    lg = lax.dot_general(wrh_ref[...], uh, NT_DIMS, preferred_element_type=F32)
    lg = lg + lax.dot_general(wrl_ref[...], uh, NT_DIMS, preferred_element_type=F32)
    lg = lg + lax.dot_general(wrh_ref[...], ul, NT_DIMS, preferred_element_type=F32)
    lg_out[...] = lg


def _outproj(ohg, omla, x2, mod3, og, g2, wo_bf, wr_hi, wr_lo, seq):
    t, d = x2.shape
    tm = min(256, seq)
    per_b = seq // tm
    full = lambda shape: pl.BlockSpec(shape, lambda i: tuple(0 for _ in shape))
    modspec = lambda col: pl.BlockSpec((1, 1, d), lambda i: (i // per_b, 0, col))
    return pl.pallas_call(
        _outproj_kernel,
        out_shape=(
            jax.ShapeDtypeStruct((t, d), F32),
            jax.ShapeDtypeStruct((t, d // 128, 128), BF16),
            jax.ShapeDtypeStruct((N_EXPERTS, t), F32),
        ),
        grid=(t // tm,),
        in_specs=[
            pl.BlockSpec((tm, HG_WIDTH), lambda i: (i, 0)),
            pl.BlockSpec((tm, MLA_WIDTH), lambda i: (i, 0)),
            pl.BlockSpec((tm, d), lambda i: (i, 0)),
            modspec(2),
            modspec(4),
            modspec(3),
            full((1, MLA_WIDTH)),
            full((1, d)),
            full(wo_bf.shape),
            full(wr_hi.shape),
            full(wr_lo.shape),
        ],
        out_specs=(
            pl.BlockSpec((tm, d), lambda i: (i, 0)),
            pl.BlockSpec((tm, d // 128, 128), lambda i: (i, 0, 0)),
            pl.BlockSpec((N_EXPERTS, tm), lambda i: (0, i)),
        ),
        compiler_params=_cparams(("arbitrary",)),
        name="outproj",
    )(ohg, omla, x2, mod3, mod3, mod3, og, g2, wo_bf, wr_hi, wr_lo)


def _route_kernel(lg_ref, bias_ref, tri_ref, ltri_ref, idx_out, w_out, dest_out, info_out,
                  carry, base, *, tt, n_info):
    phase = pl.program_id(0)
    step = pl.program_id(1)

    @pl.when((phase == 0) & (step == 0))
    def _():
        carry[...] = jnp.zeros_like(carry)

    @pl.when((phase == 1) & (step == 0))
    def _():
        cnt = carry[...]
        padded = jnp.floor((cnt + (MOE_BLK - 1)) / MOE_BLK) * MOE_BLK
        hi, lo = _split_bf16(padded)
        ps = jnp.dot(ltri_ref[...], jnp.concatenate([hi, lo], axis=1), preferred_element_type=F32)
        pad_start = ps[:, :128] + ps[:, 128:]
        base[...] = pad_start
        carry[...] = jnp.zeros_like(carry)
        pad_end = pad_start + padded
        reps = n_info // 128
        pe = jnp.concatenate([pad_end] * reps, axis=1)
        starts = (lax.broadcasted_iota(jnp.int32, (N_EXPERTS, n_info), 1) * MOE_BLK).astype(F32)
        blk_e = jnp.sum(jnp.where(pe <= starts, 1.0, 0.0), axis=0, keepdims=True)
        blk_e = jnp.minimum(blk_e, N_EXPERTS - 1.0)
        n_used = pe[N_EXPERTS - 1:N_EXPERTS, :] / MOE_BLK
        diag = (lax.broadcasted_iota(jnp.int32, (N_EXPERTS, n_info), 0)
                == lax.broadcasted_iota(jnp.int32, (N_EXPERTS, n_info), 1))
        end_blk = jnp.sum(jnp.where(diag, pe, 0.0), axis=0, keepdims=True) / MOE_BLK
        pd = jnp.concatenate([padded] * reps, axis=1)
        num_blk = jnp.sum(jnp.where(diag, pd, 0.0), axis=0, keepdims=True) / MOE_BLK
        info_out[...] = jnp.concatenate(
            [blk_e, n_used, end_blk, num_blk, jnp.zeros((4, n_info), F32)],
            axis=0).astype(jnp.int32)

    ne, ng, eg = N_EXPERTS, N_GROUPS, EXPERTS_PER_GROUP
    neg = -jnp.inf
    sc = _sigmoid(lg_ref[...])
    sel = sc + bias_ref[...]
    sel3 = sel.reshape(ng, eg, tt)
    sub = lax.broadcasted_iota(jnp.int32, (ng, eg, tt), 1)
    m1 = jnp.max(sel3, axis=1, keepdims=True)
    first = jnp.min(jnp.where(sel3 == m1, sub, eg), axis=1, keepdims=True)
    m2 = jnp.max(jnp.where(sub == first, neg, sel3), axis=1, keepdims=True)
    gs = (m1 + m2).reshape(ng, tt)

    gio = lax.broadcasted_iota(jnp.int32, (ng, tt), 0)
    gsel = jnp.zeros((ng, tt), F32)
    for _ in range(TOPK_GROUPS):
        m = jnp.max(gs, axis=0, keepdims=True)
        gi = jnp.min(jnp.where(gs == m, gio, ng), axis=0, keepdims=True)
        hit = gio == gi
        gsel = jnp.where(hit, 1.0, gsel)
        gs = jnp.where(hit, neg, gs)
    gfull = jnp.broadcast_to(gsel.reshape(ng, 1, tt), (ng, eg, tt)).reshape(ne, tt)
    selm = jnp.where(gfull > 0.0, sel, neg)

    eio = lax.broadcasted_iota(jnp.int32, (ne, tt), 0)
    chosen = jnp.zeros((ne, tt), F32)
    idxs, ws = [], []
    for _ in range(TOP_K):
        m = jnp.max(selm, axis=0, keepdims=True)
        ei = jnp.min(jnp.where(selm == m, eio, ne), axis=0, keepdims=True)
        hit = eio == ei
        ws.append(jnp.sum(jnp.where(hit, sc, 0.0), axis=0, keepdims=True))
        idxs.append(ei)
        chosen = jnp.where(hit, 1.0, chosen)
        selm = jnp.where(hit, neg, selm)
    wsum = ws[0]
    for k in range(1, TOP_K):
        wsum = wsum + ws[k]

    tot = jnp.sum(chosen, axis=1, keepdims=True)

    @pl.when(phase == 1)
    def _():
        off = base[...][:, :1] + carry[...][:, :1]
        excl = jnp.dot(chosen.astype(BF16), tri_ref[...], preferred_element_type=F32) + off
        for k in range(TOP_K):
            idx_out[k:k + 1, :] = idxs[k]
            w_out[k:k + 1, :] = ws[k] / wsum * ROUTED_SCALE
            dk = jnp.sum(jnp.where(eio == idxs[k], excl, 0.0), axis=0, keepdims=True)
            dest_out[k:k + 1, :] = dk.astype(jnp.int32)

    carry[...] = carry[...] + tot


def _route(logits_t, bias_col, n_blk):
    ne, t = logits_t.shape
    tt = min(512, t)
    n_info = -(-n_blk // 128) * 128
    tri = jnp.asarray(np.triu(np.ones((tt, tt), np.float32), 1), BF16)
    ltri = jnp.asarray(np.tril(np.ones((ne, ne), np.float32), -1), BF16)
    return pl.pallas_call(
        functools.partial(_route_kernel, tt=tt, n_info=n_info),
        out_shape=(
            jax.ShapeDtypeStruct((TOP_K, t), jnp.int32),
            jax.ShapeDtypeStruct((TOP_K, t), F32),
            jax.ShapeDtypeStruct((TOP_K, t), jnp.int32),
            jax.ShapeDtypeStruct((8, n_info), jnp.int32),
        ),
        grid=(2, t // tt),
        in_specs=[
            pl.BlockSpec((ne, tt), lambda p, i: (0, i)),
            pl.BlockSpec((ne, 1), lambda p, i: (0, 0)),
            pl.BlockSpec((tt, tt), lambda p, i: (0, 0)),
            pl.BlockSpec((ne, ne), lambda p, i: (0, 0)),
        ],
        out_specs=(
            pl.BlockSpec((TOP_K, tt), lambda p, i: (0, i * p)),
            pl.BlockSpec((TOP_K, tt), lambda p, i: (0, i * p)),
            pl.BlockSpec((TOP_K, tt), lambda p, i: (0, i * p)),
            pl.BlockSpec((8, n_info), lambda p, i: (0, 0)),
        ),
        scratch_shapes=[pltpu.VMEM((ne, 128), F32), pltpu.VMEM((ne, 128), F32)],
        compiler_params=_cparams(("arbitrary", "arbitrary")),
        name="route",
    )(logits_t, bias_col, tri, ltri)


def _dispatch_kernel(info_ref, dest_ref, u_hbm, xs_hbm, stage, zbuf, lsem, ssem, zsem, *, tm, n_blk):
    i = pl.program_id(0)
    n = pl.num_programs(0)
    slot = i % 2
    n_used = info_ref[1, 0]

    def load(tile, dst_slot):
        return pltpu.make_async_copy(u_hbm.at[pl.ds(tile * tm, tm)], stage.at[dst_slot],
                                     lsem.at[dst_slot])

    def zero_block(blk):
        return pltpu.make_async_copy(zbuf, xs_hbm.at[pl.ds(blk * MOE_BLK, MOE_BLK)], zsem)

    @pl.when(i == 0)
    def _():
        load(0, 0).start()
        zbuf[...] = jnp.zeros_like(zbuf)
        for e in range(N_EXPERTS):
            @pl.when(info_ref[3, e] > 0)
            def _():
                zero_block(info_ref[2, e] - 1).start()

        def zero_unused(j, carry):
            zero_block(j).start()
            return carry
        lax.fori_loop(n_used, n_blk, zero_unused, 0)
        for e in range(N_EXPERTS):
            @pl.when(info_ref[3, e] > 0)
            def _():
                zero_block(0).wait()

        def wait_unused(j, carry):
            zero_block(0).wait()
            return carry
        lax.fori_loop(n_used, n_blk, wait_unused, 0)

    def scatter_wait(src_slot):
        pltpu.make_async_copy(stage.at[src_slot], xs_hbm.at[pl.ds(0, tm)], ssem.at[src_slot]).wait()

    @pl.when(i >= 1)
    def _():
        for _ in range(TOP_K):
            scatter_wait(1 - slot)

    @pl.when(i + 1 < n)
    def _():
        load(i + 1, 1 - slot).start()

    load(i, slot).wait()
    for k in range(TOP_K):
        for r in range(tm):
            pltpu.make_async_copy(stage.at[slot, r], xs_hbm.at[dest_ref[0, k, r]],
                                  ssem.at[slot]).start(priority=r % 2)

    @pl.when(i == n - 1)
    def _():
        for _ in range(TOP_K):
            scatter_wait(slot)


def _dispatch(info, dest3, u3, n_blk):
    t, nt, _ = u3.shape
    tm = dest3.shape[2]
    return pl.pallas_call(
        functools.partial(_dispatch_kernel, tm=tm, n_blk=n_blk),
        out_shape=jax.ShapeDtypeStruct((n_blk * MOE_BLK, nt, 128), BF16),
        grid_spec=pltpu.PrefetchScalarGridSpec(
            num_scalar_prefetch=1,
            grid=(t // tm,),
            in_specs=[
                pl.BlockSpec((1, TOP_K, tm), lambda i, info: (i, 0, 0), memory_space=pltpu.SMEM),
                pl.BlockSpec(memory_space=pl.ANY),
            ],
            out_specs=pl.BlockSpec(memory_space=pl.ANY),
            scratch_shapes=[
                pltpu.VMEM((2, tm, nt, 128), BF16),
                pltpu.VMEM((MOE_BLK, nt, 128), BF16),
                pltpu.SemaphoreType.DMA((2,)),
                pltpu.SemaphoreType.DMA((2,)),
                pltpu.SemaphoreType.DMA(()),
            ],
        ),
        compiler_params=_cparams(("arbitrary",)),
        name="moe_dispatch",
    )(info, dest3, u3)


def _gmm_kernel(info_ref, xs_ref, wg_hbm, wu_hbm, wd_hbm, y_ref,
                wg_f, wu_f, wd_f, wsem, wg_s, wu_s, wd_s, nswap):
    i = pl.program_id(0)
    n_used = info_ref[1, 0]
    e = info_ref[0, i]
    prev = info_ref[0, jnp.maximum(i - 1, 0)]
    fresh = (i == 0) | (e != prev)

    def fetch(expert, slot):
        return (pltpu.make_async_copy(wg_hbm.at[expert], wg_f.at[slot], wsem.at[slot]),
                pltpu.make_async_copy(wu_hbm.at[expert], wu_f.at[slot], wsem.at[slot]),
                pltpu.make_async_copy(wd_hbm.at[expert], wd_f.at[slot], wsem.at[slot]))

    @pl.when(i == 0)
    def _():
        nswap[0] = 0
        for c in fetch(e, 0):
            c.start()

    @pl.when(fresh & (i < n_used))
    def _():
        slot = nswap[0] % 2
        nswap[0] = nswap[0] + 1
        for c in fetch(e, slot):
            c.wait()
        nxt = info_ref[2, e]

        @pl.when(nxt < n_used)
        def _():
            for c in fetch(info_ref[0, nxt], 1 - slot):
                c.start()

        wg_s[...] = wg_f[slot].astype(BF16)
        wu_s[...] = wu_f[slot].astype(BF16)
        wd_s[...] = wd_f[slot].astype(BF16)

    @pl.when(i < n_used)
    def _():
        xb = _tiles_to_rows(xs_ref[...])
        a = jnp.dot(xb, wg_s[...], preferred_element_type=F32)
        u = jnp.dot(xb, wu_s[...], preferred_element_type=F32)
        hid = (a * _sigmoid(a) * u).astype(BF16)
        y = jnp.dot(hid, wd_s[...], preferred_element_type=F32)
        y_ref[...] = _rows_to_tiles(y.astype(BF16))

    @pl.when(i >= n_used)
    def _():
        y_ref[...] = jnp.zeros_like(y_ref)


def _gmm(info, xs3, w_gate, w_up, w_down):
    p, nt, _ = xs3.shape
    d = nt * 128
    n_blk = p // MOE_BLK
    de = w_gate.shape[-1]
    return pl.pallas_call(
        _gmm_kernel,
        out_shape=jax.ShapeDtypeStruct((p, nt, 128), BF16),
        grid_spec=pltpu.PrefetchScalarGridSpec(
            num_scalar_prefetch=1,
            grid=(n_blk,),
            in_specs=[
                pl.BlockSpec((MOE_BLK, nt, 128),
                             lambda i, info: (jnp.minimum(i, info[1, 0] - 1), 0, 0)),
                pl.BlockSpec(memory_space=pl.ANY),
                pl.BlockSpec(memory_space=pl.ANY),
                pl.BlockSpec(memory_space=pl.ANY),
            ],
            out_specs=pl.BlockSpec((MOE_BLK, nt, 128), lambda i, info: (i, 0, 0)),
            scratch_shapes=[
                pltpu.VMEM((2, d, de), F32),
                pltpu.VMEM((2, d, de), F32),
                pltpu.VMEM((2, de, d), F32),
                pltpu.SemaphoreType.DMA((2,)),
                pltpu.VMEM((d, de), BF16),
                pltpu.VMEM((d, de), BF16),
                pltpu.VMEM((de, d), BF16),
                pltpu.SMEM((1,), jnp.int32),
            ],
        ),
        compiler_params=_cparams(("arbitrary",)),
        name="moe_gmm",
    )(info, xs3, w_gate, w_up, w_down)


def _final_kernel(cur_ref, nxt_ref, u_ref, h_ref, gt_ref, wt_ref, wsg_ref, wsu_ref, wsd_ref, y_hbm,
                  o_ref, ybuf, sem, *, tm):
    i = pl.program_id(0)
    n = pl.num_programs(0)
    slot = i % 2
    n_rows = TOP_K * tm

    def issue(idx_ref, dst_slot):
        for k in range(TOP_K):
            for r in range(tm):
                pltpu.make_async_copy(y_hbm.at[idx_ref[0, k, r]], ybuf.at[dst_slot, k * tm + r],
                                      sem.at[dst_slot]).start(priority=r % 2)

    def gather_wait(dst_slot):
        pltpu.make_async_copy(y_hbm.at[pl.ds(0, n_rows)], ybuf.at[dst_slot], sem.at[dst_slot]).wait()

    @pl.when(i == 0)
    def _():
        issue(cur_ref, 0)

    issue(nxt_ref, 1 - slot)
    ub = _tiles_to_rows(u_ref[...])
    a = jnp.dot(ub, wsg_ref[...], preferred_element_type=F32)
    b = jnp.dot(ub, wsu_ref[...], preferred_element_type=F32)
    hid = (a * _sigmoid(a) * b).astype(BF16)
    acc = jnp.dot(hid, wsd_ref[...], preferred_element_type=F32)

    gather_wait(slot)
    wt = wt_ref[...]
    for k in range(TOP_K):
        yk = _tiles_to_rows(ybuf[slot, k * tm:(k + 1) * tm])
        acc = acc + yk.astype(F32) * wt[:, k:k + 1]
    o_ref[...] = h_ref[...] + gt_ref[0] * acc

    @pl.when(i == n - 1)
    def _():
        gather_wait(1 - slot)


def _final(dest3, u2, h1, mod3, wt, wsg, wsu, wsd, y, seq):
    t, d = h1.shape
    tm = dest3.shape[2]
    n = t // tm
    per_b = seq // tm
    full = lambda shape: pl.BlockSpec(shape, lambda i: tuple(0 for _ in shape))
    return pl.pallas_call(
        functools.partial(_final_kernel, tm=tm),
        out_shape=jax.ShapeDtypeStruct((t, d), F32),
        grid=(n,),
        in_specs=[
            pl.BlockSpec((1, TOP_K, tm), lambda i: (i, 0, 0), memory_space=pltpu.SMEM),
            pl.BlockSpec((1, TOP_K, tm), lambda i: (jnp.minimum(i + 1, n - 1), 0, 0),
                         memory_space=pltpu.SMEM),
            pl.BlockSpec((tm, d // 128, 128), lambda i: (i, 0, 0)),
            pl.BlockSpec((tm, d), lambda i: (i, 0)),
            pl.BlockSpec((1, 1, d), lambda i: (i // per_b, 0, 5)),
            pl.BlockSpec((tm, TOP_K), lambda i: (i, 0)),
            full(wsg.shape),
            full(wsu.shape),
            full(wsd.shape),
            pl.BlockSpec(memory_space=pl.ANY),
        ],
        out_specs=pl.BlockSpec((tm, d), lambda i: (i, 0)),
        scratch_shapes=[
            pltpu.VMEM((2, TOP_K * tm, d // 128, 128), BF16),
            pltpu.SemaphoreType.DMA((2,)),
        ],
        compiler_params=_cparams(("arbitrary",)),
        name="final",
    )(dest3, dest3, u2, h1, mod3, wt, wsg, wsu, wsd, y)


def _layer(h2, mod3, positions, lb, p, batch, seq):
    t, d = h2.shape

    w_in_bf = jnp.pad(p["w_in"], ((0, 0), (0, IN_PAD - IN_COLS))).astype(BF16)
    proj = _inproj(h2, p["norm_mix_g"][None, :], mod3, w_in_bf, seq)

    o_hg = _hgrn(proj, lb[None, :], p["hg_out_g"][None, :], batch, seq)

    wuq = p["mla_w_uq"].reshape(MLA_Q_RANK, MLA_HEADS, MLA_QK)
    wuq_p = jnp.pad(wuq, ((0, 0), (0, 0), (0, MLA_QK_PAD - MLA_QK))).reshape(
        MLA_Q_RANK, MLA_HEADS * MLA_QK_PAD).astype(BF16)
    wukv = p["mla_w_ukv"].astype(BF16)
    gq = jnp.pad(p["mla_q_norm_g"], (0, MLA_QK_PAD - MLA_QK))[None, :]
    gkn = p["mla_k_norm_g"][None, :MLA_NOPE]
    gkr = jnp.pad(p["mla_k_norm_g"][MLA_NOPE:], (0, 128 - MLA_ROPE))[None, :]
    half = MLA_ROPE // 2
    inv_freq = ROPE_THETA ** (-jnp.arange(half, dtype=F32) / half)
    zeros64 = jnp.zeros((128 - MLA_ROPE,), F32)
    invf = jnp.concatenate([inv_freq, inv_freq, zeros64])[None, :]
    sgn = jnp.concatenate([-jnp.ones((half,), F32), jnp.ones((half,), F32), zeros64])[None, :]
    cm = jnp.concatenate([jnp.ones((MLA_ROPE,), F32), zeros64])[None, :]
    q, k, v = _mla_prep(proj, positions.reshape(t, 1), p["mla_q_a_g"][None, :],
                        p["mla_kv_a_g"][None, :], wuq_p, wukv, gq, gkn, gkr, invf, sgn, cm,
                        batch, seq)
    o_mla = _attention(q, k, v)

    wr_t = p["w_router"].T
    wr_hi = wr_t.astype(BF16)
    wr_lo = (wr_t - wr_hi.astype(F32)).astype(BF16)
    h1, u2, logits_t = _outproj(o_hg, o_mla, h2, mod3, p["mla_out_g"][None, :],
                                p["norm_ffn_g"][None, :], p["w_out"].astype(BF16),
                                wr_hi, wr_lo, seq)

    n_blk = (t * TOP_K) // MOE_BLK + N_EXPERTS
    _, top_w, dest, info = _route(logits_t, p["router_bias"][:, None], n_blk)

    tm = min(128, seq)
    dest3 = dest.reshape(TOP_K, t // tm, tm).transpose(1, 0, 2)
    xs = _dispatch(info, dest3, u2, n_blk)
    y = _gmm(info, xs, p["w_gate"], p["w_up"], p["w_down"])
    return _final(dest3, u2, h1, mod3, top_w.T, p["ws_gate"].astype(BF16),
                  p["ws_up"].astype(BF16), p["ws_down"].astype(BF16), y, seq)


def kernel(x, c, positions, w_ada, b_ada, norm_mix_g, norm_ffn_g, w_in, hg_lb_logits, hg_out_g,
           mla_q_a_g, mla_w_uq, mla_kv_a_g, mla_w_ukv, mla_q_norm_g, mla_k_norm_g, mla_out_g,
           w_out, w_router, router_bias, w_gate, w_up, w_down, ws_gate, ws_up, ws_down):
    batch, seq, d = x.shape
    depth = w_ada.shape[0]
    lb_all = jnp.cumsum(jax.nn.softmax(hg_lb_logits.astype(F32), axis=0), axis=0)
    c8 = jnp.pad(c, ((0, 8 - batch), (0, 0)))
    h2 = x.reshape(batch * seq, d)
    for l in range(depth):
        mod = _adaln(c8, w_ada[l], b_ada[l][None, :])[:batch]
        mod3 = mod.reshape(batch, 1, 6 * d)
        p = dict(norm_mix_g=norm_mix_g[l], norm_ffn_g=norm_ffn_g[l], w_in=w_in[l],
                 hg_out_g=hg_out_g[l], mla_q_a_g=mla_q_a_g[l], mla_w_uq=mla_w_uq[l],
                 mla_kv_a_g=mla_kv_a_g[l], mla_w_ukv=mla_w_ukv[l], mla_q_norm_g=mla_q_norm_g[l],
                 mla_k_norm_g=mla_k_norm_g[l], mla_out_g=mla_out_g[l], w_out=w_out[l],
                 w_router=w_router[l], router_bias=router_bias[l], w_gate=w_gate[l],
                 w_up=w_up[l], w_down=w_down[l], ws_gate=ws_gate[l], ws_up=ws_up[l],
                 ws_down=ws_down[l])
        h2 = _layer(h2, mod3, positions, lb_all[l], p, batch, seq)
    return h2.reshape(batch, seq, d)
```

```python
import functools

import numpy as np
import jax
import jax.numpy as jnp
from jax import lax
from jax.experimental import pallas as pl
from jax.experimental.pallas import tpu as pltpu

F32 = jnp.float32
BF16 = jnp.bfloat16

D_MODEL = 2048
EPS = 1e-6

HG_HEADS = 8
HG_DK = 128
HG_DV = 128
HG_WIDTH = HG_HEADS * HG_DV
HG_CHUNK = 128
HG_ROW_LEVEL = 8

MLA_HEADS = 8
MLA_Q_RANK = 512
MLA_KV_RANK = 256
MLA_NOPE = 128
MLA_ROPE = 64
MLA_QK = MLA_NOPE + MLA_ROPE
MLA_QK_PAD = 256
MLA_V = 128
MLA_WIDTH = MLA_HEADS * MLA_V
ROPE_THETA = 10000.0

IN_COLS = 4928
IN_PAD = 5120
COL_CQ = 4096
COL_CKV = 4608
COL_KR = 4864

N_EXPERTS = 64
TOP_K = 8
N_GROUPS = 8
TOPK_GROUPS = 4
EXPERTS_PER_GROUP = N_EXPERTS // N_GROUPS
D_EXPERT = 512
ROUTED_SCALE = 2.5
MOE_BLK = 256

VMEM_LIMIT = 56 * 1024 * 1024

NT_DIMS = (((1,), (1,)), ((), ()))
TN_DIMS = (((0,), (0,)), ((), ()))


def _cparams(sem):
    return pltpu.CompilerParams(dimension_semantics=sem, vmem_limit_bytes=VMEM_LIMIT)


def _sigmoid(x):
    return 1.0 / (1.0 + jnp.exp(-x))


def _split_bf16(x):
    hi = x.astype(BF16)
    lo = (x - hi.astype(F32)).astype(BF16)
    return hi, lo


def _rows_to_tiles(x):
    n = x.shape[1] // 128
    return jnp.swapaxes(jnp.stack([x[:, s * 128:(s + 1) * 128] for s in range(n)], axis=0), 0, 1)


def _tiles_to_rows(x3):
    xt = jnp.swapaxes(x3, 0, 1)
    return jnp.concatenate([xt[s] for s in range(x3.shape[1])], axis=1)


def _adaln_kernel(c_ref, w_ref, b_ref, o_ref):
    c = c_ref[...]
    cond = c * _sigmoid(c)
    hi, lo = _split_bf16(cond)
    lhs = jnp.concatenate([hi, lo], axis=0)
    r = jnp.dot(lhs, w_ref[...].astype(BF16), preferred_element_type=F32)
    o_ref[...] = r[:8] + r[8:] + b_ref[...]


def _adaln(c8, w, b):
    d, n = w.shape
    tn = 1024
    return pl.pallas_call(
        _adaln_kernel,
        out_shape=jax.ShapeDtypeStruct((8, n), F32),
        grid=(n // tn,),
        in_specs=[
            pl.BlockSpec((8, d), lambda j: (0, 0)),
            pl.BlockSpec((d, tn), lambda j: (0, j)),
            pl.BlockSpec((1, tn), lambda j: (0, j)),
        ],
        out_specs=pl.BlockSpec((8, tn), lambda j: (0, j)),
        compiler_params=_cparams(("arbitrary",)),
        name="adaln",
    )(c8, w, b)


def _inproj_kernel(x_ref, g_ref, sc_ref, sh_ref, w_ref, o_ref, u_scr, *, tm):
    @pl.when(pl.program_id(1) == 0)
    def _():
        g = g_ref[...]
        sc = 1.0 + sc_ref[0]
        sh = sh_ref[0]

        def body(r, carry):
            rows = pl.ds(pl.multiple_of(r * 128, 128), 128)
            x = x_ref[rows, :]
            ms = jnp.mean(x * x, axis=-1, keepdims=True)
            u = x * lax.rsqrt(ms + EPS) * g * sc + sh
            u_scr[rows, :] = u.astype(BF16)
            return carry

        lax.fori_loop(0, tm // 128, body, 0)

    o_ref[...] = jnp.dot(u_scr[...], w_ref[...], preferred_element_type=F32)


def _inproj(x2, g, mod3, w_bf, seq):
    t, d = x2.shape
    n = w_bf.shape[1]
    tm = min(1024, seq)
    tn = 1280
    per_b = seq // tm
    return pl.pallas_call(
        functools.partial(_inproj_kernel, tm=tm),
        out_shape=jax.ShapeDtypeStruct((t, n), F32),
        grid=(t // tm, n // tn),
        in_specs=[
            pl.BlockSpec((tm, d), lambda i, j: (i, 0)),
            pl.BlockSpec((1, d), lambda i, j: (0, 0)),
            pl.BlockSpec((1, 1, d), lambda i, j: (i // per_b, 0, 1)),
            pl.BlockSpec((1, 1, d), lambda i, j: (i // per_b, 0, 0)),
            pl.BlockSpec((d, tn), lambda i, j: (0, j)),
        ],
        out_specs=pl.BlockSpec((tm, tn), lambda i, j: (i, j)),
        scratch_shapes=[pltpu.VMEM((tm, d), BF16)],
        compiler_params=_cparams(("arbitrary", "arbitrary")),
        name="inproj",
    )(x2, g, mod3, mod3, w_bf)


def _hgrn_consts():
    c = HG_CHUNK
    t = np.arange(c)[:, None]
    j = np.arange(c)[None, :]
    tril = (j <= t).astype(np.float32)
    mats = [tril]
    masks = [np.eye(c, dtype=np.float32)]
    s = c // 2
    while s >= 1:
        ref = (t // (2 * s)) * (2 * s) + s - 1
        if s < HG_ROW_LEVEL:
            mats.append(tril - (j <= ref).astype(np.float32))
        masks.append((((t // s) % 2 == 1) & ((j // s) == (t // s) - 1)).astype(np.float32))
        s //= 2
    w = np.concatenate(mats, 0)
    return np.concatenate([w, w], 1), np.stack(masks, 0)


def _hgrn_kernel(q_ref, f_ref, i_ref, g_ref, lb_ref, og_ref, w_ref, m_ref, o_ref, st_scr, *, seq):
    c = HG_CHUNK
    n_lvl = m_ref.shape[0]
    st_scr[...] = jnp.zeros_like(st_scr)
    lb = lb_ref[...]
    og = og_ref[...]

    def chunk(ci, carry):
        rows = pl.ds(pl.multiple_of(ci * c, c), c)
        q = q_ref[rows, :]
        v = i_ref[rows, :]
        g = g_ref[rows, :]
        f = lb + (1.0 - lb) * _sigmoid(f_ref[rows, :])
        lf = jnp.log(f)
        kk = 1.0 - f
        hi, lo = _split_bf16(lf)
        xs = jnp.dot(w_ref[...], jnp.concatenate([hi, lo], axis=0), preferred_element_type=F32)
        b = xs[0:c]
        bl = b[c - 1:c, :] - b
        st = st_scr[...]
        vb = v.astype(BF16)
        o = lax.dot_general((q * jnp.exp(b)).astype(BF16), st.astype(BF16), NT_DIMS,
                            preferred_element_type=F32)
        attn = m_ref[0] * lax.dot_general(q.astype(BF16), kk.astype(BF16), NT_DIMS,
                                          preferred_element_type=F32)
        s = c // 2
        fine = 1
        for l in range(1, n_lvl):
            if s >= HG_ROW_LEVEL:
                b3 = b.reshape(c // (2 * s), 2 * s, HG_DK)
                x = (b3 - b3[:, s - 1:s, :]).reshape(c, HG_DK)
            else:
                x = xs[fine * c:(fine + 1) * c]
                fine += 1
            s //= 2
            e = jnp.exp(-jnp.abs(x))
            attn = attn + m_ref[l] * lax.dot_general(
                (q * e).astype(BF16), (kk * e).astype(BF16), NT_DIMS, preferred_element_type=F32)
        o = o + jnp.dot(attn.astype(BF16), vb, preferred_element_type=F32)
        khat = (kk * jnp.exp(bl)).astype(BF16)
        st_scr[...] = st * jnp.exp(b[c - 1:c, :]) + lax.dot_general(
            vb, khat, TN_DIMS, preferred_element_type=F32)
        ms = jnp.mean(o * o, axis=-1, keepdims=True)
        y = o * lax.rsqrt(ms + EPS) * og * (g * _sigmoid(g))
        o_ref[rows, :] = y.astype(BF16)
        return carry

    lax.fori_loop(0, seq // c, chunk, 0, unroll=4)


def _hgrn(proj, lb, og, batch, seq):
    t = proj.shape[0]
    wm, mm = _hgrn_consts()
    wm = jnp.asarray(wm, BF16)
    mm = jnp.asarray(mm, F32)
    h = HG_HEADS
    return pl.pallas_call(
        functools.partial(_hgrn_kernel, seq=seq),
        out_shape=jax.ShapeDtypeStruct((t, HG_WIDTH), BF16),
        grid=(batch, h),
        in_specs=[
            pl.BlockSpec((seq, 128), lambda b, i: (b, i)),
            pl.BlockSpec((seq, 128), lambda b, i: (b, h + i)),
            pl.BlockSpec((seq, 128), lambda b, i: (b, 2 * h + i)),
            pl.BlockSpec((seq, 128), lambda b, i: (b, 3 * h + i)),
            pl.BlockSpec((1, 128), lambda b, i: (0, i)),
            pl.BlockSpec((1, 128), lambda b, i: (0, 0)),
            pl.BlockSpec(wm.shape, lambda b, i: (0, 0)),
            pl.BlockSpec(mm.shape, lambda b, i: (0, 0, 0)),
        ],
        out_specs=pl.BlockSpec((seq, 128), lambda b, i: (b, i)),
        scratch_shapes=[pltpu.VMEM((HG_DV, HG_DK), F32)],
        compiler_params=_cparams(("arbitrary", "arbitrary")),
        name="hgrn2",
    )(proj, proj, proj, proj, lb, og, wm, mm)


def _mla_prep_kernel(cq_ref, ckv_ref, kr_ref, pos_ref, qag_ref, kvag_ref, wuq_ref, wukv_ref,
                     gq_ref, gkn_ref, gkr_ref, invf_ref, sgn_ref, cm_ref,
                     q_out, k_out, v_out):
    cq = cq_ref[...]
    cqn = cq * lax.rsqrt(jnp.mean(cq * cq, axis=-1, keepdims=True) + EPS) * qag_ref[...]
    qf = jnp.dot(cqn.astype(BF16), wuq_ref[...], preferred_element_type=F32)
    ckv = ckv_ref[...]
    ckvn = ckv * lax.rsqrt(jnp.mean(ckv * ckv, axis=-1, keepdims=True) + EPS) * kvag_ref[...]
    kvf = jnp.dot(ckvn.astype(BF16), wukv_ref[...], preferred_element_type=F32)

    ang = pos_ref[...].astype(F32) * invf_ref[...]
    cos_t = jnp.cos(ang) * cm_ref[...]
    sin_t = jnp.sin(ang) * sgn_ref[...]

    def rope(r):
        return r * cos_t + (pltpu.roll(r, 96, 1) + pltpu.roll(r, 32, 1)) * sin_t

    kr = kr_ref[...]
    kr_ss = jnp.sum(kr * kr, axis=-1, keepdims=True)
    krr = rope(kr * gkr_ref[...])
    gq = gq_ref[...]
    gkn = gkn_ref[...]
    scale = MLA_QK ** -0.5
    for h in range(MLA_HEADS):
        qh = qf[:, h * MLA_QK_PAD:(h + 1) * MLA_QK_PAD]
        inv = lax.rsqrt(jnp.sum(qh * qh, axis=-1, keepdims=True) / MLA_QK + EPS)
        qn = qh * inv * gq
        qo = jnp.concatenate([qn[:, :MLA_NOPE], rope(qn[:, MLA_NOPE:])], axis=-1) * scale
        q_out[0, h] = qo.astype(BF16)
        kn = kvf[:, h * 256:h * 256 + MLA_NOPE]
        inv = lax.rsqrt((jnp.sum(kn * kn, axis=-1, keepdims=True) + kr_ss) / MLA_QK + EPS)
        k_out[0, h] = jnp.concatenate([kn * inv * gkn, krr * inv], axis=-1).astype(BF16)
        v_out[0, h] = kvf[:, h * 256 + MLA_NOPE:(h + 1) * 256].astype(BF16)


def _mla_prep(proj, pos2, qag, kvag, wuq_p, wukv, gq, gkn, gkr, invf, sgn, cm, batch, seq):
    tm = min(256, seq)
    per_b = seq // tm
    hh = MLA_HEADS
    full = lambda shape: pl.BlockSpec(shape, lambda b, i: tuple(0 for _ in shape))
    return pl.pallas_call(
        _mla_prep_kernel,
        out_shape=(
            jax.ShapeDtypeStruct((batch, hh, seq, MLA_QK_PAD), BF16),
            jax.ShapeDtypeStruct((batch, hh, seq, MLA_QK_PAD), BF16),
            jax.ShapeDtypeStruct((batch, hh, seq, MLA_V), BF16),
        ),
        grid=(batch, per_b),
        in_specs=[
            pl.BlockSpec((tm, MLA_Q_RANK), lambda b, i: (b * per_b + i, COL_CQ // MLA_Q_RANK)),
            pl.BlockSpec((tm, MLA_KV_RANK), lambda b, i: (b * per_b + i, COL_CKV // MLA_KV_RANK)),
            pl.BlockSpec((tm, 128), lambda b, i: (b * per_b + i, COL_KR // 128)),
            pl.BlockSpec((tm, 1), lambda b, i: (b * per_b + i, 0)),
            full((1, MLA_Q_RANK)),
            full((1, MLA_KV_RANK)),
            full(wuq_p.shape),
            full(wukv.shape),
            full((1, MLA_QK_PAD)),
            full((1, 128)),
            full((1, 128)),
            full((1, 128)),
            full((1, 128)),
            full((1, 128)),
        ],
        out_specs=(
            pl.BlockSpec((1, hh, tm, MLA_QK_PAD), lambda b, i: (b, 0, i, 0)),
            pl.BlockSpec((1, hh, tm, MLA_QK_PAD), lambda b, i: (b, 0, i, 0)),
            pl.BlockSpec((1, hh, tm, MLA_V), lambda b, i: (b, 0, i, 0)),
        ),
        compiler_params=_cparams(("arbitrary", "arbitrary")),
        name="mla_prep",
    )(proj, proj, proj, pos2, qag, kvag, wuq_p, wukv, gq, gkn, gkr, invf, sgn, cm)


def _attn_kernel(q_ref, k_ref, v_ref, o_ref, *, tq, seq):
    row = lax.broadcasted_iota(jnp.int32, (tq, tq), 0)
    col = lax.broadcasted_iota(jnp.int32, (tq, tq), 1)
    causal = col <= row
    for i in range(seq // tq):
        lo = i * tq
        q = q_ref[0, 0, lo:lo + tq, :]
        sd = lax.dot_general(q, k_ref[0, 0, lo:lo + tq, :], NT_DIMS, preferred_element_type=F32)
        sd = jnp.where(causal, sd, -jnp.inf)
        m = jnp.max(sd, axis=-1, keepdims=True)
        if i > 0:
            so = lax.dot_general(q, k_ref[0, 0, :lo, :], NT_DIMS, preferred_element_type=F32)
            m = jnp.maximum(m, jnp.max(so, axis=-1, keepdims=True))
        pd = jnp.exp(sd - m)
        l = jnp.sum(pd, axis=-1, keepdims=True)
        o = jnp.dot(pd.astype(BF16), v_ref[0, 0, lo:lo + tq, :], preferred_element_type=F32)
        if i > 0:
            po = jnp.exp(so - m)
            l = l + jnp.sum(po, axis=-1, keepdims=True)
            o = o + jnp.dot(po.astype(BF16), v_ref[0, 0, :lo, :], preferred_element_type=F32)
        o_ref[lo:lo + tq, :] = o / l


def _attention(q, k, v):
    batch, hh, seq, dq = q.shape
    tq = min(256, seq)
    return pl.pallas_call(
        functools.partial(_attn_kernel, tq=tq, seq=seq),
        out_shape=jax.ShapeDtypeStruct((batch * seq, hh * MLA_V), F32),
        grid=(batch, hh),
        in_specs=[
            pl.BlockSpec((1, 1, seq, dq), lambda b, h: (b, h, 0, 0)),
            pl.BlockSpec((1, 1, seq, dq), lambda b, h: (b, h, 0, 0)),
            pl.BlockSpec((1, 1, seq, MLA_V), lambda b, h: (b, h, 0, 0)),
        ],
        out_specs=pl.BlockSpec((seq, MLA_V), lambda b, h: (b, h)),
        compiler_params=_cparams(("arbitrary", "arbitrary")),
        name="mla_attn",
    )(q, k, v)


def _outproj_kernel(ohg_ref, omla_ref, x_ref, gt_ref, sc_ref, sh_ref, og_ref, g2_ref,
                    wo_ref, wrh_ref, wrl_ref, h_out, u_out, lg_out):
    om = omla_ref[...]
    omn = om * lax.rsqrt(jnp.mean(om * om, axis=-1, keepdims=True) + EPS) * og_ref[...]
    y = jnp.dot(ohg_ref[...], wo_ref[:HG_WIDTH, :], preferred_element_type=F32)
    y = y + jnp.dot(omn.astype(BF16), wo_ref[HG_WIDTH:, :], preferred_element_type=F32)
    h1 = x_ref[...] + gt_ref[0] * y
    h_out[...] = h1
    u = h1 * lax.rsqrt(jnp.mean(h1 * h1, axis=-1, keepdims=True) + EPS) * g2_ref[...]
    u = u * (1.0 + sc_ref[0]) + sh_ref[0]
    uh, ul = _split_bf16(u)
    u_out[...] = _rows_to_tiles(uh)
    lg = lax.dot_general(wrh_ref[...], uh, NT_DIMS, preferred_element_type=F32)
    lg = lg + lax.dot_general(wrl_ref[...], uh, NT_DIMS, preferred_element_type=F32)
    lg = lg + lax.dot_general(wrh_ref[...], ul, NT_DIMS, preferred_element_type=F32)
    lg_out[...] = lg


def _outproj(ohg, omla, x2, mod3, og, g2, wo_bf, wr_hi, wr_lo, seq):
    t, d = x2.shape
    tm = min(256, seq)
    per_b = seq // tm
    full = lambda shape: pl.BlockSpec(shape, lambda i: tuple(0 for _ in shape))
    modspec = lambda col: pl.BlockSpec((1, 1, d), lambda i: (i // per_b, 0, col))
    return pl.pallas_call(
        _outproj_kernel,
        out_shape=(
            jax.ShapeDtypeStruct((t, d), F32),
            jax.ShapeDtypeStruct((t, d // 128, 128), BF16),
            jax.ShapeDtypeStruct((N_EXPERTS, t), F32),
        ),
        grid=(t // tm,),
        in_specs=[
            pl.BlockSpec((tm, HG_WIDTH), lambda i: (i, 0)),
            pl.BlockSpec((tm, MLA_WIDTH), lambda i: (i, 0)),
            pl.BlockSpec((tm, d), lambda i: (i, 0)),
            modspec(2),
            modspec(4),
            modspec(3),
            full((1, MLA_WIDTH)),
            full((1, d)),
            full(wo_bf.shape),
            full(wr_hi.shape),
            full(wr_lo.shape),
        ],
        out_specs=(
            pl.BlockSpec((tm, d), lambda i: (i, 0)),
            pl.BlockSpec((tm, d // 128, 128), lambda i: (i, 0, 0)),
            pl.BlockSpec((N_EXPERTS, tm), lambda i: (0, i)),
        ),
        compiler_params=_cparams(("arbitrary",)),
        name="outproj",
    )(ohg, omla, x2, mod3, mod3, mod3, og, g2, wo_bf, wr_hi, wr_lo)


def _route_kernel(lg_ref, bias_ref, tri_ref, ltri_ref, idx_out, w_out, dest_out, info_out,
                  carry, base, *, tt, n_info):
    phase = pl.program_id(0)
    step = pl.program_id(1)

    @pl.when((phase == 0) & (step == 0))
    def _():
        carry[...] = jnp.zeros_like(carry)

    @pl.when((phase == 1) & (step == 0))
    def _():
        cnt = carry[...]
        padded = jnp.floor((cnt + (MOE_BLK - 1)) / MOE_BLK) * MOE_BLK
        hi, lo = _split_bf16(padded)
        ps = jnp.dot(ltri_ref[...], jnp.concatenate([hi, lo], axis=1), preferred_element_type=F32)
        pad_start = ps[:, :128] + ps[:, 128:]
        base[...] = pad_start
        carry[...] = jnp.zeros_like(carry)
        pad_end = pad_start + padded
        reps = n_info // 128
        pe = jnp.concatenate([pad_end] * reps, axis=1)
        starts = (lax.broadcasted_iota(jnp.int32, (N_EXPERTS, n_info), 1) * MOE_BLK).astype(F32)
        blk_e = jnp.sum(jnp.where(pe <= starts, 1.0, 0.0), axis=0, keepdims=True)
        blk_e = jnp.minimum(blk_e, N_EXPERTS - 1.0)
        n_used = pe[N_EXPERTS - 1:N_EXPERTS, :] / MOE_BLK
        diag = (lax.broadcasted_iota(jnp.int32, (N_EXPERTS, n_info), 0)
                == lax.broadcasted_iota(jnp.int32, (N_EXPERTS, n_info), 1))
        end_blk = jnp.sum(jnp.where(diag, pe, 0.0), axis=0, keepdims=True) / MOE_BLK
        pd = jnp.concatenate([padded] * reps, axis=1)
        num_blk = jnp.sum(jnp.where(diag, pd, 0.0), axis=0, keepdims=True) / MOE_BLK
        info_out[...] = jnp.concatenate(
            [blk_e, n_used, end_blk, num_blk, jnp.zeros((4, n_info), F32)],
            axis=0).astype(jnp.int32)

    ne, ng, eg = N_EXPERTS, N_GROUPS, EXPERTS_PER_GROUP
    neg = -jnp.inf
    sc = _sigmoid(lg_ref[...])
    sel = sc + bias_ref[...]
    sel3 = sel.reshape(ng, eg, tt)
    sub = lax.broadcasted_iota(jnp.int32, (ng, eg, tt), 1)
    m1 = jnp.max(sel3, axis=1, keepdims=True)
    first = jnp.min(jnp.where(sel3 == m1, sub, eg), axis=1, keepdims=True)
    m2 = jnp.max(jnp.where(sub == first, neg, sel3), axis=1, keepdims=True)
    gs = (m1 + m2).reshape(ng, tt)

    gio = lax.broadcasted_iota(jnp.int32, (ng, tt), 0)
    gsel = jnp.zeros((ng, tt), F32)
    for _ in range(TOPK_GROUPS):
        m = jnp.max(gs, axis=0, keepdims=True)
        gi = jnp.min(jnp.where(gs == m, gio, ng), axis=0, keepdims=True)
        hit = gio == gi
        gsel = jnp.where(hit, 1.0, gsel)
        gs = jnp.where(hit, neg, gs)
    gfull = jnp.broadcast_to(gsel.reshape(ng, 1, tt), (ng, eg, tt)).reshape(ne, tt)
    selm = jnp.where(gfull > 0.0, sel, neg)

    eio = lax.broadcasted_iota(jnp.int32, (ne, tt), 0)
    chosen = jnp.zeros((ne, tt), F32)
    idxs, ws = [], []
    for _ in range(TOP_K):
        m = jnp.max(selm, axis=0, keepdims=True)
        ei = jnp.min(jnp.where(selm == m, eio, ne), axis=0, keepdims=True)
        hit = eio == ei
        ws.append(jnp.sum(jnp.where(hit, sc, 0.0), axis=0, keepdims=True))
        idxs.append(ei)
        chosen = jnp.where(hit, 1.0, chosen)
        selm = jnp.where(hit, neg, selm)
    wsum = ws[0]
    for k in range(1, TOP_K):
        wsum = wsum + ws[k]

    tot = jnp.sum(chosen, axis=1, keepdims=True)

    @pl.when(phase == 1)
    def _():
        off = base[...][:, :1] + carry[...][:, :1]
        excl = jnp.dot(chosen.astype(BF16), tri_ref[...], preferred_element_type=F32) + off
        for k in range(TOP_K):
            idx_out[k:k + 1, :] = idxs[k]
            w_out[k:k + 1, :] = ws[k] / wsum * ROUTED_SCALE
            dk = jnp.sum(jnp.where(eio == idxs[k], excl, 0.0), axis=0, keepdims=True)
            dest_out[k:k + 1, :] = dk.astype(jnp.int32)

    carry[...] = carry[...] + tot


def _route(logits_t, bias_col, n_blk):
    ne, t = logits_t.shape
    tt = min(512, t)
    n_info = -(-n_blk // 128) * 128
    tri = jnp.asarray(np.triu(np.ones((tt, tt), np.float32), 1), BF16)
    ltri = jnp.asarray(np.tril(np.ones((ne, ne), np.float32), -1), BF16)
    return pl.pallas_call(
        functools.partial(_route_kernel, tt=tt, n_info=n_info),
        out_shape=(
            jax.ShapeDtypeStruct((TOP_K, t), jnp.int32),
            jax.ShapeDtypeStruct((TOP_K, t), F32),
            jax.ShapeDtypeStruct((TOP_K, t), jnp.int32),
            jax.ShapeDtypeStruct((8, n_info), jnp.int32),
        ),
        grid=(2, t // tt),
        in_specs=[
            pl.BlockSpec((ne, tt), lambda p, i: (0, i)),
            pl.BlockSpec((ne, 1), lambda p, i: (0, 0)),
            pl.BlockSpec((tt, tt), lambda p, i: (0, 0)),
            pl.BlockSpec((ne, ne), lambda p, i: (0, 0)),
        ],
        out_specs=(
            pl.BlockSpec((TOP_K, tt), lambda p, i: (0, i * p)),
            pl.BlockSpec((TOP_K, tt), lambda p, i: (0, i * p)),
            pl.BlockSpec((TOP_K, tt), lambda p, i: (0, i * p)),
            pl.BlockSpec((8, n_info), lambda p, i: (0, 0)),
        ),
        scratch_shapes=[pltpu.VMEM((ne, 128), F32), pltpu.VMEM((ne, 128), F32)],
        compiler_params=_cparams(("arbitrary", "arbitrary")),
        name="route",
    )(logits_t, bias_col, tri, ltri)


def _dispatch_kernel(info_ref, dest_ref, u_hbm, xs_hbm, stage, zbuf, lsem, ssem, zsem, *, tm, n_blk):
    i = pl.program_id(0)
    n = pl.num_programs(0)
    slot = i % 2
    n_used = info_ref[1, 0]

    def load(tile, dst_slot):
        return pltpu.make_async_copy(u_hbm.at[pl.ds(tile * tm, tm)], stage.at[dst_slot],
                                     lsem.at[dst_slot])

    def zero_block(blk):
        return pltpu.make_async_copy(zbuf, xs_hbm.at[pl.ds(blk * MOE_BLK, MOE_BLK)], zsem)

    @pl.when(i == 0)
    def _():
        load(0, 0).start()
        zbuf[...] = jnp.zeros_like(zbuf)
        for e in range(N_EXPERTS):
            @pl.when(info_ref[3, e] > 0)
            def _():
                zero_block(info_ref[2, e] - 1).start()

        def zero_unused(j, carry):
            zero_block(j).start()
            return carry
        lax.fori_loop(n_used, n_blk, zero_unused, 0)
        for e in range(N_EXPERTS):
            @pl.when(info_ref[3, e] > 0)
            def _():
                zero_block(0).wait()

        def wait_unused(j, carry):
            zero_block(0).wait()
            return carry
        lax.fori_loop(n_used, n_blk, wait_unused, 0)

    def scatter_wait(src_slot):
        pltpu.make_async_copy(stage.at[src_slot], xs_hbm.at[pl.ds(0, tm)], ssem.at[src_slot]).wait()

    @pl.when(i >= 1)
    def _():
        for _ in range(TOP_K):
            scatter_wait(1 - slot)

    @pl.when(i + 1 < n)
    def _():
        load(i + 1, 1 - slot).start()

    load(i, slot).wait()
    for k in range(TOP_K):
        for r in range(tm):
            pltpu.make_async_copy(stage.at[slot, r], xs_hbm.at[dest_ref[0, k, r]],
                                  ssem.at[slot]).start(priority=r % 2)

    @pl.when(i == n - 1)
    def _():
        for _ in range(TOP_K):
            scatter_wait(slot)


def _dispatch(info, dest3, u3, n_blk):
    t, nt, _ = u3.shape
    tm = dest3.shape[2]
    return pl.pallas_call(
        functools.partial(_dispatch_kernel, tm=tm, n_blk=n_blk),
        out_shape=jax.ShapeDtypeStruct((n_blk * MOE_BLK, nt, 128), BF16),
        grid_spec=pltpu.PrefetchScalarGridSpec(
            num_scalar_prefetch=1,
            grid=(t // tm,),
            in_specs=[
                pl.BlockSpec((1, TOP_K, tm), lambda i, info: (i, 0, 0), memory_space=pltpu.SMEM),
                pl.BlockSpec(memory_space=pl.ANY),
            ],
            out_specs=pl.BlockSpec(memory_space=pl.ANY),
            scratch_shapes=[
                pltpu.VMEM((2, tm, nt, 128), BF16),
                pltpu.VMEM((MOE_BLK, nt, 128), BF16),
                pltpu.SemaphoreType.DMA((2,)),
                pltpu.SemaphoreType.DMA((2,)),
                pltpu.SemaphoreType.DMA(()),
            ],
        ),
        compiler_params=_cparams(("arbitrary",)),
        name="moe_dispatch",
    )(info, dest3, u3)


def _gmm_kernel(info_ref, xs_ref, wg_hbm, wu_hbm, wd_hbm, y_ref,
                wg_f, wu_f, wd_f, wsem, wg_s, wu_s, wd_s, nswap):
    i = pl.program_id(0)
    n_used = info_ref[1, 0]
    e = info_ref[0, i]
    prev = info_ref[0, jnp.maximum(i - 1, 0)]
    fresh = (i == 0) | (e != prev)

    def fetch(expert, slot):
        return (pltpu.make_async_copy(wg_hbm.at[expert], wg_f.at[slot], wsem.at[slot]),
                pltpu.make_async_copy(wu_hbm.at[expert], wu_f.at[slot], wsem.at[slot]),
                pltpu.make_async_copy(wd_hbm.at[expert], wd_f.at[slot], wsem.at[slot]))

    @pl.when(i == 0)
    def _():
        nswap[0] = 0
        for c in fetch(e, 0):
            c.start()

    @pl.when(fresh & (i < n_used))
    def _():
        slot = nswap[0] % 2
        nswap[0] = nswap[0] + 1
        for c in fetch(e, slot):
            c.wait()
        nxt = info_ref[2, e]

        @pl.when(nxt < n_used)
        def _():
            for c in fetch(info_ref[0, nxt], 1 - slot):
                c.start()

        wg_s[...] = wg_f[slot].astype(BF16)
        wu_s[...] = wu_f[slot].astype(BF16)
        wd_s[...] = wd_f[slot].astype(BF16)

    @pl.when(i < n_used)
    def _():
        xb = _tiles_to_rows(xs_ref[...])
        a = jnp.dot(xb, wg_s[...], preferred_element_type=F32)
        u = jnp.dot(xb, wu_s[...], preferred_element_type=F32)
        hid = (a * _sigmoid(a) * u).astype(BF16)
        y = jnp.dot(hid, wd_s[...], preferred_element_type=F32)
        y_ref[...] = _rows_to_tiles(y.astype(BF16))

    @pl.when(i >= n_used)
    def _():
        y_ref[...] = jnp.zeros_like(y_ref)


def _gmm(info, xs3, w_gate, w_up, w_down):
    p, nt, _ = xs3.shape
    d = nt * 128
    n_blk = p // MOE_BLK
    de = w_gate.shape[-1]
    return pl.pallas_call(
        _gmm_kernel,
        out_shape=jax.ShapeDtypeStruct((p, nt, 128), BF16),
        grid_spec=pltpu.PrefetchScalarGridSpec(
            num_scalar_prefetch=1,
            grid=(n_blk,),
            in_specs=[
                pl.BlockSpec((MOE_BLK, nt, 128),
                             lambda i, info: (jnp.minimum(i, info[1, 0] - 1), 0, 0)),
                pl.BlockSpec(memory_space=pl.ANY),
                pl.BlockSpec(memory_space=pl.ANY),
                pl.BlockSpec(memory_space=pl.ANY),
            ],
            out_specs=pl.BlockSpec((MOE_BLK, nt, 128), lambda i, info: (i, 0, 0)),
            scratch_shapes=[
                pltpu.VMEM((2, d, de), F32),
                pltpu.VMEM((2, d, de), F32),
                pltpu.VMEM((2, de, d), F32),
                pltpu.SemaphoreType.DMA((2,)),
                pltpu.VMEM((d, de), BF16),
                pltpu.VMEM((d, de), BF16),
                pltpu.VMEM((de, d), BF16),
                pltpu.SMEM((1,), jnp.int32),
            ],
        ),
        compiler_params=_cparams(("arbitrary",)),
        name="moe_gmm",
    )(info, xs3, w_gate, w_up, w_down)


def _final_kernel(cur_ref, nxt_ref, u_ref, h_ref, gt_ref, wt_ref, wsg_ref, wsu_ref, wsd_ref, y_hbm,
                  o_ref, ybuf_a, ybuf_b, sem, *, tm):
    i = pl.program_id(0)
    n = pl.num_programs(0)
    n_rows = TOP_K * tm

    def issue(idx_ref, tile, buf, sem_i):
        for k in range(TOP_K):
            for r in range(tm):
                pltpu.make_async_copy(y_hbm.at[idx_ref[0, tile * TOP_K + k, r]], buf.at[k * tm + r],
                                      sem.at[sem_i]).start(priority=r % 2)

    def gather_wait(buf, sem_i):
        pltpu.make_async_copy(y_hbm.at[pl.ds(0, n_rows)], buf, sem.at[sem_i]).wait()

    eye = (lax.broadcasted_iota(jnp.int32, (tm, tm), 0)
           == lax.broadcasted_iota(jnp.int32, (tm, tm), 1))

    def compute(tile, buf):
        rows = slice(tile * tm, (tile + 1) * tm)
        ub = _tiles_to_rows(u_ref[rows])
        a = jnp.dot(ub, wsg_ref[...], preferred_element_type=F32)
        b = jnp.dot(ub, wsu_ref[...], preferred_element_type=F32)
        hid = (a * _sigmoid(a) * b).astype(BF16)
        acc = jnp.dot(hid, wsd_ref[...], preferred_element_type=F32)
        wt = wt_ref[rows, :]
        wcat = jnp.concatenate([jnp.where(eye, wt[:, k:k + 1], 0.0) for k in range(TOP_K)],
                               axis=1)
        whi, wlo = _split_bf16(wcat)
        ycat = _tiles_to_rows(buf[...])
        acc = acc + jnp.dot(whi, ycat, preferred_element_type=F32)
        acc = acc + jnp.dot(wlo, ycat, preferred_element_type=F32)
        o_ref[rows, :] = h_ref[rows, :] + gt_ref[0] * acc

    @pl.when(i == 0)
    def _():
        issue(cur_ref, 0, ybuf_a, 0)

    gather_wait(ybuf_a, 0)
    issue(cur_ref, 1, ybuf_b, 1)
    compute(0, ybuf_a)
    gather_wait(ybuf_b, 1)
    issue(nxt_ref, 0, ybuf_a, 0)
    compute(1, ybuf_b)

    @pl.when(i == n - 1)
    def _():
        gather_wait(ybuf_a, 0)


def _final(dest3, u2, h1, mod3, wt, wsg, wsu, wsd, y, seq):
    t, d = h1.shape
    tm = dest3.shape[2]
    n = t // (2 * tm)
    per_b = seq // (2 * tm)
    dest_pairs = dest3.reshape(n, 2 * TOP_K, tm)
    full = lambda shape: pl.BlockSpec(shape, lambda i: tuple(0 for _ in shape))
    return pl.pallas_call(
        functools.partial(_final_kernel, tm=tm),
        out_shape=jax.ShapeDtypeStruct((t, d), F32),
        grid=(n,),
        in_specs=[
            pl.BlockSpec((1, 2 * TOP_K, tm), lambda i: (i, 0, 0), memory_space=pltpu.SMEM),
            pl.BlockSpec((1, 2 * TOP_K, tm), lambda i: (jnp.minimum(i + 1, n - 1), 0, 0),
                         memory_space=pltpu.SMEM),
            pl.BlockSpec((2 * tm, d // 128, 128), lambda i: (i, 0, 0)),
            pl.BlockSpec((2 * tm, d), lambda i: (i, 0)),
            pl.BlockSpec((1, 1, d), lambda i: (i // per_b, 0, 5)),
            pl.BlockSpec((2 * tm, TOP_K), lambda i: (i, 0)),
            full(wsg.shape),
            full(wsu.shape),
            full(wsd.shape),
            pl.BlockSpec(memory_space=pl.ANY),
        ],
        out_specs=pl.BlockSpec((2 * tm, d), lambda i: (i, 0)),
        scratch_shapes=[
            pltpu.VMEM((TOP_K * tm, d // 128, 128), BF16),
            pltpu.VMEM((TOP_K * tm, d // 128, 128), BF16),
            pltpu.SemaphoreType.DMA((2,)),
        ],
        compiler_params=_cparams(("arbitrary",)),
        name="final",
    )(dest_pairs, dest_pairs, u2, h1, mod3, wt, wsg, wsu, wsd, y)


def _layer(h2, mod3, positions, lb, p, batch, seq):
    t, d = h2.shape

    w_in_bf = jnp.pad(p["w_in"], ((0, 0), (0, IN_PAD - IN_COLS))).astype(BF16)
    proj = _inproj(h2, p["norm_mix_g"][None, :], mod3, w_in_bf, seq)

    o_hg = _hgrn(proj, lb[None, :], p["hg_out_g"][None, :], batch, seq)

    wuq = p["mla_w_uq"].reshape(MLA_Q_RANK, MLA_HEADS, MLA_QK)
    wuq_p = jnp.pad(wuq, ((0, 0), (0, 0), (0, MLA_QK_PAD - MLA_QK))).reshape(
        MLA_Q_RANK, MLA_HEADS * MLA_QK_PAD).astype(BF16)
    wukv = p["mla_w_ukv"].astype(BF16)
    gq = jnp.pad(p["mla_q_norm_g"], (0, MLA_QK_PAD - MLA_QK))[None, :]
    gkn = p["mla_k_norm_g"][None, :MLA_NOPE]
    gkr = jnp.pad(p["mla_k_norm_g"][MLA_NOPE:], (0, 128 - MLA_ROPE))[None, :]
    half = MLA_ROPE // 2
    inv_freq = ROPE_THETA ** (-jnp.arange(half, dtype=F32) / half)
    zeros64 = jnp.zeros((128 - MLA_ROPE,), F32)
    invf = jnp.concatenate([inv_freq, inv_freq, zeros64])[None, :]
    sgn = jnp.concatenate([-jnp.ones((half,), F32), jnp.ones((half,), F32), zeros64])[None, :]
    cm = jnp.concatenate([jnp.ones((MLA_ROPE,), F32), zeros64])[None, :]
    q, k, v = _mla_prep(proj, positions.reshape(t, 1), p["mla_q_a_g"][None, :],
                        p["mla_kv_a_g"][None, :], wuq_p, wukv, gq, gkn, gkr, invf, sgn, cm,
                        batch, seq)
    o_mla = _attention(q, k, v)

    wr_t = p["w_router"].T
    wr_hi = wr_t.astype(BF16)
    wr_lo = (wr_t - wr_hi.astype(F32)).astype(BF16)
    h1, u2, logits_t = _outproj(o_hg, o_mla, h2, mod3, p["mla_out_g"][None, :],
                                p["norm_ffn_g"][None, :], p["w_out"].astype(BF16),
                                wr_hi, wr_lo, seq)

    n_blk = (t * TOP_K) // MOE_BLK + N_EXPERTS
    _, top_w, dest, info = _route(logits_t, p["router_bias"][:, None], n_blk)

    tm = min(128, seq)
    dest3 = dest.reshape(TOP_K, t // tm, tm).transpose(1, 0, 2)
    xs = _dispatch(info, dest3, u2, n_blk)
    y = _gmm(info, xs, p["w_gate"], p["w_up"], p["w_down"])
    return _final(dest3, u2, h1, mod3, top_w.T, p["ws_gate"].astype(BF16),
                  p["ws_up"].astype(BF16), p["ws_down"].astype(BF16), y, seq)


def kernel(x, c, positions, w_ada, b_ada, norm_mix_g, norm_ffn_g, w_in, hg_lb_logits, hg_out_g,
           mla_q_a_g, mla_w_uq, mla_kv_a_g, mla_w_ukv, mla_q_norm_g, mla_k_norm_g, mla_out_g,
           w_out, w_router, router_bias, w_gate, w_up, w_down, ws_gate, ws_up, ws_down):
    batch, seq, d = x.shape
    depth = w_ada.shape[0]
    lb_all = jnp.cumsum(jax.nn.softmax(hg_lb_logits.astype(F32), axis=0), axis=0)
    c8 = jnp.pad(c, ((0, 8 - batch), (0, 0)))
    h2 = x.reshape(batch * seq, d)
    for l in range(depth):
        mod = _adaln(c8, w_ada[l], b_ada[l][None, :])[:batch]
        mod3 = mod.reshape(batch, 1, 6 * d)
        p = dict(norm_mix_g=norm_mix_g[l], norm_ffn_g=norm_ffn_g[l], w_in=w_in[l],
                 hg_out_g=hg_out_g[l], mla_q_a_g=mla_q_a_g[l], mla_w_uq=mla_w_uq[l],
                 mla_kv_a_g=mla_kv_a_g[l], mla_w_ukv=mla_w_ukv[l], mla_q_norm_g=mla_q_norm_g[l],
                 mla_k_norm_g=mla_k_norm_g[l], mla_out_g=mla_out_g[l], w_out=w_out[l],
                 w_router=w_router[l], router_bias=router_bias[l], w_gate=w_gate[l],
                 w_up=w_up[l], w_down=w_down[l], ws_gate=ws_gate[l], ws_up=ws_up[l],
                 ws_down=ws_down[l])
        h2 = _layer(h2, mod3, positions, lb_all[l], p, batch, seq)
    return h2.reshape(batch, seq, d)
```

```python
import functools

import numpy as np
import jax
import jax.numpy as jnp
from jax import lax
from jax.experimental import pallas as pl
from jax.experimental.pallas import tpu as pltpu

F32 = jnp.float32
BF16 = jnp.bfloat16

D_MODEL = 2048
EPS = 1e-6

HG_HEADS = 8
HG_DK = 128
HG_DV = 128
HG_WIDTH = HG_HEADS * HG_DV
HG_CHUNK = 128
HG_ROW_LEVEL = 8

MLA_HEADS = 8
MLA_Q_RANK = 512
MLA_KV_RANK = 256
MLA_NOPE = 128
MLA_ROPE = 64
MLA_QK = MLA_NOPE + MLA_ROPE
MLA_QK_PAD = 256
MLA_V = 128
MLA_WIDTH = MLA_HEADS * MLA_V
ROPE_THETA = 10000.0

IN_COLS = 4928
IN_PAD = 5120
COL_CQ = 4096
COL_CKV = 4608
COL_KR = 4864

N_EXPERTS = 64
TOP_K = 8
N_GROUPS = 8
TOPK_GROUPS = 4
EXPERTS_PER_GROUP = N_EXPERTS // N_GROUPS
D_EXPERT = 512
ROUTED_SCALE = 2.5
MOE_BLK = 256

VMEM_LIMIT = 56 * 1024 * 1024

NT_DIMS = (((1,), (1,)), ((), ()))
TN_DIMS = (((0,), (0,)), ((), ()))


def _cparams(sem):
    return pltpu.CompilerParams(dimension_semantics=sem, vmem_limit_bytes=VMEM_LIMIT)


def _sigmoid(x):
    return 1.0 / (1.0 + jnp.exp(-x))


def _split_bf16(x):
    hi = x.astype(BF16)
    lo = (x - hi.astype(F32)).astype(BF16)
    return hi, lo


def _rows_to_tiles(x):
    n = x.shape[1] // 128
    return jnp.swapaxes(jnp.stack([x[:, s * 128:(s + 1) * 128] for s in range(n)], axis=0), 0, 1)


def _tiles_to_rows(x3):
    xt = jnp.swapaxes(x3, 0, 1)
    return jnp.concatenate([xt[s] for s in range(x3.shape[1])], axis=1)


def _adaln_kernel(c_ref, w_ref, b_ref, o_ref):
    c = c_ref[...]
    cond = c * _sigmoid(c)
    hi, lo = _split_bf16(cond)
    lhs = jnp.concatenate([hi, lo], axis=0)
    r = jnp.dot(lhs, w_ref[...].astype(BF16), preferred_element_type=F32)
    o_ref[...] = r[:8] + r[8:] + b_ref[...]


def _adaln(c8, w, b):
    d, n = w.shape
    tn = 1024
    return pl.pallas_call(
        _adaln_kernel,
        out_shape=jax.ShapeDtypeStruct((8, n), F32),
        grid=(n // tn,),
        in_specs=[
            pl.BlockSpec((8, d), lambda j: (0, 0)),
            pl.BlockSpec((d, tn), lambda j: (0, j)),
            pl.BlockSpec((1, tn), lambda j: (0, j)),
        ],
        out_specs=pl.BlockSpec((8, tn), lambda j: (0, j)),
        compiler_params=_cparams(("arbitrary",)),
        name="adaln",
    )(c8, w, b)


def _inproj_kernel(x_ref, g_ref, sc_ref, sh_ref, w_ref, o_ref, u_scr, *, tm):
    @pl.when(pl.program_id(1) == 0)
    def _():
        g = g_ref[...]
        sc = 1.0 + sc_ref[0]
        sh = sh_ref[0]

        def body(r, carry):
            rows = pl.ds(pl.multiple_of(r * 128, 128), 128)
            x = x_ref[rows, :]
            ms = jnp.mean(x * x, axis=-1, keepdims=True)
            u = x * lax.rsqrt(ms + EPS) * g * sc + sh
            u_scr[rows, :] = u.astype(BF16)
            return carry

        lax.fori_loop(0, tm // 128, body, 0)

    o_ref[...] = jnp.dot(u_scr[...], w_ref[...], preferred_element_type=F32)


def _inproj(x2, g, mod3, w_bf, seq):
    t, d = x2.shape
    n = w_bf.shape[1]
    tm = min(1024, seq)
    tn = 1280
    per_b = seq // tm
    return pl.pallas_call(
        functools.partial(_inproj_kernel, tm=tm),
        out_shape=jax.ShapeDtypeStruct((t, n), F32),
        grid=(t // tm, n // tn),
        in_specs=[
            pl.BlockSpec((tm, d), lambda i, j: (i, 0)),
            pl.BlockSpec((1, d), lambda i, j: (0, 0)),
            pl.BlockSpec((1, 1, d), lambda i, j: (i // per_b, 0, 1)),
            pl.BlockSpec((1, 1, d), lambda i, j: (i // per_b, 0, 0)),
            pl.BlockSpec((d, tn), lambda i, j: (0, j)),
        ],
        out_specs=pl.BlockSpec((tm, tn), lambda i, j: (i, j)),
        scratch_shapes=[pltpu.VMEM((tm, d), BF16)],
        compiler_params=_cparams(("arbitrary", "arbitrary")),
        name="inproj",
    )(x2, g, mod3, mod3, w_bf)


def _hgrn_consts():
    c = HG_CHUNK
    t = np.arange(c)[:, None]
    j = np.arange(c)[None, :]
    tril = (j <= t).astype(np.float32)
    mats = [tril]
    masks = [np.eye(c, dtype=np.float32)]
    s = c // 2
    while s >= 1:
        ref = (t // (2 * s)) * (2 * s) + s - 1
        if s < HG_ROW_LEVEL:
            mats.append(tril - (j <= ref).astype(np.float32))
        masks.append((((t // s) % 2 == 1) & ((j // s) == (t // s) - 1)).astype(np.float32))
        s //= 2
    w = np.concatenate(mats, 0)
    return np.concatenate([w, w], 1), np.stack(masks, 0)


def _hgrn_kernel(q_ref, f_ref, i_ref, g_ref, lb_ref, og_ref, w_ref, m_ref, o_ref, st_scr, *, seq):
    c = HG_CHUNK
    n_lvl = m_ref.shape[0]
    st_scr[...] = jnp.zeros_like(st_scr)
    lb = lb_ref[...]
    og = og_ref[...]

    def chunk(ci, carry):
        rows = pl.ds(pl.multiple_of(ci * c, c), c)
        q = q_ref[rows, :]
        v = i_ref[rows, :]
        g = g_ref[rows, :]
        f = lb + (1.0 - lb) * _sigmoid(f_ref[rows, :])
        lf = jnp.log(f)
        kk = 1.0 - f
        hi, lo = _split_bf16(lf)
        xs = jnp.dot(w_ref[...], jnp.concatenate([hi, lo], axis=0), preferred_element_type=F32)
        b = xs[0:c]
        bl = b[c - 1:c, :] - b
        st = st_scr[...]
        vb = v.astype(BF16)
        o = lax.dot_general((q * jnp.exp(b)).astype(BF16), st.astype(BF16), NT_DIMS,
                            preferred_element_type=F32)
        attn = m_ref[0] * lax.dot_general(q.astype(BF16), kk.astype(BF16), NT_DIMS,
                                          preferred_element_type=F32)
        s = c // 2
        fine = 1
        for l in range(1, n_lvl):
            if s >= HG_ROW_LEVEL:
                b3 = b.reshape(c // (2 * s), 2 * s, HG_DK)
                x = (b3 - b3[:, s - 1:s, :]).reshape(c, HG_DK)
            else:
                x = xs[fine * c:(fine + 1) * c]
                fine += 1
            s //= 2
            e = jnp.exp(-jnp.abs(x))
            attn = attn + m_ref[l] * lax.dot_general(
                (q * e).astype(BF16), (kk * e).astype(BF16), NT_DIMS, preferred_element_type=F32)
        o = o + jnp.dot(attn.astype(BF16), vb, preferred_element_type=F32)
        khat = (kk * jnp.exp(bl)).astype(BF16)
        st_scr[...] = st * jnp.exp(b[c - 1:c, :]) + lax.dot_general(
            vb, khat, TN_DIMS, preferred_element_type=F32)
        ms = jnp.mean(o * o, axis=-1, keepdims=True)
        y = o * lax.rsqrt(ms + EPS) * og * (g * _sigmoid(g))
        o_ref[rows, :] = y.astype(BF16)
        return carry

    lax.fori_loop(0, seq // c, chunk, 0, unroll=4)


def _hgrn(proj, lb, og, batch, seq):
    t = proj.shape[0]
    wm, mm = _hgrn_consts()
    wm = jnp.asarray(wm, BF16)
    mm = jnp.asarray(mm, F32)
    h = HG_HEADS
    return pl.pallas_call(
        functools.partial(_hgrn_kernel, seq=seq),
        out_shape=jax.ShapeDtypeStruct((t, HG_WIDTH), BF16),
        grid=(batch, h),
        in_specs=[
            pl.BlockSpec((seq, 128), lambda b, i: (b, i)),
            pl.BlockSpec((seq, 128), lambda b, i: (b, h + i)),
            pl.BlockSpec((seq, 128), lambda b, i: (b, 2 * h + i)),
            pl.BlockSpec((seq, 128), lambda b, i: (b, 3 * h + i)),
            pl.BlockSpec((1, 128), lambda b, i: (0, i)),
            pl.BlockSpec((1, 128), lambda b, i: (0, 0)),
            pl.BlockSpec(wm.shape, lambda b, i: (0, 0)),
            pl.BlockSpec(mm.shape, lambda b, i: (0, 0, 0)),
        ],
        out_specs=pl.BlockSpec((seq, 128), lambda b, i: (b, i)),
        scratch_shapes=[pltpu.VMEM((HG_DV, HG_DK), F32)],
        compiler_params=_cparams(("arbitrary", "arbitrary")),
        name="hgrn2",
    )(proj, proj, proj, proj, lb, og, wm, mm)


def _mla_prep_kernel(cq_ref, ckv_ref, kr_ref, pos_ref, qag_ref, kvag_ref, wuq_ref, wukv_ref,
                     gq_ref, gkn_ref, gkr_ref, invf_ref, sgn_ref, cm_ref,
                     q_out, k_out, v_out):
    cq = cq_ref[...]
    cqn = cq * lax.rsqrt(jnp.mean(cq * cq, axis=-1, keepdims=True) + EPS) * qag_ref[...]
    qf = jnp.dot(cqn.astype(BF16), wuq_ref[...], preferred_element_type=F32)
    ckv = ckv_ref[...]
    ckvn = ckv * lax.rsqrt(jnp.mean(ckv * ckv, axis=-1, keepdims=True) + EPS) * kvag_ref[...]
    kvf = jnp.dot(ckvn.astype(BF16), wukv_ref[...], preferred_element_type=F32)

    ang = pos_ref[...].astype(F32) * invf_ref[...]
    cos_t = jnp.cos(ang) * cm_ref[...]
    sin_t = jnp.sin(ang) * sgn_ref[...]

    def rope(r):
        return r * cos_t + (pltpu.roll(r, 96, 1) + pltpu.roll(r, 32, 1)) * sin_t

    kr = kr_ref[...]
    kr_ss = jnp.sum(kr * kr, axis=-1, keepdims=True)
    krr = rope(kr * gkr_ref[...])
    gq = gq_ref[...]
    gkn = gkn_ref[...]
    scale = MLA_QK ** -0.5
    for h in range(MLA_HEADS):
        qh = qf[:, h * MLA_QK_PAD:(h + 1) * MLA_QK_PAD]
        inv = lax.rsqrt(jnp.sum(qh * qh, axis=-1, keepdims=True) / MLA_QK + EPS)
        qn = qh * inv * gq
        qo = jnp.concatenate([qn[:, :MLA_NOPE], rope(qn[:, MLA_NOPE:])], axis=-1) * scale
        q_out[0, h] = qo.astype(BF16)
        kn = kvf[:, h * 256:h * 256 + MLA_NOPE]
        inv = lax.rsqrt((jnp.sum(kn * kn, axis=-1, keepdims=True) + kr_ss) / MLA_QK + EPS)
        k_out[0, h] = jnp.concatenate([kn * inv * gkn, krr * inv], axis=-1).astype(BF16)
        v_out[0, h] = kvf[:, h * 256 + MLA_NOPE:(h + 1) * 256].astype(BF16)


def _mla_prep(proj, pos2, qag, kvag, wuq_p, wukv, gq, gkn, gkr, invf, sgn, cm, batch, seq):
    tm = min(256, seq)
    per_b = seq // tm
    hh = MLA_HEADS
    full = lambda shape: pl.BlockSpec(shape, lambda b, i: tuple(0 for _ in shape))
    return pl.pallas_call(
        _mla_prep_kernel,
        out_shape=(
            jax.ShapeDtypeStruct((batch, hh, seq, MLA_QK_PAD), BF16),
            jax.ShapeDtypeStruct((batch, hh, seq, MLA_QK_PAD), BF16),
            jax.ShapeDtypeStruct((batch, hh, seq, MLA_V), BF16),
        ),
        grid=(batch, per_b),
        in_specs=[
            pl.BlockSpec((tm, MLA_Q_RANK), lambda b, i: (b * per_b + i, COL_CQ // MLA_Q_RANK)),
            pl.BlockSpec((tm, MLA_KV_RANK), lambda b, i: (b * per_b + i, COL_CKV // MLA_KV_RANK)),
            pl.BlockSpec((tm, 128), lambda b, i: (b * per_b + i, COL_KR // 128)),
            pl.BlockSpec((tm, 1), lambda b, i: (b * per_b + i, 0)),
            full((1, MLA_Q_RANK)),
            full((1, MLA_KV_RANK)),
            full(wuq_p.shape),
            full(wukv.shape),
            full((1, MLA_QK_PAD)),
            full((1, 128)),
            full((1, 128)),
            full((1, 128)),
            full((1, 128)),
            full((1, 128)),
        ],
        out_specs=(
            pl.BlockSpec((1, hh, tm, MLA_QK_PAD), lambda b, i: (b, 0, i, 0)),
            pl.BlockSpec((1, hh, tm, MLA_QK_PAD), lambda b, i: (b, 0, i, 0)),
            pl.BlockSpec((1, hh, tm, MLA_V), lambda b, i: (b, 0, i, 0)),
        ),
        compiler_params=_cparams(("arbitrary", "arbitrary")),
        name="mla_prep",
    )(proj, proj, proj, pos2, qag, kvag, wuq_p, wukv, gq, gkn, gkr, invf, sgn, cm)


def _attn_kernel(q_ref, k_ref, v_ref, o_ref, *, tq, seq):
    row = lax.broadcasted_iota(jnp.int32, (tq, tq), 0)
    col = lax.broadcasted_iota(jnp.int32, (tq, tq), 1)
    causal = col <= row
    for i in range(seq // tq):
        lo = i * tq
        q = q_ref[0, 0, lo:lo + tq, :]
        sd = lax.dot_general(q, k_ref[0, 0, lo:lo + tq, :], NT_DIMS, preferred_element_type=F32)
        sd = jnp.where(causal, sd, -jnp.inf)
        m = jnp.max(sd, axis=-1, keepdims=True)
        if i > 0:
            so = lax.dot_general(q, k_ref[0, 0, :lo, :], NT_DIMS, preferred_element_type=F32)
            m = jnp.maximum(m, jnp.max(so, axis=-1, keepdims=True))
        pd = jnp.exp(sd - m)
        l = jnp.sum(pd, axis=-1, keepdims=True)
        o = jnp.dot(pd.astype(BF16), v_ref[0, 0, lo:lo + tq, :], preferred_element_type=F32)
        if i > 0:
            po = jnp.exp(so - m)
            l = l + jnp.sum(po, axis=-1, keepdims=True)
            o = o + jnp.dot(po.astype(BF16), v_ref[0, 0, :lo, :], preferred_element_type=F32)
        o_ref[lo:lo + tq, :] = o / l


def _attention(q, k, v):
    batch, hh, seq, dq = q.shape
    tq = min(256, seq)
    return pl.pallas_call(
        functools.partial(_attn_kernel, tq=tq, seq=seq),
        out_shape=jax.ShapeDtypeStruct((batch * seq, hh * MLA_V), F32),
        grid=(batch, hh),
        in_specs=[
            pl.BlockSpec((1, 1, seq, dq), lambda b, h: (b, h, 0, 0)),
            pl.BlockSpec((1, 1, seq, dq), lambda b, h: (b, h, 0, 0)),
            pl.BlockSpec((1, 1, seq, MLA_V), lambda b, h: (b, h, 0, 0)),
        ],
        out_specs=pl.BlockSpec((seq, MLA_V), lambda b, h: (b, h)),
        compiler_params=_cparams(("arbitrary", "arbitrary")),
        name="mla_attn",
    )(q, k, v)


def _outproj_kernel(ohg_ref, omla_ref, x_ref, gt_ref, sc_ref, sh_ref, og_ref, g2_ref,
                    wo_ref, wrh_ref, wrl_ref, h_out, u_out, lg_out, *, n_sub):
    sub = x_ref.shape[0] // n_sub
    for j in range(n_sub):
        rows = slice(j * sub, (j + 1) * sub)
        om = omla_ref[rows, :]
        omn = om * lax.rsqrt(jnp.mean(om * om, axis=-1, keepdims=True) + EPS) * og_ref[...]
        y = jnp.dot(ohg_ref[rows, :], wo_ref[:HG_WIDTH, :], preferred_element_type=F32)
        y = y + jnp.dot(omn.astype(BF16), wo_ref[HG_WIDTH:, :], preferred_element_type=F32)
        h1 = x_ref[rows, :] + gt_ref[0] * y
        h_out[rows, :] = h1
        u = h1 * lax.rsqrt(jnp.mean(h1 * h1, axis=-1, keepdims=True) + EPS) * g2_ref[...]
        u = u * (1.0 + sc_ref[0]) + sh_ref[0]
        uh, ul = _split_bf16(u)
        u_out[rows] = _rows_to_tiles(uh)
        lg = lax.dot_general(wrh_ref[...], uh, NT_DIMS, preferred_element_type=F32)
        lg = lg + lax.dot_general(wrl_ref[...], uh, NT_DIMS, preferred_element_type=F32)
        lg = lg + lax.dot_general(wrh_ref[...], ul, NT_DIMS, preferred_element_type=F32)
        lg_out[:, rows] = lg


def _outproj(ohg, omla, x2, mod3, og, g2, wo_bf, wr_hi, wr_lo, seq):
    t, d = x2.shape
    tm = min(512, seq)
    per_b = seq // tm
    full = lambda shape: pl.BlockSpec(shape, lambda i: tuple(0 for _ in shape))
    modspec = lambda col: pl.BlockSpec((1, 1, d), lambda i: (i // per_b, 0, col))
    return pl.pallas_call(
        functools.partial(_outproj_kernel, n_sub=2),
        out_shape=(
            jax.ShapeDtypeStruct((t, d), F32),
            jax.ShapeDtypeStruct((t, d // 128, 128), BF16),
            jax.ShapeDtypeStruct((N_EXPERTS, t), F32),
        ),
        grid=(t // tm,),
        in_specs=[
            pl.BlockSpec((tm, HG_WIDTH), lambda i: (i, 0)),
            pl.BlockSpec((tm, MLA_WIDTH), lambda i: (i, 0)),
            pl.BlockSpec((tm, d), lambda i: (i, 0)),
            modspec(2),
            modspec(4),
            modspec(3),
            full((1, MLA_WIDTH)),
            full((1, d)),
            full(wo_bf.shape),
            full(wr_hi.shape),
            full(wr_lo.shape),
        ],
        out_specs=(
            pl.BlockSpec((tm, d), lambda i: (i, 0)),
            pl.BlockSpec((tm, d // 128, 128), lambda i: (i, 0, 0)),
            pl.BlockSpec((N_EXPERTS, tm), lambda i: (0, i)),
        ),
        compiler_params=_cparams(("arbitrary",)),
        name="outproj",
    )(ohg, omla, x2, mod3, mod3, mod3, og, g2, wo_bf, wr_hi, wr_lo)


def _route_kernel(lg_ref, bias_ref, tri_ref, ltri_ref, idx_out, w_out, dest_out, info_out,
                  carry, base, *, tt, n_info):
    phase = pl.program_id(0)
    step = pl.program_id(1)

    @pl.when((phase == 0) & (step == 0))
    def _():
        carry[...] = jnp.zeros_like(carry)

    @pl.when((phase == 1) & (step == 0))
    def _():
        cnt = carry[...]
        padded = jnp.floor((cnt + (MOE_BLK - 1)) / MOE_BLK) * MOE_BLK
        hi, lo = _split_bf16(padded)
        ps = jnp.dot(ltri_ref[...], jnp.concatenate([hi, lo], axis=1), preferred_element_type=F32)
        pad_start = ps[:, :128] + ps[:, 128:]
        base[...] = pad_start
        carry[...] = jnp.zeros_like(carry)
        pad_end = pad_start + padded
        reps = n_info // 128
        pe = jnp.concatenate([pad_end] * reps, axis=1)
        starts = (lax.broadcasted_iota(jnp.int32, (N_EXPERTS, n_info), 1) * MOE_BLK).astype(F32)
        blk_e = jnp.sum(jnp.where(pe <= starts, 1.0, 0.0), axis=0, keepdims=True)
        blk_e = jnp.minimum(blk_e, N_EXPERTS - 1.0)
        n_used = pe[N_EXPERTS - 1:N_EXPERTS, :] / MOE_BLK
        diag = (lax.broadcasted_iota(jnp.int32, (N_EXPERTS, n_info), 0)
                == lax.broadcasted_iota(jnp.int32, (N_EXPERTS, n_info), 1))
        end_blk = jnp.sum(jnp.where(diag, pe, 0.0), axis=0, keepdims=True) / MOE_BLK
        pd = jnp.concatenate([padded] * reps, axis=1)
        num_blk = jnp.sum(jnp.where(diag, pd, 0.0), axis=0, keepdims=True) / MOE_BLK
        info_out[...] = jnp.concatenate(
            [blk_e, n_used, end_blk, num_blk, jnp.zeros((4, n_info), F32)],
            axis=0).astype(jnp.int32)

    ne, ng, eg = N_EXPERTS, N_GROUPS, EXPERTS_PER_GROUP
    neg = -jnp.inf
    sc = _sigmoid(lg_ref[...])
    sel = sc + bias_ref[...]
    sel3 = sel.reshape(ng, eg, tt)
    sub = lax.broadcasted_iota(jnp.int32, (ng, eg, tt), 1)
    m1 = jnp.max(sel3, axis=1, keepdims=True)
    first = jnp.min(jnp.where(sel3 == m1, sub, eg), axis=1, keepdims=True)
    m2 = jnp.max(jnp.where(sub == first, neg, sel3), axis=1, keepdims=True)
    gs = (m1 + m2).reshape(ng, tt)

    gio = lax.broadcasted_iota(jnp.int32, (ng, tt), 0)
    gsel = jnp.zeros((ng, tt), F32)
    for _ in range(TOPK_GROUPS):
        m = jnp.max(gs, axis=0, keepdims=True)
        gi = jnp.min(jnp.where(gs == m, gio, ng), axis=0, keepdims=True)
        hit = gio == gi
        gsel = jnp.where(hit, 1.0, gsel)
        gs = jnp.where(hit, neg, gs)
    gfull = jnp.broadcast_to(gsel.reshape(ng, 1, tt), (ng, eg, tt)).reshape(ne, tt)
    selm = jnp.where(gfull > 0.0, sel, neg)

    eio = lax.broadcasted_iota(jnp.int32, (ne, tt), 0)
    chosen = jnp.zeros((ne, tt), F32)
    idxs, ws = [], []
    for _ in range(TOP_K):
        m = jnp.max(selm, axis=0, keepdims=True)
        ei = jnp.min(jnp.where(selm == m, eio, ne), axis=0, keepdims=True)
        hit = eio == ei
        ws.append(jnp.sum(jnp.where(hit, sc, 0.0), axis=0, keepdims=True))
        idxs.append(ei)
        chosen = jnp.where(hit, 1.0, chosen)
        selm = jnp.where(hit, neg, selm)
    wsum = ws[0]
    for k in range(1, TOP_K):
        wsum = wsum + ws[k]

    tot = jnp.sum(chosen, axis=1, keepdims=True)

    @pl.when(phase == 1)
    def _():
        off = base[...][:, :1] + carry[...][:, :1]
        excl = jnp.dot(chosen.astype(BF16), tri_ref[...], preferred_element_type=F32) + off
        for k in range(TOP_K):
            idx_out[k:k + 1, :] = idxs[k]
            w_out[k:k + 1, :] = ws[k] / wsum * ROUTED_SCALE
            dk = jnp.sum(jnp.where(eio == idxs[k], excl, 0.0), axis=0, keepdims=True)
            dest_out[k:k + 1, :] = dk.astype(jnp.int32)

    carry[...] = carry[...] + tot


def _route(logits_t, bias_col, n_blk):
    ne, t = logits_t.shape
    tt = min(512, t)
    n_info = -(-n_blk // 128) * 128
    tri = jnp.asarray(np.triu(np.ones((tt, tt), np.float32), 1), BF16)
    ltri = jnp.asarray(np.tril(np.ones((ne, ne), np.float32), -1), BF16)
    return pl.pallas_call(
        functools.partial(_route_kernel, tt=tt, n_info=n_info),
        out_shape=(
            jax.ShapeDtypeStruct((TOP_K, t), jnp.int32),
            jax.ShapeDtypeStruct((TOP_K, t), F32),
            jax.ShapeDtypeStruct((TOP_K, t), jnp.int32),
            jax.ShapeDtypeStruct((8, n_info), jnp.int32),
        ),
        grid=(2, t // tt),
        in_specs=[
            pl.BlockSpec((ne, tt), lambda p, i: (0, i)),
            pl.BlockSpec((ne, 1), lambda p, i: (0, 0)),
            pl.BlockSpec((tt, tt), lambda p, i: (0, 0)),
            pl.BlockSpec((ne, ne), lambda p, i: (0, 0)),
        ],
        out_specs=(
            pl.BlockSpec((TOP_K, tt), lambda p, i: (0, i * p)),
            pl.BlockSpec((TOP_K, tt), lambda p, i: (0, i * p)),
            pl.BlockSpec((TOP_K, tt), lambda p, i: (0, i * p)),
            pl.BlockSpec((8, n_info), lambda p, i: (0, 0)),
        ),
        scratch_shapes=[pltpu.VMEM((ne, 128), F32), pltpu.VMEM((ne, 128), F32)],
        compiler_params=_cparams(("arbitrary", "arbitrary")),
        name="route",
    )(logits_t, bias_col, tri, ltri)


def _dispatch_kernel(info_ref, dest_ref, u_hbm, xs_hbm, stage, zbuf, lsem, ssem, zsem, *, tm, n_blk):
    i = pl.program_id(0)
    n = pl.num_programs(0)
    slot = i % 2
    n_used = info_ref[1, 0]

    def load(tile, dst_slot):
        return pltpu.make_async_copy(u_hbm.at[pl.ds(tile * tm, tm)], stage.at[dst_slot],
                                     lsem.at[dst_slot])

    def zero_block(blk):
        return pltpu.make_async_copy(zbuf, xs_hbm.at[pl.ds(blk * MOE_BLK, MOE_BLK)], zsem)

    @pl.when(i == 0)
    def _():
        load(0, 0).start()
        zbuf[...] = jnp.zeros_like(zbuf)
        for e in range(N_EXPERTS):
            @pl.when(info_ref[3, e] > 0)
            def _():
                zero_block(info_ref[2, e] - 1).start()

        def zero_unused(j, carry):
            zero_block(j).start()
            return carry
        lax.fori_loop(n_used, n_blk, zero_unused, 0)
        for e in range(N_EXPERTS):
            @pl.when(info_ref[3, e] > 0)
            def _():
                zero_block(0).wait()

        def wait_unused(j, carry):
            zero_block(0).wait()
            return carry
        lax.fori_loop(n_used, n_blk, wait_unused, 0)

    def scatter_wait(src_slot):
        pltpu.make_async_copy(stage.at[src_slot], xs_hbm.at[pl.ds(0, tm)], ssem.at[src_slot]).wait()

    @pl.when(i >= 1)
    def _():
        for _ in range(TOP_K):
            scatter_wait(1 - slot)

    @pl.when(i + 1 < n)
    def _():
        load(i + 1, 1 - slot).start()

    load(i, slot).wait()
    for k in range(TOP_K):
        for r in range(tm):
            pltpu.make_async_copy(stage.at[slot, r], xs_hbm.at[dest_ref[0, k, r]],
                                  ssem.at[slot]).start(priority=r % 2)

    @pl.when(i == n - 1)
    def _():
        for _ in range(TOP_K):
            scatter_wait(slot)


def _dispatch(info, dest3, u3, n_blk):
    t, nt, _ = u3.shape
    tm = dest3.shape[2]
    return pl.pallas_call(
        functools.partial(_dispatch_kernel, tm=tm, n_blk=n_blk),
        out_shape=jax.ShapeDtypeStruct((n_blk * MOE_BLK, nt, 128), BF16),
        grid_spec=pltpu.PrefetchScalarGridSpec(
            num_scalar_prefetch=1,
            grid=(t // tm,),
            in_specs=[
                pl.BlockSpec((1, TOP_K, tm), lambda i, info: (i, 0, 0), memory_space=pltpu.SMEM),
                pl.BlockSpec(memory_space=pl.ANY),
            ],
            out_specs=pl.BlockSpec(memory_space=pl.ANY),
            scratch_shapes=[
                pltpu.VMEM((2, tm, nt, 128), BF16),
                pltpu.VMEM((MOE_BLK, nt, 128), BF16),
                pltpu.SemaphoreType.DMA((2,)),
                pltpu.SemaphoreType.DMA((2,)),
                pltpu.SemaphoreType.DMA(()),
            ],
        ),
        compiler_params=_cparams(("arbitrary",)),
        name="moe_dispatch",
    )(info, dest3, u3)


def _gmm_kernel(info_ref, xs_ref, wg_hbm, wu_hbm, wd_hbm, y_ref,
                wg_f, wu_f, wd_f, wsem, wg_s, wu_s, wd_s, nswap):
    i = pl.program_id(0)
    n_used = info_ref[1, 0]
    e = info_ref[0, i]
    prev = info_ref[0, jnp.maximum(i - 1, 0)]
    fresh = (i == 0) | (e != prev)

    def fetch(expert, slot):
        return (pltpu.make_async_copy(wg_hbm.at[expert], wg_f.at[slot], wsem.at[slot]),
                pltpu.make_async_copy(wu_hbm.at[expert], wu_f.at[slot], wsem.at[slot]),
                pltpu.make_async_copy(wd_hbm.at[expert], wd_f.at[slot], wsem.at[slot]))

    @pl.when(i == 0)
    def _():
        nswap[0] = 0
        for c in fetch(e, 0):
            c.start()

    @pl.when(fresh & (i < n_used))
    def _():
        slot = nswap[0] % 2
        nswap[0] = nswap[0] + 1
        for c in fetch(e, slot):
            c.wait()
        nxt = info_ref[2, e]

        @pl.when(nxt < n_used)
        def _():
            for c in fetch(info_ref[0, nxt], 1 - slot):
                c.start()

        wg_s[...] = wg_f[slot].astype(BF16)
        wu_s[...] = wu_f[slot].astype(BF16)
        wd_s[...] = wd_f[slot].astype(BF16)

    @pl.when(i < n_used)
    def _():
        xb = _tiles_to_rows(xs_ref[...])
        a = jnp.dot(xb, wg_s[...], preferred_element_type=F32)
        u = jnp.dot(xb, wu_s[...], preferred_element_type=F32)
        hid = (a * _sigmoid(a) * u).astype(BF16)
        y = jnp.dot(hid, wd_s[...], preferred_element_type=F32)
        y_ref[...] = _rows_to_tiles(y.astype(BF16))

    @pl.when(i >= n_used)
    def _():
        y_ref[...] = jnp.zeros_like(y_ref)


def _gmm(info, xs3, w_gate, w_up, w_down):
    p, nt, _ = xs3.shape
    d = nt * 128
    n_blk = p // MOE_BLK
    de = w_gate.shape[-1]
    return pl.pallas_call(
        _gmm_kernel,
        out_shape=jax.ShapeDtypeStruct((p, nt, 128), BF16),
        grid_spec=pltpu.PrefetchScalarGridSpec(
            num_scalar_prefetch=1,
            grid=(n_blk,),
            in_specs=[
                pl.BlockSpec((MOE_BLK, nt, 128),
                             lambda i, info: (jnp.minimum(i, info[1, 0] - 1), 0, 0)),
                pl.BlockSpec(memory_space=pl.ANY),
                pl.BlockSpec(memory_space=pl.ANY),
                pl.BlockSpec(memory_space=pl.ANY),
            ],
            out_specs=pl.BlockSpec((MOE_BLK, nt, 128), lambda i, info: (i, 0, 0)),
            scratch_shapes=[
                pltpu.VMEM((2, d, de), F32),
                pltpu.VMEM((2, d, de), F32),
                pltpu.VMEM((2, de, d), F32),
                pltpu.SemaphoreType.DMA((2,)),
                pltpu.VMEM((d, de), BF16),
                pltpu.VMEM((d, de), BF16),
                pltpu.VMEM((de, d), BF16),
                pltpu.SMEM((1,), jnp.int32),
            ],
        ),
        compiler_params=_cparams(("arbitrary",)),
        name="moe_gmm",
    )(info, xs3, w_gate, w_up, w_down)


def _final_kernel(cur_ref, nxt_ref, u_ref, h_ref, gt_ref, wt_ref, wsg_ref, wsu_ref, wsd_ref, y_hbm,
                  o_ref, ybuf_a, ybuf_b, sem, *, tm):
    i = pl.program_id(0)
    n = pl.num_programs(0)
    n_rows = TOP_K * tm

    def issue(idx_ref, tile, buf, sem_i):
        for k in range(TOP_K):
            for r in range(tm):
                pltpu.make_async_copy(y_hbm.at[idx_ref[0, tile * TOP_K + k, r]], buf.at[k * tm + r],
                                      sem.at[sem_i]).start(priority=r % 2)

    def gather_wait(buf, sem_i):
        pltpu.make_async_copy(y_hbm.at[pl.ds(0, n_rows)], buf, sem.at[sem_i]).wait()

    eye = (lax.broadcasted_iota(jnp.int32, (tm, tm), 0)
           == lax.broadcasted_iota(jnp.int32, (tm, tm), 1))

    def compute(tile, buf):
        rows = slice(tile * tm, (tile + 1) * tm)
        ub = _tiles_to_rows(u_ref[rows])
        a = jnp.dot(ub, wsg_ref[...], preferred_element_type=F32)
        b = jnp.dot(ub, wsu_ref[...], preferred_element_type=F32)
        hid = (a * _sigmoid(a) * b).astype(BF16)
        acc = jnp.dot(hid, wsd_ref[...], preferred_element_type=F32)
        wt = wt_ref[rows, :]
        wcat = jnp.concatenate([jnp.where(eye, wt[:, k:k + 1], 0.0) for k in range(TOP_K)],
                               axis=1)
        whi, wlo = _split_bf16(wcat)
        ycat = _tiles_to_rows(buf[...])
        acc = acc + jnp.dot(whi, ycat, preferred_element_type=F32)
        acc = acc + jnp.dot(wlo, ycat, preferred_element_type=F32)
        o_ref[rows, :] = h_ref[rows, :] + gt_ref[0] * acc

    @pl.when(i == 0)
    def _():
        issue(cur_ref, 0, ybuf_a, 0)

    gather_wait(ybuf_a, 0)
    issue(cur_ref, 1, ybuf_b, 1)
    compute(0, ybuf_a)
    gather_wait(ybuf_b, 1)
    issue(nxt_ref, 0, ybuf_a, 0)
    compute(1, ybuf_b)

    @pl.when(i == n - 1)
    def _():
        gather_wait(ybuf_a, 0)


def _final(dest3, u2, h1, mod3, wt, wsg, wsu, wsd, y, seq):
    t, d = h1.shape
    tm = dest3.shape[2]
    n = t // (2 * tm)
    per_b = seq // (2 * tm)
    dest_pairs = dest3.reshape(n, 2 * TOP_K, tm)
    full = lambda shape: pl.BlockSpec(shape, lambda i: tuple(0 for _ in shape))
    return pl.pallas_call(
        functools.partial(_final_kernel, tm=tm),
        out_shape=jax.ShapeDtypeStruct((t, d), F32),
        grid=(n,),
        in_specs=[
            pl.BlockSpec((1, 2 * TOP_K, tm), lambda i: (i, 0, 0), memory_space=pltpu.SMEM),
            pl.BlockSpec((1, 2 * TOP_K, tm), lambda i: (jnp.minimum(i + 1, n - 1), 0, 0),
                         memory_space=pltpu.SMEM),
            pl.BlockSpec((2 * tm, d // 128, 128), lambda i: (i, 0, 0)),
            pl.BlockSpec((2 * tm, d), lambda i: (i, 0)),
            pl.BlockSpec((1, 1, d), lambda i: (i // per_b, 0, 5)),
            pl.BlockSpec((2 * tm, TOP_K), lambda i: (i, 0)),
            full(wsg.shape),
            full(wsu.shape),
            full(wsd.shape),
            pl.BlockSpec(memory_space=pl.ANY),
        ],
        out_specs=pl.BlockSpec((2 * tm, d), lambda i: (i, 0)),
        scratch_shapes=[
            pltpu.VMEM((TOP_K * tm, d // 128, 128), BF16),
            pltpu.VMEM((TOP_K * tm, d // 128, 128), BF16),
            pltpu.SemaphoreType.DMA((2,)),
        ],
        compiler_params=_cparams(("arbitrary",)),
        name="final",
    )(dest_pairs, dest_pairs, u2, h1, mod3, wt, wsg, wsu, wsd, y)


def _layer(h2, mod3, positions, lb, p, batch, seq):
    t, d = h2.shape

    w_in_bf = jnp.pad(p["w_in"], ((0, 0), (0, IN_PAD - IN_COLS))).astype(BF16)
    proj = _inproj(h2, p["norm_mix_g"][None, :], mod3, w_in_bf, seq)

    o_hg = _hgrn(proj, lb[None, :], p["hg_out_g"][None, :], batch, seq)

    wuq = p["mla_w_uq"].reshape(MLA_Q_RANK, MLA_HEADS, MLA_QK)
    wuq_p = jnp.pad(wuq, ((0, 0), (0, 0), (0, MLA_QK_PAD - MLA_QK))).reshape(
        MLA_Q_RANK, MLA_HEADS * MLA_QK_PAD).astype(BF16)
    wukv = p["mla_w_ukv"].astype(BF16)
    gq = jnp.pad(p["mla_q_norm_g"], (0, MLA_QK_PAD - MLA_QK))[None, :]
    gkn = p["mla_k_norm_g"][None, :MLA_NOPE]
    gkr = jnp.pad(p["mla_k_norm_g"][MLA_NOPE:], (0, 128 - MLA_ROPE))[None, :]
    half = MLA_ROPE // 2
    inv_freq = ROPE_THETA ** (-jnp.arange(half, dtype=F32) / half)
    zeros64 = jnp.zeros((128 - MLA_ROPE,), F32)
    invf = jnp.concatenate([inv_freq, inv_freq, zeros64])[None, :]
    sgn = jnp.concatenate([-jnp.ones((half,), F32), jnp.ones((half,), F32), zeros64])[None, :]
    cm = jnp.concatenate([jnp.ones((MLA_ROPE,), F32), zeros64])[None, :]
    q, k, v = _mla_prep(proj, positions.reshape(t, 1), p["mla_q_a_g"][None, :],
                        p["mla_kv_a_g"][None, :], wuq_p, wukv, gq, gkn, gkr, invf, sgn, cm,
                        batch, seq)
    o_mla = _attention(q, k, v)

    wr_t = p["w_router"].T
    wr_hi = wr_t.astype(BF16)
    wr_lo = (wr_t - wr_hi.astype(F32)).astype(BF16)
    h1, u2, logits_t = _outproj(o_hg, o_mla, h2, mod3, p["mla_out_g"][None, :],
                                p["norm_ffn_g"][None, :], p["w_out"].astype(BF16),
                                wr_hi, wr_lo, seq)

    n_blk = (t * TOP_K) // MOE_BLK + N_EXPERTS
    _, top_w, dest, info = _route(logits_t, p["router_bias"][:, None], n_blk)

    tm = min(128, seq)
    dest3 = dest.reshape(TOP_K, t // tm, tm).transpose(1, 0, 2)
    xs = _dispatch(info, dest3, u2, n_blk)
    y = _gmm(info, xs, p["w_gate"], p["w_up"], p["w_down"])
    return _final(dest3, u2, h1, mod3, top_w.T, p["ws_gate"].astype(BF16),
                  p["ws_up"].astype(BF16), p["ws_down"].astype(BF16), y, seq)


def kernel(x, c, positions, w_ada, b_ada, norm_mix_g, norm_ffn_g, w_in, hg_lb_logits, hg_out_g,
           mla_q_a_g, mla_w_uq, mla_kv_a_g, mla_w_ukv, mla_q_norm_g, mla_k_norm_g, mla_out_g,
           w_out, w_router, router_bias, w_gate, w_up, w_down, ws_gate, ws_up, ws_down):
    batch, seq, d = x.shape
    depth = w_ada.shape[0]
    lb_all = jnp.cumsum(jax.nn.softmax(hg_lb_logits.astype(F32), axis=0), axis=0)
    c8 = jnp.pad(c, ((0, 8 - batch), (0, 0)))
    h2 = x.reshape(batch * seq, d)
    for l in range(depth):
        mod = _adaln(c8, w_ada[l], b_ada[l][None, :])[:batch]
        mod3 = mod.reshape(batch, 1, 6 * d)
        p = dict(norm_mix_g=norm_mix_g[l], norm_ffn_g=norm_ffn_g[l], w_in=w_in[l],
                 hg_out_g=hg_out_g[l], mla_q_a_g=mla_q_a_g[l], mla_w_uq=mla_w_uq[l],
                 mla_kv_a_g=mla_kv_a_g[l], mla_w_ukv=mla_w_ukv[l], mla_q_norm_g=mla_q_norm_g[l],
                 mla_k_norm_g=mla_k_norm_g[l], mla_out_g=mla_out_g[l], w_out=w_out[l],
                 w_router=w_router[l], router_bias=router_bias[l], w_gate=w_gate[l],
                 w_up=w_up[l], w_down=w_down[l], ws_gate=ws_gate[l], ws_up=ws_up[l],
                 ws_down=ws_down[l])
        h2 = _layer(h2, mod3, positions, lb_all[l], p, batch, seq)
    return h2.reshape(batch, seq, d)
```

```python
import functools

import numpy as np
import jax
import jax.numpy as jnp
from jax import lax
from jax.experimental import pallas as pl
from jax.experimental.pallas import tpu as pltpu

F32 = jnp.float32
BF16 = jnp.bfloat16

D_MODEL = 2048
EPS = 1e-6

HG_HEADS = 8
HG_DK = 128
HG_DV = 128
HG_WIDTH = HG_HEADS * HG_DV
HG_CHUNK = 128
HG_ROW_LEVEL = 8

MLA_HEADS = 8
MLA_Q_RANK = 512
MLA_KV_RANK = 256
MLA_NOPE = 128
MLA_ROPE = 64
MLA_QK = MLA_NOPE + MLA_ROPE
MLA_QK_PAD = 256
MLA_V = 128
MLA_WIDTH = MLA_HEADS * MLA_V
ROPE_THETA = 10000.0

IN_COLS = 4928
IN_PAD = 5120
COL_CQ = 4096
COL_CKV = 4608
COL_KR = 4864

N_EXPERTS = 64
TOP_K = 8
N_GROUPS = 8
TOPK_GROUPS = 4
EXPERTS_PER_GROUP = N_EXPERTS // N_GROUPS
D_EXPERT = 512
ROUTED_SCALE = 2.5
MOE_BLK = 256

VMEM_LIMIT = 56 * 1024 * 1024

NT_DIMS = (((1,), (1,)), ((), ()))
TN_DIMS = (((0,), (0,)), ((), ()))


def _cparams(sem):
    return pltpu.CompilerParams(dimension_semantics=sem, vmem_limit_bytes=VMEM_LIMIT)


def _sigmoid(x):
    return 1.0 / (1.0 + jnp.exp(-x))


def _split_bf16(x):
    hi = x.astype(BF16)
    lo = (x - hi.astype(F32)).astype(BF16)
    return hi, lo


def _rows_to_tiles(x):
    n = x.shape[1] // 128
    return jnp.swapaxes(jnp.stack([x[:, s * 128:(s + 1) * 128] for s in range(n)], axis=0), 0, 1)


def _tiles_to_rows(x3):
    xt = jnp.swapaxes(x3, 0, 1)
    return jnp.concatenate([xt[s] for s in range(x3.shape[1])], axis=1)


def _adaln_kernel(c_ref, w_ref, b_ref, o_ref):
    c = c_ref[...]
    cond = c * _sigmoid(c)
    hi, lo = _split_bf16(cond)
    lhs = jnp.concatenate([hi, lo], axis=0)
    r = jnp.dot(lhs, w_ref[...].astype(BF16), preferred_element_type=F32)
    o_ref[...] = r[:8] + r[8:] + b_ref[...]


def _adaln(c8, w, b):
    d, n = w.shape
    tn = 1024
    return pl.pallas_call(
        _adaln_kernel,
        out_shape=jax.ShapeDtypeStruct((8, n), F32),
        grid=(n // tn,),
        in_specs=[
            pl.BlockSpec((8, d), lambda j: (0, 0)),
            pl.BlockSpec((d, tn), lambda j: (0, j)),
            pl.BlockSpec((1, tn), lambda j: (0, j)),
        ],
        out_specs=pl.BlockSpec((8, tn), lambda j: (0, j)),
        compiler_params=_cparams(("arbitrary",)),
        name="adaln",
    )(c8, w, b)


def _inproj_kernel(x_ref, g_ref, sc_ref, sh_ref, w_ref, o_ref, u_scr, *, tm):
    @pl.when(pl.program_id(1) == 0)
    def _():
        g = g_ref[...]
        sc = 1.0 + sc_ref[0]
        sh = sh_ref[0]

        def body(r, carry):
            rows = pl.ds(pl.multiple_of(r * 128, 128), 128)
            x = x_ref[rows, :]
            ms = jnp.mean(x * x, axis=-1, keepdims=True)
            u = x * lax.rsqrt(ms + EPS) * g * sc + sh
            u_scr[rows, :] = u.astype(BF16)
            return carry

        lax.fori_loop(0, tm // 128, body, 0)

    o_ref[...] = jnp.dot(u_scr[...], w_ref[...], preferred_element_type=F32)


def _inproj(x2, g, mod3, w_bf, seq):
    t, d = x2.shape
    n = w_bf.shape[1]
    tm = min(1024, seq)
    tn = 1280
    per_b = seq // tm
    return pl.pallas_call(
        functools.partial(_inproj_kernel, tm=tm),
        out_shape=jax.ShapeDtypeStruct((t, n), F32),
        grid=(t // tm, n // tn),
        in_specs=[
            pl.BlockSpec((tm, d), lambda i, j: (i, 0)),
            pl.BlockSpec((1, d), lambda i, j: (0, 0)),
            pl.BlockSpec((1, 1, d), lambda i, j: (i // per_b, 0, 1)),
            pl.BlockSpec((1, 1, d), lambda i, j: (i // per_b, 0, 0)),
            pl.BlockSpec((d, tn), lambda i, j: (0, j)),
        ],
        out_specs=pl.BlockSpec((tm, tn), lambda i, j: (i, j)),
        scratch_shapes=[pltpu.VMEM((tm, d), BF16)],
        compiler_params=_cparams(("arbitrary", "arbitrary")),
        name="inproj",
    )(x2, g, mod3, mod3, w_bf)


def _hgrn_consts():
    c = HG_CHUNK
    t = np.arange(c)[:, None]
    j = np.arange(c)[None, :]
    tril = (j <= t).astype(np.float32)
    mats = [tril]
    masks = [np.eye(c, dtype=np.float32)]
    s = c // 2
    while s >= 1:
        ref = (t // (2 * s)) * (2 * s) + s - 1
        if s < HG_ROW_LEVEL:
            mats.append(tril - (j <= ref).astype(np.float32))
        masks.append((((t // s) % 2 == 1) & ((j // s) == (t // s) - 1)).astype(np.float32))
        s //= 2
    w = np.concatenate(mats, 0)
    return np.concatenate([w, w], 1), np.stack(masks, 0)


def _hgrn_kernel(q_ref, f_ref, i_ref, g_ref, lb_ref, og_ref, w_ref, m_ref, o_ref, st_scr, *, seq):
    c = HG_CHUNK
    n_lvl = m_ref.shape[0]
    st_scr[...] = jnp.zeros_like(st_scr)
    lb = lb_ref[...]
    og = og_ref[...]

    def chunk(ci, carry):
        rows = pl.ds(pl.multiple_of(ci * c, c), c)
        q = q_ref[rows, :]
        v = i_ref[rows, :]
        g = g_ref[rows, :]
        f = lb + (1.0 - lb) * _sigmoid(f_ref[rows, :])
        lf = jnp.log(f)
        kk = 1.0 - f
        hi, lo = _split_bf16(lf)
        xs = jnp.dot(w_ref[...], jnp.concatenate([hi, lo], axis=0), preferred_element_type=F32)
        b = xs[0:c]
        bl = b[c - 1:c, :] - b
        st = st_scr[...]
        vb = v.astype(BF16)
        o = lax.dot_general((q * jnp.exp(b)).astype(BF16), st.astype(BF16), NT_DIMS,
                            preferred_element_type=F32)
        attn = m_ref[0] * lax.dot_general(q.astype(BF16), kk.astype(BF16), NT_DIMS,
                                          preferred_element_type=F32)
        s = c // 2
        fine = 1
        for l in range(1, n_lvl):
            if s >= HG_ROW_LEVEL:
                b3 = b.reshape(c // (2 * s), 2 * s, HG_DK)
                x = (b3 - b3[:, s - 1:s, :]).reshape(c, HG_DK)
            else:
                x = xs[fine * c:(fine + 1) * c]
                fine += 1
            s //= 2
            e = jnp.exp(-jnp.abs(x))
            attn = attn + m_ref[l] * lax.dot_general(
                (q * e).astype(BF16), (kk * e).astype(BF16), NT_DIMS, preferred_element_type=F32)
        o = o + jnp.dot(attn.astype(BF16), vb, preferred_element_type=F32)
        khat = (kk * jnp.exp(bl)).astype(BF16)
        st_scr[...] = st * jnp.exp(b[c - 1:c, :]) + lax.dot_general(
            vb, khat, TN_DIMS, preferred_element_type=F32)
        ms = jnp.mean(o * o, axis=-1, keepdims=True)
        y = o * lax.rsqrt(ms + EPS) * og * (g * _sigmoid(g))
        o_ref[rows, :] = y.astype(BF16)
        return carry

    lax.fori_loop(0, seq // c, chunk, 0, unroll=8)


def _hgrn(proj, lb, og, batch, seq):
    t = proj.shape[0]
    wm, mm = _hgrn_consts()
    wm = jnp.asarray(wm, BF16)
    mm = jnp.asarray(mm, F32)
    h = HG_HEADS
    return pl.pallas_call(
        functools.partial(_hgrn_kernel, seq=seq),
        out_shape=jax.ShapeDtypeStruct((t, HG_WIDTH), BF16),
        grid=(batch, h),
        in_specs=[
            pl.BlockSpec((seq, 128), lambda b, i: (b, i)),
            pl.BlockSpec((seq, 128), lambda b, i: (b, h + i)),
            pl.BlockSpec((seq, 128), lambda b, i: (b, 2 * h + i)),
            pl.BlockSpec((seq, 128), lambda b, i: (b, 3 * h + i)),
            pl.BlockSpec((1, 128), lambda b, i: (0, i)),
            pl.BlockSpec((1, 128), lambda b, i: (0, 0)),
            pl.BlockSpec(wm.shape, lambda b, i: (0, 0)),
            pl.BlockSpec(mm.shape, lambda b, i: (0, 0, 0)),
        ],
        out_specs=pl.BlockSpec((seq, 128), lambda b, i: (b, i)),
        scratch_shapes=[pltpu.VMEM((HG_DV, HG_DK), F32)],
        compiler_params=_cparams(("arbitrary", "arbitrary")),
        name="hgrn2",
    )(proj, proj, proj, proj, lb, og, wm, mm)


def _mla_prep_kernel(cq_ref, ckv_ref, kr_ref, pos_ref, qag_ref, kvag_ref, wuq_ref, wukv_ref,
                     gq_ref, gkn_ref, gkr_ref, invf_ref, sgn_ref, cm_ref,
                     q_out, k_out, v_out):
    cq = cq_ref[...]
    cqn = cq * lax.rsqrt(jnp.mean(cq * cq, axis=-1, keepdims=True) + EPS) * qag_ref[...]
    qf = jnp.dot(cqn.astype(BF16), wuq_ref[...], preferred_element_type=F32)
    ckv = ckv_ref[...]
    ckvn = ckv * lax.rsqrt(jnp.mean(ckv * ckv, axis=-1, keepdims=True) + EPS) * kvag_ref[...]
    kvf = jnp.dot(ckvn.astype(BF16), wukv_ref[...], preferred_element_type=F32)

    ang = pos_ref[...].astype(F32) * invf_ref[...]
    cos_t = jnp.cos(ang) * cm_ref[...]
    sin_t = jnp.sin(ang) * sgn_ref[...]

    def rope(r):
        return r * cos_t + (pltpu.roll(r, 96, 1) + pltpu.roll(r, 32, 1)) * sin_t

    kr = kr_ref[...]
    kr_ss = jnp.sum(kr * kr, axis=-1, keepdims=True)
    krr = rope(kr * gkr_ref[...])
    gq = gq_ref[...]
    gkn = gkn_ref[...]
    scale = MLA_QK ** -0.5
    for h in range(MLA_HEADS):
        qh = qf[:, h * MLA_QK_PAD:(h + 1) * MLA_QK_PAD]
        inv = lax.rsqrt(jnp.sum(qh * qh, axis=-1, keepdims=True) / MLA_QK + EPS)
        qn = qh * inv * gq
        qo = jnp.concatenate([qn[:, :MLA_NOPE], rope(qn[:, MLA_NOPE:])], axis=-1) * scale
        q_out[0, h] = qo.astype(BF16)
        kn = kvf[:, h * 256:h * 256 + MLA_NOPE]
        inv = lax.rsqrt((jnp.sum(kn * kn, axis=-1, keepdims=True) + kr_ss) / MLA_QK + EPS)
        k_out[0, h] = jnp.concatenate([kn * inv * gkn, krr * inv], axis=-1).astype(BF16)
        v_out[0, h] = kvf[:, h * 256 + MLA_NOPE:(h + 1) * 256].astype(BF16)


def _mla_prep(proj, pos2, qag, kvag, wuq_p, wukv, gq, gkn, gkr, invf, sgn, cm, batch, seq):
    tm = min(256, seq)
    per_b = seq // tm
    hh = MLA_HEADS
    full = lambda shape: pl.BlockSpec(shape, lambda b, i: tuple(0 for _ in shape))
    return pl.pallas_call(
        _mla_prep_kernel,
        out_shape=(
            jax.ShapeDtypeStruct((batch, hh, seq, MLA_QK_PAD), BF16),
            jax.ShapeDtypeStruct((batch, hh, seq, MLA_QK_PAD), BF16),
            jax.ShapeDtypeStruct((batch, hh, seq, MLA_V), BF16),
        ),
        grid=(batch, per_b),
        in_specs=[
            pl.BlockSpec((tm, MLA_Q_RANK), lambda b, i: (b * per_b + i, COL_CQ // MLA_Q_RANK)),
            pl.BlockSpec((tm, MLA_KV_RANK), lambda b, i: (b * per_b + i, COL_CKV // MLA_KV_RANK)),
            pl.BlockSpec((tm, 128), lambda b, i: (b * per_b + i, COL_KR // 128)),
            pl.BlockSpec((tm, 1), lambda b, i: (b * per_b + i, 0)),
            full((1, MLA_Q_RANK)),
            full((1, MLA_KV_RANK)),
            full(wuq_p.shape),
            full(wukv.shape),
            full((1, MLA_QK_PAD)),
            full((1, 128)),
            full((1, 128)),
            full((1, 128)),
            full((1, 128)),
            full((1, 128)),
        ],
        out_specs=(
            pl.BlockSpec((1, hh, tm, MLA_QK_PAD), lambda b, i: (b, 0, i, 0)),
            pl.BlockSpec((1, hh, tm, MLA_QK_PAD), lambda b, i: (b, 0, i, 0)),
            pl.BlockSpec((1, hh, tm, MLA_V), lambda b, i: (b, 0, i, 0)),
        ),
        compiler_params=_cparams(("arbitrary", "arbitrary")),
        name="mla_prep",
    )(proj, proj, proj, pos2, qag, kvag, wuq_p, wukv, gq, gkn, gkr, invf, sgn, cm)


def _attn_kernel(q_ref, k_ref, v_ref, o_ref, *, tq, seq):
    row = lax.broadcasted_iota(jnp.int32, (tq, tq), 0)
    col = lax.broadcasted_iota(jnp.int32, (tq, tq), 1)
    causal = col <= row
    for i in range(seq // tq):
        lo = i * tq
        q = q_ref[0, 0, lo:lo + tq, :]
        sd = lax.dot_general(q, k_ref[0, 0, lo:lo + tq, :], NT_DIMS, preferred_element_type=F32)
        sd = jnp.where(causal, sd, -jnp.inf)
        m = jnp.max(sd, axis=-1, keepdims=True)
        if i > 0:
            so = lax.dot_general(q, k_ref[0, 0, :lo, :], NT_DIMS, preferred_element_type=F32)
            m = jnp.maximum(m, jnp.max(so, axis=-1, keepdims=True))
        pd = jnp.exp(sd - m)
        l = jnp.sum(pd, axis=-1, keepdims=True)
        o = jnp.dot(pd.astype(BF16), v_ref[0, 0, lo:lo + tq, :], preferred_element_type=F32)
        if i > 0:
            po = jnp.exp(so - m)
            l = l + jnp.sum(po, axis=-1, keepdims=True)
            o = o + jnp.dot(po.astype(BF16), v_ref[0, 0, :lo, :], preferred_element_type=F32)
        o_ref[lo:lo + tq, :] = o / l


def _attention(q, k, v):
    batch, hh, seq, dq = q.shape
    tq = min(256, seq)
    return pl.pallas_call(
        functools.partial(_attn_kernel, tq=tq, seq=seq),
        out_shape=jax.ShapeDtypeStruct((batch * seq, hh * MLA_V), F32),
        grid=(batch, hh),
        in_specs=[
            pl.BlockSpec((1, 1, seq, dq), lambda b, h: (b, h, 0, 0)),
            pl.BlockSpec((1, 1, seq, dq), lambda b, h: (b, h, 0, 0)),
            pl.BlockSpec((1, 1, seq, MLA_V), lambda b, h: (b, h, 0, 0)),
        ],
        out_specs=pl.BlockSpec((seq, MLA_V), lambda b, h: (b, h)),
        compiler_params=_cparams(("arbitrary", "arbitrary")),
        name="mla_attn",
    )(q, k, v)


def _outproj_kernel(ohg_ref, omla_ref, x_ref, gt_ref, sc_ref, sh_ref, og_ref, g2_ref,
                    wo_ref, wrh_ref, wrl_ref, h_out, u_out, lg_out, *, n_sub):
    sub = x_ref.shape[0] // n_sub
    for j in range(n_sub):
        rows = slice(j * sub, (j + 1) * sub)
        om = omla_ref[rows, :]
        omn = om * lax.rsqrt(jnp.mean(om * om, axis=-1, keepdims=True) + EPS) * og_ref[...]
        y = jnp.dot(ohg_ref[rows, :], wo_ref[:HG_WIDTH, :], preferred_element_type=F32)
        y = y + jnp.dot(omn.astype(BF16), wo_ref[HG_WIDTH:, :], preferred_element_type=F32)
        h1 = x_ref[rows, :] + gt_ref[0] * y
        h_out[rows, :] = h1
        u = h1 * lax.rsqrt(jnp.mean(h1 * h1, axis=-1, keepdims=True) + EPS) * g2_ref[...]
        u = u * (1.0 + sc_ref[0]) + sh_ref[0]
        uh, ul = _split_bf16(u)
        u_out[rows] = _rows_to_tiles(uh)
        lg = lax.dot_general(wrh_ref[...], uh, NT_DIMS, preferred_element_type=F32)
        lg = lg + lax.dot_general(wrl_ref[...], uh, NT_DIMS, preferred_element_type=F32)
        lg = lg + lax.dot_general(wrh_ref[...], ul, NT_DIMS, preferred_element_type=F32)
        lg_out[:, rows] = lg


def _outproj(ohg, omla, x2, mod3, og, g2, wo_bf, wr_hi, wr_lo, seq):
    t, d = x2.shape
    tm = min(512, seq)
    per_b = seq // tm
    full = lambda shape: pl.BlockSpec(shape, lambda i: tuple(0 for _ in shape))
    modspec = lambda col: pl.BlockSpec((1, 1, d), lambda i: (i // per_b, 0, col))
    return pl.pallas_call(
        functools.partial(_outproj_kernel, n_sub=2),
        out_shape=(
            jax.ShapeDtypeStruct((t, d), F32),
            jax.ShapeDtypeStruct((t, d // 128, 128), BF16),
            jax.ShapeDtypeStruct((N_EXPERTS, t), F32),
        ),
        grid=(t // tm,),
        in_specs=[
            pl.BlockSpec((tm, HG_WIDTH), lambda i: (i, 0)),
            pl.BlockSpec((tm, MLA_WIDTH), lambda i: (i, 0)),
            pl.BlockSpec((tm, d), lambda i: (i, 0)),
            modspec(2),
            modspec(4),
            modspec(3),
            full((1, MLA_WIDTH)),
            full((1, d)),
            full(wo_bf.shape),
            full(wr_hi.shape),
            full(wr_lo.shape),
        ],
        out_specs=(
            pl.BlockSpec((tm, d), lambda i: (i, 0)),
            pl.BlockSpec((tm, d // 128, 128), lambda i: (i, 0, 0)),
            pl.BlockSpec((N_EXPERTS, tm), lambda i: (0, i)),
        ),
        compiler_params=_cparams(("arbitrary",)),
        name="outproj",
    )(ohg, omla, x2, mod3, mod3, mod3, og, g2, wo_bf, wr_hi, wr_lo)


def _route_kernel(lg_ref, bias_ref, tri_ref, ltri_ref, idx_out, w_out, dest_out, info_out,
                  carry, base, *, tt, n_info):
    phase = pl.program_id(0)
    step = pl.program_id(1)

    @pl.when((phase == 0) & (step == 0))
    def _():
        carry[...] = jnp.zeros_like(carry)

    @pl.when((phase == 1) & (step == 0))
    def _():
        cnt = carry[...]
        padded = jnp.floor((cnt + (MOE_BLK - 1)) / MOE_BLK) * MOE_BLK
        hi, lo = _split_bf16(padded)
        ps = jnp.dot(ltri_ref[...], jnp.concatenate([hi, lo], axis=1), preferred_element_type=F32)
        pad_start = ps[:, :128] + ps[:, 128:]
        base[...] = pad_start
        carry[...] = jnp.zeros_like(carry)
        pad_end = pad_start + padded
        reps = n_info // 128
        pe = jnp.concatenate([pad_end] * reps, axis=1)
        starts = (lax.broadcasted_iota(jnp.int32, (N_EXPERTS, n_info), 1) * MOE_BLK).astype(F32)
        blk_e = jnp.sum(jnp.where(pe <= starts, 1.0, 0.0), axis=0, keepdims=True)
        blk_e = jnp.minimum(blk_e, N_EXPERTS - 1.0)
        n_used = pe[N_EXPERTS - 1:N_EXPERTS, :] / MOE_BLK
        diag = (lax.broadcasted_iota(jnp.int32, (N_EXPERTS, n_info), 0)
                == lax.broadcasted_iota(jnp.int32, (N_EXPERTS, n_info), 1))
        end_blk = jnp.sum(jnp.where(diag, pe, 0.0), axis=0, keepdims=True) / MOE_BLK
        pd = jnp.concatenate([padded] * reps, axis=1)
        num_blk = jnp.sum(jnp.where(diag, pd, 0.0), axis=0, keepdims=True) / MOE_BLK
        info_out[...] = jnp.concatenate(
            [blk_e, n_used, end_blk, num_blk, jnp.zeros((4, n_info), F32)],
            axis=0).astype(jnp.int32)

    ne, ng, eg = N_EXPERTS, N_GROUPS, EXPERTS_PER_GROUP
    neg = -jnp.inf
    sc = _sigmoid(lg_ref[...])
    sel = sc + bias_ref[...]
    sel3 = sel.reshape(ng, eg, tt)
    sub = lax.broadcasted_iota(jnp.int32, (ng, eg, tt), 1)
    m1 = jnp.max(sel3, axis=1, keepdims=True)
    first = jnp.min(jnp.where(sel3 == m1, sub, eg), axis=1, keepdims=True)
    m2 = jnp.max(jnp.where(sub == first, neg, sel3), axis=1, keepdims=True)
    gs = (m1 + m2).reshape(ng, tt)

    gio = lax.broadcasted_iota(jnp.int32, (ng, tt), 0)
    gsel = jnp.zeros((ng, tt), F32)
    for _ in range(TOPK_GROUPS):
        m = jnp.max(gs, axis=0, keepdims=True)
        gi = jnp.min(jnp.where(gs == m, gio, ng), axis=0, keepdims=True)
        hit = gio == gi
        gsel = jnp.where(hit, 1.0, gsel)
        gs = jnp.where(hit, neg, gs)
    gfull = jnp.broadcast_to(gsel.reshape(ng, 1, tt), (ng, eg, tt)).reshape(ne, tt)
    selm = jnp.where(gfull > 0.0, sel, neg)

    eio = lax.broadcasted_iota(jnp.int32, (ne, tt), 0)
    chosen = jnp.zeros((ne, tt), F32)
    idxs, ws = [], []
    for _ in range(TOP_K):
        m = jnp.max(selm, axis=0, keepdims=True)
        ei = jnp.min(jnp.where(selm == m, eio, ne), axis=0, keepdims=True)
        hit = eio == ei
        ws.append(jnp.sum(jnp.where(hit, sc, 0.0), axis=0, keepdims=True))
        idxs.append(ei)
        chosen = jnp.where(hit, 1.0, chosen)
        selm = jnp.where(hit, neg, selm)
    wsum = ws[0]
    for k in range(1, TOP_K):
        wsum = wsum + ws[k]

    tot = jnp.sum(chosen, axis=1, keepdims=True)

    @pl.when(phase == 1)
    def _():
        off = base[...][:, :1] + carry[...][:, :1]
        excl = jnp.dot(chosen.astype(BF16), tri_ref[...], preferred_element_type=F32) + off
        for k in range(TOP_K):
            idx_out[k:k + 1, :] = idxs[k]
            w_out[k:k + 1, :] = ws[k] / wsum * ROUTED_SCALE
            dk = jnp.sum(jnp.where(eio == idxs[k], excl, 0.0), axis=0, keepdims=True)
            dest_out[k:k + 1, :] = dk.astype(jnp.int32)

    carry[...] = carry[...] + tot


def _route(logits_t, bias_col, n_blk):
    ne, t = logits_t.shape
    tt = min(512, t)
    n_info = -(-n_blk // 128) * 128
    tri = jnp.asarray(np.triu(np.ones((tt, tt), np.float32), 1), BF16)
    ltri = jnp.asarray(np.tril(np.ones((ne, ne), np.float32), -1), BF16)
    return pl.pallas_call(
        functools.partial(_route_kernel, tt=tt, n_info=n_info),
        out_shape=(
            jax.ShapeDtypeStruct((TOP_K, t), jnp.int32),
            jax.ShapeDtypeStruct((TOP_K, t), F32),
            jax.ShapeDtypeStruct((TOP_K, t), jnp.int32),
            jax.ShapeDtypeStruct((8, n_info), jnp.int32),
        ),
        grid=(2, t // tt),
        in_specs=[
            pl.BlockSpec((ne, tt), lambda p, i: (0, i)),
            pl.BlockSpec((ne, 1), lambda p, i: (0, 0)),
            pl.BlockSpec((tt, tt), lambda p, i: (0, 0)),
            pl.BlockSpec((ne, ne), lambda p, i: (0, 0)),
        ],
        out_specs=(
            pl.BlockSpec((TOP_K, tt), lambda p, i: (0, i * p)),
            pl.BlockSpec((TOP_K, tt), lambda p, i: (0, i * p)),
            pl.BlockSpec((TOP_K, tt), lambda p, i: (0, i * p)),
            pl.BlockSpec((8, n_info), lambda p, i: (0, 0)),
        ),
        scratch_shapes=[pltpu.VMEM((ne, 128), F32), pltpu.VMEM((ne, 128), F32)],
        compiler_params=_cparams(("arbitrary", "arbitrary")),
        name="route",
    )(logits_t, bias_col, tri, ltri)


def _dispatch_kernel(info_ref, dest_ref, u_hbm, xs_hbm, stage, zbuf, lsem, ssem, zsem, *, tm, n_blk):
    i = pl.program_id(0)
    n = pl.num_programs(0)
    slot = i % 2
    n_used = info_ref[1, 0]

    def load(tile, dst_slot):
        return pltpu.make_async_copy(u_hbm.at[pl.ds(tile * tm, tm)], stage.at[dst_slot],
                                     lsem.at[dst_slot])

    def zero_block(blk):
        return pltpu.make_async_copy(zbuf, xs_hbm.at[pl.ds(blk * MOE_BLK, MOE_BLK)], zsem)

    @pl.when(i == 0)
    def _():
        load(0, 0).start()
        zbuf[...] = jnp.zeros_like(zbuf)
        for e in range(N_EXPERTS):
            @pl.when(info_ref[3, e] > 0)
            def _():
                zero_block(info_ref[2, e] - 1).start()

        def zero_unused(j, carry):
            zero_block(j).start()
            return carry
        lax.fori_loop(n_used, n_blk, zero_unused, 0)
        for e in range(N_EXPERTS):
            @pl.when(info_ref[3, e] > 0)
            def _():
                zero_block(0).wait()

        def wait_unused(j, carry):
            zero_block(0).wait()
            return carry
        lax.fori_loop(n_used, n_blk, wait_unused, 0)

    def scatter_wait(src_slot):
        pltpu.make_async_copy(stage.at[src_slot], xs_hbm.at[pl.ds(0, tm)], ssem.at[src_slot]).wait()

    @pl.when(i >= 1)
    def _():
        for _ in range(TOP_K):
            scatter_wait(1 - slot)

    @pl.when(i + 1 < n)
    def _():
        load(i + 1, 1 - slot).start()

    load(i, slot).wait()
    for k in range(TOP_K):
        for r in range(tm):
            pltpu.make_async_copy(stage.at[slot, r], xs_hbm.at[dest_ref[0, k, r]],
                                  ssem.at[slot]).start(priority=r % 2)

    @pl.when(i == n - 1)
    def _():
        for _ in range(TOP_K):
            scatter_wait(slot)


def _dispatch(info, dest3, u3, n_blk):
    t, nt, _ = u3.shape
    tm = dest3.shape[2]
    return pl.pallas_call(
        functools.partial(_dispatch_kernel, tm=tm, n_blk=n_blk),
        out_shape=jax.ShapeDtypeStruct((n_blk * MOE_BLK, nt, 128), BF16),
        grid_spec=pltpu.PrefetchScalarGridSpec(
            num_scalar_prefetch=1,
            grid=(t // tm,),
            in_specs=[
                pl.BlockSpec((1, TOP_K, tm), lambda i, info: (i, 0, 0), memory_space=pltpu.SMEM),
                pl.BlockSpec(memory_space=pl.ANY),
            ],
            out_specs=pl.BlockSpec(memory_space=pl.ANY),
            scratch_shapes=[
                pltpu.VMEM((2, tm, nt, 128), BF16),
                pltpu.VMEM((MOE_BLK, nt, 128), BF16),
                pltpu.SemaphoreType.DMA((2,)),
                pltpu.SemaphoreType.DMA((2,)),
                pltpu.SemaphoreType.DMA(()),
            ],
        ),
        compiler_params=_cparams(("arbitrary",)),
        name="moe_dispatch",
    )(info, dest3, u3)


def _gmm_kernel(info_ref, xs_ref, wg_hbm, wu_hbm, wd_hbm, y_ref,
                wg_f, wu_f, wd_f, wsem, wg_s, wu_s, wd_s, nswap):
    i = pl.program_id(0)
    n_used = info_ref[1, 0]
    e = info_ref[0, i]
    prev = info_ref[0, jnp.maximum(i - 1, 0)]
    fresh = (i == 0) | (e != prev)

    def fetch(expert, slot):
        return (pltpu.make_async_copy(wg_hbm.at[expert], wg_f.at[slot], wsem.at[slot]),
                pltpu.make_async_copy(wu_hbm.at[expert], wu_f.at[slot], wsem.at[slot]),
                pltpu.make_async_copy(wd_hbm.at[expert], wd_f.at[slot], wsem.at[slot]))

    @pl.when(i == 0)
    def _():
        nswap[0] = 0
        for c in fetch(e, 0):
            c.start()

    @pl.when(fresh & (i < n_used))
    def _():
        slot = nswap[0] % 2
        nswap[0] = nswap[0] + 1
        for c in fetch(e, slot):
            c.wait()
        nxt = info_ref[2, e]

        @pl.when(nxt < n_used)
        def _():
            for c in fetch(info_ref[0, nxt], 1 - slot):
                c.start(priority=1)

        wg_s[...] = wg_f[slot].astype(BF16)
        wu_s[...] = wu_f[slot].astype(BF16)
        wd_s[...] = wd_f[slot].astype(BF16)

    @pl.when(i < n_used)
    def _():
        xb = _tiles_to_rows(xs_ref[...])
        a = jnp.dot(xb, wg_s[...], preferred_element_type=F32)
        u = jnp.dot(xb, wu_s[...], preferred_element_type=F32)
        hid = (a * _sigmoid(a) * u).astype(BF16)
        y = jnp.dot(hid, wd_s[...], preferred_element_type=F32)
        y_ref[...] = _rows_to_tiles(y.astype(BF16))

    @pl.when(i >= n_used)
    def _():
        y_ref[...] = jnp.zeros_like(y_ref)


def _gmm(info, xs3, w_gate, w_up, w_down):
    p, nt, _ = xs3.shape
    d = nt * 128
    n_blk = p // MOE_BLK
    de = w_gate.shape[-1]
    return pl.pallas_call(
        _gmm_kernel,
        out_shape=jax.ShapeDtypeStruct((p, nt, 128), BF16),
        grid_spec=pltpu.PrefetchScalarGridSpec(
            num_scalar_prefetch=1,
            grid=(n_blk,),
            in_specs=[
                pl.BlockSpec((MOE_BLK, nt, 128),
                             lambda i, info: (jnp.minimum(i, info[1, 0] - 1), 0, 0)),
                pl.BlockSpec(memory_space=pl.ANY),
                pl.BlockSpec(memory_space=pl.ANY),
                pl.BlockSpec(memory_space=pl.ANY),
            ],
            out_specs=pl.BlockSpec((MOE_BLK, nt, 128), lambda i, info: (i, 0, 0)),
            scratch_shapes=[
                pltpu.VMEM((2, d, de), F32),
                pltpu.VMEM((2, d, de), F32),
                pltpu.VMEM((2, de, d), F32),
                pltpu.SemaphoreType.DMA((2,)),
                pltpu.VMEM((d, de), BF16),
                pltpu.VMEM((d, de), BF16),
                pltpu.VMEM((de, d), BF16),
                pltpu.SMEM((1,), jnp.int32),
            ],
        ),
        compiler_params=_cparams(("arbitrary",)),
        name="moe_gmm",
    )(info, xs3, w_gate, w_up, w_down)


def _final_kernel(cur_ref, nxt_ref, u_ref, h_ref, gt_ref, wt_ref, wsg_ref, wsu_ref, wsd_ref, y_hbm,
                  o_ref, ybuf_a, ybuf_b, sem, *, tm):
    i = pl.program_id(0)
    n = pl.num_programs(0)
    n_rows = TOP_K * tm

    def issue(idx_ref, tile, buf, sem_i):
        for k in range(TOP_K):
            for r in range(tm):
                pltpu.make_async_copy(y_hbm.at[idx_ref[0, tile * TOP_K + k, r]], buf.at[k * tm + r],
                                      sem.at[sem_i]).start(priority=r % 2)

    def gather_wait(buf, sem_i):
        pltpu.make_async_copy(y_hbm.at[pl.ds(0, n_rows)], buf, sem.at[sem_i]).wait()

    eye = (lax.broadcasted_iota(jnp.int32, (tm, tm), 0)
           == lax.broadcasted_iota(jnp.int32, (tm, tm), 1))

    def compute(tile, buf):
        rows = slice(tile * tm, (tile + 1) * tm)
        ub = _tiles_to_rows(u_ref[rows])
        a = jnp.dot(ub, wsg_ref[...], preferred_element_type=F32)
        b = jnp.dot(ub, wsu_ref[...], preferred_element_type=F32)
        hid = (a * _sigmoid(a) * b).astype(BF16)
        acc = jnp.dot(hid, wsd_ref[...], preferred_element_type=F32)
        wt = wt_ref[rows, :]
        wcat = jnp.concatenate([jnp.where(eye, wt[:, k:k + 1], 0.0) for k in range(TOP_K)],
                               axis=1)
        whi, wlo = _split_bf16(wcat)
        ycat = _tiles_to_rows(buf[...])
        acc = acc + jnp.dot(whi, ycat, preferred_element_type=F32)
        acc = acc + jnp.dot(wlo, ycat, preferred_element_type=F32)
        o_ref[rows, :] = h_ref[rows, :] + gt_ref[0] * acc

    @pl.when(i == 0)
    def _():
        issue(cur_ref, 0, ybuf_a, 0)

    gather_wait(ybuf_a, 0)
    issue(cur_ref, 1, ybuf_b, 1)
    compute(0, ybuf_a)
    gather_wait(ybuf_b, 1)
    issue(nxt_ref, 0, ybuf_a, 0)
    compute(1, ybuf_b)

    @pl.when(i == n - 1)
    def _():
        gather_wait(ybuf_a, 0)


def _final(dest3, u2, h1, mod3, wt, wsg, wsu, wsd, y, seq):
    t, d = h1.shape
    tm = dest3.shape[2]
    n = t // (2 * tm)
    per_b = seq // (2 * tm)
    dest_pairs = dest3.reshape(n, 2 * TOP_K, tm)
    full = lambda shape: pl.BlockSpec(shape, lambda i: tuple(0 for _ in shape))
    return pl.pallas_call(
        functools.partial(_final_kernel, tm=tm),
        out_shape=jax.ShapeDtypeStruct((t, d), F32),
        grid=(n,),
        in_specs=[
            pl.BlockSpec((1, 2 * TOP_K, tm), lambda i: (i, 0, 0), memory_space=pltpu.SMEM),
            pl.BlockSpec((1, 2 * TOP_K, tm), lambda i: (jnp.minimum(i + 1, n - 1), 0, 0),
                         memory_space=pltpu.SMEM),
            pl.BlockSpec((2 * tm, d // 128, 128), lambda i: (i, 0, 0)),
            pl.BlockSpec((2 * tm, d), lambda i: (i, 0)),
            pl.BlockSpec((1, 1, d), lambda i: (i // per_b, 0, 5)),
            pl.BlockSpec((2 * tm, TOP_K), lambda i: (i, 0)),
            full(wsg.shape),
            full(wsu.shape),
            full(wsd.shape),
            pl.BlockSpec(memory_space=pl.ANY),
        ],
        out_specs=pl.BlockSpec((2 * tm, d), lambda i: (i, 0)),
        scratch_shapes=[
            pltpu.VMEM((TOP_K * tm, d // 128, 128), BF16),
            pltpu.VMEM((TOP_K * tm, d // 128, 128), BF16),
            pltpu.SemaphoreType.DMA((2,)),
        ],
        compiler_params=_cparams(("arbitrary",)),
        name="final",
    )(dest_pairs, dest_pairs, u2, h1, mod3, wt, wsg, wsu, wsd, y)


def _layer(h2, mod3, positions, lb, p, batch, seq):
    t, d = h2.shape

    w_in_bf = jnp.pad(p["w_in"], ((0, 0), (0, IN_PAD - IN_COLS))).astype(BF16)
    proj = _inproj(h2, p["norm_mix_g"][None, :], mod3, w_in_bf, seq)

    o_hg = _hgrn(proj, lb[None, :], p["hg_out_g"][None, :], batch, seq)

    wuq = p["mla_w_uq"].reshape(MLA_Q_RANK, MLA_HEADS, MLA_QK)
    wuq_p = jnp.pad(wuq, ((0, 0), (0, 0), (0, MLA_QK_PAD - MLA_QK))).reshape(
        MLA_Q_RANK, MLA_HEADS * MLA_QK_PAD).astype(BF16)
    wukv = p["mla_w_ukv"].astype(BF16)
    gq = jnp.pad(p["mla_q_norm_g"], (0, MLA_QK_PAD - MLA_QK))[None, :]
    gkn = p["mla_k_norm_g"][None, :MLA_NOPE]
    gkr = jnp.pad(p["mla_k_norm_g"][MLA_NOPE:], (0, 128 - MLA_ROPE))[None, :]
    half = MLA_ROPE // 2
    inv_freq = ROPE_THETA ** (-jnp.arange(half, dtype=F32) / half)
    zeros64 = jnp.zeros((128 - MLA_ROPE,), F32)
    invf = jnp.concatenate([inv_freq, inv_freq, zeros64])[None, :]
    sgn = jnp.concatenate([-jnp.ones((half,), F32), jnp.ones((half,), F32), zeros64])[None, :]
    cm = jnp.concatenate([jnp.ones((MLA_ROPE,), F32), zeros64])[None, :]
    q, k, v = _mla_prep(proj, positions.reshape(t, 1), p["mla_q_a_g"][None, :],
                        p["mla_kv_a_g"][None, :], wuq_p, wukv, gq, gkn, gkr, invf, sgn, cm,
                        batch, seq)
    o_mla = _attention(q, k, v)

    wr_t = p["w_router"].T
    wr_hi = wr_t.astype(BF16)
    wr_lo = (wr_t - wr_hi.astype(F32)).astype(BF16)
    h1, u2, logits_t = _outproj(o_hg, o_mla, h2, mod3, p["mla_out_g"][None, :],
                                p["norm_ffn_g"][None, :], p["w_out"].astype(BF16),
                                wr_hi, wr_lo, seq)

    n_blk = (t * TOP_K) // MOE_BLK + N_EXPERTS
    _, top_w, dest, info = _route(logits_t, p["router_bias"][:, None], n_blk)

    tm = min(128, seq)
    dest3 = dest.reshape(TOP_K, t // tm, tm).transpose(1, 0, 2)
    xs = _dispatch(info, dest3, u2, n_blk)
    y = _gmm(info, xs, p["w_gate"], p["w_up"], p["w_down"])
    return _final(dest3, u2, h1, mod3, top_w.T, p["ws_gate"].astype(BF16),
                  p["ws_up"].astype(BF16), p["ws_down"].astype(BF16), y, seq)


def kernel(x, c, positions, w_ada, b_ada, norm_mix_g, norm_ffn_g, w_in, hg_lb_logits, hg_out_g,
           mla_q_a_g, mla_w_uq, mla_kv_a_g, mla_w_ukv, mla_q_norm_g, mla_k_norm_g, mla_out_g,
           w_out, w_router, router_bias, w_gate, w_up, w_down, ws_gate, ws_up, ws_down):
    batch, seq, d = x.shape
    depth = w_ada.shape[0]
    lb_all = jnp.cumsum(jax.nn.softmax(hg_lb_logits.astype(F32), axis=0), axis=0)
    c8 = jnp.pad(c, ((0, 8 - batch), (0, 0)))
    h2 = x.reshape(batch * seq, d)
    for l in range(depth):
        mod = _adaln(c8, w_ada[l], b_ada[l][None, :])[:batch]
        mod3 = mod.reshape(batch, 1, 6 * d)
        p = dict(norm_mix_g=norm_mix_g[l], norm_ffn_g=norm_ffn_g[l], w_in=w_in[l],
                 hg_out_g=hg_out_g[l], mla_q_a_g=mla_q_a_g[l], mla_w_uq=mla_w_uq[l],
                 mla_kv_a_g=mla_kv_a_g[l], mla_w_ukv=mla_w_ukv[l], mla_q_norm_g=mla_q_norm_g[l],
                 mla_k_norm_g=mla_k_norm_g[l], mla_out_g=mla_out_g[l], w_out=w_out[l],
                 w_router=w_router[l], router_bias=router_bias[l], w_gate=w_gate[l],
                 w_up=w_up[l], w_down=w_down[l], ws_gate=ws_gate[l], ws_up=ws_up[l],
                 ws_down=ws_down[l])
        h2 = _layer(h2, mod3, positions, lb_all[l], p, batch, seq)
    return h2.reshape(batch, seq, d)
```

```python
import functools

import numpy as np
import jax
import jax.numpy as jnp
from jax import lax
from jax.experimental import pallas as pl
from jax.experimental.pallas import tpu as pltpu

F32 = jnp.float32
BF16 = jnp.bfloat16

D_MODEL = 2048
EPS = 1e-6

HG_HEADS = 8
HG_DK = 128
HG_DV = 128
HG_WIDTH = HG_HEADS * HG_DV
HG_CHUNK = 128
HG_ROW_LEVEL = 8

MLA_HEADS = 8
MLA_Q_RANK = 512
MLA_KV_RANK = 256
MLA_NOPE = 128
MLA_ROPE = 64
MLA_QK = MLA_NOPE + MLA_ROPE
MLA_QK_PAD = 256
MLA_V = 128
MLA_WIDTH = MLA_HEADS * MLA_V
ROPE_THETA = 10000.0

IN_COLS = 4928
IN_PAD = 5120
COL_CQ = 4096
COL_CKV = 4608
COL_KR = 4864

N_EXPERTS = 64
TOP_K = 8
N_GROUPS = 8
TOPK_GROUPS = 4
EXPERTS_PER_GROUP = N_EXPERTS // N_GROUPS
D_EXPERT = 512
ROUTED_SCALE = 2.5
MOE_BLK = 256
GMM_SUB = 64

VMEM_LIMIT = 56 * 1024 * 1024

NT_DIMS = (((1,), (1,)), ((), ()))
TN_DIMS = (((0,), (0,)), ((), ()))


def _cparams(sem):
    return pltpu.CompilerParams(dimension_semantics=sem, vmem_limit_bytes=VMEM_LIMIT)


def _sigmoid(x):
    return 1.0 / (1.0 + jnp.exp(-x))


def _split_bf16(x):
    hi = x.astype(BF16)
    lo = (x - hi.astype(F32)).astype(BF16)
    return hi, lo


def _rows_to_tiles(x):
    n = x.shape[1] // 128
    return jnp.swapaxes(jnp.stack([x[:, s * 128:(s + 1) * 128] for s in range(n)], axis=0), 0, 1)


def _tiles_to_rows(x3):
    xt = jnp.swapaxes(x3, 0, 1)
    return jnp.concatenate([xt[s] for s in range(x3.shape[1])], axis=1)


def _adaln_kernel(c_ref, w_ref, b_ref, o_ref):
    c = c_ref[...]
    cond = c * _sigmoid(c)
    hi, lo = _split_bf16(cond)
    lhs = jnp.concatenate([hi, lo], axis=0)
    r = jnp.dot(lhs, w_ref[...].astype(BF16), preferred_element_type=F32)
    o_ref[...] = r[:8] + r[8:] + b_ref[...]


def _adaln(c8, w, b):
    d, n = w.shape
    tn = 1024
    return pl.pallas_call(
        _adaln_kernel,
        out_shape=jax.ShapeDtypeStruct((8, n), F32),
        grid=(n // tn,),
        in_specs=[
            pl.BlockSpec((8, d), lambda j: (0, 0)),
            pl.BlockSpec((d, tn), lambda j: (0, j)),
            pl.BlockSpec((1, tn), lambda j: (0, j)),
        ],
        out_specs=pl.BlockSpec((8, tn), lambda j: (0, j)),
        compiler_params=_cparams(("arbitrary",)),
        name="adaln",
    )(c8, w, b)


def _inproj_kernel(x_ref, g_ref, sc_ref, sh_ref, w_ref, o_ref, u_scr, *, tm):
    @pl.when(pl.program_id(1) == 0)
    def _():
        g = g_ref[...]
        sc = 1.0 + sc_ref[0]
        sh = sh_ref[0]

        def body(r, carry):
            rows = pl.ds(pl.multiple_of(r * 128, 128), 128)
            x = x_ref[rows, :]
            ms = jnp.mean(x * x, axis=-1, keepdims=True)
            u = x * lax.rsqrt(ms + EPS) * g * sc + sh
            u_scr[rows, :] = u.astype(BF16)
            return carry

        lax.fori_loop(0, tm // 128, body, 0)

    o_ref[...] = jnp.dot(u_scr[...], w_ref[...], preferred_element_type=F32)


def _inproj(x2, g, mod3, w_bf, seq):
    t, d = x2.shape
    n = w_bf.shape[1]
    tm = min(1024, seq)
    tn = 1280
    per_b = seq // tm
    return pl.pallas_call(
        functools.partial(_inproj_kernel, tm=tm),
        out_shape=jax.ShapeDtypeStruct((t, n), F32),
        grid=(t // tm, n // tn),
        in_specs=[
            pl.BlockSpec((tm, d), lambda i, j: (i, 0)),
            pl.BlockSpec((1, d), lambda i, j: (0, 0)),
            pl.BlockSpec((1, 1, d), lambda i, j: (i // per_b, 0, 1)),
            pl.BlockSpec((1, 1, d), lambda i, j: (i // per_b, 0, 0)),
            pl.BlockSpec((d, tn), lambda i, j: (0, j)),
        ],
        out_specs=pl.BlockSpec((tm, tn), lambda i, j: (i, j)),
        scratch_shapes=[pltpu.VMEM((tm, d), BF16)],
        compiler_params=_cparams(("arbitrary", "arbitrary")),
        name="inproj",
    )(x2, g, mod3, mod3, w_bf)


def _hgrn_consts():
    c = HG_CHUNK
    t = np.arange(c)[:, None]
    j = np.arange(c)[None, :]
    tril = (j <= t).astype(np.float32)
    mats = [tril]
    masks = [np.eye(c, dtype=np.float32)]
    s = c // 2
    while s >= 1:
        ref = (t // (2 * s)) * (2 * s) + s - 1
        if s < HG_ROW_LEVEL:
            mats.append(tril - (j <= ref).astype(np.float32))
        masks.append((((t // s) % 2 == 1) & ((j // s) == (t // s) - 1)).astype(np.float32))
        s //= 2
    w = np.concatenate(mats, 0)
    return np.concatenate([w, w], 1), np.stack(masks, 0)


def _hgrn_kernel(q_ref, f_ref, i_ref, g_ref, lb_ref, og_ref, w_ref, m_ref, o_ref, st_scr, *, seq):
    c = HG_CHUNK
    n_lvl = m_ref.shape[0]
    st_scr[...] = jnp.zeros_like(st_scr)
    lb = lb_ref[...]
    og = og_ref[...]

    def chunk(ci, carry):
        rows = pl.ds(pl.multiple_of(ci * c, c), c)
        q = q_ref[rows, :]
        v = i_ref[rows, :]
        g = g_ref[rows, :]
        f = lb + (1.0 - lb) * _sigmoid(f_ref[rows, :])
        lf = jnp.log(f)
        kk = 1.0 - f
        hi, lo = _split_bf16(lf)
        xs = jnp.dot(w_ref[...], jnp.concatenate([hi, lo], axis=0), preferred_element_type=F32)
        b = xs[0:c]
        bl = b[c - 1:c, :] - b
        st = st_scr[...]
        vb = v.astype(BF16)
        o = lax.dot_general((q * jnp.exp(b)).astype(BF16), st.astype(BF16), NT_DIMS,
                            preferred_element_type=F32)
        attn = m_ref[0] * lax.dot_general(q.astype(BF16), kk.astype(BF16), NT_DIMS,
                                          preferred_element_type=F32)
        s = c // 2
        fine = 1
        for l in range(1, n_lvl):
            if s >= HG_ROW_LEVEL:
                b3 = b.reshape(c // (2 * s), 2 * s, HG_DK)
                x = (b3 - b3[:, s - 1:s, :]).reshape(c, HG_DK)
            else:
                x = xs[fine * c:(fine + 1) * c]
                fine += 1
            s //= 2
            e = jnp.exp(-jnp.abs(x))
            attn = attn + m_ref[l] * lax.dot_general(
                (q * e).astype(BF16), (kk * e).astype(BF16), NT_DIMS, preferred_element_type=F32)
        o = o + jnp.dot(attn.astype(BF16), vb, preferred_element_type=F32)
        khat = (kk * jnp.exp(bl)).astype(BF16)
        st_scr[...] = st * jnp.exp(b[c - 1:c, :]) + lax.dot_general(
            vb, khat, TN_DIMS, preferred_element_type=F32)
        ms = jnp.mean(o * o, axis=-1, keepdims=True)
        y = o * lax.rsqrt(ms + EPS) * og * (g * _sigmoid(g))
        o_ref[rows, :] = y.astype(BF16)
        return carry

    lax.fori_loop(0, seq // c, chunk, 0, unroll=8)


def _hgrn(proj, lb, og, batch, seq):
    t = proj.shape[0]
    wm, mm = _hgrn_consts()
    wm = jnp.asarray(wm, BF16)
    mm = jnp.asarray(mm, F32)
    h = HG_HEADS
    return pl.pallas_call(
        functools.partial(_hgrn_kernel, seq=seq),
        out_shape=jax.ShapeDtypeStruct((t, HG_WIDTH), BF16),
        grid=(batch, h),
        in_specs=[
            pl.BlockSpec((seq, 128), lambda b, i: (b, i)),
            pl.BlockSpec((seq, 128), lambda b, i: (b, h + i)),
            pl.BlockSpec((seq, 128), lambda b, i: (b, 2 * h + i)),
            pl.BlockSpec((seq, 128), lambda b, i: (b, 3 * h + i)),
            pl.BlockSpec((1, 128), lambda b, i: (0, i)),
            pl.BlockSpec((1, 128), lambda b, i: (0, 0)),
            pl.BlockSpec(wm.shape, lambda b, i: (0, 0)),
            pl.BlockSpec(mm.shape, lambda b, i: (0, 0, 0)),
        ],
        out_specs=pl.BlockSpec((seq, 128), lambda b, i: (b, i)),
        scratch_shapes=[pltpu.VMEM((HG_DV, HG_DK), F32)],
        compiler_params=_cparams(("arbitrary", "arbitrary")),
        name="hgrn2",
    )(proj, proj, proj, proj, lb, og, wm, mm)


def _mla_prep_kernel(cq_ref, ckv_ref, kr_ref, pos_ref, qag_ref, kvag_ref, wuq_ref, wukv_ref,
                     gq_ref, gkn_ref, gkr_ref, invf_ref, sgn_ref, cm_ref,
                     q_out, k_out, v_out):
    cq = cq_ref[...]
    cqn = cq * lax.rsqrt(jnp.mean(cq * cq, axis=-1, keepdims=True) + EPS) * qag_ref[...]
    qf = jnp.dot(cqn.astype(BF16), wuq_ref[...], preferred_element_type=F32)
    ckv = ckv_ref[...]
    ckvn = ckv * lax.rsqrt(jnp.mean(ckv * ckv, axis=-1, keepdims=True) + EPS) * kvag_ref[...]
    kvf = jnp.dot(ckvn.astype(BF16), wukv_ref[...], preferred_element_type=F32)

    ang = pos_ref[...].astype(F32) * invf_ref[...]
    cos_t = jnp.cos(ang) * cm_ref[...]
    sin_t = jnp.sin(ang) * sgn_ref[...]

    def rope(r):
        return r * cos_t + (pltpu.roll(r, 96, 1) + pltpu.roll(r, 32, 1)) * sin_t

    kr = kr_ref[...]
    kr_ss = jnp.sum(kr * kr, axis=-1, keepdims=True)
    krr = rope(kr * gkr_ref[...])
    gq = gq_ref[...]
    gkn = gkn_ref[...]
    scale = MLA_QK ** -0.5
    for h in range(MLA_HEADS):
        qh = qf[:, h * MLA_QK_PAD:(h + 1) * MLA_QK_PAD]
        inv = lax.rsqrt(jnp.sum(qh * qh, axis=-1, keepdims=True) / MLA_QK + EPS)
        qn = qh * inv * gq
        qo = jnp.concatenate([qn[:, :MLA_NOPE], rope(qn[:, MLA_NOPE:])], axis=-1) * scale
        q_out[0, h] = qo.astype(BF16)
        kn = kvf[:, h * 256:h * 256 + MLA_NOPE]
        inv = lax.rsqrt((jnp.sum(kn * kn, axis=-1, keepdims=True) + kr_ss) / MLA_QK + EPS)
        k_out[0, h] = jnp.concatenate([kn * inv * gkn, krr * inv], axis=-1).astype(BF16)
        v_out[0, h] = kvf[:, h * 256 + MLA_NOPE:(h + 1) * 256].astype(BF16)


def _mla_prep(proj, pos2, qag, kvag, wuq_p, wukv, gq, gkn, gkr, invf, sgn, cm, batch, seq):
    tm = min(256, seq)
    per_b = seq // tm
    hh = MLA_HEADS
    full = lambda shape: pl.BlockSpec(shape, lambda b, i: tuple(0 for _ in shape))
    return pl.pallas_call(
        _mla_prep_kernel,
        out_shape=(
            jax.ShapeDtypeStruct((batch, hh, seq, MLA_QK_PAD), BF16),
            jax.ShapeDtypeStruct((batch, hh, seq, MLA_QK_PAD), BF16),
            jax.ShapeDtypeStruct((batch, hh, seq, MLA_V), BF16),
        ),
        grid=(batch, per_b),
        in_specs=[
            pl.BlockSpec((tm, MLA_Q_RANK), lambda b, i: (b * per_b + i, COL_CQ // MLA_Q_RANK)),
            pl.BlockSpec((tm, MLA_KV_RANK), lambda b, i: (b * per_b + i, COL_CKV // MLA_KV_RANK)),
            pl.BlockSpec((tm, 128), lambda b, i: (b * per_b + i, COL_KR // 128)),
            pl.BlockSpec((tm, 1), lambda b, i: (b * per_b + i, 0)),
            full((1, MLA_Q_RANK)),
            full((1, MLA_KV_RANK)),
            full(wuq_p.shape),
            full(wukv.shape),
            full((1, MLA_QK_PAD)),
            full((1, 128)),
            full((1, 128)),
            full((1, 128)),
            full((1, 128)),
            full((1, 128)),
        ],
        out_specs=(
            pl.BlockSpec((1, hh, tm, MLA_QK_PAD), lambda b, i: (b, 0, i, 0)),
            pl.BlockSpec((1, hh, tm, MLA_QK_PAD), lambda b, i: (b, 0, i, 0)),
            pl.BlockSpec((1, hh, tm, MLA_V), lambda b, i: (b, 0, i, 0)),
        ),
        compiler_params=_cparams(("arbitrary", "arbitrary")),
        name="mla_prep",
    )(proj, proj, proj, pos2, qag, kvag, wuq_p, wukv, gq, gkn, gkr, invf, sgn, cm)


def _attn_kernel(q_ref, k_ref, v_ref, o_ref, *, tq, seq):
    row = lax.broadcasted_iota(jnp.int32, (tq, tq), 0)
    col = lax.broadcasted_iota(jnp.int32, (tq, tq), 1)
    causal = col <= row
    for i in range(seq // tq):
        lo = i * tq
        q = q_ref[0, 0, lo:lo + tq, :]
        sd = lax.dot_general(q, k_ref[0, 0, lo:lo + tq, :], NT_DIMS, preferred_element_type=F32)
        sd = jnp.where(causal, sd, -jnp.inf)
        m = jnp.max(sd, axis=-1, keepdims=True)
        if i > 0:
            so = lax.dot_general(q, k_ref[0, 0, :lo, :], NT_DIMS, preferred_element_type=F32)
            m = jnp.maximum(m, jnp.max(so, axis=-1, keepdims=True))
        pd = jnp.exp(sd - m)
        l = jnp.sum(pd, axis=-1, keepdims=True)
        o = jnp.dot(pd.astype(BF16), v_ref[0, 0, lo:lo + tq, :], preferred_element_type=F32)
        if i > 0:
            po = jnp.exp(so - m)
            l = l + jnp.sum(po, axis=-1, keepdims=True)
            o = o + jnp.dot(po.astype(BF16), v_ref[0, 0, :lo, :], preferred_element_type=F32)
        o_ref[lo:lo + tq, :] = o / l


def _attention(q, k, v):
    batch, hh, seq, dq = q.shape
    tq = min(256, seq)
    return pl.pallas_call(
        functools.partial(_attn_kernel, tq=tq, seq=seq),
        out_shape=jax.ShapeDtypeStruct((batch * seq, hh * MLA_V), F32),
        grid=(batch, hh),
        in_specs=[
            pl.BlockSpec((1, 1, seq, dq), lambda b, h: (b, h, 0, 0)),
            pl.BlockSpec((1, 1, seq, dq), lambda b, h: (b, h, 0, 0)),
            pl.BlockSpec((1, 1, seq, MLA_V), lambda b, h: (b, h, 0, 0)),
        ],
        out_specs=pl.BlockSpec((seq, MLA_V), lambda b, h: (b, h)),
        compiler_params=_cparams(("arbitrary", "arbitrary")),
        name="mla_attn",
    )(q, k, v)


def _outproj_kernel(ohg_ref, omla_ref, x_ref, gt_ref, sc_ref, sh_ref, og_ref, g2_ref,
                    wo_ref, wrh_ref, wrl_ref, h_out, u_out, lg_out, *, n_sub):
    sub = x_ref.shape[0] // n_sub
    for j in range(n_sub):
        rows = slice(j * sub, (j + 1) * sub)
        om = omla_ref[rows, :]
        omn = om * lax.rsqrt(jnp.mean(om * om, axis=-1, keepdims=True) + EPS) * og_ref[...]
        y = jnp.dot(ohg_ref[rows, :], wo_ref[:HG_WIDTH, :], preferred_element_type=F32)
        y = y + jnp.dot(omn.astype(BF16), wo_ref[HG_WIDTH:, :], preferred_element_type=F32)
        h1 = x_ref[rows, :] + gt_ref[0] * y
        h_out[rows, :] = h1
        u = h1 * lax.rsqrt(jnp.mean(h1 * h1, axis=-1, keepdims=True) + EPS) * g2_ref[...]
        u = u * (1.0 + sc_ref[0]) + sh_ref[0]
        uh, ul = _split_bf16(u)
        u_out[rows] = _rows_to_tiles(uh)
        lg = lax.dot_general(wrh_ref[...], uh, NT_DIMS, preferred_element_type=F32)
        lg = lg + lax.dot_general(wrl_ref[...], uh, NT_DIMS, preferred_element_type=F32)
        lg = lg + lax.dot_general(wrh_ref[...], ul, NT_DIMS, preferred_element_type=F32)
        lg_out[:, rows] = lg


def _outproj(ohg, omla, x2, mod3, og, g2, wo_bf, wr_hi, wr_lo, seq):
    t, d = x2.shape
    tm = min(512, seq)
    per_b = seq // tm
    full = lambda shape: pl.BlockSpec(shape, lambda i: tuple(0 for _ in shape))
    modspec = lambda col: pl.BlockSpec((1, 1, d), lambda i: (i // per_b, 0, col))
    return pl.pallas_call(
        functools.partial(_outproj_kernel, n_sub=2),
        out_shape=(
            jax.ShapeDtypeStruct((t, d), F32),
            jax.ShapeDtypeStruct((t, d // 128, 128), BF16),
            jax.ShapeDtypeStruct((N_EXPERTS, t), F32),
        ),
        grid=(t // tm,),
        in_specs=[
            pl.BlockSpec((tm, HG_WIDTH), lambda i: (i, 0)),
            pl.BlockSpec((tm, MLA_WIDTH), lambda i: (i, 0)),
            pl.BlockSpec((tm, d), lambda i: (i, 0)),
            modspec(2),
            modspec(4),
            modspec(3),
            full((1, MLA_WIDTH)),
            full((1, d)),
            full(wo_bf.shape),
            full(wr_hi.shape),
            full(wr_lo.shape),
        ],
        out_specs=(
            pl.BlockSpec((tm, d), lambda i: (i, 0)),
            pl.BlockSpec((tm, d // 128, 128), lambda i: (i, 0, 0)),
            pl.BlockSpec((N_EXPERTS, tm), lambda i: (0, i)),
        ),
        compiler_params=_cparams(("arbitrary",)),
        name="outproj",
    )(ohg, omla, x2, mod3, mod3, mod3, og, g2, wo_bf, wr_hi, wr_lo)


def _route_kernel(lg_ref, bias_ref, tri_ref, ltri_ref, idx_out, w_out, dest_out, info_out,
                  carry, base, *, tt, n_info):
    phase = pl.program_id(0)
    step = pl.program_id(1)

    @pl.when((phase == 0) & (step == 0))
    def _():
        carry[...] = jnp.zeros_like(carry)

    @pl.when((phase == 1) & (step == 0))
    def _():
        cnt = carry[...]
        padded = jnp.floor((cnt + (MOE_BLK - 1)) / MOE_BLK) * MOE_BLK
        hi, lo = _split_bf16(padded)
        ps = jnp.dot(ltri_ref[...], jnp.concatenate([hi, lo], axis=1), preferred_element_type=F32)
        pad_start = ps[:, :128] + ps[:, 128:]
        base[...] = pad_start
        carry[...] = jnp.zeros_like(carry)
        pad_end = pad_start + padded
        reps = n_info // 128
        pe = jnp.concatenate([pad_end] * reps, axis=1)
        starts = (lax.broadcasted_iota(jnp.int32, (N_EXPERTS, n_info), 1) * MOE_BLK).astype(F32)
        blk_e = jnp.sum(jnp.where(pe <= starts, 1.0, 0.0), axis=0, keepdims=True)
        blk_e = jnp.minimum(blk_e, N_EXPERTS - 1.0)
        n_used = pe[N_EXPERTS - 1:N_EXPERTS, :] / MOE_BLK
        diag = (lax.broadcasted_iota(jnp.int32, (N_EXPERTS, n_info), 0)
                == lax.broadcasted_iota(jnp.int32, (N_EXPERTS, n_info), 1))
        end_blk = jnp.sum(jnp.where(diag, pe, 0.0), axis=0, keepdims=True) / MOE_BLK
        pd = jnp.concatenate([padded] * reps, axis=1)
        num_blk = jnp.sum(jnp.where(diag, pd, 0.0), axis=0, keepdims=True) / MOE_BLK
        ps = jnp.concatenate([pad_start] * reps, axis=1)
        ve = jnp.concatenate([pad_start + cnt] * reps, axis=1)
        valid = jnp.sum(jnp.maximum(jnp.minimum(ve, starts + MOE_BLK) - jnp.maximum(ps, starts), 0.0),
                        axis=0, keepdims=True)
        info_out[...] = jnp.concatenate(
            [blk_e, n_used, end_blk, num_blk, valid, jnp.zeros((3, n_info), F32)],
            axis=0).astype(jnp.int32)

    ne, ng, eg = N_EXPERTS, N_GROUPS, EXPERTS_PER_GROUP
    neg = -jnp.inf
    sc = _sigmoid(lg_ref[...])
    sel = sc + bias_ref[...]
    sel3 = sel.reshape(ng, eg, tt)
    sub = lax.broadcasted_iota(jnp.int32, (ng, eg, tt), 1)
    m1 = jnp.max(sel3, axis=1, keepdims=True)
    first = jnp.min(jnp.where(sel3 == m1, sub, eg), axis=1, keepdims=True)
    m2 = jnp.max(jnp.where(sub == first, neg, sel3), axis=1, keepdims=True)
    gs = (m1 + m2).reshape(ng, tt)

    gio = lax.broadcasted_iota(jnp.int32, (ng, tt), 0)
    gsel = jnp.zeros((ng, tt), F32)
    for _ in range(TOPK_GROUPS):
        m = jnp.max(gs, axis=0, keepdims=True)
        gi = jnp.min(jnp.where(gs == m, gio, ng), axis=0, keepdims=True)
        hit = gio == gi
        gsel = jnp.where(hit, 1.0, gsel)
        gs = jnp.where(hit, neg, gs)
    gfull = jnp.broadcast_to(gsel.reshape(ng, 1, tt), (ng, eg, tt)).reshape(ne, tt)
    selm = jnp.where(gfull > 0.0, sel, neg)

    eio = lax.broadcasted_iota(jnp.int32, (ne, tt), 0)
    chosen = jnp.zeros((ne, tt), F32)
    idxs, ws = [], []
    for _ in range(TOP_K):
        m = jnp.max(selm, axis=0, keepdims=True)
        ei = jnp.min(jnp.where(selm == m, eio, ne), axis=0, keepdims=True)
        hit = eio == ei
        ws.append(jnp.sum(jnp.where(hit, sc, 0.0), axis=0, keepdims=True))
        idxs.append(ei)
        chosen = jnp.where(hit, 1.0, chosen)
        selm = jnp.where(hit, neg, selm)
    wsum = ws[0]
    for k in range(1, TOP_K):
        wsum = wsum + ws[k]

    tot = jnp.sum(chosen, axis=1, keepdims=True)

    @pl.when(phase == 1)
    def _():
        off = base[...][:, :1] + carry[...][:, :1]
        excl = jnp.dot(chosen.astype(BF16), tri_ref[...], preferred_element_type=F32) + off
        for k in range(TOP_K):
            idx_out[k:k + 1, :] = idxs[k]
            w_out[k:k + 1, :] = ws[k] / wsum * ROUTED_SCALE
            dk = jnp.sum(jnp.where(eio == idxs[k], excl, 0.0), axis=0, keepdims=True)
            dest_out[k:k + 1, :] = dk.astype(jnp.int32)

    carry[...] = carry[...] + tot


def _route(logits_t, bias_col, n_blk):
    ne, t = logits_t.shape
    tt = min(512, t)
    n_info = -(-n_blk // 128) * 128
    tri = jnp.asarray(np.triu(np.ones((tt, tt), np.float32), 1), BF16)
    ltri = jnp.asarray(np.tril(np.ones((ne, ne), np.float32), -1), BF16)
    return pl.pallas_call(
        functools.partial(_route_kernel, tt=tt, n_info=n_info),
        out_shape=(
            jax.ShapeDtypeStruct((TOP_K, t), jnp.int32),
            jax.ShapeDtypeStruct((TOP_K, t), F32),
            jax.ShapeDtypeStruct((TOP_K, t), jnp.int32),
            jax.ShapeDtypeStruct((8, n_info), jnp.int32),
        ),
        grid=(2, t // tt),
        in_specs=[
            pl.BlockSpec((ne, tt), lambda p, i: (0, i)),
            pl.BlockSpec((ne, 1), lambda p, i: (0, 0)),
            pl.BlockSpec((tt, tt), lambda p, i: (0, 0)),
            pl.BlockSpec((ne, ne), lambda p, i: (0, 0)),
        ],
        out_specs=(
            pl.BlockSpec((TOP_K, tt), lambda p, i: (0, i * p)),
            pl.BlockSpec((TOP_K, tt), lambda p, i: (0, i * p)),
            pl.BlockSpec((TOP_K, tt), lambda p, i: (0, i * p)),
            pl.BlockSpec((8, n_info), lambda p, i: (0, 0)),
        ),
        scratch_shapes=[pltpu.VMEM((ne, 128), F32), pltpu.VMEM((ne, 128), F32)],
        compiler_params=_cparams(("arbitrary", "arbitrary")),
        name="route",
    )(logits_t, bias_col, tri, ltri)


def _dispatch_kernel(info_ref, dest_ref, u_hbm, xs_hbm, stage, zbuf, lsem, ssem, zsem, *, tm, n_blk):
    i = pl.program_id(0)
    n = pl.num_programs(0)
    slot = i % 2
    n_used = info_ref[1, 0]

    def load(tile, dst_slot):
        return pltpu.make_async_copy(u_hbm.at[pl.ds(tile * tm, tm)], stage.at[dst_slot],
                                     lsem.at[dst_slot])

    def zero_block(blk):
        return pltpu.make_async_copy(zbuf, xs_hbm.at[pl.ds(blk * MOE_BLK, MOE_BLK)], zsem)

    @pl.when(i == 0)
    def _():
        load(0, 0).start()
        zbuf[...] = jnp.zeros_like(zbuf)
        for e in range(N_EXPERTS):
            @pl.when(info_ref[3, e] > 0)
            def _():
                zero_block(info_ref[2, e] - 1).start()

        def zero_unused(j, carry):
            zero_block(j).start()
            return carry
        lax.fori_loop(n_used, n_blk, zero_unused, 0)
        for e in range(N_EXPERTS):
            @pl.when(info_ref[3, e] > 0)
            def _():
                zero_block(0).wait()

        def wait_unused(j, carry):
            zero_block(0).wait()
            return carry
        lax.fori_loop(n_used, n_blk, wait_unused, 0)

    def scatter_wait(src_slot):
        pltpu.make_async_copy(stage.at[src_slot], xs_hbm.at[pl.ds(0, tm)], ssem.at[src_slot]).wait()

    @pl.when(i >= 1)
    def _():
        for _ in range(TOP_K):
            scatter_wait(1 - slot)

    @pl.when(i + 1 < n)
    def _():
        load(i + 1, 1 - slot).start()

    load(i, slot).wait()
    for k in range(TOP_K):
        for r in range(tm):
            pltpu.make_async_copy(stage.at[slot, r], xs_hbm.at[dest_ref[0, k, r]],
                                  ssem.at[slot]).start(priority=r % 2)

    @pl.when(i == n - 1)
    def _():
        for _ in range(TOP_K):
            scatter_wait(slot)


def _dispatch(info, dest3, u3, n_blk):
    t, nt, _ = u3.shape
    tm = dest3.shape[2]
    return pl.pallas_call(
        functools.partial(_dispatch_kernel, tm=tm, n_blk=n_blk),
        out_shape=jax.ShapeDtypeStruct((n_blk * MOE_BLK, nt, 128), BF16),
        grid_spec=pltpu.PrefetchScalarGridSpec(
            num_scalar_prefetch=1,
            grid=(t // tm,),
            in_specs=[
                pl.BlockSpec((1, TOP_K, tm), lambda i, info: (i, 0, 0), memory_space=pltpu.SMEM),
                pl.BlockSpec(memory_space=pl.ANY),
            ],
            out_specs=pl.BlockSpec(memory_space=pl.ANY),
            scratch_shapes=[
                pltpu.VMEM((2, tm, nt, 128), BF16),
                pltpu.VMEM((MOE_BLK, nt, 128), BF16),
                pltpu.SemaphoreType.DMA((2,)),
                pltpu.SemaphoreType.DMA((2,)),
                pltpu.SemaphoreType.DMA(()),
            ],
        ),
        compiler_params=_cparams(("arbitrary",)),
        name="moe_dispatch",
    )(info, dest3, u3)


def _gmm_kernel(info_ref, xs_ref, wg_hbm, wu_hbm, wd_hbm, y_ref,
                wg_f, wu_f, wd_f, wsem, wg_s, wu_s, wd_s, nswap):
    i = pl.program_id(0)
    n_used = info_ref[1, 0]
    e = info_ref[0, i]
    prev = info_ref[0, jnp.maximum(i - 1, 0)]
    fresh = (i == 0) | (e != prev)

    def fetch(expert, slot):
        return (pltpu.make_async_copy(wg_hbm.at[expert], wg_f.at[slot], wsem.at[slot]),
                pltpu.make_async_copy(wu_hbm.at[expert], wu_f.at[slot], wsem.at[slot]),
                pltpu.make_async_copy(wd_hbm.at[expert], wd_f.at[slot], wsem.at[slot]))

    @pl.when(i == 0)
    def _():
        nswap[0] = 0
        for c in fetch(e, 0):
            c.start()

    @pl.when(fresh & (i < n_used))
    def _():
        slot = nswap[0] % 2
        nswap[0] = nswap[0] + 1
        for c in fetch(e, slot):
            c.wait()
        nxt = info_ref[2, e]

        @pl.when(nxt < n_used)
        def _():
            for c in fetch(info_ref[0, nxt], 1 - slot):
                c.start(priority=1)

        wg_s[...] = wg_f[slot].astype(BF16)
        wu_s[...] = wu_f[slot].astype(BF16)
        wd_s[...] = wd_f[slot].astype(BF16)

    n_quarters = (info_ref[4, i] + GMM_SUB - 1) // GMM_SUB
    for nq in range(1, MOE_BLK // GMM_SUB + 1):
        m = nq * GMM_SUB

        @pl.when((i < n_used) & (n_quarters == nq))
        def _():
            xb = _tiles_to_rows(xs_ref[:m])
            a = jnp.dot(xb, wg_s[...], preferred_element_type=F32)
            u = jnp.dot(xb, wu_s[...], preferred_element_type=F32)
            hid = (a * _sigmoid(a) * u).astype(BF16)
            y = jnp.dot(hid, wd_s[...], preferred_element_type=F32)
            y_ref[:m] = _rows_to_tiles(y.astype(BF16))
            if m < MOE_BLK:
                y_ref[m:] = jnp.zeros((MOE_BLK - m,) + y_ref.shape[1:], BF16)

    @pl.when(i >= n_used)
    def _():
        y_ref[...] = jnp.zeros_like(y_ref)


def _gmm(info, xs3, w_gate, w_up, w_down):
    p, nt, _ = xs3.shape
    d = nt * 128
    n_blk = p // MOE_BLK
    de = w_gate.shape[-1]
    return pl.pallas_call(
        _gmm_kernel,
        out_shape=jax.ShapeDtypeStruct((p, nt, 128), BF16),
        grid_spec=pltpu.PrefetchScalarGridSpec(
            num_scalar_prefetch=1,
            grid=(n_blk,),
            in_specs=[
                pl.BlockSpec((MOE_BLK, nt, 128),
                             lambda i, info: (jnp.minimum(i, info[1, 0] - 1), 0, 0)),
                pl.BlockSpec(memory_space=pl.ANY),
                pl.BlockSpec(memory_space=pl.ANY),
                pl.BlockSpec(memory_space=pl.ANY),
            ],
            out_specs=pl.BlockSpec((MOE_BLK, nt, 128), lambda i, info: (i, 0, 0)),
            scratch_shapes=[
                pltpu.VMEM((2, d, de), F32),
                pltpu.VMEM((2, d, de), F32),
                pltpu.VMEM((2, de, d), F32),
                pltpu.SemaphoreType.DMA((2,)),
                pltpu.VMEM((d, de), BF16),
                pltpu.VMEM((d, de), BF16),
                pltpu.VMEM((de, d), BF16),
                pltpu.SMEM((1,), jnp.int32),
            ],
        ),
        compiler_params=_cparams(("arbitrary",)),
        name="moe_gmm",
    )(info, xs3, w_gate, w_up, w_down)


def _final_kernel(cur_ref, nxt_ref, u_ref, h_ref, gt_ref, wt_ref, wsg_ref, wsu_ref, wsd_ref, y_hbm,
                  o_ref, ybuf_a, ybuf_b, sem, *, tm):
    i = pl.program_id(0)
    n = pl.num_programs(0)
    n_rows = TOP_K * tm

    def issue(idx_ref, tile, buf, sem_i):
        for k in range(TOP_K):
            for r in range(tm):
                pltpu.make_async_copy(y_hbm.at[idx_ref[0, tile * TOP_K + k, r]], buf.at[k * tm + r],
                                      sem.at[sem_i]).start(priority=r % 2)

    def gather_wait(buf, sem_i):
        pltpu.make_async_copy(y_hbm.at[pl.ds(0, n_rows)], buf, sem.at[sem_i]).wait()

    eye = (lax.broadcasted_iota(jnp.int32, (tm, tm), 0)
           == lax.broadcasted_iota(jnp.int32, (tm, tm), 1))

    def compute(tile, buf):
        rows = slice(tile * tm, (tile + 1) * tm)
        ub = _tiles_to_rows(u_ref[rows])
        a = jnp.dot(ub, wsg_ref[...], preferred_element_type=F32)
        b = jnp.dot(ub, wsu_ref[...], preferred_element_type=F32)
        hid = (a * _sigmoid(a) * b).astype(BF16)
        acc = jnp.dot(hid, wsd_ref[...], preferred_element_type=F32)
        wt = wt_ref[rows, :]
        wcat = jnp.concatenate([jnp.where(eye, wt[:, k:k + 1], 0.0) for k in range(TOP_K)],
                               axis=1)
        whi, wlo = _split_bf16(wcat)
        ycat = _tiles_to_rows(buf[...])
        acc = acc + jnp.dot(whi, ycat, preferred_element_type=F32)
        acc = acc + jnp.dot(wlo, ycat, preferred_element_type=F32)
        o_ref[rows, :] = h_ref[rows, :] + gt_ref[0] * acc

    @pl.when(i == 0)
    def _():
        issue(cur_ref, 0, ybuf_a, 0)

    gather_wait(ybuf_a, 0)
    issue(cur_ref, 1, ybuf_b, 1)
    compute(0, ybuf_a)
    gather_wait(ybuf_b, 1)
    issue(nxt_ref, 0, ybuf_a, 0)
    compute(1, ybuf_b)

    @pl.when(i == n - 1)
    def _():
        gather_wait(ybuf_a, 0)


def _final(dest3, u2, h1, mod3, wt, wsg, wsu, wsd, y, seq):
    t, d = h1.shape
    tm = dest3.shape[2]
    n = t // (2 * tm)
    per_b = seq // (2 * tm)
    dest_pairs = dest3.reshape(n, 2 * TOP_K, tm)
    full = lambda shape: pl.BlockSpec(shape, lambda i: tuple(0 for _ in shape))
    return pl.pallas_call(
        functools.partial(_final_kernel, tm=tm),
        out_shape=jax.ShapeDtypeStruct((t, d), F32),
        grid=(n,),
        in_specs=[
            pl.BlockSpec((1, 2 * TOP_K, tm), lambda i: (i, 0, 0), memory_space=pltpu.SMEM),
            pl.BlockSpec((1, 2 * TOP_K, tm), lambda i: (jnp.minimum(i + 1, n - 1), 0, 0),
                         memory_space=pltpu.SMEM),
            pl.BlockSpec((2 * tm, d // 128, 128), lambda i: (i, 0, 0)),
            pl.BlockSpec((2 * tm, d), lambda i: (i, 0)),
            pl.BlockSpec((1, 1, d), lambda i: (i // per_b, 0, 5)),
            pl.BlockSpec((2 * tm, TOP_K), lambda i: (i, 0)),
            full(wsg.shape),
            full(wsu.shape),
            full(wsd.shape),
            pl.BlockSpec(memory_space=pl.ANY),
        ],
        out_specs=pl.BlockSpec((2 * tm, d), lambda i: (i, 0)),
        scratch_shapes=[
            pltpu.VMEM((TOP_K * tm, d // 128, 128), BF16),
            pltpu.VMEM((TOP_K * tm, d // 128, 128), BF16),
            pltpu.SemaphoreType.DMA((2,)),
        ],
        compiler_params=_cparams(("arbitrary",)),
        name="final",
    )(dest_pairs, dest_pairs, u2, h1, mod3, wt, wsg, wsu, wsd, y)


def _layer(h2, mod3, positions, lb, p, batch, seq):
    t, d = h2.shape

    w_in_bf = jnp.pad(p["w_in"], ((0, 0), (0, IN_PAD - IN_COLS))).astype(BF16)
    proj = _inproj(h2, p["norm_mix_g"][None, :], mod3, w_in_bf, seq)

    o_hg = _hgrn(proj, lb[None, :], p["hg_out_g"][None, :], batch, seq)

    wuq = p["mla_w_uq"].reshape(MLA_Q_RANK, MLA_HEADS, MLA_QK)
    wuq_p = jnp.pad(wuq, ((0, 0), (0, 0), (0, MLA_QK_PAD - MLA_QK))).reshape(
        MLA_Q_RANK, MLA_HEADS * MLA_QK_PAD).astype(BF16)
    wukv = p["mla_w_ukv"].astype(BF16)
    gq = jnp.pad(p["mla_q_norm_g"], (0, MLA_QK_PAD - MLA_QK))[None, :]
    gkn = p["mla_k_norm_g"][None, :MLA_NOPE]
    gkr = jnp.pad(p["mla_k_norm_g"][MLA_NOPE:], (0, 128 - MLA_ROPE))[None, :]
    half = MLA_ROPE // 2
    inv_freq = ROPE_THETA ** (-jnp.arange(half, dtype=F32) / half)
    zeros64 = jnp.zeros((128 - MLA_ROPE,), F32)
    invf = jnp.concatenate([inv_freq, inv_freq, zeros64])[None, :]
    sgn = jnp.concatenate([-jnp.ones((half,), F32), jnp.ones((half,), F32), zeros64])[None, :]
    cm = jnp.concatenate([jnp.ones((MLA_ROPE,), F32), zeros64])[None, :]
    q, k, v = _mla_prep(proj, positions.reshape(t, 1), p["mla_q_a_g"][None, :],
                        p["mla_kv_a_g"][None, :], wuq_p, wukv, gq, gkn, gkr, invf, sgn, cm,
                        batch, seq)
    o_mla = _attention(q, k, v)

    wr_t = p["w_router"].T
    wr_hi = wr_t.astype(BF16)
    wr_lo = (wr_t - wr_hi.astype(F32)).astype(BF16)
    h1, u2, logits_t = _outproj(o_hg, o_mla, h2, mod3, p["mla_out_g"][None, :],
                                p["norm_ffn_g"][None, :], p["w_out"].astype(BF16),
                                wr_hi, wr_lo, seq)

    n_blk = (t * TOP_K) // MOE_BLK + N_EXPERTS
    _, top_w, dest, info = _route(logits_t, p["router_bias"][:, None], n_blk)

    tm = min(128, seq)
    dest3 = dest.reshape(TOP_K, t // tm, tm).transpose(1, 0, 2)
    xs = _dispatch(info, dest3, u2, n_blk)
    y = _gmm(info, xs, p["w_gate"], p["w_up"], p["w_down"])
    return _final(dest3, u2, h1, mod3, top_w.T, p["ws_gate"].astype(BF16),
                  p["ws_up"].astype(BF16), p["ws_down"].astype(BF16), y, seq)


def kernel(x, c, positions, w_ada, b_ada, norm_mix_g, norm_ffn_g, w_in, hg_lb_logits, hg_out_g,
           mla_q_a_g, mla_w_uq, mla_kv_a_g, mla_w_ukv, mla_q_norm_g, mla_k_norm_g, mla_out_g,
           w_out, w_router, router_bias, w_gate, w_up, w_down, ws_gate, ws_up, ws_down):
    batch, seq, d = x.shape
    depth = w_ada.shape[0]
    lb_all = jnp.cumsum(jax.nn.softmax(hg_lb_logits.astype(F32), axis=0), axis=0)
    c8 = jnp.pad(c, ((0, 8 - batch), (0, 0)))
    h2 = x.reshape(batch * seq, d)
    for l in range(depth):
        mod = _adaln(c8, w_ada[l], b_ada[l][None, :])[:batch]
        mod3 = mod.reshape(batch, 1, 6 * d)
        p = dict(norm_mix_g=norm_mix_g[l], norm_ffn_g=norm_ffn_g[l], w_in=w_in[l],
                 hg_out_g=hg_out_g[l], mla_q_a_g=mla_q_a_g[l], mla_w_uq=mla_w_uq[l],
                 mla_kv_a_g=mla_kv_a_g[l], mla_w_ukv=mla_w_ukv[l], mla_q_norm_g=mla_q_norm_g[l],
                 mla_k_norm_g=mla_k_norm_g[l], mla_out_g=mla_out_g[l], w_out=w_out[l],
                 w_router=w_router[l], router_bias=router_bias[l], w_gate=w_gate[l],
                 w_up=w_up[l], w_down=w_down[l], ws_gate=ws_gate[l], ws_up=ws_up[l],
                 ws_down=ws_down[l])
        h2 = _layer(h2, mod3, positions, lb_all[l], p, batch, seq)
    return h2.reshape(batch, seq, d)
```

```python
import functools

import numpy as np
import jax
import jax.numpy as jnp
from jax import lax
from jax.experimental import pallas as pl
from jax.experimental.pallas import tpu as pltpu

F32 = jnp.float32
BF16 = jnp.bfloat16

D_MODEL = 2048
EPS = 1e-6

HG_HEADS = 8
HG_DK = 128
HG_DV = 128
HG_WIDTH = HG_HEADS * HG_DV
HG_CHUNK = 128
HG_ROW_LEVEL = 8

MLA_HEADS = 8
MLA_Q_RANK = 512
MLA_KV_RANK = 256
MLA_NOPE = 128
MLA_ROPE = 64
MLA_QK = MLA_NOPE + MLA_ROPE
MLA_QK_PAD = 256
MLA_V = 128
MLA_WIDTH = MLA_HEADS * MLA_V
ROPE_THETA = 10000.0

COL_CQ = 4096

N_EXPERTS = 64
TOP_K = 8
N_GROUPS = 8
TOPK_GROUPS = 4
EXPERTS_PER_GROUP = N_EXPERTS // N_GROUPS
D_EXPERT = 512
ROUTED_SCALE = 2.5
MOE_BLK = 256

VMEM_LIMIT = 56 * 1024 * 1024

NT_DIMS = (((1,), (1,)), ((), ()))
TN_DIMS = (((0,), (0,)), ((), ()))


def _cparams(sem):
    return pltpu.CompilerParams(dimension_semantics=sem, vmem_limit_bytes=VMEM_LIMIT)


def _sigmoid(x):
    return 1.0 / (1.0 + jnp.exp(-x))


def _split_bf16(x):
    hi = x.astype(BF16)
    lo = (x - hi.astype(F32)).astype(BF16)
    return hi, lo


def _rows_to_tiles(x):
    n = x.shape[1] // 128
    return jnp.swapaxes(jnp.stack([x[:, s * 128:(s + 1) * 128] for s in range(n)], axis=0), 0, 1)


def _tiles_to_rows(x3):
    xt = jnp.swapaxes(x3, 0, 1)
    return jnp.concatenate([xt[s] for s in range(x3.shape[1])], axis=1)


def _adaln_kernel(c_ref, w_ref, b_ref, o_ref):
    c = c_ref[...]
    cond = c * _sigmoid(c)
    hi, lo = _split_bf16(cond)
    lhs = jnp.concatenate([hi, lo], axis=0)
    r = jnp.dot(lhs, w_ref[...].astype(BF16), preferred_element_type=F32)
    o_ref[...] = r[:8] + r[8:] + b_ref[...]


def _adaln(c8, w, b):
    d, n = w.shape
    tn = 1024
    return pl.pallas_call(
        _adaln_kernel,
        out_shape=jax.ShapeDtypeStruct((8, n), F32),
        grid=(n // tn,),
        in_specs=[
            pl.BlockSpec((8, d), lambda j: (0, 0)),
            pl.BlockSpec((d, tn), lambda j: (0, j)),
            pl.BlockSpec((1, tn), lambda j: (0, j)),
        ],
        out_specs=pl.BlockSpec((8, tn), lambda j: (0, j)),
        compiler_params=_cparams(("arbitrary",)),
        name="adaln",
    )(c8, w, b)


def _inproj_kernel(x_ref, g_ref, sc_ref, sh_ref, whg_ref, wml_ref, hg_out, cq_out, ckv_out, kr_out,
                   u_scr, *, tm, n_hg):
    j = pl.program_id(1)

    @pl.when(j == 0)
    def _():
        g = g_ref[...]
        sc = 1.0 + sc_ref[0]
        sh = sh_ref[0]

        def body(r, carry):
            rows = pl.ds(pl.multiple_of(r * 128, 128), 128)
            x = x_ref[rows, :]
            ms = jnp.mean(x * x, axis=-1, keepdims=True)
            u = x * lax.rsqrt(ms + EPS) * g * sc + sh
            u_scr[rows, :] = u.astype(BF16)
            return carry

        lax.fori_loop(0, tm // 128, body, 0)

    @pl.when(j < n_hg)
    def _():
        hg_out[...] = lax.dot_general(u_scr[...], whg_ref[...], NT_DIMS,
                                      preferred_element_type=F32)

    @pl.when(j == n_hg)
    def _():
        r = lax.dot_general(u_scr[...], wml_ref[...], NT_DIMS, preferred_element_type=F32)
        cq_out[...] = r[:, :MLA_Q_RANK]
        ckv_out[...] = r[:, MLA_Q_RANK:MLA_Q_RANK + MLA_KV_RANK]
        kr_out[...] = r[:, MLA_Q_RANK + MLA_KV_RANK:]


def _inproj(x2, g, mod3, w_hg_t, w_ml_t, seq):
    t, d = x2.shape
    tm = min(1024, seq)
    tn = 1024
    n_hg = w_hg_t.shape[0] // tn
    per_b = seq // tm
    last = lambda j: jnp.minimum(j, n_hg - 1)
    return pl.pallas_call(
        functools.partial(_inproj_kernel, tm=tm, n_hg=n_hg),
        out_shape=(
            jax.ShapeDtypeStruct((t, w_hg_t.shape[0]), F32),
            jax.ShapeDtypeStruct((t, MLA_Q_RANK), F32),
            jax.ShapeDtypeStruct((t, MLA_KV_RANK), F32),
            jax.ShapeDtypeStruct((t, MLA_ROPE), F32),
        ),
        grid=(t // tm, n_hg + 1),
        in_specs=[
            pl.BlockSpec((tm, d), lambda i, j: (i, 0)),
            pl.BlockSpec((1, d), lambda i, j: (0, 0)),
            pl.BlockSpec((1, 1, d), lambda i, j: (i // per_b, 0, 1)),
            pl.BlockSpec((1, 1, d), lambda i, j: (i // per_b, 0, 0)),
            pl.BlockSpec((tn, d), lambda i, j: (last(j), 0)),
            pl.BlockSpec(w_ml_t.shape, lambda i, j: (0, 0)),
        ],
        out_specs=(
            pl.BlockSpec((tm, tn), lambda i, j: (i, last(j))),
            pl.BlockSpec((tm, MLA_Q_RANK), lambda i, j: (i, 0)),
            pl.BlockSpec((tm, MLA_KV_RANK), lambda i, j: (i, 0)),
            pl.BlockSpec((tm, MLA_ROPE), lambda i, j: (i, 0)),
        ),
        scratch_shapes=[pltpu.VMEM((tm, d), BF16)],
        compiler_params=_cparams(("arbitrary", "arbitrary")),
        name="inproj",
    )(x2, g, mod3, mod3, w_hg_t, w_ml_t)


def _hgrn_consts():
    c = HG_CHUNK
    t = np.arange(c)[:, None]
    j = np.arange(c)[None, :]
    tril = (j <= t).astype(np.float32)
    mats = [tril]
    masks = [np.eye(c, dtype=np.float32)]
    s = c // 2
    while s >= 1:
        ref = (t // (2 * s)) * (2 * s) + s - 1
        if s < HG_ROW_LEVEL:
            mats.append(tril - (j <= ref).astype(np.float32))
        masks.append((((t // s) % 2 == 1) & ((j // s) == (t // s) - 1)).astype(np.float32))
        s //= 2
    w = np.concatenate(mats, 0)
    return np.concatenate([w, w], 1), np.stack(masks, 0)


def _hgrn_kernel(q_ref, f_ref, i_ref, g_ref, lb_ref, og_ref, w_ref, m_ref, o_ref, st_scr, *, seq):
    c = HG_CHUNK
    n_lvl = m_ref.shape[0]
    st_scr[...] = jnp.zeros_like(st_scr)
    lb = lb_ref[...]
    og = og_ref[...]

    def chunk(ci, carry):
        rows = pl.ds(pl.multiple_of(ci * c, c), c)
        q = q_ref[rows, :]
        v = i_ref[rows, :]
        g = g_ref[rows, :]
        f = lb + (1.0 - lb) * _sigmoid(f_ref[rows, :])
        lf = jnp.log(f)
        kk = 1.0 - f
        hi, lo = _split_bf16(lf)
        xs = jnp.dot(w_ref[...], jnp.concatenate([hi, lo], axis=0), preferred_element_type=F32)
        b = xs[0:c]
        bl = b[c - 1:c, :] - b
        st = st_scr[...]
        vb = v.astype(BF16)
        o = lax.dot_general((q * jnp.exp(b)).astype(BF16), st.astype(BF16), NT_DIMS,
                            preferred_element_type=F32)
        attn = m_ref[0] * lax.dot_general(q.astype(BF16), kk.astype(BF16), NT_DIMS,
                                          preferred_element_type=F32)
        s = c // 2
        fine = 1
        for l in range(1, n_lvl):
            if s >= HG_ROW_LEVEL:
                b3 = b.reshape(c // (2 * s), 2 * s, HG_DK)
                x = (b3 - b3[:, s - 1:s, :]).reshape(c, HG_DK)
            else:
                x = xs[fine * c:(fine + 1) * c]
                fine += 1
            s //= 2
            e = jnp.exp(-jnp.abs(x))
            attn = attn + m_ref[l] * lax.dot_general(
                (q * e).astype(BF16), (kk * e).astype(BF16), NT_DIMS, preferred_element_type=F32)
        o = o + jnp.dot(attn.astype(BF16), vb, preferred_element_type=F32)
        khat = (kk * jnp.exp(bl)).astype(BF16)
        st_scr[...] = st * jnp.exp(b[c - 1:c, :]) + lax.dot_general(
            vb, khat, TN_DIMS, preferred_element_type=F32)
        ms = jnp.mean(o * o, axis=-1, keepdims=True)
        y = o * lax.rsqrt(ms + EPS) * og * (g * _sigmoid(g))
        o_ref[rows, :] = y.astype(BF16)
        return carry

    lax.fori_loop(0, seq // c, chunk, 0, unroll=8)


def _hgrn(proj, lb, og, batch, seq):
    t = proj.shape[0]
    wm, mm = _hgrn_consts()
    wm = jnp.asarray(wm, BF16)
    mm = jnp.asarray(mm, F32)
    h = HG_HEADS
    return pl.pallas_call(
        functools.partial(_hgrn_kernel, seq=seq),
        out_shape=jax.ShapeDtypeStruct((t, HG_WIDTH), BF16),
        grid=(batch, h),
        in_specs=[
            pl.BlockSpec((seq, 128), lambda b, i: (b, i)),
            pl.BlockSpec((seq, 128), lambda b, i: (b, h + i)),
            pl.BlockSpec((seq, 128), lambda b, i: (b, 2 * h + i)),
            pl.BlockSpec((seq, 128), lambda b, i: (b, 3 * h + i)),
            pl.BlockSpec((1, 128), lambda b, i: (0, i)),
            pl.BlockSpec((1, 128), lambda b, i: (0, 0)),
            pl.BlockSpec(wm.shape, lambda b, i: (0, 0)),
            pl.BlockSpec(mm.shape, lambda b, i: (0, 0, 0)),
        ],
        out_specs=pl.BlockSpec((seq, 128), lambda b, i: (b, i)),
        scratch_shapes=[pltpu.VMEM((HG_DV, HG_DK), F32)],
        compiler_params=_cparams(("arbitrary", "arbitrary")),
        name="hgrn2",
    )(proj, proj, proj, proj, lb, og, wm, mm)


def _mla_prep_kernel(cq_ref, ckv_ref, kr_ref, pos_ref, qag_ref, kvag_ref, wuq_ref, wukv_ref,
                     gq_ref, gkn_ref, gkr_ref, invf_ref, sgn_ref, cm_ref,
                     q_out, k_out, v_out):
    cq = cq_ref[...]
    cqn = cq * lax.rsqrt(jnp.mean(cq * cq, axis=-1, keepdims=True) + EPS) * qag_ref[...]
    qf = jnp.dot(cqn.astype(BF16), wuq_ref[...], preferred_element_type=F32)
    ckv = ckv_ref[...]
    ckvn = ckv * lax.rsqrt(jnp.mean(ckv * ckv, axis=-1, keepdims=True) + EPS) * kvag_ref[...]
    kvf = jnp.dot(ckvn.astype(BF16), wukv_ref[...], preferred_element_type=F32)

    ang = pos_ref[...].astype(F32) * invf_ref[...]
    cos_t = jnp.cos(ang) * cm_ref[...]
    sin_t = jnp.sin(ang) * sgn_ref[...]

    def rope(r):
        return r * cos_t + (pltpu.roll(r, 96, 1) + pltpu.roll(r, 32, 1)) * sin_t

    kr64 = kr_ref[...]
    kr = jnp.concatenate([kr64, jnp.zeros_like(kr64)], axis=-1)
    kr_ss = jnp.sum(kr * kr, axis=-1, keepdims=True)
    krr = rope(kr * gkr_ref[...])
    gq = gq_ref[...]
    gkn = gkn_ref[...]
    scale = MLA_QK ** -0.5
    for h in range(MLA_HEADS):
        qh = qf[:, h * MLA_QK_PAD:(h + 1) * MLA_QK_PAD]
        inv = lax.rsqrt(jnp.sum(qh * qh, axis=-1, keepdims=True) / MLA_QK + EPS)
        qn = qh * inv * gq
        qo = jnp.concatenate([qn[:, :MLA_NOPE], rope(qn[:, MLA_NOPE:])], axis=-1) * scale
        q_out[0, h] = qo.astype(BF16)
        kn = kvf[:, h * 256:h * 256 + MLA_NOPE]
        inv = lax.rsqrt((jnp.sum(kn * kn, axis=-1, keepdims=True) + kr_ss) / MLA_QK + EPS)
        k_out[0, h] = jnp.concatenate([kn * inv * gkn, krr * inv], axis=-1).astype(BF16)
        v_out[0, h] = kvf[:, h * 256 + MLA_NOPE:(h + 1) * 256].astype(BF16)


def _mla_prep(cq, ckv, kr, pos2, qag, kvag, wuq_p, wukv, gq, gkn, gkr, invf, sgn, cm, batch, seq):
    tm = min(256, seq)
    per_b = seq // tm
    hh = MLA_HEADS
    full = lambda shape: pl.BlockSpec(shape, lambda b, i: tuple(0 for _ in shape))
    return pl.pallas_call(
        _mla_prep_kernel,
        out_shape=(
            jax.ShapeDtypeStruct((batch, hh, seq, MLA_QK_PAD), BF16),
            jax.ShapeDtypeStruct((batch, hh, seq, MLA_QK_PAD), BF16),
            jax.ShapeDtypeStruct((batch, hh, seq, MLA_V), BF16),
        ),
        grid=(batch, per_b),
        in_specs=[
            pl.BlockSpec((tm, MLA_Q_RANK), lambda b, i: (b * per_b + i, 0)),
            pl.BlockSpec((tm, MLA_KV_RANK), lambda b, i: (b * per_b + i, 0)),
            pl.BlockSpec((tm, MLA_ROPE), lambda b, i: (b * per_b + i, 0)),
            pl.BlockSpec((tm, 1), lambda b, i: (b * per_b + i, 0)),
            full((1, MLA_Q_RANK)),
            full((1, MLA_KV_RANK)),
            full(wuq_p.shape),
            full(wukv.shape),
            full((1, MLA_QK_PAD)),
            full((1, 128)),
            full((1, 128)),
            full((1, 128)),
            full((1, 128)),
            full((1, 128)),
        ],
        out_specs=(
            pl.BlockSpec((1, hh, tm, MLA_QK_PAD), lambda b, i: (b, 0, i, 0)),
            pl.BlockSpec((1, hh, tm, MLA_QK_PAD), lambda b, i: (b, 0, i, 0)),
            pl.BlockSpec((1, hh, tm, MLA_V), lambda b, i: (b, 0, i, 0)),
        ),
        compiler_params=_cparams(("arbitrary", "arbitrary")),
        name="mla_prep",
    )(cq, ckv, kr, pos2, qag, kvag, wuq_p, wukv, gq, gkn, gkr, invf, sgn, cm)


def _attn_kernel(q_ref, k_ref, v_ref, o_ref, *, tq, seq):
    row = lax.broadcasted_iota(jnp.int32, (tq, tq), 0)
    col = lax.broadcasted_iota(jnp.int32, (tq, tq), 1)
    causal = col <= row
    for i in range(seq // tq):
        lo = i * tq
        q = q_ref[0, 0, lo:lo + tq, :]
        sd = lax.dot_general(q, k_ref[0, 0, lo:lo + tq, :], NT_DIMS, preferred_element_type=F32)
        sd = jnp.where(causal, sd, -jnp.inf)
        m = jnp.max(sd, axis=-1, keepdims=True)
        if i > 0:
            so = lax.dot_general(q, k_ref[0, 0, :lo, :], NT_DIMS, preferred_element_type=F32)
            m = jnp.maximum(m, jnp.max(so, axis=-1, keepdims=True))
        pd = jnp.exp(sd - m)
        l = jnp.sum(pd, axis=-1, keepdims=True)
        o = jnp.dot(pd.astype(BF16), v_ref[0, 0, lo:lo + tq, :], preferred_element_type=F32)
        if i > 0:
            po = jnp.exp(so - m)
            l = l + jnp.sum(po, axis=-1, keepdims=True)
            o = o + jnp.dot(po.astype(BF16), v_ref[0, 0, :lo, :], preferred_element_type=F32)
        o_ref[lo:lo + tq, :] = o / l


def _attention(q, k, v):
    batch, hh, seq, dq = q.shape
    tq = min(256, seq)
    return pl.pallas_call(
        functools.partial(_attn_kernel, tq=tq, seq=seq),
        out_shape=jax.ShapeDtypeStruct((batch * seq, hh * MLA_V), F32),
        grid=(batch, hh),
        in_specs=[
            pl.BlockSpec((1, 1, seq, dq), lambda b, h: (b, h, 0, 0)),
            pl.BlockSpec((1, 1, seq, dq), lambda b, h: (b, h, 0, 0)),
            pl.BlockSpec((1, 1, seq, MLA_V), lambda b, h: (b, h, 0, 0)),
        ],
        out_specs=pl.BlockSpec((seq, MLA_V), lambda b, h: (b, h)),
        compiler_params=_cparams(("arbitrary", "arbitrary")),
        name="mla_attn",
    )(q, k, v)


def _outproj_kernel(ohg_ref, omla_ref, x_ref, gt_ref, sc_ref, sh_ref, og_ref, g2_ref,
                    wo_ref, wrh_ref, wrl_ref, h_out, u_out, lg_out, *, n_sub):
    sub = x_ref.shape[0] // n_sub
    for j in range(n_sub):
        rows = slice(j * sub, (j + 1) * sub)
        om = omla_ref[rows, :]
        omn = om * lax.rsqrt(jnp.mean(om * om, axis=-1, keepdims=True) + EPS) * og_ref[...]
        y = jnp.dot(ohg_ref[rows, :], wo_ref[:HG_WIDTH, :], preferred_element_type=F32)
        y = y + jnp.dot(omn.astype(BF16), wo_ref[HG_WIDTH:, :], preferred_element_type=F32)
        h1 = x_ref[rows, :] + gt_ref[0] * y
        h_out[rows, :] = h1
        u = h1 * lax.rsqrt(jnp.mean(h1 * h1, axis=-1, keepdims=True) + EPS) * g2_ref[...]
        u = u * (1.0 + sc_ref[0]) + sh_ref[0]
        uh, ul = _split_bf16(u)
        u_out[rows] = _rows_to_tiles(uh)
        lg = lax.dot_general(wrh_ref[...], uh, NT_DIMS, preferred_element_type=F32)
        lg = lg + lax.dot_general(wrl_ref[...], uh, NT_DIMS, preferred_element_type=F32)
        lg = lg + lax.dot_general(wrh_ref[...], ul, NT_DIMS, preferred_element_type=F32)
        lg_out[:, rows] = lg


def _outproj(ohg, omla, x2, mod3, og, g2, wo_bf, wr_hi, wr_lo, seq):
    t, d = x2.shape
    tm = min(512, seq)
    per_b = seq // tm
    full = lambda shape: pl.BlockSpec(shape, lambda i: tuple(0 for _ in shape))
    modspec = lambda col: pl.BlockSpec((1, 1, d), lambda i: (i // per_b, 0, col))
    return pl.pallas_call(
        functools.partial(_outproj_kernel, n_sub=2),
        out_shape=(
            jax.ShapeDtypeStruct((t, d), F32),
            jax.ShapeDtypeStruct((t, d // 128, 128), BF16),
            jax.ShapeDtypeStruct((N_EXPERTS, t), F32),
        ),
        grid=(t // tm,),
        in_specs=[
            pl.BlockSpec((tm, HG_WIDTH), lambda i: (i, 0)),
            pl.BlockSpec((tm, MLA_WIDTH), lambda i: (i, 0)),
            pl.BlockSpec((tm, d), lambda i: (i, 0)),
            modspec(2),
            modspec(4),
            modspec(3),
            full((1, MLA_WIDTH)),
            full((1, d)),
            full(wo_bf.shape),
            full(wr_hi.shape),
            full(wr_lo.shape),
        ],
        out_specs=(
            pl.BlockSpec((tm, d), lambda i: (i, 0)),
            pl.BlockSpec((tm, d // 128, 128), lambda i: (i, 0, 0)),
            pl.BlockSpec((N_EXPERTS, tm), lambda i: (0, i)),
        ),
        compiler_params=_cparams(("arbitrary",)),
        name="outproj",
    )(ohg, omla, x2, mod3, mod3, mod3, og, g2, wo_bf, wr_hi, wr_lo)


def _route_kernel(lg_ref, bias_ref, tri_ref, ltri_ref, idx_out, w_out, dest_out, info_out,
                  carry, base, *, tt, n_info):
    phase = pl.program_id(0)
    step = pl.program_id(1)

    @pl.when((phase == 0) & (step == 0))
    def _():
        carry[...] = jnp.zeros_like(carry)

    @pl.when((phase == 1) & (step == 0))
    def _():
        cnt = carry[...]
        padded = jnp.floor((cnt + (MOE_BLK - 1)) / MOE_BLK) * MOE_BLK
        hi, lo = _split_bf16(padded)
        ps = jnp.dot(ltri_ref[...], jnp.concatenate([hi, lo], axis=1), preferred_element_type=F32)
        pad_start = ps[:, :128] + ps[:, 128:]
        base[...] = pad_start
        carry[...] = jnp.zeros_like(carry)
        pad_end = pad_start + padded
        reps = n_info // 128
        pe = jnp.concatenate([pad_end] * reps, axis=1)
        starts = (lax.broadcasted_iota(jnp.int32, (N_EXPERTS, n_info), 1) * MOE_BLK).astype(F32)
        blk_e = jnp.sum(jnp.where(pe <= starts, 1.0, 0.0), axis=0, keepdims=True)
        blk_e = jnp.minimum(blk_e, N_EXPERTS - 1.0)
        n_used = pe[N_EXPERTS - 1:N_EXPERTS, :] / MOE_BLK
        diag = (lax.broadcasted_iota(jnp.int32, (N_EXPERTS, n_info), 0)
                == lax.broadcasted_iota(jnp.int32, (N_EXPERTS, n_info), 1))
        end_blk = jnp.sum(jnp.where(diag, pe, 0.0), axis=0, keepdims=True) / MOE_BLK
        pd = jnp.concatenate([padded] * reps, axis=1)
        num_blk = jnp.sum(jnp.where(diag, pd, 0.0), axis=0, keepdims=True) / MOE_BLK
        info_out[...] = jnp.concatenate(
            [blk_e, n_used, end_blk, num_blk, jnp.zeros((4, n_info), F32)],
            axis=0).astype(jnp.int32)

    ne, ng, eg = N_EXPERTS, N_GROUPS, EXPERTS_PER_GROUP
    neg = -jnp.inf
    sc = _sigmoid(lg_ref[...])
    sel = sc + bias_ref[...]
    sel3 = sel.reshape(ng, eg, tt)
    sub = lax.broadcasted_iota(jnp.int32, (ng, eg, tt), 1)
    m1 = jnp.max(sel3, axis=1, keepdims=True)
    first = jnp.min(jnp.where(sel3 == m1, sub, eg), axis=1, keepdims=True)
    m2 = jnp.max(jnp.where(sub == first, neg, sel3), axis=1, keepdims=True)
    gs = (m1 + m2).reshape(ng, tt)

    gio = lax.broadcasted_iota(jnp.int32, (ng, tt), 0)
    gsel = jnp.zeros((ng, tt), F32)
    for _ in range(TOPK_GROUPS):
        m = jnp.max(gs, axis=0, keepdims=True)
        gi = jnp.min(jnp.where(gs == m, gio, ng), axis=0, keepdims=True)
        hit = gio == gi
        gsel = jnp.where(hit, 1.0, gsel)
        gs = jnp.where(hit, neg, gs)
    gfull = jnp.broadcast_to(gsel.reshape(ng, 1, tt), (ng, eg, tt)).reshape(ne, tt)
    selm = jnp.where(gfull > 0.0, sel, neg)

    eio = lax.broadcasted_iota(jnp.int32, (ne, tt), 0)
    chosen = jnp.zeros((ne, tt), F32)
    idxs, ws = [], []
    for _ in range(TOP_K):
        m = jnp.max(selm, axis=0, keepdims=True)
        ei = jnp.min(jnp.where(selm == m, eio, ne), axis=0, keepdims=True)
        hit = eio == ei
        ws.append(jnp.sum(jnp.where(hit, sc, 0.0), axis=0, keepdims=True))
        idxs.append(ei)
        chosen = jnp.where(hit, 1.0, chosen)
        selm = jnp.where(hit, neg, selm)
    wsum = ws[0]
    for k in range(1, TOP_K):
        wsum = wsum + ws[k]

    tot = jnp.sum(chosen, axis=1, keepdims=True)

    @pl.when(phase == 1)
    def _():
        off = base[...][:, :1] + carry[...][:, :1]
        excl = jnp.dot(chosen.astype(BF16), tri_ref[...], preferred_element_type=F32) + off
        for k in range(TOP_K):
            idx_out[k:k + 1, :] = idxs[k]
            w_out[k:k + 1, :] = ws[k] / wsum * ROUTED_SCALE
            dk = jnp.sum(jnp.where(eio == idxs[k], excl, 0.0), axis=0, keepdims=True)
            dest_out[k:k + 1, :] = dk.astype(jnp.int32)

    carry[...] = carry[...] + tot


def _route(logits_t, bias_col, n_blk):
    ne, t = logits_t.shape
    tt = min(512, t)
    n_info = -(-n_blk // 128) * 128
    tri = jnp.asarray(np.triu(np.ones((tt, tt), np.float32), 1), BF16)
    ltri = jnp.asarray(np.tril(np.ones((ne, ne), np.float32), -1), BF16)
    return pl.pallas_call(
        functools.partial(_route_kernel, tt=tt, n_info=n_info),
        out_shape=(
            jax.ShapeDtypeStruct((TOP_K, t), jnp.int32),
            jax.ShapeDtypeStruct((TOP_K, t), F32),
            jax.ShapeDtypeStruct((TOP_K, t), jnp.int32),
            jax.ShapeDtypeStruct((8, n_info), jnp.int32),
        ),
        grid=(2, t // tt),
        in_specs=[
            pl.BlockSpec((ne, tt), lambda p, i: (0, i)),
            pl.BlockSpec((ne, 1), lambda p, i: (0, 0)),
            pl.BlockSpec((tt, tt), lambda p, i: (0, 0)),
            pl.BlockSpec((ne, ne), lambda p, i: (0, 0)),
        ],
        out_specs=(
            pl.BlockSpec((TOP_K, tt), lambda p, i: (0, i * p)),
            pl.BlockSpec((TOP_K, tt), lambda p, i: (0, i * p)),
            pl.BlockSpec((TOP_K, tt), lambda p, i: (0, i * p)),
            pl.BlockSpec((8, n_info), lambda p, i: (0, 0)),
        ),
        scratch_shapes=[pltpu.VMEM((ne, 128), F32), pltpu.VMEM((ne, 128), F32)],
        compiler_params=_cparams(("arbitrary", "arbitrary")),
        name="route",
    )(logits_t, bias_col, tri, ltri)


def _dispatch_kernel(info_ref, dest_ref, u_hbm, xs_hbm, stage, zbuf, lsem, ssem, zsem, *, tm, n_blk):
    i = pl.program_id(0)
    n = pl.num_programs(0)
    slot = i % 2
    n_used = info_ref[1, 0]

    def load(tile, dst_slot):
        return pltpu.make_async_copy(u_hbm.at[pl.ds(tile * tm, tm)], stage.at[dst_slot],
                                     lsem.at[dst_slot])

    def zero_block(blk):
        return pltpu.make_async_copy(zbuf, xs_hbm.at[pl.ds(blk * MOE_BLK, MOE_BLK)], zsem)

    @pl.when(i == 0)
    def _():
        load(0, 0).start()
        zbuf[...] = jnp.zeros_like(zbuf)
        for e in range(N_EXPERTS):
            @pl.when(info_ref[3, e] > 0)
            def _():
                zero_block(info_ref[2, e] - 1).start()

        def zero_unused(j, carry):
            zero_block(j).start()
            return carry
        lax.fori_loop(n_used, n_blk, zero_unused, 0)
        for e in range(N_EXPERTS):
            @pl.when(info_ref[3, e] > 0)
            def _():
                zero_block(0).wait()

        def wait_unused(j, carry):
            zero_block(0).wait()
            return carry
        lax.fori_loop(n_used, n_blk, wait_unused, 0)

    def scatter_wait(src_slot):
        pltpu.make_async_copy(stage.at[src_slot], xs_hbm.at[pl.ds(0, tm)], ssem.at[src_slot]).wait()

    @pl.when(i >= 1)
    def _():
        for _ in range(TOP_K):
            scatter_wait(1 - slot)

    @pl.when(i + 1 < n)
    def _():
        load(i + 1, 1 - slot).start()

    load(i, slot).wait()
    for k in range(TOP_K):
        for r in range(tm):
            pltpu.make_async_copy(stage.at[slot, r], xs_hbm.at[dest_ref[0, k, r]],
                                  ssem.at[slot]).start(priority=r % 2)

    @pl.when(i == n - 1)
    def _():
        for _ in range(TOP_K):
            scatter_wait(slot)


def _dispatch(info, dest3, u3, n_blk):
    t, nt, _ = u3.shape
    tm = dest3.shape[2]
    return pl.pallas_call(
        functools.partial(_dispatch_kernel, tm=tm, n_blk=n_blk),
        out_shape=jax.ShapeDtypeStruct((n_blk * MOE_BLK, nt, 128), BF16),
        grid_spec=pltpu.PrefetchScalarGridSpec(
            num_scalar_prefetch=1,
            grid=(t // tm,),
            in_specs=[
                pl.BlockSpec((1, TOP_K, tm), lambda i, info: (i, 0, 0), memory_space=pltpu.SMEM),
                pl.BlockSpec(memory_space=pl.ANY),
            ],
            out_specs=pl.BlockSpec(memory_space=pl.ANY),
            scratch_shapes=[
                pltpu.VMEM((2, tm, nt, 128), BF16),
                pltpu.VMEM((MOE_BLK, nt, 128), BF16),
                pltpu.SemaphoreType.DMA((2,)),
                pltpu.SemaphoreType.DMA((2,)),
                pltpu.SemaphoreType.DMA(()),
            ],
        ),
        compiler_params=_cparams(("arbitrary",)),
        name="moe_dispatch",
    )(info, dest3, u3)


def _gmm_kernel(info_ref, xs_ref, wg_hbm, wu_hbm, wd_hbm, y_ref,
                wg_f, wu_f, wd_f, wsem, wg_s, wu_s, wd_s, nswap):
    i = pl.program_id(0)
    n_used = info_ref[1, 0]
    e = info_ref[0, i]
    prev = info_ref[0, jnp.maximum(i - 1, 0)]
    fresh = (i == 0) | (e != prev)

    def fetch(expert, slot):
        return (pltpu.make_async_copy(wg_hbm.at[expert], wg_f.at[slot], wsem.at[slot]),
                pltpu.make_async_copy(wu_hbm.at[expert], wu_f.at[slot], wsem.at[slot]),
                pltpu.make_async_copy(wd_hbm.at[expert], wd_f.at[slot], wsem.at[slot]))

    @pl.when(i == 0)
    def _():
        nswap[0] = 0
        for c in fetch(e, 0):
            c.start()

    @pl.when(fresh & (i < n_used))
    def _():
        slot = nswap[0] % 2
        nswap[0] = nswap[0] + 1
        for c in fetch(e, slot):
            c.wait()
        nxt = info_ref[2, e]

        @pl.when(nxt < n_used)
        def _():
            for c in fetch(info_ref[0, nxt], 1 - slot):
                c.start(priority=1)

        wg_s[...] = wg_f[slot].astype(BF16)
        wu_s[...] = wu_f[slot].astype(BF16)
        wd_s[...] = wd_f[slot].astype(BF16)

    @pl.when(i < n_used)
    def _():
        xb = _tiles_to_rows(xs_ref[...])
        a = jnp.dot(xb, wg_s[...], preferred_element_type=F32)
        u = jnp.dot(xb, wu_s[...], preferred_element_type=F32)
        hid = (a * _sigmoid(a) * u).astype(BF16)
        y = jnp.dot(hid, wd_s[...], preferred_element_type=F32)
        y_ref[...] = _rows_to_tiles(y.astype(BF16))

    @pl.when(i >= n_used)
    def _():
        y_ref[...] = jnp.zeros_like(y_ref)


def _gmm(info, xs3, w_gate, w_up, w_down):
    p, nt, _ = xs3.shape
    d = nt * 128
    n_blk = p // MOE_BLK
    de = w_gate.shape[-1]
    return pl.pallas_call(
        _gmm_kernel,
        out_shape=jax.ShapeDtypeStruct((p, nt, 128), BF16),
        grid_spec=pltpu.PrefetchScalarGridSpec(
            num_scalar_prefetch=1,
            grid=(n_blk,),
            in_specs=[
                pl.BlockSpec((MOE_BLK, nt, 128),
                             lambda i, info: (jnp.minimum(i, info[1, 0] - 1), 0, 0)),
                pl.BlockSpec(memory_space=pl.ANY),
                pl.BlockSpec(memory_space=pl.ANY),
                pl.BlockSpec(memory_space=pl.ANY),
            ],
            out_specs=pl.BlockSpec((MOE_BLK, nt, 128), lambda i, info: (i, 0, 0)),
            scratch_shapes=[
                pltpu.VMEM((2, d, de), F32),
                pltpu.VMEM((2, d, de), F32),
                pltpu.VMEM((2, de, d), F32),
                pltpu.SemaphoreType.DMA((2,)),
                pltpu.VMEM((d, de), BF16),
                pltpu.VMEM((d, de), BF16),
                pltpu.VMEM((de, d), BF16),
                pltpu.SMEM((1,), jnp.int32),
            ],
        ),
        compiler_params=_cparams(("arbitrary",)),
        name="moe_gmm",
    )(info, xs3, w_gate, w_up, w_down)


def _final_kernel(cur_ref, nxt_ref, u_ref, h_ref, gt_ref, wt_ref, wsg_ref, wsu_ref, wsd_ref, y_hbm,
                  o_ref, ybuf_a, ybuf_b, sem, *, tm):
    i = pl.program_id(0)
    n = pl.num_programs(0)
    n_rows = TOP_K * tm

    def issue(idx_ref, tile, buf, sem_i):
        for k in range(TOP_K):
            for r in range(tm):
                pltpu.make_async_copy(y_hbm.at[idx_ref[0, tile * TOP_K + k, r]], buf.at[k * tm + r],
                                      sem.at[sem_i]).start(priority=r % 2)

    def gather_wait(buf, sem_i):
        pltpu.make_async_copy(y_hbm.at[pl.ds(0, n_rows)], buf, sem.at[sem_i]).wait()

    eye = (lax.broadcasted_iota(jnp.int32, (tm, tm), 0)
           == lax.broadcasted_iota(jnp.int32, (tm, tm), 1))

    def compute(tile, buf):
        rows = slice(tile * tm, (tile + 1) * tm)
        ub = _tiles_to_rows(u_ref[rows])
        a = jnp.dot(ub, wsg_ref[...], preferred_element_type=F32)
        b = jnp.dot(ub, wsu_ref[...], preferred_element_type=F32)
        hid = (a * _sigmoid(a) * b).astype(BF16)
        acc = jnp.dot(hid, wsd_ref[...], preferred_element_type=F32)
        wt = wt_ref[rows, :]
        wcat = jnp.concatenate([jnp.where(eye, wt[:, k:k + 1], 0.0) for k in range(TOP_K)],
                               axis=1)
        whi, wlo = _split_bf16(wcat)
        ycat = _tiles_to_rows(buf[...])
        acc = acc + jnp.dot(whi, ycat, preferred_element_type=F32)
        acc = acc + jnp.dot(wlo, ycat, preferred_element_type=F32)
        o_ref[rows, :] = h_ref[rows, :] + gt_ref[0] * acc

    @pl.when(i == 0)
    def _():
        issue(cur_ref, 0, ybuf_a, 0)

    gather_wait(ybuf_a, 0)
    issue(cur_ref, 1, ybuf_b, 1)
    compute(0, ybuf_a)
    gather_wait(ybuf_b, 1)
    issue(nxt_ref, 0, ybuf_a, 0)
    compute(1, ybuf_b)

    @pl.when(i == n - 1)
    def _():
        gather_wait(ybuf_a, 0)


def _final(dest3, u2, h1, mod3, wt, wsg, wsu, wsd, y, seq):
    t, d = h1.shape
    tm = dest3.shape[2]
    n = t // (2 * tm)
    per_b = seq // (2 * tm)
    dest_pairs = dest3.reshape(n, 2 * TOP_K, tm)
    full = lambda shape: pl.BlockSpec(shape, lambda i: tuple(0 for _ in shape))
    return pl.pallas_call(
        functools.partial(_final_kernel, tm=tm),
        out_shape=jax.ShapeDtypeStruct((t, d), F32),
        grid=(n,),
        in_specs=[
            pl.BlockSpec((1, 2 * TOP_K, tm), lambda i: (i, 0, 0), memory_space=pltpu.SMEM),
            pl.BlockSpec((1, 2 * TOP_K, tm), lambda i: (jnp.minimum(i + 1, n - 1), 0, 0),
                         memory_space=pltpu.SMEM),
            pl.BlockSpec((2 * tm, d // 128, 128), lambda i: (i, 0, 0)),
            pl.BlockSpec((2 * tm, d), lambda i: (i, 0)),
            pl.BlockSpec((1, 1, d), lambda i: (i // per_b, 0, 5)),
            pl.BlockSpec((2 * tm, TOP_K), lambda i: (i, 0)),
            full(wsg.shape),
            full(wsu.shape),
            full(wsd.shape),
            pl.BlockSpec(memory_space=pl.ANY),
        ],
        out_specs=pl.BlockSpec((2 * tm, d), lambda i: (i, 0)),
        scratch_shapes=[
            pltpu.VMEM((TOP_K * tm, d // 128, 128), BF16),
            pltpu.VMEM((TOP_K * tm, d // 128, 128), BF16),
            pltpu.SemaphoreType.DMA((2,)),
        ],
        compiler_params=_cparams(("arbitrary",)),
        name="final",
    )(dest_pairs, dest_pairs, u2, h1, mod3, wt, wsg, wsu, wsd, y)


def _layer(h2, mod3, positions, lb, p, batch, seq):
    t, d = h2.shape

    w_in_t = p["w_in"].T
    proj, cq, ckv, kr = _inproj(h2, p["norm_mix_g"][None, :], mod3,
                                w_in_t[:COL_CQ].astype(BF16), w_in_t[COL_CQ:].astype(BF16), seq)

    o_hg = _hgrn(proj, lb[None, :], p["hg_out_g"][None, :], batch, seq)

    wuq = p["mla_w_uq"].reshape(MLA_Q_RANK, MLA_HEADS, MLA_QK)
    wuq_p = jnp.pad(wuq, ((0, 0), (0, 0), (0, MLA_QK_PAD - MLA_QK))).reshape(
        MLA_Q_RANK, MLA_HEADS * MLA_QK_PAD).astype(BF16)
    wukv = p["mla_w_ukv"].astype(BF16)
    gq = jnp.pad(p["mla_q_norm_g"], (0, MLA_QK_PAD - MLA_QK))[None, :]
    gkn = p["mla_k_norm_g"][None, :MLA_NOPE]
    gkr = jnp.pad(p["mla_k_norm_g"][MLA_NOPE:], (0, 128 - MLA_ROPE))[None, :]
    half = MLA_ROPE // 2
    inv_freq = ROPE_THETA ** (-jnp.arange(half, dtype=F32) / half)
    zeros64 = jnp.zeros((128 - MLA_ROPE,), F32)
    invf = jnp.concatenate([inv_freq, inv_freq, zeros64])[None, :]
    sgn = jnp.concatenate([-jnp.ones((half,), F32), jnp.ones((half,), F32), zeros64])[None, :]
    cm = jnp.concatenate([jnp.ones((MLA_ROPE,), F32), zeros64])[None, :]
    q, k, v = _mla_prep(cq, ckv, kr, positions.reshape(t, 1), p["mla_q_a_g"][None, :],
                        p["mla_kv_a_g"][None, :], wuq_p, wukv, gq, gkn, gkr, invf, sgn, cm,
                        batch, seq)
    o_mla = _attention(q, k, v)

    wr_t = p["w_router"].T
    wr_hi = wr_t.astype(BF16)
    wr_lo = (wr_t - wr_hi.astype(F32)).astype(BF16)
    h1, u2, logits_t = _outproj(o_hg, o_mla, h2, mod3, p["mla_out_g"][None, :],
                                p["norm_ffn_g"][None, :], p["w_out"].astype(BF16),
                                wr_hi, wr_lo, seq)

    n_blk = (t * TOP_K) // MOE_BLK + N_EXPERTS
    _, top_w, dest, info = _route(logits_t, p["router_bias"][:, None], n_blk)

    tm = min(128, seq)
    dest3 = dest.reshape(TOP_K, t // tm, tm).transpose(1, 0, 2)
    xs = _dispatch(info, dest3, u2, n_blk)
    y = _gmm(info, xs, p["w_gate"], p["w_up"], p["w_down"])
    return _final(dest3, u2, h1, mod3, top_w.T, p["ws_gate"].astype(BF16),
                  p["ws_up"].astype(BF16), p["ws_down"].astype(BF16), y, seq)


def kernel(x, c, positions, w_ada, b_ada, norm_mix_g, norm_ffn_g, w_in, hg_lb_logits, hg_out_g,
           mla_q_a_g, mla_w_uq, mla_kv_a_g, mla_w_ukv, mla_q_norm_g, mla_k_norm_g, mla_out_g,
           w_out, w_router, router_bias, w_gate, w_up, w_down, ws_gate, ws_up, ws_down):
    batch, seq, d = x.shape
    depth = w_ada.shape[0]
    lb_all = jnp.cumsum(jax.nn.softmax(hg_lb_logits.astype(F32), axis=0), axis=0)
    c8 = jnp.pad(c, ((0, 8 - batch), (0, 0)))
    h2 = x.reshape(batch * seq, d)
    for l in range(depth):
        mod = _adaln(c8, w_ada[l], b_ada[l][None, :])[:batch]
        mod3 = mod.reshape(batch, 1, 6 * d)
        p = dict(norm_mix_g=norm_mix_g[l], norm_ffn_g=norm_ffn_g[l], w_in=w_in[l],
                 hg_out_g=hg_out_g[l], mla_q_a_g=mla_q_a_g[l], mla_w_uq=mla_w_uq[l],
                 mla_kv_a_g=mla_kv_a_g[l], mla_w_ukv=mla_w_ukv[l], mla_q_norm_g=mla_q_norm_g[l],
                 mla_k_norm_g=mla_k_norm_g[l], mla_out_g=mla_out_g[l], w_out=w_out[l],
                 w_router=w_router[l], router_bias=router_bias[l], w_gate=w_gate[l],
                 w_up=w_up[l], w_down=w_down[l], ws_gate=ws_gate[l], ws_up=ws_up[l],
                 ws_down=ws_down[l])
        h2 = _layer(h2, mod3, positions, lb_all[l], p, batch, seq)
    return h2.reshape(batch, seq, d)
```

```python
import functools

import numpy as np
import jax
import jax.numpy as jnp
from jax import lax
from jax.experimental import pallas as pl
from jax.experimental.pallas import tpu as pltpu

F32 = jnp.float32
BF16 = jnp.bfloat16

D_MODEL = 2048
EPS = 1e-6

HG_HEADS = 8
HG_DK = 128
HG_DV = 128
HG_WIDTH = HG_HEADS * HG_DV
HG_CHUNK = 128
HG_ROW_LEVEL = 8
HG_HEADS_PER_STEP = 2
HG_UNROLL = 8

MLA_HEADS = 8
MLA_Q_RANK = 512
MLA_KV_RANK = 256
MLA_NOPE = 128
MLA_ROPE = 64
MLA_QK = MLA_NOPE + MLA_ROPE
MLA_QK_PAD = 256
MLA_V = 128
MLA_WIDTH = MLA_HEADS * MLA_V
ROPE_THETA = 10000.0

COL_CQ = 4096

N_EXPERTS = 64
TOP_K = 8
N_GROUPS = 8
TOPK_GROUPS = 4
EXPERTS_PER_GROUP = N_EXPERTS // N_GROUPS
D_EXPERT = 512
ROUTED_SCALE = 2.5
MOE_BLK = 256

VMEM_LIMIT = 56 * 1024 * 1024

NT_DIMS = (((1,), (1,)), ((), ()))
TN_DIMS = (((0,), (0,)), ((), ()))


def _cparams(sem):
    return pltpu.CompilerParams(dimension_semantics=sem, vmem_limit_bytes=VMEM_LIMIT)


def _sigmoid(x):
    return 1.0 / (1.0 + jnp.exp(-x))


def _split_bf16(x):
    hi = x.astype(BF16)
    lo = (x - hi.astype(F32)).astype(BF16)
    return hi, lo


def _rows_to_tiles(x):
    n = x.shape[1] // 128
    return jnp.swapaxes(jnp.stack([x[:, s * 128:(s + 1) * 128] for s in range(n)], axis=0), 0, 1)


def _tiles_to_rows(x3):
    xt = jnp.swapaxes(x3, 0, 1)
    return jnp.concatenate([xt[s] for s in range(x3.shape[1])], axis=1)


def _adaln_kernel(c_ref, w_ref, b_ref, o_ref):
    c = c_ref[...]
    cond = c * _sigmoid(c)
    hi, lo = _split_bf16(cond)
    lhs = jnp.concatenate([hi, lo], axis=0)
    r = jnp.dot(lhs, w_ref[...].astype(BF16), preferred_element_type=F32)
    o_ref[...] = r[:8] + r[8:] + b_ref[...]


def _adaln(c8, w, b):
    d, n = w.shape
    tn = 1024
    return pl.pallas_call(
        _adaln_kernel,
        out_shape=jax.ShapeDtypeStruct((8, n), F32),
        grid=(n // tn,),
        in_specs=[
            pl.BlockSpec((8, d), lambda j: (0, 0)),
            pl.BlockSpec((d, tn), lambda j: (0, j)),
            pl.BlockSpec((1, tn), lambda j: (0, j)),
        ],
        out_specs=pl.BlockSpec((8, tn), lambda j: (0, j)),
        compiler_params=_cparams(("arbitrary",)),
        name="adaln",
    )(c8, w, b)


def _inproj_kernel(x_ref, g_ref, sc_ref, sh_ref, whg_ref, wml_ref, hg_out, cq_out, ckv_out, kr_out,
                   u_scr, *, tm, n_hg):
    j = pl.program_id(1)

    @pl.when(j == 0)
    def _():
        g = g_ref[...]
        sc = 1.0 + sc_ref[0]
        sh = sh_ref[0]

        def body(r, carry):
            rows = pl.ds(pl.multiple_of(r * 128, 128), 128)
            x = x_ref[rows, :]
            ms = jnp.mean(x * x, axis=-1, keepdims=True)
            u = x * lax.rsqrt(ms + EPS) * g * sc + sh
            u_scr[rows, :] = u.astype(BF16)
            return carry

        lax.fori_loop(0, tm // 128, body, 0)

    @pl.when(j < n_hg)
    def _():
        hg_out[...] = lax.dot_general(u_scr[...], whg_ref[...], NT_DIMS,
                                      preferred_element_type=F32)

    @pl.when(j == n_hg)
    def _():
        r = lax.dot_general(u_scr[...], wml_ref[...], NT_DIMS, preferred_element_type=F32)
        cq_out[...] = r[:, :MLA_Q_RANK]
        ckv_out[...] = r[:, MLA_Q_RANK:MLA_Q_RANK + MLA_KV_RANK]
        kr_out[...] = r[:, MLA_Q_RANK + MLA_KV_RANK:]


def _inproj(x2, g, mod3, w_hg_t, w_ml_t, seq):
    t, d = x2.shape
    tm = min(1024, seq)
    tn = 1024
    n_hg = w_hg_t.shape[0] // tn
    per_b = seq // tm
    last = lambda j: jnp.minimum(j, n_hg - 1)
    return pl.pallas_call(
        functools.partial(_inproj_kernel, tm=tm, n_hg=n_hg),
        out_shape=(
            jax.ShapeDtypeStruct((t, w_hg_t.shape[0]), F32),
            jax.ShapeDtypeStruct((t, MLA_Q_RANK), F32),
            jax.ShapeDtypeStruct((t, MLA_KV_RANK), F32),
            jax.ShapeDtypeStruct((t, MLA_ROPE), F32),
        ),
        grid=(t // tm, n_hg + 1),
        in_specs=[
            pl.BlockSpec((tm, d), lambda i, j: (i, 0)),
            pl.BlockSpec((1, d), lambda i, j: (0, 0)),
            pl.BlockSpec((1, 1, d), lambda i, j: (i // per_b, 0, 1)),
            pl.BlockSpec((1, 1, d), lambda i, j: (i // per_b, 0, 0)),
            pl.BlockSpec((tn, d), lambda i, j: (last(j), 0)),
            pl.BlockSpec(w_ml_t.shape, lambda i, j: (0, 0)),
        ],
        out_specs=(
            pl.BlockSpec((tm, tn), lambda i, j: (i, last(j))),
            pl.BlockSpec((tm, MLA_Q_RANK), lambda i, j: (i, 0)),
            pl.BlockSpec((tm, MLA_KV_RANK), lambda i, j: (i, 0)),
            pl.BlockSpec((tm, MLA_ROPE), lambda i, j: (i, 0)),
        ),
        scratch_shapes=[pltpu.VMEM((tm, d), BF16)],
        compiler_params=_cparams(("arbitrary", "arbitrary")),
        name="inproj",
    )(x2, g, mod3, mod3, w_hg_t, w_ml_t)


def _hgrn_consts():
    c = HG_CHUNK
    t = np.arange(c)[:, None]
    j = np.arange(c)[None, :]
    tril = (j <= t).astype(np.float32)
    mats = [tril]
    masks = [np.eye(c, dtype=np.float32)]
    s = c // 2
    while s >= 1:
        ref = (t // (2 * s)) * (2 * s) + s - 1
        if s < HG_ROW_LEVEL:
            mats.append(tril - (j <= ref).astype(np.float32))
        masks.append((((t // s) % 2 == 1) & ((j // s) == (t // s) - 1)).astype(np.float32))
        s //= 2
    w = np.concatenate(mats, 0)
    return np.concatenate([w, w], 1), np.stack(masks, 0)


def _hgrn_head(q, v, g, kk, xs, og, m_ref, st_ref, o_ref):
    c = HG_CHUNK
    n_lvl = m_ref.shape[0]
    b = xs[0:c]
    bl = b[c - 1:c, :] - b
    st = st_ref[...]
    vb = v.astype(BF16)
    o = lax.dot_general((q * jnp.exp(b)).astype(BF16), st.astype(BF16), NT_DIMS,
                        preferred_element_type=F32)
    attn = m_ref[0] * lax.dot_general(q.astype(BF16), kk.astype(BF16), NT_DIMS,
                                      preferred_element_type=F32)
    s = c // 2
    fine = 1
    for l in range(1, n_lvl):
        if s >= HG_ROW_LEVEL:
            b3 = b.reshape(c // (2 * s), 2 * s, HG_DK)
            x = (b3 - b3[:, s - 1:s, :]).reshape(c, HG_DK)
        else:
            x = xs[fine * c:(fine + 1) * c]
            fine += 1
        s //= 2
        e = jnp.exp(-jnp.abs(x))
        attn = attn + m_ref[l] * lax.dot_general(
            (q * e).astype(BF16), (kk * e).astype(BF16), NT_DIMS, preferred_element_type=F32)
    o = o + jnp.dot(attn.astype(BF16), vb, preferred_element_type=F32)
    khat = (kk * jnp.exp(bl)).astype(BF16)
    st_ref[...] = st * jnp.exp(b[c - 1:c, :]) + lax.dot_general(
        vb, khat, TN_DIMS, preferred_element_type=F32)
    ms = jnp.mean(o * o, axis=-1, keepdims=True)
    y = o * lax.rsqrt(ms + EPS) * og * (g * _sigmoid(g))
    o_ref[...] = y.astype(BF16)


def _hgrn_kernel(q_ref, f_ref, i_ref, g_ref, lb_ref, og_ref, w_ref, m_ref, o_ref, st_scr, *,
                 seq, nh):
    c = HG_CHUNK
    st_scr[...] = jnp.zeros_like(st_scr)
    lb = lb_ref[...]
    og = og_ref[...]

    def chunk(ci, carry):
        rows = pl.ds(pl.multiple_of(ci * c, c), c)
        f = lb + (1.0 - lb) * _sigmoid(f_ref[rows, :])
        hi, lo = _split_bf16(jnp.log(f))
        xs = jnp.dot(w_ref[...], jnp.concatenate([hi, lo], axis=0), preferred_element_type=F32)
        for hh in range(nh):
            lanes = slice(hh * HG_DK, (hh + 1) * HG_DK)
            _hgrn_head(q_ref[rows, lanes], i_ref[rows, lanes], g_ref[rows, lanes],
                       1.0 - f[:, lanes], xs[:, lanes], og, m_ref, st_scr.at[hh],
                       o_ref.at[rows, lanes])
        return carry

    lax.fori_loop(0, seq // c, chunk, 0, unroll=HG_UNROLL // nh)


def _hgrn(proj, lb, og, batch, seq):
    t = proj.shape[0]
    wm, mm = _hgrn_consts()
    wm = jnp.asarray(wm, BF16)
    mm = jnp.asarray(mm, F32)
    nh = HG_HEADS_PER_STEP
    hg = HG_HEADS // nh
    wd = nh * HG_DK
    return pl.pallas_call(
        functools.partial(_hgrn_kernel, seq=seq, nh=nh),
        out_shape=jax.ShapeDtypeStruct((t, HG_WIDTH), BF16),
        grid=(batch, hg),
        in_specs=[
            pl.BlockSpec((seq, wd), lambda b, i: (b, i)),
            pl.BlockSpec((seq, wd), lambda b, i: (b, hg + i)),
            pl.BlockSpec((seq, wd), lambda b, i: (b, 2 * hg + i)),
            pl.BlockSpec((seq, wd), lambda b, i: (b, 3 * hg + i)),
            pl.BlockSpec((1, wd), lambda b, i: (0, i)),
            pl.BlockSpec((1, HG_DV), lambda b, i: (0, 0)),
            pl.BlockSpec(wm.shape, lambda b, i: (0, 0)),
            pl.BlockSpec(mm.shape, lambda b, i: (0, 0, 0)),
        ],
        out_specs=pl.BlockSpec((seq, wd), lambda b, i: (b, i)),
        scratch_shapes=[pltpu.VMEM((nh, HG_DV, HG_DK), F32)],
        compiler_params=_cparams(("arbitrary", "arbitrary")),
        name="hgrn2",
    )(proj, proj, proj, proj, lb, og, wm, mm)


def _mla_prep_kernel(cq_ref, ckv_ref, kr_ref, pos_ref, qag_ref, kvag_ref, wuq_ref, wukv_ref,
                     gq_ref, gkn_ref, gkr_ref, invf_ref, sgn_ref, cm_ref,
                     q_out, k_out, v_out):
    cq = cq_ref[...]
    cqn = cq * lax.rsqrt(jnp.mean(cq * cq, axis=-1, keepdims=True) + EPS) * qag_ref[...]
    qf = jnp.dot(cqn.astype(BF16), wuq_ref[...], preferred_element_type=F32)
    ckv = ckv_ref[...]
    ckvn = ckv * lax.rsqrt(jnp.mean(ckv * ckv, axis=-1, keepdims=True) + EPS) * kvag_ref[...]
    kvf = jnp.dot(ckvn.astype(BF16), wukv_ref[...], preferred_element_type=F32)

    ang = pos_ref[...].astype(F32) * invf_ref[...]
    cos_t = jnp.cos(ang) * cm_ref[...]
    sin_t = jnp.sin(ang) * sgn_ref[...]

    def rope(r):
        return r * cos_t + (pltpu.roll(r, 96, 1) + pltpu.roll(r, 32, 1)) * sin_t

    kr64 = kr_ref[...]
    kr = jnp.concatenate([kr64, jnp.zeros_like(kr64)], axis=-1)
    kr_ss = jnp.sum(kr * kr, axis=-1, keepdims=True)
    krr = rope(kr * gkr_ref[...])
    gq = gq_ref[...]
    gkn = gkn_ref[...]
    scale = MLA_QK ** -0.5
    for h in range(MLA_HEADS):
        qh = qf[:, h * MLA_QK_PAD:(h + 1) * MLA_QK_PAD]
        inv = lax.rsqrt(jnp.sum(qh * qh, axis=-1, keepdims=True) / MLA_QK + EPS)
        qn = qh * inv * gq
        qo = jnp.concatenate([qn[:, :MLA_NOPE], rope(qn[:, MLA_NOPE:])], axis=-1) * scale
        q_out[0, h] = qo.astype(BF16)
        kn = kvf[:, h * 256:h * 256 + MLA_NOPE]
        inv = lax.rsqrt((jnp.sum(kn * kn, axis=-1, keepdims=True) + kr_ss) / MLA_QK + EPS)
        k_out[0, h] = jnp.concatenate([kn * inv * gkn, krr * inv], axis=-1).astype(BF16)
        v_out[0, h] = kvf[:, h * 256 + MLA_NOPE:(h + 1) * 256].astype(BF16)


def _mla_prep(cq, ckv, kr, pos2, qag, kvag, wuq_p, wukv, gq, gkn, gkr, invf, sgn, cm, batch, seq):
    tm = min(256, seq)
    per_b = seq // tm
    hh = MLA_HEADS
    full = lambda shape: pl.BlockSpec(shape, lambda b, i: tuple(0 for _ in shape))
    return pl.pallas_call(
        _mla_prep_kernel,
        out_shape=(
            jax.ShapeDtypeStruct((batch, hh, seq, MLA_QK_PAD), BF16),
            jax.ShapeDtypeStruct((batch, hh, seq, MLA_QK_PAD), BF16),
            jax.ShapeDtypeStruct((batch, hh, seq, MLA_V), BF16),
        ),
        grid=(batch, per_b),
        in_specs=[
            pl.BlockSpec((tm, MLA_Q_RANK), lambda b, i: (b * per_b + i, 0)),
            pl.BlockSpec((tm, MLA_KV_RANK), lambda b, i: (b * per_b + i, 0)),
            pl.BlockSpec((tm, MLA_ROPE), lambda b, i: (b * per_b + i, 0)),
            pl.BlockSpec((tm, 1), lambda b, i: (b * per_b + i, 0)),
            full((1, MLA_Q_RANK)),
            full((1, MLA_KV_RANK)),
            full(wuq_p.shape),
            full(wukv.shape),
            full((1, MLA_QK_PAD)),
            full((1, 128)),
            full((1, 128)),
            full((1, 128)),
            full((1, 128)),
            full((1, 128)),
        ],
        out_specs=(
            pl.BlockSpec((1, hh, tm, MLA_QK_PAD), lambda b, i: (b, 0, i, 0)),
            pl.BlockSpec((1, hh, tm, MLA_QK_PAD), lambda b, i: (b, 0, i, 0)),
            pl.BlockSpec((1, hh, tm, MLA_V), lambda b, i: (b, 0, i, 0)),
        ),
        compiler_params=_cparams(("arbitrary", "arbitrary")),
        name="mla_prep",
    )(cq, ckv, kr, pos2, qag, kvag, wuq_p, wukv, gq, gkn, gkr, invf, sgn, cm)


def _attn_kernel(q_ref, k_ref, v_ref, o_ref, *, tq, seq):
    row = lax.broadcasted_iota(jnp.int32, (tq, tq), 0)
    col = lax.broadcasted_iota(jnp.int32, (tq, tq), 1)
    causal = col <= row
    for i in range(seq // tq):
        lo = i * tq
        q = q_ref[0, 0, lo:lo + tq, :]
        sd = lax.dot_general(q, k_ref[0, 0, lo:lo + tq, :], NT_DIMS, preferred_element_type=F32)
        sd = jnp.where(causal, sd, -jnp.inf)
        m = jnp.max(sd, axis=-1, keepdims=True)
        if i > 0:
            so = lax.dot_general(q, k_ref[0, 0, :lo, :], NT_DIMS, preferred_element_type=F32)
            m = jnp.maximum(m, jnp.max(so, axis=-1, keepdims=True))
        pd = jnp.exp(sd - m)
        l = jnp.sum(pd, axis=-1, keepdims=True)
        o = jnp.dot(pd.astype(BF16), v_ref[0, 0, lo:lo + tq, :], preferred_element_type=F32)
        if i > 0:
            po = jnp.exp(so - m)
            l = l + jnp.sum(po, axis=-1, keepdims=True)
            o = o + jnp.dot(po.astype(BF16), v_ref[0, 0, :lo, :], preferred_element_type=F32)
        o_ref[lo:lo + tq, :] = o / l


def _attention(q, k, v):
    batch, hh, seq, dq = q.shape
    tq = min(256, seq)
    return pl.pallas_call(
        functools.partial(_attn_kernel, tq=tq, seq=seq),
        out_shape=jax.ShapeDtypeStruct((batch * seq, hh * MLA_V), F32),
        grid=(batch, hh),
        in_specs=[
            pl.BlockSpec((1, 1, seq, dq), lambda b, h: (b, h, 0, 0)),
            pl.BlockSpec((1, 1, seq, dq), lambda b, h: (b, h, 0, 0)),
            pl.BlockSpec((1, 1, seq, MLA_V), lambda b, h: (b, h, 0, 0)),
        ],
        out_specs=pl.BlockSpec((seq, MLA_V), lambda b, h: (b, h)),
        compiler_params=_cparams(("arbitrary", "arbitrary")),
        name="mla_attn",
    )(q, k, v)


def _outproj_kernel(ohg_ref, omla_ref, x_ref, gt_ref, sc_ref, sh_ref, og_ref, g2_ref,
                    wo_ref, wrh_ref, wrl_ref, h_out, u_out, lg_out, *, n_sub):
    sub = x_ref.shape[0] // n_sub
    for j in range(n_sub):
        rows = slice(j * sub, (j + 1) * sub)
        om = omla_ref[rows, :]
        omn = om * lax.rsqrt(jnp.mean(om * om, axis=-1, keepdims=True) + EPS) * og_ref[...]
        y = jnp.dot(ohg_ref[rows, :], wo_ref[:HG_WIDTH, :], preferred_element_type=F32)
        y = y + jnp.dot(omn.astype(BF16), wo_ref[HG_WIDTH:, :], preferred_element_type=F32)
        h1 = x_ref[rows, :] + gt_ref[0] * y
        h_out[rows, :] = h1
        u = h1 * lax.rsqrt(jnp.mean(h1 * h1, axis=-1, keepdims=True) + EPS) * g2_ref[...]
        u = u * (1.0 + sc_ref[0]) + sh_ref[0]
        uh, ul = _split_bf16(u)
        u_out[rows] = _rows_to_tiles(uh)
        lg = lax.dot_general(wrh_ref[...], uh, NT_DIMS, preferred_element_type=F32)
        lg = lg + lax.dot_general(wrl_ref[...], uh, NT_DIMS, preferred_element_type=F32)
        lg = lg + lax.dot_general(wrh_ref[...], ul, NT_DIMS, preferred_element_type=F32)
        lg_out[:, rows] = lg


def _outproj(ohg, omla, x2, mod3, og, g2, wo_bf, wr_hi, wr_lo, seq):
    t, d = x2.shape
    tm = min(512, seq)
    per_b = seq // tm
    full = lambda shape: pl.BlockSpec(shape, lambda i: tuple(0 for _ in shape))
    modspec = lambda col: pl.BlockSpec((1, 1, d), lambda i: (i // per_b, 0, col))
    return pl.pallas_call(
        functools.partial(_outproj_kernel, n_sub=2),
        out_shape=(
            jax.ShapeDtypeStruct((t, d), F32),
            jax.ShapeDtypeStruct((t, d // 128, 128), BF16),
            jax.ShapeDtypeStruct((N_EXPERTS, t), F32),
        ),
        grid=(t // tm,),
        in_specs=[
            pl.BlockSpec((tm, HG_WIDTH), lambda i: (i, 0)),
            pl.BlockSpec((tm, MLA_WIDTH), lambda i: (i, 0)),
            pl.BlockSpec((tm, d), lambda i: (i, 0)),
            modspec(2),
            modspec(4),
            modspec(3),
            full((1, MLA_WIDTH)),
            full((1, d)),
            full(wo_bf.shape),
            full(wr_hi.shape),
            full(wr_lo.shape),
        ],
        out_specs=(
            pl.BlockSpec((tm, d), lambda i: (i, 0)),
            pl.BlockSpec((tm, d // 128, 128), lambda i: (i, 0, 0)),
            pl.BlockSpec((N_EXPERTS, tm), lambda i: (0, i)),
        ),
        compiler_params=_cparams(("arbitrary",)),
        name="outproj",
    )(ohg, omla, x2, mod3, mod3, mod3, og, g2, wo_bf, wr_hi, wr_lo)


def _route_kernel(lg_ref, bias_ref, tri_ref, ltri_ref, idx_out, w_out, dest_out, info_out,
                  carry, base, *, tt, n_info):
    phase = pl.program_id(0)
    step = pl.program_id(1)

    @pl.when((phase == 0) & (step == 0))
    def _():
        carry[...] = jnp.zeros_like(carry)

    @pl.when((phase == 1) & (step == 0))
    def _():
        cnt = carry[...]
        padded = jnp.floor((cnt + (MOE_BLK - 1)) / MOE_BLK) * MOE_BLK
        hi, lo = _split_bf16(padded)
        ps = jnp.dot(ltri_ref[...], jnp.concatenate([hi, lo], axis=1), preferred_element_type=F32)
        pad_start = ps[:, :128] + ps[:, 128:]
        base[...] = pad_start
        carry[...] = jnp.zeros_like(carry)
        pad_end = pad_start + padded
        reps = n_info // 128
        pe = jnp.concatenate([pad_end] * reps, axis=1)
        starts = (lax.broadcasted_iota(jnp.int32, (N_EXPERTS, n_info), 1) * MOE_BLK).astype(F32)
        blk_e = jnp.sum(jnp.where(pe <= starts, 1.0, 0.0), axis=0, keepdims=True)
        blk_e = jnp.minimum(blk_e, N_EXPERTS - 1.0)
        n_used = pe[N_EXPERTS - 1:N_EXPERTS, :] / MOE_BLK
        diag = (lax.broadcasted_iota(jnp.int32, (N_EXPERTS, n_info), 0)
                == lax.broadcasted_iota(jnp.int32, (N_EXPERTS, n_info), 1))
        end_blk = jnp.sum(jnp.where(diag, pe, 0.0), axis=0, keepdims=True) / MOE_BLK
        pd = jnp.concatenate([padded] * reps, axis=1)
        num_blk = jnp.sum(jnp.where(diag, pd, 0.0), axis=0, keepdims=True) / MOE_BLK
        info_out[...] = jnp.concatenate(
            [blk_e, n_used, end_blk, num_blk, jnp.zeros((4, n_info), F32)],
            axis=0).astype(jnp.int32)

    ne, ng, eg = N_EXPERTS, N_GROUPS, EXPERTS_PER_GROUP
    neg = -jnp.inf
    sc = _sigmoid(lg_ref[...])
    sel = sc + bias_ref[...]
    sel3 = sel.reshape(ng, eg, tt)
    sub = lax.broadcasted_iota(jnp.int32, (ng, eg, tt), 1)
    m1 = jnp.max(sel3, axis=1, keepdims=True)
    first = jnp.min(jnp.where(sel3 == m1, sub, eg), axis=1, keepdims=True)
    m2 = jnp.max(jnp.where(sub == first, neg, sel3), axis=1, keepdims=True)
    gs = (m1 + m2).reshape(ng, tt)

    gio = lax.broadcasted_iota(jnp.int32, (ng, tt), 0)
    gsel = jnp.zeros((ng, tt), F32)
    for _ in range(TOPK_GROUPS):
        m = jnp.max(gs, axis=0, keepdims=True)
        gi = jnp.min(jnp.where(gs == m, gio, ng), axis=0, keepdims=True)
        hit = gio == gi
        gsel = jnp.where(hit, 1.0, gsel)
        gs = jnp.where(hit, neg, gs)
    gfull = jnp.broadcast_to(gsel.reshape(ng, 1, tt), (ng, eg, tt)).reshape(ne, tt)
    selm = jnp.where(gfull > 0.0, sel, neg)

    eio = lax.broadcasted_iota(jnp.int32, (ne, tt), 0)
    chosen = jnp.zeros((ne, tt), F32)
    idxs, ws = [], []
    for _ in range(TOP_K):
        m = jnp.max(selm, axis=0, keepdims=True)
        ei = jnp.min(jnp.where(selm == m, eio, ne), axis=0, keepdims=True)
        hit = eio == ei
        ws.append(jnp.sum(jnp.where(hit, sc, 0.0), axis=0, keepdims=True))
        idxs.append(ei)
        chosen = jnp.where(hit, 1.0, chosen)
        selm = jnp.where(hit, neg, selm)
    wsum = ws[0]
    for k in range(1, TOP_K):
        wsum = wsum + ws[k]

    tot = jnp.sum(chosen, axis=1, keepdims=True)

    @pl.when(phase == 1)
    def _():
        off = base[...][:, :1] + carry[...][:, :1]
        excl = jnp.dot(chosen.astype(BF16), tri_ref[...], preferred_element_type=F32) + off
        for k in range(TOP_K):
            idx_out[k:k + 1, :] = idxs[k]
            w_out[k:k + 1, :] = ws[k] / wsum * ROUTED_SCALE
            dk = jnp.sum(jnp.where(eio == idxs[k], excl, 0.0), axis=0, keepdims=True)
            dest_out[k:k + 1, :] = dk.astype(jnp.int32)

    carry[...] = carry[...] + tot


def _route(logits_t, bias_col, n_blk):
    ne, t = logits_t.shape
    tt = min(512, t)
    n_info = -(-n_blk // 128) * 128
    tri = jnp.asarray(np.triu(np.ones((tt, tt), np.float32), 1), BF16)
    ltri = jnp.asarray(np.tril(np.ones((ne, ne), np.float32), -1), BF16)
    return pl.pallas_call(
        functools.partial(_route_kernel, tt=tt, n_info=n_info),
        out_shape=(
            jax.ShapeDtypeStruct((TOP_K, t), jnp.int32),
            jax.ShapeDtypeStruct((TOP_K, t), F32),
            jax.ShapeDtypeStruct((TOP_K, t), jnp.int32),
            jax.ShapeDtypeStruct((8, n_info), jnp.int32),
        ),
        grid=(2, t // tt),
        in_specs=[
            pl.BlockSpec((ne, tt), lambda p, i: (0, i)),
            pl.BlockSpec((ne, 1), lambda p, i: (0, 0)),
            pl.BlockSpec((tt, tt), lambda p, i: (0, 0)),
            pl.BlockSpec((ne, ne), lambda p, i: (0, 0)),
        ],
        out_specs=(
            pl.BlockSpec((TOP_K, tt), lambda p, i: (0, i * p)),
            pl.BlockSpec((TOP_K, tt), lambda p, i: (0, i * p)),
            pl.BlockSpec((TOP_K, tt), lambda p, i: (0, i * p)),
            pl.BlockSpec((8, n_info), lambda p, i: (0, 0)),
        ),
        scratch_shapes=[pltpu.VMEM((ne, 128), F32), pltpu.VMEM((ne, 128), F32)],
        compiler_params=_cparams(("arbitrary", "arbitrary")),
        name="route",
    )(logits_t, bias_col, tri, ltri)


def _dispatch_kernel(info_ref, dest_ref, u_hbm, xs_hbm, stage, zbuf, lsem, ssem, zsem, *, tm, n_blk):
    i = pl.program_id(0)
    n = pl.num_programs(0)
    slot = i % 2
    n_used = info_ref[1, 0]

    def load(tile, dst_slot):
        return pltpu.make_async_copy(u_hbm.at[pl.ds(tile * tm, tm)], stage.at[dst_slot],
                                     lsem.at[dst_slot])

    def zero_block(blk):
        return pltpu.make_async_copy(zbuf, xs_hbm.at[pl.ds(blk * MOE_BLK, MOE_BLK)], zsem)

    @pl.when(i == 0)
    def _():
        load(0, 0).start()
        zbuf[...] = jnp.zeros_like(zbuf)
        for e in range(N_EXPERTS):
            @pl.when(info_ref[3, e] > 0)
            def _():
                zero_block(info_ref[2, e] - 1).start()

        def zero_unused(j, carry):
            zero_block(j).start()
            return carry
        lax.fori_loop(n_used, n_blk, zero_unused, 0)
        for e in range(N_EXPERTS):
            @pl.when(info_ref[3, e] > 0)
            def _():
                zero_block(0).wait()

        def wait_unused(j, carry):
            zero_block(0).wait()
            return carry
        lax.fori_loop(n_used, n_blk, wait_unused, 0)

    def scatter_wait(src_slot):
        pltpu.make_async_copy(stage.at[src_slot], xs_hbm.at[pl.ds(0, tm)], ssem.at[src_slot]).wait()

    @pl.when(i >= 1)
    def _():
        for _ in range(TOP_K):
            scatter_wait(1 - slot)

    @pl.when(i + 1 < n)
    def _():
        load(i + 1, 1 - slot).start()

    load(i, slot).wait()
    for k in range(TOP_K):
        for r in range(tm):
            pltpu.make_async_copy(stage.at[slot, r], xs_hbm.at[dest_ref[0, k, r]],
                                  ssem.at[slot]).start(priority=r % 2)

    @pl.when(i == n - 1)
    def _():
        for _ in range(TOP_K):
            scatter_wait(slot)


def _dispatch(info, dest3, u3, n_blk):
    t, nt, _ = u3.shape
    tm = dest3.shape[2]
    return pl.pallas_call(
        functools.partial(_dispatch_kernel, tm=tm, n_blk=n_blk),
        out_shape=jax.ShapeDtypeStruct((n_blk * MOE_BLK, nt, 128), BF16),
        grid_spec=pltpu.PrefetchScalarGridSpec(
            num_scalar_prefetch=1,
            grid=(t // tm,),
            in_specs=[
                pl.BlockSpec((1, TOP_K, tm), lambda i, info: (i, 0, 0), memory_space=pltpu.SMEM),
                pl.BlockSpec(memory_space=pl.ANY),
            ],
            out_specs=pl.BlockSpec(memory_space=pl.ANY),
            scratch_shapes=[
                pltpu.VMEM((2, tm, nt, 128), BF16),
                pltpu.VMEM((MOE_BLK, nt, 128), BF16),
                pltpu.SemaphoreType.DMA((2,)),
                pltpu.SemaphoreType.DMA((2,)),
                pltpu.SemaphoreType.DMA(()),
            ],
        ),
        compiler_params=_cparams(("arbitrary",)),
        name="moe_dispatch",
    )(info, dest3, u3)


def _gmm_kernel(info_ref, xs_ref, wg_hbm, wu_hbm, wd_hbm, y_ref,
                wg_f, wu_f, wd_f, wsem, wg_s, wu_s, wd_s, nswap):
    i = pl.program_id(0)
    n_used = info_ref[1, 0]
    e = info_ref[0, i]
    prev = info_ref[0, jnp.maximum(i - 1, 0)]
    fresh = (i == 0) | (e != prev)

    def fetch(expert, slot):
        return (pltpu.make_async_copy(wg_hbm.at[expert], wg_f.at[slot], wsem.at[slot]),
                pltpu.make_async_copy(wu_hbm.at[expert], wu_f.at[slot], wsem.at[slot]),
                pltpu.make_async_copy(wd_hbm.at[expert], wd_f.at[slot], wsem.at[slot]))

    @pl.when(i == 0)
    def _():
        nswap[0] = 0
        for c in fetch(e, 0):
            c.start()

    @pl.when(fresh & (i < n_used))
    def _():
        slot = nswap[0] % 2
        nswap[0] = nswap[0] + 1
        for c in fetch(e, slot):
            c.wait()
        nxt = info_ref[2, e]

        @pl.when(nxt < n_used)
        def _():
            for c in fetch(info_ref[0, nxt], 1 - slot):
                c.start(priority=1)

        wg_s[...] = wg_f[slot].astype(BF16)
        wu_s[...] = wu_f[slot].astype(BF16)
        wd_s[...] = wd_f[slot].astype(BF16)

    @pl.when(i < n_used)
    def _():
        xb = _tiles_to_rows(xs_ref[...])
        a = jnp.dot(xb, wg_s[...], preferred_element_type=F32)
        u = jnp.dot(xb, wu_s[...], preferred_element_type=F32)
        hid = (a * _sigmoid(a) * u).astype(BF16)
        y = jnp.dot(hid, wd_s[...], preferred_element_type=F32)
        y_ref[...] = _rows_to_tiles(y.astype(BF16))

    @pl.when(i >= n_used)
    def _():
        y_ref[...] = jnp.zeros_like(y_ref)


def _gmm(info, xs3, w_gate, w_up, w_down):
    p, nt, _ = xs3.shape
    d = nt * 128
    n_blk = p // MOE_BLK
    de = w_gate.shape[-1]
    return pl.pallas_call(
        _gmm_kernel,
        out_shape=jax.ShapeDtypeStruct((p, nt, 128), BF16),
        grid_spec=pltpu.PrefetchScalarGridSpec(
            num_scalar_prefetch=1,
            grid=(n_blk,),
            in_specs=[
                pl.BlockSpec((MOE_BLK, nt, 128),
                             lambda i, info: (jnp.minimum(i, info[1, 0] - 1), 0, 0)),
                pl.BlockSpec(memory_space=pl.ANY),
                pl.BlockSpec(memory_space=pl.ANY),
                pl.BlockSpec(memory_space=pl.ANY),
            ],
            out_specs=pl.BlockSpec((MOE_BLK, nt, 128), lambda i, info: (i, 0, 0)),
            scratch_shapes=[
                pltpu.VMEM((2, d, de), F32),
                pltpu.VMEM((2, d, de), F32),
                pltpu.VMEM((2, de, d), F32),
                pltpu.SemaphoreType.DMA((2,)),
                pltpu.VMEM((d, de), BF16),
                pltpu.VMEM((d, de), BF16),
                pltpu.VMEM((de, d), BF16),
                pltpu.SMEM((1,), jnp.int32),
            ],
        ),
        compiler_params=_cparams(("arbitrary",)),
        name="moe_gmm",
    )(info, xs3, w_gate, w_up, w_down)


def _final_kernel(cur_ref, nxt_ref, u_ref, h_ref, gt_ref, wt_ref, wsg_ref, wsu_ref, wsd_ref, y_hbm,
                  o_ref, ybuf_a, ybuf_b, sem, *, tm):
    i = pl.program_id(0)
    n = pl.num_programs(0)
    n_rows = TOP_K * tm

    def issue(idx_ref, tile, buf, sem_i):
        for k in range(TOP_K):
            for r in range(tm):
                pltpu.make_async_copy(y_hbm.at[idx_ref[0, tile * TOP_K + k, r]], buf.at[k * tm + r],
                                      sem.at[sem_i]).start(priority=r % 2)

    def gather_wait(buf, sem_i):
        pltpu.make_async_copy(y_hbm.at[pl.ds(0, n_rows)], buf, sem.at[sem_i]).wait()

    eye = (lax.broadcasted_iota(jnp.int32, (tm, tm), 0)
           == lax.broadcasted_iota(jnp.int32, (tm, tm), 1))

    def compute(tile, buf):
        rows = slice(tile * tm, (tile + 1) * tm)
        ub = _tiles_to_rows(u_ref[rows])
        a = jnp.dot(ub, wsg_ref[...], preferred_element_type=F32)
        b = jnp.dot(ub, wsu_ref[...], preferred_element_type=F32)
        hid = (a * _sigmoid(a) * b).astype(BF16)
        acc = jnp.dot(hid, wsd_ref[...], preferred_element_type=F32)
        wt = wt_ref[rows, :]
        wcat = jnp.concatenate([jnp.where(eye, wt[:, k:k + 1], 0.0) for k in range(TOP_K)],
                               axis=1)
        whi, wlo = _split_bf16(wcat)
        ycat = _tiles_to_rows(buf[...])
        acc = acc + jnp.dot(whi, ycat, preferred_element_type=F32)
        acc = acc + jnp.dot(wlo, ycat, preferred_element_type=F32)
        o_ref[rows, :] = h_ref[rows, :] + gt_ref[0] * acc

    @pl.when(i == 0)
    def _():
        issue(cur_ref, 0, ybuf_a, 0)

    gather_wait(ybuf_a, 0)
    issue(cur_ref, 1, ybuf_b, 1)
    compute(0, ybuf_a)
    gather_wait(ybuf_b, 1)
    issue(nxt_ref, 0, ybuf_a, 0)
    compute(1, ybuf_b)

    @pl.when(i == n - 1)
    def _():
        gather_wait(ybuf_a, 0)


def _final(dest3, u2, h1, mod3, wt, wsg, wsu, wsd, y, seq):
    t, d = h1.shape
    tm = dest3.shape[2]
    n = t // (2 * tm)
    per_b = seq // (2 * tm)
    dest_pairs = dest3.reshape(n, 2 * TOP_K, tm)
    full = lambda shape: pl.BlockSpec(shape, lambda i: tuple(0 for _ in shape))
    return pl.pallas_call(
        functools.partial(_final_kernel, tm=tm),
        out_shape=jax.ShapeDtypeStruct((t, d), F32),
        grid=(n,),
        in_specs=[
            pl.BlockSpec((1, 2 * TOP_K, tm), lambda i: (i, 0, 0), memory_space=pltpu.SMEM),
            pl.BlockSpec((1, 2 * TOP_K, tm), lambda i: (jnp.minimum(i + 1, n - 1), 0, 0),
                         memory_space=pltpu.SMEM),
            pl.BlockSpec((2 * tm, d // 128, 128), lambda i: (i, 0, 0)),
            pl.BlockSpec((2 * tm, d), lambda i: (i, 0)),
            pl.BlockSpec((1, 1, d), lambda i: (i // per_b, 0, 5)),
            pl.BlockSpec((2 * tm, TOP_K), lambda i: (i, 0)),
            full(wsg.shape),
            full(wsu.shape),
            full(wsd.shape),
            pl.BlockSpec(memory_space=pl.ANY),
        ],
        out_specs=pl.BlockSpec((2 * tm, d), lambda i: (i, 0)),
        scratch_shapes=[
            pltpu.VMEM((TOP_K * tm, d // 128, 128), BF16),
            pltpu.VMEM((TOP_K * tm, d // 128, 128), BF16),
            pltpu.SemaphoreType.DMA((2,)),
        ],
        compiler_params=_cparams(("arbitrary",)),
        name="final",
    )(dest_pairs, dest_pairs, u2, h1, mod3, wt, wsg, wsu, wsd, y)


def _layer(h2, mod3, positions, lb, p, batch, seq):
    t, d = h2.shape

    w_in_t = p["w_in"].T
    proj, cq, ckv, kr = _inproj(h2, p["norm_mix_g"][None, :], mod3,
                                w_in_t[:COL_CQ].astype(BF16), w_in_t[COL_CQ:].astype(BF16), seq)

    o_hg = _hgrn(proj, lb[None, :], p["hg_out_g"][None, :], batch, seq)

    wuq = p["mla_w_uq"].reshape(MLA_Q_RANK, MLA_HEADS, MLA_QK)
    wuq_p = jnp.pad(wuq, ((0, 0), (0, 0), (0, MLA_QK_PAD - MLA_QK))).reshape(
        MLA_Q_RANK, MLA_HEADS * MLA_QK_PAD).astype(BF16)
    wukv = p["mla_w_ukv"].astype(BF16)
    gq = jnp.pad(p["mla_q_norm_g"], (0, MLA_QK_PAD - MLA_QK))[None, :]
    gkn = p["mla_k_norm_g"][None, :MLA_NOPE]
    gkr = jnp.pad(p["mla_k_norm_g"][MLA_NOPE:], (0, 128 - MLA_ROPE))[None, :]
    half = MLA_ROPE // 2
    inv_freq = ROPE_THETA ** (-jnp.arange(half, dtype=F32) / half)
    zeros64 = jnp.zeros((128 - MLA_ROPE,), F32)
    invf = jnp.concatenate([inv_freq, inv_freq, zeros64])[None, :]
    sgn = jnp.concatenate([-jnp.ones((half,), F32), jnp.ones((half,), F32), zeros64])[None, :]
    cm = jnp.concatenate([jnp.ones((MLA_ROPE,), F32), zeros64])[None, :]
    q, k, v = _mla_prep(cq, ckv, kr, positions.reshape(t, 1), p["mla_q_a_g"][None, :],
                        p["mla_kv_a_g"][None, :], wuq_p, wukv, gq, gkn, gkr, invf, sgn, cm,
                        batch, seq)
    o_mla = _attention(q, k, v)

    wr_t = p["w_router"].T
    wr_hi = wr_t.astype(BF16)
    wr_lo = (wr_t - wr_hi.astype(F32)).astype(BF16)
    h1, u2, logits_t = _outproj(o_hg, o_mla, h2, mod3, p["mla_out_g"][None, :],
                                p["norm_ffn_g"][None, :], p["w_out"].astype(BF16),
                                wr_hi, wr_lo, seq)

    n_blk = (t * TOP_K) // MOE_BLK + N_EXPERTS
    _, top_w, dest, info = _route(logits_t, p["router_bias"][:, None], n_blk)

    tm = min(128, seq)
    dest3 = dest.reshape(TOP_K, t // tm, tm).transpose(1, 0, 2)
    xs = _dispatch(info, dest3, u2, n_blk)
    y = _gmm(info, xs, p["w_gate"], p["w_up"], p["w_down"])
    return _final(dest3, u2, h1, mod3, top_w.T, p["ws_gate"].astype(BF16),
                  p["ws_up"].astype(BF16), p["ws_down"].astype(BF16), y, seq)


def kernel(x, c, positions, w_ada, b_ada, norm_mix_g, norm_ffn_g, w_in, hg_lb_logits, hg_out_g,
           mla_q_a_g, mla_w_uq, mla_kv_a_g, mla_w_ukv, mla_q_norm_g, mla_k_norm_g, mla_out_g,
           w_out, w_router, router_bias, w_gate, w_up, w_down, ws_gate, ws_up, ws_down):
    batch, seq, d = x.shape
    depth = w_ada.shape[0]
    lb_all = jnp.cumsum(jax.nn.softmax(hg_lb_logits.astype(F32), axis=0), axis=0)
    c8 = jnp.pad(c, ((0, 8 - batch), (0, 0)))
    h2 = x.reshape(batch * seq, d)
    for l in range(depth):
        mod = _adaln(c8, w_ada[l], b_ada[l][None, :])[:batch]
        mod3 = mod.reshape(batch, 1, 6 * d)
        p = dict(norm_mix_g=norm_mix_g[l], norm_ffn_g=norm_ffn_g[l], w_in=w_in[l],
                 hg_out_g=hg_out_g[l], mla_q_a_g=mla_q_a_g[l], mla_w_uq=mla_w_uq[l],
                 mla_kv_a_g=mla_kv_a_g[l], mla_w_ukv=mla_w_ukv[l], mla_q_norm_g=mla_q_norm_g[l],
                 mla_k_norm_g=mla_k_norm_g[l], mla_out_g=mla_out_g[l], w_out=w_out[l],
                 w_router=w_router[l], router_bias=router_bias[l], w_gate=w_gate[l],
                 w_up=w_up[l], w_down=w_down[l], ws_gate=ws_gate[l], ws_up=ws_up[l],
                 ws_down=ws_down[l])
        h2 = _layer(h2, mod3, positions, lb_all[l], p, batch, seq)
    return h2.reshape(batch, seq, d)
```

```python
import functools

import numpy as np
import jax
import jax.numpy as jnp
from jax import lax
from jax.experimental import pallas as pl
from jax.experimental.pallas import tpu as pltpu

F32 = jnp.float32
BF16 = jnp.bfloat16

D_MODEL = 2048
EPS = 1e-6

HG_HEADS = 8
HG_DK = 128
HG_DV = 128
HG_WIDTH = HG_HEADS * HG_DV
HG_CHUNK = 128
HG_ROW_LEVEL = 8
HG_HEADS_PER_STEP = 2
HG_UNROLL = 8

MLA_HEADS = 8
MLA_Q_RANK = 512
MLA_KV_RANK = 256
MLA_NOPE = 128
MLA_ROPE = 64
MLA_QK = MLA_NOPE + MLA_ROPE
MLA_QK_PAD = 256
MLA_V = 128
MLA_WIDTH = MLA_HEADS * MLA_V
ROPE_THETA = 10000.0

COL_CQ = 4096

N_EXPERTS = 64
TOP_K = 8
N_GROUPS = 8
TOPK_GROUPS = 4
EXPERTS_PER_GROUP = N_EXPERTS // N_GROUPS
D_EXPERT = 512
ROUTED_SCALE = 2.5
MOE_BLK = 256

VMEM_LIMIT = 56 * 1024 * 1024

ADALN_TN = 1024
INPROJ_TM = 1024
INPROJ_TN = 1024
MLA_PREP_TM = 256
ATTN_TQ = 256
OUTPROJ_TM = 512
OUTPROJ_SUB = 2
ROUTE_TT = 512
MOE_TM = 128

NT_DIMS = (((1,), (1,)), ((), ()))
TN_DIMS = (((0,), (0,)), ((), ()))


def _cparams(sem):
    return pltpu.CompilerParams(dimension_semantics=sem, vmem_limit_bytes=VMEM_LIMIT)


def _sigmoid(x):
    return 1.0 / (1.0 + jnp.exp(-x))


def _split_bf16(x):
    hi = x.astype(BF16)
    lo = (x - hi.astype(F32)).astype(BF16)
    return hi, lo


def _rows_to_tiles(x):
    n = x.shape[1] // 128
    return jnp.swapaxes(jnp.stack([x[:, s * 128:(s + 1) * 128] for s in range(n)], axis=0), 0, 1)


def _tiles_to_rows(x3):
    xt = jnp.swapaxes(x3, 0, 1)
    return jnp.concatenate([xt[s] for s in range(x3.shape[1])], axis=1)


def _adaln_kernel(c_ref, w_ref, b_ref, o_ref):
    c = c_ref[...]
    cond = c * _sigmoid(c)
    hi, lo = _split_bf16(cond)
    lhs = jnp.concatenate([hi, lo], axis=0)
    r = jnp.dot(lhs, w_ref[...].astype(BF16), preferred_element_type=F32)
    o_ref[...] = r[:8] + r[8:] + b_ref[...]


def _adaln(c8, w, b):
    d, n = w.shape
    tn = ADALN_TN
    return pl.pallas_call(
        _adaln_kernel,
        out_shape=jax.ShapeDtypeStruct((8, n), F32),
        grid=(n // tn,),
        in_specs=[
            pl.BlockSpec((8, d), lambda j: (0, 0)),
            pl.BlockSpec((d, tn), lambda j: (0, j)),
            pl.BlockSpec((1, tn), lambda j: (0, j)),
        ],
        out_specs=pl.BlockSpec((8, tn), lambda j: (0, j)),
        compiler_params=_cparams(("arbitrary",)),
        name="adaln",
    )(c8, w, b)


def _inproj_kernel(x_ref, g_ref, sc_ref, sh_ref, whg_ref, wml_ref, hg_out, cq_out, ckv_out, kr_out,
                   u_scr, *, tm, n_hg):
    j = pl.program_id(1)

    @pl.when(j == 0)
    def _():
        g = g_ref[...]
        sc = 1.0 + sc_ref[0]
        sh = sh_ref[0]

        def body(r, carry):
            rows = pl.ds(pl.multiple_of(r * 128, 128), 128)
            x = x_ref[rows, :]
            ms = jnp.mean(x * x, axis=-1, keepdims=True)
            u = x * lax.rsqrt(ms + EPS) * g * sc + sh
            u_scr[rows, :] = u.astype(BF16)
            return carry

        lax.fori_loop(0, tm // 128, body, 0)

    @pl.when(j < n_hg)
    def _():
        hg_out[...] = lax.dot_general(u_scr[...], whg_ref[...], NT_DIMS,
                                      preferred_element_type=F32)

    @pl.when(j == n_hg)
    def _():
        r = lax.dot_general(u_scr[...], wml_ref[...], NT_DIMS, preferred_element_type=F32)
        cq_out[...] = r[:, :MLA_Q_RANK]
        ckv_out[...] = r[:, MLA_Q_RANK:MLA_Q_RANK + MLA_KV_RANK]
        kr_out[...] = r[:, MLA_Q_RANK + MLA_KV_RANK:]


def _inproj(x2, g, mod3, w_hg_t, w_ml_t, seq):
    t, d = x2.shape
    tm = min(INPROJ_TM, seq)
    tn = INPROJ_TN
    n_hg = w_hg_t.shape[0] // tn
    per_b = seq // tm
    last = lambda j: jnp.minimum(j, n_hg - 1)
    return pl.pallas_call(
        functools.partial(_inproj_kernel, tm=tm, n_hg=n_hg),
        out_shape=(
            jax.ShapeDtypeStruct((t, w_hg_t.shape[0]), F32),
            jax.ShapeDtypeStruct((t, MLA_Q_RANK), F32),
            jax.ShapeDtypeStruct((t, MLA_KV_RANK), F32),
            jax.ShapeDtypeStruct((t, MLA_ROPE), F32),
        ),
        grid=(t // tm, n_hg + 1),
        in_specs=[
            pl.BlockSpec((tm, d), lambda i, j: (i, 0)),
            pl.BlockSpec((1, d), lambda i, j: (0, 0)),
            pl.BlockSpec((1, 1, d), lambda i, j: (i // per_b, 0, 1)),
            pl.BlockSpec((1, 1, d), lambda i, j: (i // per_b, 0, 0)),
            pl.BlockSpec((tn, d), lambda i, j: (last(j), 0)),
            pl.BlockSpec(w_ml_t.shape, lambda i, j: (0, 0)),
        ],
        out_specs=(
            pl.BlockSpec((tm, tn), lambda i, j: (i, last(j))),
            pl.BlockSpec((tm, MLA_Q_RANK), lambda i, j: (i, 0)),
            pl.BlockSpec((tm, MLA_KV_RANK), lambda i, j: (i, 0)),
            pl.BlockSpec((tm, MLA_ROPE), lambda i, j: (i, 0)),
        ),
        scratch_shapes=[pltpu.VMEM((tm, d), BF16)],
        compiler_params=_cparams(("arbitrary", "arbitrary")),
        name="inproj",
    )(x2, g, mod3, mod3, w_hg_t, w_ml_t)


def _hgrn_consts():
    c = HG_CHUNK
    t = np.arange(c)[:, None]
    j = np.arange(c)[None, :]
    tril = (j <= t).astype(np.float32)
    mats = [tril]
    masks = [np.eye(c, dtype=np.float32)]
    s = c // 2
    while s >= 1:
        ref = (t // (2 * s)) * (2 * s) + s - 1
        if s < HG_ROW_LEVEL:
            mats.append(tril - (j <= ref).astype(np.float32))
        masks.append((((t // s) % 2 == 1) & ((j // s) == (t // s) - 1)).astype(np.float32))
        s //= 2
    w = np.concatenate(mats, 0)
    return np.concatenate([w, w], 1), np.stack(masks, 0)


def _hgrn_head(q, v, g, kk, xs, og, m_ref, st_ref, o_ref):
    c = HG_CHUNK
    n_lvl = m_ref.shape[0]
    b = xs[0:c]
    bl = b[c - 1:c, :] - b
    st = st_ref[...]
    vb = v.astype(BF16)
    o = lax.dot_general((q * jnp.exp(b)).astype(BF16), st.astype(BF16), NT_DIMS,
                        preferred_element_type=F32)
    attn = m_ref[0] * lax.dot_general(q.astype(BF16), kk.astype(BF16), NT_DIMS,
                                      preferred_element_type=F32)
    s = c // 2
    fine = 1
    for l in range(1, n_lvl):
        if s >= HG_ROW_LEVEL:
            b3 = b.reshape(c // (2 * s), 2 * s, HG_DK)
            x = (b3 - b3[:, s - 1:s, :]).reshape(c, HG_DK)
        else:
            x = xs[fine * c:(fine + 1) * c]
            fine += 1
        s //= 2
        e = jnp.exp(-jnp.abs(x))
        attn = attn + m_ref[l] * lax.dot_general(
            (q * e).astype(BF16), (kk * e).astype(BF16), NT_DIMS, preferred_element_type=F32)
    o = o + jnp.dot(attn.astype(BF16), vb, preferred_element_type=F32)
    khat = (kk * jnp.exp(bl)).astype(BF16)
    st_ref[...] = st * jnp.exp(b[c - 1:c, :]) + lax.dot_general(
        vb, khat, TN_DIMS, preferred_element_type=F32)
    ms = jnp.mean(o * o, axis=-1, keepdims=True)
    y = o * lax.rsqrt(ms + EPS) * og * (g * _sigmoid(g))
    o_ref[...] = y.astype(BF16)


def _hgrn_kernel(q_ref, f_ref, i_ref, g_ref, lb_ref, og_ref, w_ref, m_ref, o_ref, st_scr, *,
                 seq, nh):
    c = HG_CHUNK
    st_scr[...] = jnp.zeros_like(st_scr)
    lb = lb_ref[...]
    og = og_ref[...]

    def chunk(ci, carry):
        rows = pl.ds(pl.multiple_of(ci * c, c), c)
        f = lb + (1.0 - lb) * _sigmoid(f_ref[rows, :])
        hi, lo = _split_bf16(jnp.log(f))
        xs = jnp.dot(w_ref[...], jnp.concatenate([hi, lo], axis=0), preferred_element_type=F32)
        for hh in range(nh):
            lanes = slice(hh * HG_DK, (hh + 1) * HG_DK)
            _hgrn_head(q_ref[rows, lanes], i_ref[rows, lanes], g_ref[rows, lanes],
                       1.0 - f[:, lanes], xs[:, lanes], og, m_ref, st_scr.at[hh],
                       o_ref.at[rows, lanes])
        return carry

    lax.fori_loop(0, seq // c, chunk, 0, unroll=HG_UNROLL // nh)


def _hgrn(proj, lb, og, batch, seq):
    t = proj.shape[0]
    wm, mm = _hgrn_consts()
    wm = jnp.asarray(wm, BF16)
    mm = jnp.asarray(mm, F32)
    nh = HG_HEADS_PER_STEP
    hg = HG_HEADS // nh
    wd = nh * HG_DK
    return pl.pallas_call(
        functools.partial(_hgrn_kernel, seq=seq, nh=nh),
        out_shape=jax.ShapeDtypeStruct((t, HG_WIDTH), BF16),
        grid=(batch, hg),
        in_specs=[
            pl.BlockSpec((seq, wd), lambda b, i: (b, i)),
            pl.BlockSpec((seq, wd), lambda b, i: (b, hg + i)),
            pl.BlockSpec((seq, wd), lambda b, i: (b, 2 * hg + i)),
            pl.BlockSpec((seq, wd), lambda b, i: (b, 3 * hg + i)),
            pl.BlockSpec((1, wd), lambda b, i: (0, i)),
            pl.BlockSpec((1, HG_DV), lambda b, i: (0, 0)),
            pl.BlockSpec(wm.shape, lambda b, i: (0, 0)),
            pl.BlockSpec(mm.shape, lambda b, i: (0, 0, 0)),
        ],
        out_specs=pl.BlockSpec((seq, wd), lambda b, i: (b, i)),
        scratch_shapes=[pltpu.VMEM((nh, HG_DV, HG_DK), F32)],
        compiler_params=_cparams(("arbitrary", "arbitrary")),
        name="hgrn2",
    )(proj, proj, proj, proj, lb, og, wm, mm)


def _mla_prep_kernel(cq_ref, ckv_ref, kr_ref, pos_ref, qag_ref, kvag_ref, wuq_ref, wukv_ref,
                     gq_ref, gkn_ref, gkr_ref, invf_ref, sgn_ref, cm_ref,
                     q_out, k_out, v_out):
    cq = cq_ref[...]
    cqn = cq * lax.rsqrt(jnp.mean(cq * cq, axis=-1, keepdims=True) + EPS) * qag_ref[...]
    qf = jnp.dot(cqn.astype(BF16), wuq_ref[...], preferred_element_type=F32)
    ckv = ckv_ref[...]
    ckvn = ckv * lax.rsqrt(jnp.mean(ckv * ckv, axis=-1, keepdims=True) + EPS) * kvag_ref[...]
    kvf = jnp.dot(ckvn.astype(BF16), wukv_ref[...], preferred_element_type=F32)

    ang = pos_ref[...].astype(F32) * invf_ref[...]
    cos_t = jnp.cos(ang) * cm_ref[...]
    sin_t = jnp.sin(ang) * sgn_ref[...]

    def rope(r):
        return r * cos_t + (pltpu.roll(r, 96, 1) + pltpu.roll(r, 32, 1)) * sin_t

    kr64 = kr_ref[...]
    kr = jnp.concatenate([kr64, jnp.zeros_like(kr64)], axis=-1)
    kr_ss = jnp.sum(kr * kr, axis=-1, keepdims=True)
    krr = rope(kr * gkr_ref[...])
    gq = gq_ref[...]
    gkn = gkn_ref[...]
    scale = MLA_QK ** -0.5
    for h in range(MLA_HEADS):
        qh = qf[:, h * MLA_QK_PAD:(h + 1) * MLA_QK_PAD]
        inv = lax.rsqrt(jnp.sum(qh * qh, axis=-1, keepdims=True) / MLA_QK + EPS)
        qn = qh * inv * gq
        qo = jnp.concatenate([qn[:, :MLA_NOPE], rope(qn[:, MLA_NOPE:])], axis=-1) * scale
        q_out[0, h] = qo.astype(BF16)
        kn = kvf[:, h * 256:h * 256 + MLA_NOPE]
        inv = lax.rsqrt((jnp.sum(kn * kn, axis=-1, keepdims=True) + kr_ss) / MLA_QK + EPS)
        k_out[0, h] = jnp.concatenate([kn * inv * gkn, krr * inv], axis=-1).astype(BF16)
        v_out[0, h] = kvf[:, h * 256 + MLA_NOPE:(h + 1) * 256].astype(BF16)


def _mla_prep(cq, ckv, kr, pos2, qag, kvag, wuq_p, wukv, gq, gkn, gkr, invf, sgn, cm, batch, seq):
    tm = min(MLA_PREP_TM, seq)
    per_b = seq // tm
    hh = MLA_HEADS
    full = lambda shape: pl.BlockSpec(shape, lambda b, i: tuple(0 for _ in shape))
    return pl.pallas_call(
        _mla_prep_kernel,
        out_shape=(
            jax.ShapeDtypeStruct((batch, hh, seq, MLA_QK_PAD), BF16),
            jax.ShapeDtypeStruct((batch, hh, seq, MLA_QK_PAD), BF16),
            jax.ShapeDtypeStruct((batch, hh, seq, MLA_V), BF16),
        ),
        grid=(batch, per_b),
        in_specs=[
            pl.BlockSpec((tm, MLA_Q_RANK), lambda b, i: (b * per_b + i, 0)),
            pl.BlockSpec((tm, MLA_KV_RANK), lambda b, i: (b * per_b + i, 0)),
            pl.BlockSpec((tm, MLA_ROPE), lambda b, i: (b * per_b + i, 0)),
            pl.BlockSpec((tm, 1), lambda b, i: (b * per_b + i, 0)),
            full((1, MLA_Q_RANK)),
            full((1, MLA_KV_RANK)),
            full(wuq_p.shape),
            full(wukv.shape),
            full((1, MLA_QK_PAD)),
            full((1, 128)),
            full((1, 128)),
            full((1, 128)),
            full((1, 128)),
            full((1, 128)),
        ],
        out_specs=(
            pl.BlockSpec((1, hh, tm, MLA_QK_PAD), lambda b, i: (b, 0, i, 0)),
            pl.BlockSpec((1, hh, tm, MLA_QK_PAD), lambda b, i: (b, 0, i, 0)),
            pl.BlockSpec((1, hh, tm, MLA_V), lambda b, i: (b, 0, i, 0)),
        ),
        compiler_params=_cparams(("arbitrary", "arbitrary")),
        name="mla_prep",
    )(cq, ckv, kr, pos2, qag, kvag, wuq_p, wukv, gq, gkn, gkr, invf, sgn, cm)


def _attn_kernel(q_ref, k_ref, v_ref, o_ref, *, tq, seq):
    row = lax.broadcasted_iota(jnp.int32, (tq, tq), 0)
    col = lax.broadcasted_iota(jnp.int32, (tq, tq), 1)
    causal = col <= row
    for i in range(seq // tq):
        lo = i * tq
        q = q_ref[0, 0, lo:lo + tq, :]
        sd = lax.dot_general(q, k_ref[0, 0, lo:lo + tq, :], NT_DIMS, preferred_element_type=F32)
        sd = jnp.where(causal, sd, -jnp.inf)
        m = jnp.max(sd, axis=-1, keepdims=True)
        if i > 0:
            so = lax.dot_general(q, k_ref[0, 0, :lo, :], NT_DIMS, preferred_element_type=F32)
            m = jnp.maximum(m, jnp.max(so, axis=-1, keepdims=True))
        pd = jnp.exp(sd - m)
        l = jnp.sum(pd, axis=-1, keepdims=True)
        o = jnp.dot(pd.astype(BF16), v_ref[0, 0, lo:lo + tq, :], preferred_element_type=F32)
        if i > 0:
            po = jnp.exp(so - m)
            l = l + jnp.sum(po, axis=-1, keepdims=True)
            o = o + jnp.dot(po.astype(BF16), v_ref[0, 0, :lo, :], preferred_element_type=F32)
        o_ref[lo:lo + tq, :] = o / l


def _attention(q, k, v):
    batch, hh, seq, dq = q.shape
    tq = min(ATTN_TQ, seq)
    return pl.pallas_call(
        functools.partial(_attn_kernel, tq=tq, seq=seq),
        out_shape=jax.ShapeDtypeStruct((batch * seq, hh * MLA_V), F32),
        grid=(batch, hh),
        in_specs=[
            pl.BlockSpec((1, 1, seq, dq), lambda b, h: (b, h, 0, 0)),
            pl.BlockSpec((1, 1, seq, dq), lambda b, h: (b, h, 0, 0)),
            pl.BlockSpec((1, 1, seq, MLA_V), lambda b, h: (b, h, 0, 0)),
        ],
        out_specs=pl.BlockSpec((seq, MLA_V), lambda b, h: (b, h)),
        compiler_params=_cparams(("arbitrary", "arbitrary")),
        name="mla_attn",
    )(q, k, v)


def _outproj_kernel(ohg_ref, omla_ref, x_ref, gt_ref, sc_ref, sh_ref, og_ref, g2_ref,
                    wo_ref, wrh_ref, wrl_ref, h_out, u_out, lg_out, *, n_sub):
    sub = x_ref.shape[0] // n_sub
    for j in range(n_sub):
        rows = slice(j * sub, (j + 1) * sub)
        om = omla_ref[rows, :]
        omn = om * lax.rsqrt(jnp.mean(om * om, axis=-1, keepdims=True) + EPS) * og_ref[...]
        y = jnp.dot(ohg_ref[rows, :], wo_ref[:HG_WIDTH, :], preferred_element_type=F32)
        y = y + jnp.dot(omn.astype(BF16), wo_ref[HG_WIDTH:, :], preferred_element_type=F32)
        h1 = x_ref[rows, :] + gt_ref[0] * y
        h_out[rows, :] = h1
        u = h1 * lax.rsqrt(jnp.mean(h1 * h1, axis=-1, keepdims=True) + EPS) * g2_ref[...]
        u = u * (1.0 + sc_ref[0]) + sh_ref[0]
        uh, ul = _split_bf16(u)
        u_out[rows] = _rows_to_tiles(uh)
        lg = lax.dot_general(wrh_ref[...], uh, NT_DIMS, preferred_element_type=F32)
        lg = lg + lax.dot_general(wrl_ref[...], uh, NT_DIMS, preferred_element_type=F32)
        lg = lg + lax.dot_general(wrh_ref[...], ul, NT_DIMS, preferred_element_type=F32)
        lg_out[:, rows] = lg


def _outproj(ohg, omla, x2, mod3, og, g2, wo_bf, wr_hi, wr_lo, seq):
    t, d = x2.shape
    tm = min(OUTPROJ_TM, seq)
    per_b = seq // tm
    full = lambda shape: pl.BlockSpec(shape, lambda i: tuple(0 for _ in shape))
    modspec = lambda col: pl.BlockSpec((1, 1, d), lambda i: (i // per_b, 0, col))
    return pl.pallas_call(
        functools.partial(_outproj_kernel, n_sub=OUTPROJ_SUB),
        out_shape=(
            jax.ShapeDtypeStruct((t, d), F32),
            jax.ShapeDtypeStruct((t, d // 128, 128), BF16),
            jax.ShapeDtypeStruct((N_EXPERTS, t), F32),
        ),
        grid=(t // tm,),
        in_specs=[
            pl.BlockSpec((tm, HG_WIDTH), lambda i: (i, 0)),
            pl.BlockSpec((tm, MLA_WIDTH), lambda i: (i, 0)),
            pl.BlockSpec((tm, d), lambda i: (i, 0)),
            modspec(2),
            modspec(4),
            modspec(3),
            full((1, MLA_WIDTH)),
            full((1, d)),
            full(wo_bf.shape),
            full(wr_hi.shape),
            full(wr_lo.shape),
        ],
        out_specs=(
            pl.BlockSpec((tm, d), lambda i: (i, 0)),
            pl.BlockSpec((tm, d // 128, 128), lambda i: (i, 0, 0)),
            pl.BlockSpec((N_EXPERTS, tm), lambda i: (0, i)),
        ),
        compiler_params=_cparams(("arbitrary",)),
        name="outproj",
    )(ohg, omla, x2, mod3, mod3, mod3, og, g2, wo_bf, wr_hi, wr_lo)


def _route_kernel(lg_ref, bias_ref, tri_ref, ltri_ref, idx_out, w_out, dest_out, info_out,
                  carry, base, *, tt, n_info):
    phase = pl.program_id(0)
    step = pl.program_id(1)

    @pl.when((phase == 0) & (step == 0))
    def _():
        carry[...] = jnp.zeros_like(carry)

    @pl.when((phase == 1) & (step == 0))
    def _():
        cnt = carry[...]
        padded = jnp.floor((cnt + (MOE_BLK - 1)) / MOE_BLK) * MOE_BLK
        hi, lo = _split_bf16(padded)
        ps = jnp.dot(ltri_ref[...], jnp.concatenate([hi, lo], axis=1), preferred_element_type=F32)
        pad_start = ps[:, :128] + ps[:, 128:]
        base[...] = pad_start
        carry[...] = jnp.zeros_like(carry)
        pad_end = pad_start + padded
        reps = n_info // 128
        pe = jnp.concatenate([pad_end] * reps, axis=1)
        starts = (lax.broadcasted_iota(jnp.int32, (N_EXPERTS, n_info), 1) * MOE_BLK).astype(F32)
        blk_e = jnp.sum(jnp.where(pe <= starts, 1.0, 0.0), axis=0, keepdims=True)
        blk_e = jnp.minimum(blk_e, N_EXPERTS - 1.0)
        n_used = pe[N_EXPERTS - 1:N_EXPERTS, :] / MOE_BLK
        diag = (lax.broadcasted_iota(jnp.int32, (N_EXPERTS, n_info), 0)
                == lax.broadcasted_iota(jnp.int32, (N_EXPERTS, n_info), 1))
        end_blk = jnp.sum(jnp.where(diag, pe, 0.0), axis=0, keepdims=True) / MOE_BLK
        pd = jnp.concatenate([padded] * reps, axis=1)
        num_blk = jnp.sum(jnp.where(diag, pd, 0.0), axis=0, keepdims=True) / MOE_BLK
        info_out[...] = jnp.concatenate(
            [blk_e, n_used, end_blk, num_blk, jnp.zeros((4, n_info), F32)],
            axis=0).astype(jnp.int32)

    ne, ng, eg = N_EXPERTS, N_GROUPS, EXPERTS_PER_GROUP
    neg = -jnp.inf
    sc = _sigmoid(lg_ref[...])
    sel = sc + bias_ref[...]
    sel3 = sel.reshape(ng, eg, tt)
    sub = lax.broadcasted_iota(jnp.int32, (ng, eg, tt), 1)
    m1 = jnp.max(sel3, axis=1, keepdims=True)
    first = jnp.min(jnp.where(sel3 == m1, sub, eg), axis=1, keepdims=True)
    m2 = jnp.max(jnp.where(sub == first, neg, sel3), axis=1, keepdims=True)
    gs = (m1 + m2).reshape(ng, tt)

    gio = lax.broadcasted_iota(jnp.int32, (ng, tt), 0)
    gsel = jnp.zeros((ng, tt), F32)
    for _ in range(TOPK_GROUPS):
        m = jnp.max(gs, axis=0, keepdims=True)
        gi = jnp.min(jnp.where(gs == m, gio, ng), axis=0, keepdims=True)
        hit = gio == gi
        gsel = jnp.where(hit, 1.0, gsel)
        gs = jnp.where(hit, neg, gs)
    gfull = jnp.broadcast_to(gsel.reshape(ng, 1, tt), (ng, eg, tt)).reshape(ne, tt)
    selm = jnp.where(gfull > 0.0, sel, neg)

    eio = lax.broadcasted_iota(jnp.int32, (ne, tt), 0)
    chosen = jnp.zeros((ne, tt), F32)
    idxs, ws = [], []
    for _ in range(TOP_K):
        m = jnp.max(selm, axis=0, keepdims=True)
        ei = jnp.min(jnp.where(selm == m, eio, ne), axis=0, keepdims=True)
        hit = eio == ei
        ws.append(jnp.sum(jnp.where(hit, sc, 0.0), axis=0, keepdims=True))
        idxs.append(ei)
        chosen = jnp.where(hit, 1.0, chosen)
        selm = jnp.where(hit, neg, selm)
    wsum = ws[0]
    for k in range(1, TOP_K):
        wsum = wsum + ws[k]

    tot = jnp.sum(chosen, axis=1, keepdims=True)

    @pl.when(phase == 1)
    def _():
        off = base[...][:, :1] + carry[...][:, :1]
        excl = jnp.dot(chosen.astype(BF16), tri_ref[...], preferred_element_type=F32) + off
        for k in range(TOP_K):
            idx_out[k:k + 1, :] = idxs[k]
            w_out[k:k + 1, :] = ws[k] / wsum * ROUTED_SCALE
            dk = jnp.sum(jnp.where(eio == idxs[k], excl, 0.0), axis=0, keepdims=True)
            dest_out[k:k + 1, :] = dk.astype(jnp.int32)

    carry[...] = carry[...] + tot


def _route(logits_t, bias_col, n_blk):
    ne, t = logits_t.shape
    tt = min(ROUTE_TT, t)
    n_info = -(-n_blk // 128) * 128
    tri = jnp.asarray(np.triu(np.ones((tt, tt), np.float32), 1), BF16)
    ltri = jnp.asarray(np.tril(np.ones((ne, ne), np.float32), -1), BF16)
    return pl.pallas_call(
        functools.partial(_route_kernel, tt=tt, n_info=n_info),
        out_shape=(
            jax.ShapeDtypeStruct((TOP_K, t), jnp.int32),
            jax.ShapeDtypeStruct((TOP_K, t), F32),
            jax.ShapeDtypeStruct((TOP_K, t), jnp.int32),
            jax.ShapeDtypeStruct((8, n_info), jnp.int32),
        ),
        grid=(2, t // tt),
        in_specs=[
            pl.BlockSpec((ne, tt), lambda p, i: (0, i)),
            pl.BlockSpec((ne, 1), lambda p, i: (0, 0)),
            pl.BlockSpec((tt, tt), lambda p, i: (0, 0)),
            pl.BlockSpec((ne, ne), lambda p, i: (0, 0)),
        ],
        out_specs=(
            pl.BlockSpec((TOP_K, tt), lambda p, i: (0, i * p)),
            pl.BlockSpec((TOP_K, tt), lambda p, i: (0, i * p)),
            pl.BlockSpec((TOP_K, tt), lambda p, i: (0, i * p)),
            pl.BlockSpec((8, n_info), lambda p, i: (0, 0)),
        ),
        scratch_shapes=[pltpu.VMEM((ne, 128), F32), pltpu.VMEM((ne, 128), F32)],
        compiler_params=_cparams(("arbitrary", "arbitrary")),
        name="route",
    )(logits_t, bias_col, tri, ltri)


def _dispatch_kernel(info_ref, dest_ref, u_hbm, xs_hbm, stage, zbuf, lsem, ssem, zsem, *, tm, n_blk):
    i = pl.program_id(0)
    n = pl.num_programs(0)
    slot = i % 2
    n_used = info_ref[1, 0]

    def load(tile, dst_slot):
        return pltpu.make_async_copy(u_hbm.at[pl.ds(tile * tm, tm)], stage.at[dst_slot],
                                     lsem.at[dst_slot])

    def zero_block(blk):
        return pltpu.make_async_copy(zbuf, xs_hbm.at[pl.ds(blk * MOE_BLK, MOE_BLK)], zsem)

    @pl.when(i == 0)
    def _():
        load(0, 0).start()
        zbuf[...] = jnp.zeros_like(zbuf)
        for e in range(N_EXPERTS):
            @pl.when(info_ref[3, e] > 0)
            def _():
                zero_block(info_ref[2, e] - 1).start()

        def zero_unused(j, carry):
            zero_block(j).start()
            return carry
        lax.fori_loop(n_used, n_blk, zero_unused, 0)
        for e in range(N_EXPERTS):
            @pl.when(info_ref[3, e] > 0)
            def _():
                zero_block(0).wait()

        def wait_unused(j, carry):
            zero_block(0).wait()
            return carry
        lax.fori_loop(n_used, n_blk, wait_unused, 0)

    def scatter_wait(src_slot):
        pltpu.make_async_copy(stage.at[src_slot], xs_hbm.at[pl.ds(0, tm)], ssem.at[src_slot]).wait()

    @pl.when(i >= 1)
    def _():
        for _ in range(TOP_K):
            scatter_wait(1 - slot)

    @pl.when(i + 1 < n)
    def _():
        load(i + 1, 1 - slot).start()

    load(i, slot).wait()
    for k in range(TOP_K):
        for r in range(tm):
            pltpu.make_async_copy(stage.at[slot, r], xs_hbm.at[dest_ref[0, k, r]],
                                  ssem.at[slot]).start(priority=r % 2)

    @pl.when(i == n - 1)
    def _():
        for _ in range(TOP_K):
            scatter_wait(slot)


def _dispatch(info, dest3, u3, n_blk):
    t, nt, _ = u3.shape
    tm = dest3.shape[2]
    return pl.pallas_call(
        functools.partial(_dispatch_kernel, tm=tm, n_blk=n_blk),
        out_shape=jax.ShapeDtypeStruct((n_blk * MOE_BLK, nt, 128), BF16),
        grid_spec=pltpu.PrefetchScalarGridSpec(
            num_scalar_prefetch=1,
            grid=(t // tm,),
            in_specs=[
                pl.BlockSpec((1, TOP_K, tm), lambda i, info: (i, 0, 0), memory_space=pltpu.SMEM),
                pl.BlockSpec(memory_space=pl.ANY),
            ],
            out_specs=pl.BlockSpec(memory_space=pl.ANY),
            scratch_shapes=[
                pltpu.VMEM((2, tm, nt, 128), BF16),
                pltpu.VMEM((MOE_BLK, nt, 128), BF16),
                pltpu.SemaphoreType.DMA((2,)),
                pltpu.SemaphoreType.DMA((2,)),
                pltpu.SemaphoreType.DMA(()),
            ],
        ),
        compiler_params=_cparams(("arbitrary",)),
        name="moe_dispatch",
    )(info, dest3, u3)


def _gmm_kernel(info_ref, xs_ref, wg_hbm, wu_hbm, wd_hbm, y_ref,
                wg_f, wu_f, wd_f, wsem, wg_s, wu_s, wd_s, nswap):
    i = pl.program_id(0)
    n_used = info_ref[1, 0]
    e = info_ref[0, i]
    prev = info_ref[0, jnp.maximum(i - 1, 0)]
    fresh = (i == 0) | (e != prev)

    def fetch(expert, slot):
        return (pltpu.make_async_copy(wg_hbm.at[expert], wg_f.at[slot], wsem.at[slot]),
                pltpu.make_async_copy(wu_hbm.at[expert], wu_f.at[slot], wsem.at[slot]),
                pltpu.make_async_copy(wd_hbm.at[expert], wd_f.at[slot], wsem.at[slot]))

    @pl.when(i == 0)
    def _():
        nswap[0] = 0
        for c in fetch(e, 0):
            c.start()

    @pl.when(fresh & (i < n_used))
    def _():
        slot = nswap[0] % 2
        nswap[0] = nswap[0] + 1
        for c in fetch(e, slot):
            c.wait()
        nxt = info_ref[2, e]

        @pl.when(nxt < n_used)
        def _():
            for c in fetch(info_ref[0, nxt], 1 - slot):
                c.start(priority=1)

        wg_s[...] = wg_f[slot].astype(BF16)
        wu_s[...] = wu_f[slot].astype(BF16)
        wd_s[...] = wd_f[slot].astype(BF16)

    @pl.when(i < n_used)
    def _():
        xb = _tiles_to_rows(xs_ref[...])
        a = jnp.dot(xb, wg_s[...], preferred_element_type=F32)
        u = jnp.dot(xb, wu_s[...], preferred_element_type=F32)
        hid = (a * _sigmoid(a) * u).astype(BF16)
        y = jnp.dot(hid, wd_s[...], preferred_element_type=F32)
        y_ref[...] = _rows_to_tiles(y.astype(BF16))

    @pl.when(i >= n_used)
    def _():
        y_ref[...] = jnp.zeros_like(y_ref)


def _gmm(info, xs3, w_gate, w_up, w_down):
    p, nt, _ = xs3.shape
    d = nt * 128
    n_blk = p // MOE_BLK
    de = w_gate.shape[-1]
    return pl.pallas_call(
        _gmm_kernel,
        out_shape=jax.ShapeDtypeStruct((p, nt, 128), BF16),
        grid_spec=pltpu.PrefetchScalarGridSpec(
            num_scalar_prefetch=1,
            grid=(n_blk,),
            in_specs=[
                pl.BlockSpec((MOE_BLK, nt, 128),
                             lambda i, info: (jnp.minimum(i, info[1, 0] - 1), 0, 0)),
                pl.BlockSpec(memory_space=pl.ANY),
                pl.BlockSpec(memory_space=pl.ANY),
                pl.BlockSpec(memory_space=pl.ANY),
            ],
            out_specs=pl.BlockSpec((MOE_BLK, nt, 128), lambda i, info: (i, 0, 0)),
            scratch_shapes=[
                pltpu.VMEM((2, d, de), F32),
                pltpu.VMEM((2, d, de), F32),
                pltpu.VMEM((2, de, d), F32),
                pltpu.SemaphoreType.DMA((2,)),
                pltpu.VMEM((d, de), BF16),
                pltpu.VMEM((d, de), BF16),
                pltpu.VMEM((de, d), BF16),
                pltpu.SMEM((1,), jnp.int32),
            ],
        ),
        compiler_params=_cparams(("arbitrary",)),
        name="moe_gmm",
    )(info, xs3, w_gate, w_up, w_down)


def _final_kernel(cur_ref, nxt_ref, u_ref, h_ref, gt_ref, wt_ref, wsg_ref, wsu_ref, wsd_ref, y_hbm,
                  o_ref, ybuf_a, ybuf_b, sem, *, tm):
    i = pl.program_id(0)
    n = pl.num_programs(0)
    n_rows = TOP_K * tm

    def issue(idx_ref, tile, buf, sem_i):
        for k in range(TOP_K):
            for r in range(tm):
                pltpu.make_async_copy(y_hbm.at[idx_ref[0, tile * TOP_K + k, r]], buf.at[k * tm + r],
                                      sem.at[sem_i]).start(priority=r % 2)

    def gather_wait(buf, sem_i):
        pltpu.make_async_copy(y_hbm.at[pl.ds(0, n_rows)], buf, sem.at[sem_i]).wait()

    eye = (lax.broadcasted_iota(jnp.int32, (tm, tm), 0)
           == lax.broadcasted_iota(jnp.int32, (tm, tm), 1))

    def compute(tile, buf):
        rows = slice(tile * tm, (tile + 1) * tm)
        ub = _tiles_to_rows(u_ref[rows])
        a = jnp.dot(ub, wsg_ref[...], preferred_element_type=F32)
        b = jnp.dot(ub, wsu_ref[...], preferred_element_type=F32)
        hid = (a * _sigmoid(a) * b).astype(BF16)
        acc = jnp.dot(hid, wsd_ref[...], preferred_element_type=F32)
        wt = wt_ref[rows, :]
        wcat = jnp.concatenate([jnp.where(eye, wt[:, k:k + 1], 0.0) for k in range(TOP_K)],
                               axis=1)
        whi, wlo = _split_bf16(wcat)
        ycat = _tiles_to_rows(buf[...])
        acc = acc + jnp.dot(whi, ycat, preferred_element_type=F32)
        acc = acc + jnp.dot(wlo, ycat, preferred_element_type=F32)
        o_ref[rows, :] = h_ref[rows, :] + gt_ref[0] * acc

    @pl.when(i == 0)
    def _():
        issue(cur_ref, 0, ybuf_a, 0)

    gather_wait(ybuf_a, 0)
    issue(cur_ref, 1, ybuf_b, 1)
    compute(0, ybuf_a)
    gather_wait(ybuf_b, 1)
    issue(nxt_ref, 0, ybuf_a, 0)
    compute(1, ybuf_b)

    @pl.when(i == n - 1)
    def _():
        gather_wait(ybuf_a, 0)


def _final(dest3, u2, h1, mod3, wt, wsg, wsu, wsd, y, seq):
    t, d = h1.shape
    tm = dest3.shape[2]
    n = t // (2 * tm)
    per_b = seq // (2 * tm)
    dest_pairs = dest3.reshape(n, 2 * TOP_K, tm)
    full = lambda shape: pl.BlockSpec(shape, lambda i: tuple(0 for _ in shape))
    return pl.pallas_call(
        functools.partial(_final_kernel, tm=tm),
        out_shape=jax.ShapeDtypeStruct((t, d), F32),
        grid=(n,),
        in_specs=[
            pl.BlockSpec((1, 2 * TOP_K, tm), lambda i: (i, 0, 0), memory_space=pltpu.SMEM),
            pl.BlockSpec((1, 2 * TOP_K, tm), lambda i: (jnp.minimum(i + 1, n - 1), 0, 0),
                         memory_space=pltpu.SMEM),
            pl.BlockSpec((2 * tm, d // 128, 128), lambda i: (i, 0, 0)),
            pl.BlockSpec((2 * tm, d), lambda i: (i, 0)),
            pl.BlockSpec((1, 1, d), lambda i: (i // per_b, 0, 5)),
            pl.BlockSpec((2 * tm, TOP_K), lambda i: (i, 0)),
            full(wsg.shape),
            full(wsu.shape),
            full(wsd.shape),
            pl.BlockSpec(memory_space=pl.ANY),
        ],
        out_specs=pl.BlockSpec((2 * tm, d), lambda i: (i, 0)),
        scratch_shapes=[
            pltpu.VMEM((TOP_K * tm, d // 128, 128), BF16),
            pltpu.VMEM((TOP_K * tm, d // 128, 128), BF16),
            pltpu.SemaphoreType.DMA((2,)),
        ],
        compiler_params=_cparams(("arbitrary",)),
        name="final",
    )(dest_pairs, dest_pairs, u2, h1, mod3, wt, wsg, wsu, wsd, y)


def _layer(h2, mod3, positions, lb, p, batch, seq):
    t, d = h2.shape

    w_in_t = p["w_in"].T
    proj, cq, ckv, kr = _inproj(h2, p["norm_mix_g"][None, :], mod3,
                                w_in_t[:COL_CQ].astype(BF16), w_in_t[COL_CQ:].astype(BF16), seq)

    o_hg = _hgrn(proj, lb[None, :], p["hg_out_g"][None, :], batch, seq)

    wuq = p["mla_w_uq"].reshape(MLA_Q_RANK, MLA_HEADS, MLA_QK)
    wuq_p = jnp.pad(wuq, ((0, 0), (0, 0), (0, MLA_QK_PAD - MLA_QK))).reshape(
        MLA_Q_RANK, MLA_HEADS * MLA_QK_PAD).astype(BF16)
    wukv = p["mla_w_ukv"].astype(BF16)
    gq = jnp.pad(p["mla_q_norm_g"], (0, MLA_QK_PAD - MLA_QK))[None, :]
    gkn = p["mla_k_norm_g"][None, :MLA_NOPE]
    gkr = jnp.pad(p["mla_k_norm_g"][MLA_NOPE:], (0, 128 - MLA_ROPE))[None, :]
    half = MLA_ROPE // 2
    inv_freq = ROPE_THETA ** (-jnp.arange(half, dtype=F32) / half)
    zeros64 = jnp.zeros((128 - MLA_ROPE,), F32)
    invf = jnp.concatenate([inv_freq, inv_freq, zeros64])[None, :]
    sgn = jnp.concatenate([-jnp.ones((half,), F32), jnp.ones((half,), F32), zeros64])[None, :]
    cm = jnp.concatenate([jnp.ones((MLA_ROPE,), F32), zeros64])[None, :]
    q, k, v = _mla_prep(cq, ckv, kr, positions.reshape(t, 1), p["mla_q_a_g"][None, :],
                        p["mla_kv_a_g"][None, :], wuq_p, wukv, gq, gkn, gkr, invf, sgn, cm,
                        batch, seq)
    o_mla = _attention(q, k, v)

    wr_t = p["w_router"].T
    wr_hi = wr_t.astype(BF16)
    wr_lo = (wr_t - wr_hi.astype(F32)).astype(BF16)
    h1, u2, logits_t = _outproj(o_hg, o_mla, h2, mod3, p["mla_out_g"][None, :],
                                p["norm_ffn_g"][None, :], p["w_out"].astype(BF16),
                                wr_hi, wr_lo, seq)

    n_blk = (t * TOP_K) // MOE_BLK + N_EXPERTS
    _, top_w, dest, info = _route(logits_t, p["router_bias"][:, None], n_blk)

    tm = min(MOE_TM, seq)
    dest3 = dest.reshape(TOP_K, t // tm, tm).transpose(1, 0, 2)
    xs = _dispatch(info, dest3, u2, n_blk)
    y = _gmm(info, xs, p["w_gate"], p["w_up"], p["w_down"])
    return _final(dest3, u2, h1, mod3, top_w.T, p["ws_gate"].astype(BF16),
                  p["ws_up"].astype(BF16), p["ws_down"].astype(BF16), y, seq)


def kernel(x, c, positions, w_ada, b_ada, norm_mix_g, norm_ffn_g, w_in, hg_lb_logits, hg_out_g,
           mla_q_a_g, mla_w_uq, mla_kv_a_g, mla_w_ukv, mla_q_norm_g, mla_k_norm_g, mla_out_g,
           w_out, w_router, router_bias, w_gate, w_up, w_down, ws_gate, ws_up, ws_down):
    batch, seq, d = x.shape
    depth = w_ada.shape[0]
    lb_all = jnp.cumsum(jax.nn.softmax(hg_lb_logits.astype(F32), axis=0), axis=0)
    c8 = jnp.pad(c, ((0, 8 - batch), (0, 0)))
    h2 = x.reshape(batch * seq, d)
    for l in range(depth):
        mod = _adaln(c8, w_ada[l], b_ada[l][None, :])[:batch]
        mod3 = mod.reshape(batch, 1, 6 * d)
        p = dict(norm_mix_g=norm_mix_g[l], norm_ffn_g=norm_ffn_g[l], w_in=w_in[l],
                 hg_out_g=hg_out_g[l], mla_q_a_g=mla_q_a_g[l], mla_w_uq=mla_w_uq[l],
                 mla_kv_a_g=mla_kv_a_g[l], mla_w_ukv=mla_w_ukv[l], mla_q_norm_g=mla_q_norm_g[l],
                 mla_k_norm_g=mla_k_norm_g[l], mla_out_g=mla_out_g[l], w_out=w_out[l],
                 w_router=w_router[l], router_bias=router_bias[l], w_gate=w_gate[l],
                 w_up=w_up[l], w_down=w_down[l], ws_gate=ws_gate[l], ws_up=ws_up[l],
                 ws_down=ws_down[l])
        h2 = _layer(h2, mod3, positions, lb_all[l], p, batch, seq)
    return h2.reshape(batch, seq, d)
```

```python
import functools

import numpy as np
import jax
import jax.numpy as jnp
from jax import lax
from jax.experimental import pallas as pl
from jax.experimental.pallas import tpu as pltpu

F32 = jnp.float32
BF16 = jnp.bfloat16

D_MODEL = 2048
EPS = 1e-6

HG_HEADS = 8
HG_DK = 128
HG_DV = 128
HG_WIDTH = HG_HEADS * HG_DV
HG_CHUNK = 128
HG_ROW_LEVEL = 2
HG_HEADS_PER_STEP = 2
HG_UNROLL = 16

MLA_HEADS = 8
MLA_Q_RANK = 512
MLA_KV_RANK = 256
MLA_NOPE = 128
MLA_ROPE = 64
MLA_QK = MLA_NOPE + MLA_ROPE
MLA_QK_PAD = 256
MLA_V = 128
MLA_WIDTH = MLA_HEADS * MLA_V
ROPE_THETA = 10000.0

COL_CQ = 4096

N_EXPERTS = 64
TOP_K = 8
N_GROUPS = 8
TOPK_GROUPS = 4
EXPERTS_PER_GROUP = N_EXPERTS // N_GROUPS
D_EXPERT = 512
ROUTED_SCALE = 2.5
MOE_BLK = 256

VMEM_LIMIT = 56 * 1024 * 1024

ADALN_TN = 1024
INPROJ_TM = 1024
INPROJ_TN = 1024
MLA_PREP_TM = 256
ATTN_TQ = 256
OUTPROJ_TM = 512
OUTPROJ_SUB = 2
ROUTE_TT = 512
MOE_TM = 128

NT_DIMS = (((1,), (1,)), ((), ()))
TN_DIMS = (((0,), (0,)), ((), ()))


def _cparams(sem):
    return pltpu.CompilerParams(dimension_semantics=sem, vmem_limit_bytes=VMEM_LIMIT)


def _sigmoid(x):
    return 1.0 / (1.0 + jnp.exp(-x))


def _split_bf16(x):
    hi = x.astype(BF16)
    lo = (x - hi.astype(F32)).astype(BF16)
    return hi, lo


def _rows_to_tiles(x):
    n = x.shape[1] // 128
    return jnp.swapaxes(jnp.stack([x[:, s * 128:(s + 1) * 128] for s in range(n)], axis=0), 0, 1)


def _tiles_to_rows(x3):
    xt = jnp.swapaxes(x3, 0, 1)
    return jnp.concatenate([xt[s] for s in range(x3.shape[1])], axis=1)


def _adaln_kernel(c_ref, w_ref, b_ref, o_ref):
    c = c_ref[...]
    cond = c * _sigmoid(c)
    hi, lo = _split_bf16(cond)
    lhs = jnp.concatenate([hi, lo], axis=0)
    r = jnp.dot(lhs, w_ref[...].astype(BF16), preferred_element_type=F32)
    o_ref[...] = r[:8] + r[8:] + b_ref[...]


def _adaln(c8, w, b):
    d, n = w.shape
    tn = ADALN_TN
    return pl.pallas_call(
        _adaln_kernel,
        out_shape=jax.ShapeDtypeStruct((8, n), F32),
        grid=(n // tn,),
        in_specs=[
            pl.BlockSpec((8, d), lambda j: (0, 0)),
            pl.BlockSpec((d, tn), lambda j: (0, j)),
            pl.BlockSpec((1, tn), lambda j: (0, j)),
        ],
        out_specs=pl.BlockSpec((8, tn), lambda j: (0, j)),
        compiler_params=_cparams(("arbitrary",)),
        name="adaln",
    )(c8, w, b)


def _inproj_kernel(x_ref, g_ref, sc_ref, sh_ref, whg_ref, wml_ref, hg_out, cq_out, ckv_out, kr_out,
                   u_scr, *, tm, n_hg):
    j = pl.program_id(1)

    @pl.when(j == 0)
    def _():
        g = g_ref[...]
        sc = 1.0 + sc_ref[0]
        sh = sh_ref[0]

        def body(r, carry):
            rows = pl.ds(pl.multiple_of(r * 128, 128), 128)
            x = x_ref[rows, :]
            ms = jnp.mean(x * x, axis=-1, keepdims=True)
            u = x * lax.rsqrt(ms + EPS) * g * sc + sh
            u_scr[rows, :] = u.astype(BF16)
            return carry

        lax.fori_loop(0, tm // 128, body, 0)

    @pl.when(j < n_hg)
    def _():
        hg_out[...] = lax.dot_general(u_scr[...], whg_ref[...], NT_DIMS,
                                      preferred_element_type=F32)

    @pl.when(j == n_hg)
    def _():
        r = lax.dot_general(u_scr[...], wml_ref[...], NT_DIMS, preferred_element_type=F32)
        cq_out[...] = r[:, :MLA_Q_RANK]
        ckv_out[...] = r[:, MLA_Q_RANK:MLA_Q_RANK + MLA_KV_RANK]
        kr_out[...] = r[:, MLA_Q_RANK + MLA_KV_RANK:]


def _inproj(x2, g, mod3, w_hg_t, w_ml_t, seq):
    t, d = x2.shape
    tm = min(INPROJ_TM, seq)
    tn = INPROJ_TN
    n_hg = w_hg_t.shape[0] // tn
    per_b = seq // tm
    last = lambda j: jnp.minimum(j, n_hg - 1)
    return pl.pallas_call(
        functools.partial(_inproj_kernel, tm=tm, n_hg=n_hg),
        out_shape=(
            jax.ShapeDtypeStruct((t, w_hg_t.shape[0]), F32),
            jax.ShapeDtypeStruct((t, MLA_Q_RANK), F32),
            jax.ShapeDtypeStruct((t, MLA_KV_RANK), F32),
            jax.ShapeDtypeStruct((t, MLA_ROPE), F32),
        ),
        grid=(t // tm, n_hg + 1),
        in_specs=[
            pl.BlockSpec((tm, d), lambda i, j: (i, 0)),
            pl.BlockSpec((1, d), lambda i, j: (0, 0)),
            pl.BlockSpec((1, 1, d), lambda i, j: (i // per_b, 0, 1)),
            pl.BlockSpec((1, 1, d), lambda i, j: (i // per_b, 0, 0)),
            pl.BlockSpec((tn, d), lambda i, j: (last(j), 0)),
            pl.BlockSpec(w_ml_t.shape, lambda i, j: (0, 0)),
        ],
        out_specs=(
            pl.BlockSpec((tm, tn), lambda i, j: (i, last(j))),
            pl.BlockSpec((tm, MLA_Q_RANK), lambda i, j: (i, 0)),
            pl.BlockSpec((tm, MLA_KV_RANK), lambda i, j: (i, 0)),
            pl.BlockSpec((tm, MLA_ROPE), lambda i, j: (i, 0)),
        ),
        scratch_shapes=[pltpu.VMEM((tm, d), BF16)],
        compiler_params=_cparams(("arbitrary", "arbitrary")),
        name="inproj",
    )(x2, g, mod3, mod3, w_hg_t, w_ml_t)


def _hgrn_consts():
    c = HG_CHUNK
    t = np.arange(c)[:, None]
    j = np.arange(c)[None, :]
    tril = (j <= t).astype(np.float32)
    mats = [tril]
    masks = [np.eye(c, dtype=np.float32)]
    s = c // 2
    while s >= 1:
        ref = (t // (2 * s)) * (2 * s) + s - 1
        if s < HG_ROW_LEVEL:
            mats.append(tril - (j <= ref).astype(np.float32))
        masks.append((((t // s) % 2 == 1) & ((j // s) == (t // s) - 1)).astype(np.float32))
        s //= 2
    w = np.concatenate(mats, 0)
    return np.concatenate([w, w], 1), np.stack(masks, 0)


def _hgrn_head(q, v, g, kk, xs, og, m_ref, st_ref, o_ref):
    c = HG_CHUNK
    n_lvl = m_ref.shape[0]
    b = xs[0:c]
    bl = b[c - 1:c, :] - b
    st = st_ref[...]
    vb = v.astype(BF16)
    o = lax.dot_general((q * jnp.exp(b)).astype(BF16), st.astype(BF16), NT_DIMS,
                        preferred_element_type=F32)
    attn = m_ref[0] * lax.dot_general(q.astype(BF16), kk.astype(BF16), NT_DIMS,
                                      preferred_element_type=F32)
    s = c // 2
    fine = 1
    for l in range(1, n_lvl):
        if s >= HG_ROW_LEVEL:
            b3 = b.reshape(c // (2 * s), 2 * s, HG_DK)
            x = (b3 - b3[:, s - 1:s, :]).reshape(c, HG_DK)
        else:
            x = xs[fine * c:(fine + 1) * c]
            fine += 1
        s //= 2
        e = jnp.exp(-jnp.abs(x))
        attn = attn + m_ref[l] * lax.dot_general(
            (q * e).astype(BF16), (kk * e).astype(BF16), NT_DIMS, preferred_element_type=F32)
    o = o + jnp.dot(attn.astype(BF16), vb, preferred_element_type=F32)
    khat = (kk * jnp.exp(bl)).astype(BF16)
    st_ref[...] = st * jnp.exp(b[c - 1:c, :]) + lax.dot_general(
        vb, khat, TN_DIMS, preferred_element_type=F32)
    ms = jnp.mean(o * o, axis=-1, keepdims=True)
    y = o * lax.rsqrt(ms + EPS) * og * (g * _sigmoid(g))
    o_ref[...] = y.astype(BF16)


def _hgrn_kernel(q_ref, f_ref, i_ref, g_ref, lb_ref, og_ref, w_ref, m_ref, o_ref, st_scr, *,
                 seq, nh):
    c = HG_CHUNK
    st_scr[...] = jnp.zeros_like(st_scr)
    lb = lb_ref[...]
    og = og_ref[...]

    def chunk(ci, carry):
        rows = pl.ds(pl.multiple_of(ci * c, c), c)
        f = lb + (1.0 - lb) * _sigmoid(f_ref[rows, :])
        hi, lo = _split_bf16(jnp.log(f))
        xs = jnp.dot(w_ref[...], jnp.concatenate([hi, lo], axis=0), preferred_element_type=F32)
        for hh in range(nh):
            lanes = slice(hh * HG_DK, (hh + 1) * HG_DK)
            _hgrn_head(q_ref[rows, lanes], i_ref[rows, lanes], g_ref[rows, lanes],
                       1.0 - f[:, lanes], xs[:, lanes], og, m_ref, st_scr.at[hh],
                       o_ref.at[rows, lanes])
        return carry

    lax.fori_loop(0, seq // c, chunk, 0, unroll=HG_UNROLL // nh)


def _hgrn(proj, lb, og, batch, seq):
    t = proj.shape[0]
    wm, mm = _hgrn_consts()
    wm = jnp.asarray(wm, BF16)
    mm = jnp.asarray(mm, F32)
    nh = HG_HEADS_PER_STEP
    hg = HG_HEADS // nh
    wd = nh * HG_DK
    return pl.pallas_call(
        functools.partial(_hgrn_kernel, seq=seq, nh=nh),
        out_shape=jax.ShapeDtypeStruct((t, HG_WIDTH), BF16),
        grid=(batch, hg),
        in_specs=[
            pl.BlockSpec((seq, wd), lambda b, i: (b, i)),
            pl.BlockSpec((seq, wd), lambda b, i: (b, hg + i)),
            pl.BlockSpec((seq, wd), lambda b, i: (b, 2 * hg + i)),
            pl.BlockSpec((seq, wd), lambda b, i: (b, 3 * hg + i)),
            pl.BlockSpec((1, wd), lambda b, i: (0, i)),
            pl.BlockSpec((1, HG_DV), lambda b, i: (0, 0)),
            pl.BlockSpec(wm.shape, lambda b, i: (0, 0)),
            pl.BlockSpec(mm.shape, lambda b, i: (0, 0, 0)),
        ],
        out_specs=pl.BlockSpec((seq, wd), lambda b, i: (b, i)),
        scratch_shapes=[pltpu.VMEM((nh, HG_DV, HG_DK), F32)],
        compiler_params=_cparams(("arbitrary", "arbitrary")),
        name="hgrn2",
    )(proj, proj, proj, proj, lb, og, wm, mm)


def _mla_prep_kernel(cq_ref, ckv_ref, kr_ref, pos_ref, qag_ref, kvag_ref, wuq_ref, wukv_ref,
                     gq_ref, gkn_ref, gkr_ref, invf_ref, sgn_ref, cm_ref,
                     q_out, k_out, v_out):
    cq = cq_ref[...]
    cqn = cq * lax.rsqrt(jnp.mean(cq * cq, axis=-1, keepdims=True) + EPS) * qag_ref[...]
    qf = jnp.dot(cqn.astype(BF16), wuq_ref[...], preferred_element_type=F32)
    ckv = ckv_ref[...]
    ckvn = ckv * lax.rsqrt(jnp.mean(ckv * ckv, axis=-1, keepdims=True) + EPS) * kvag_ref[...]
    kvf = jnp.dot(ckvn.astype(BF16), wukv_ref[...], preferred_element_type=F32)

    ang = pos_ref[...].astype(F32) * invf_ref[...]
    cos_t = jnp.cos(ang) * cm_ref[...]
    sin_t = jnp.sin(ang) * sgn_ref[...]

    def rope(r):
        return r * cos_t + (pltpu.roll(r, 96, 1) + pltpu.roll(r, 32, 1)) * sin_t

    kr64 = kr_ref[...]
    kr = jnp.concatenate([kr64, jnp.zeros_like(kr64)], axis=-1)
    kr_ss = jnp.sum(kr * kr, axis=-1, keepdims=True)
    krr = rope(kr * gkr_ref[...])
    gq = gq_ref[...]
    gkn = gkn_ref[...]
    scale = MLA_QK ** -0.5
    for h in range(MLA_HEADS):
        qh = qf[:, h * MLA_QK_PAD:(h + 1) * MLA_QK_PAD]
        inv = lax.rsqrt(jnp.sum(qh * qh, axis=-1, keepdims=True) / MLA_QK + EPS)
        qn = qh * inv * gq
        qo = jnp.concatenate([qn[:, :MLA_NOPE], rope(qn[:, MLA_NOPE:])], axis=-1) * scale
        q_out[0, h] = qo.astype(BF16)
        kn = kvf[:, h * 256:h * 256 + MLA_NOPE]
        inv = lax.rsqrt((jnp.sum(kn * kn, axis=-1, keepdims=True) + kr_ss) / MLA_QK + EPS)
        k_out[0, h] = jnp.concatenate([kn * inv * gkn, krr * inv], axis=-1).astype(BF16)
        v_out[0, h] = kvf[:, h * 256 + MLA_NOPE:(h + 1) * 256].astype(BF16)


def _mla_prep(cq, ckv, kr, pos2, qag, kvag, wuq_p, wukv, gq, gkn, gkr, invf, sgn, cm, batch, seq):
    tm = min(MLA_PREP_TM, seq)
    per_b = seq // tm
    hh = MLA_HEADS
    full = lambda shape: pl.BlockSpec(shape, lambda b, i: tuple(0 for _ in shape))
    return pl.pallas_call(
        _mla_prep_kernel,
        out_shape=(
            jax.ShapeDtypeStruct((batch, hh, seq, MLA_QK_PAD), BF16),
            jax.ShapeDtypeStruct((batch, hh, seq, MLA_QK_PAD), BF16),
            jax.ShapeDtypeStruct((batch, hh, seq, MLA_V), BF16),
        ),
        grid=(batch, per_b),
        in_specs=[
            pl.BlockSpec((tm, MLA_Q_RANK), lambda b, i: (b * per_b + i, 0)),
            pl.BlockSpec((tm, MLA_KV_RANK), lambda b, i: (b * per_b + i, 0)),
            pl.BlockSpec((tm, MLA_ROPE), lambda b, i: (b * per_b + i, 0)),
            pl.BlockSpec((tm, 1), lambda b, i: (b * per_b + i, 0)),
            full((1, MLA_Q_RANK)),
            full((1, MLA_KV_RANK)),
            full(wuq_p.shape),
            full(wukv.shape),
            full((1, MLA_QK_PAD)),
            full((1, 128)),
            full((1, 128)),
            full((1, 128)),
            full((1, 128)),
            full((1, 128)),
        ],
        out_specs=(
            pl.BlockSpec((1, hh, tm, MLA_QK_PAD), lambda b, i: (b, 0, i, 0)),
            pl.BlockSpec((1, hh, tm, MLA_QK_PAD), lambda b, i: (b, 0, i, 0)),
            pl.BlockSpec((1, hh, tm, MLA_V), lambda b, i: (b, 0, i, 0)),
        ),
        compiler_params=_cparams(("arbitrary", "arbitrary")),
        name="mla_prep",
    )(cq, ckv, kr, pos2, qag, kvag, wuq_p, wukv, gq, gkn, gkr, invf, sgn, cm)


def _attn_kernel(q_ref, k_ref, v_ref, o_ref, *, tq, seq):
    row = lax.broadcasted_iota(jnp.int32, (tq, tq), 0)
    col = lax.broadcasted_iota(jnp.int32, (tq, tq), 1)
    causal = col <= row
    for i in range(seq // tq):
        lo = i * tq
        q = q_ref[0, 0, lo:lo + tq, :]
        sd = lax.dot_general(q, k_ref[0, 0, lo:lo + tq, :], NT_DIMS, preferred_element_type=F32)
        sd = jnp.where(causal, sd, -jnp.inf)
        m = jnp.max(sd, axis=-1, keepdims=True)
        if i > 0:
            so = lax.dot_general(q, k_ref[0, 0, :lo, :], NT_DIMS, preferred_element_type=F32)
            m = jnp.maximum(m, jnp.max(so, axis=-1, keepdims=True))
        pd = jnp.exp(sd - m)
        l = jnp.sum(pd, axis=-1, keepdims=True)
        o = jnp.dot(pd.astype(BF16), v_ref[0, 0, lo:lo + tq, :], preferred_element_type=F32)
        if i > 0:
            po = jnp.exp(so - m)
            l = l + jnp.sum(po, axis=-1, keepdims=True)
            o = o + jnp.dot(po.astype(BF16), v_ref[0, 0, :lo, :], preferred_element_type=F32)
        o_ref[lo:lo + tq, :] = o / l


def _attention(q, k, v):
    batch, hh, seq, dq = q.shape
    tq = min(ATTN_TQ, seq)
    return pl.pallas_call(
        functools.partial(_attn_kernel, tq=tq, seq=seq),
        out_shape=jax.ShapeDtypeStruct((batch * seq, hh * MLA_V), F32),
        grid=(batch, hh),
        in_specs=[
            pl.BlockSpec((1, 1, seq, dq), lambda b, h: (b, h, 0, 0)),
            pl.BlockSpec((1, 1, seq, dq), lambda b, h: (b, h, 0, 0)),
            pl.BlockSpec((1, 1, seq, MLA_V), lambda b, h: (b, h, 0, 0)),
        ],
        out_specs=pl.BlockSpec((seq, MLA_V), lambda b, h: (b, h)),
        compiler_params=_cparams(("arbitrary", "arbitrary")),
        name="mla_attn",
    )(q, k, v)


def _outproj_kernel(ohg_ref, omla_ref, x_ref, gt_ref, sc_ref, sh_ref, og_ref, g2_ref,
                    wo_ref, wrh_ref, wrl_ref, h_out, u_out, lg_out, *, n_sub):
    sub = x_ref.shape[0] // n_sub
    for j in range(n_sub):
        rows = slice(j * sub, (j + 1) * sub)
        om = omla_ref[rows, :]
        omn = om * lax.rsqrt(jnp.mean(om * om, axis=-1, keepdims=True) + EPS) * og_ref[...]
        y = jnp.dot(ohg_ref[rows, :], wo_ref[:HG_WIDTH, :], preferred_element_type=F32)
        y = y + jnp.dot(omn.astype(BF16), wo_ref[HG_WIDTH:, :], preferred_element_type=F32)
        h1 = x_ref[rows, :] + gt_ref[0] * y
        h_out[rows, :] = h1
        u = h1 * lax.rsqrt(jnp.mean(h1 * h1, axis=-1, keepdims=True) + EPS) * g2_ref[...]
        u = u * (1.0 + sc_ref[0]) + sh_ref[0]
        uh, ul = _split_bf16(u)
        u_out[rows] = _rows_to_tiles(uh)
        lg = lax.dot_general(wrh_ref[...], uh, NT_DIMS, preferred_element_type=F32)
        lg = lg + lax.dot_general(wrl_ref[...], uh, NT_DIMS, preferred_element_type=F32)
        lg = lg + lax.dot_general(wrh_ref[...], ul, NT_DIMS, preferred_element_type=F32)
        lg_out[:, rows] = lg


def _outproj(ohg, omla, x2, mod3, og, g2, wo_bf, wr_hi, wr_lo, seq):
    t, d = x2.shape
    tm = min(OUTPROJ_TM, seq)
    per_b = seq // tm
    full = lambda shape: pl.BlockSpec(shape, lambda i: tuple(0 for _ in shape))
    modspec = lambda col: pl.BlockSpec((1, 1, d), lambda i: (i // per_b, 0, col))
    return pl.pallas_call(
        functools.partial(_outproj_kernel, n_sub=OUTPROJ_SUB),
        out_shape=(
            jax.ShapeDtypeStruct((t, d), F32),
            jax.ShapeDtypeStruct((t, d // 128, 128), BF16),
            jax.ShapeDtypeStruct((N_EXPERTS, t), F32),
        ),
        grid=(t // tm,),
        in_specs=[
            pl.BlockSpec((tm, HG_WIDTH), lambda i: (i, 0)),
            pl.BlockSpec((tm, MLA_WIDTH), lambda i: (i, 0)),
            pl.BlockSpec((tm, d), lambda i: (i, 0)),
            modspec(2),
            modspec(4),
            modspec(3),
            full((1, MLA_WIDTH)),
            full((1, d)),
            full(wo_bf.shape),
            full(wr_hi.shape),
            full(wr_lo.shape),
        ],
        out_specs=(
            pl.BlockSpec((tm, d), lambda i: (i, 0)),
            pl.BlockSpec((tm, d // 128, 128), lambda i: (i, 0, 0)),
            pl.BlockSpec((N_EXPERTS, tm), lambda i: (0, i)),
        ),
        compiler_params=_cparams(("arbitrary",)),
        name="outproj",
    )(ohg, omla, x2, mod3, mod3, mod3, og, g2, wo_bf, wr_hi, wr_lo)


def _route_kernel(lg_ref, bias_ref, tri_ref, ltri_ref, idx_out, w_out, dest_out, info_out,
                  carry, base, *, tt, n_info):
    phase = pl.program_id(0)
    step = pl.program_id(1)

    @pl.when((phase == 0) & (step == 0))
    def _():
        carry[...] = jnp.zeros_like(carry)

    @pl.when((phase == 1) & (step == 0))
    def _():
        cnt = carry[...]
        padded = jnp.floor((cnt + (MOE_BLK - 1)) / MOE_BLK) * MOE_BLK
        hi, lo = _split_bf16(padded)
        ps = jnp.dot(ltri_ref[...], jnp.concatenate([hi, lo], axis=1), preferred_element_type=F32)
        pad_start = ps[:, :128] + ps[:, 128:]
        base[...] = pad_start
        carry[...] = jnp.zeros_like(carry)
        pad_end = pad_start + padded
        reps = n_info // 128
        pe = jnp.concatenate([pad_end] * reps, axis=1)
        starts = (lax.broadcasted_iota(jnp.int32, (N_EXPERTS, n_info), 1) * MOE_BLK).astype(F32)
        blk_e = jnp.sum(jnp.where(pe <= starts, 1.0, 0.0), axis=0, keepdims=True)
        blk_e = jnp.minimum(blk_e, N_EXPERTS - 1.0)
        n_used = pe[N_EXPERTS - 1:N_EXPERTS, :] / MOE_BLK
        diag = (lax.broadcasted_iota(jnp.int32, (N_EXPERTS, n_info), 0)
                == lax.broadcasted_iota(jnp.int32, (N_EXPERTS, n_info), 1))
        end_blk = jnp.sum(jnp.where(diag, pe, 0.0), axis=0, keepdims=True) / MOE_BLK
        pd = jnp.concatenate([padded] * reps, axis=1)
        num_blk = jnp.sum(jnp.where(diag, pd, 0.0), axis=0, keepdims=True) / MOE_BLK
        info_out[...] = jnp.concatenate(
            [blk_e, n_used, end_blk, num_blk, jnp.zeros((4, n_info), F32)],
            axis=0).astype(jnp.int32)

    ne, ng, eg = N_EXPERTS, N_GROUPS, EXPERTS_PER_GROUP
    neg = -jnp.inf
    sc = _sigmoid(lg_ref[...])
    sel = sc + bias_ref[...]
    sel3 = sel.reshape(ng, eg, tt)
    sub = lax.broadcasted_iota(jnp.int32, (ng, eg, tt), 1)
    m1 = jnp.max(sel3, axis=1, keepdims=True)
    first = jnp.min(jnp.where(sel3 == m1, sub, eg), axis=1, keepdims=True)
    m2 = jnp.max(jnp.where(sub == first, neg, sel3), axis=1, keepdims=True)
    gs = (m1 + m2).reshape(ng, tt)

    gio = lax.broadcasted_iota(jnp.int32, (ng, tt), 0)
    gsel = jnp.zeros((ng, tt), F32)
    for _ in range(TOPK_GROUPS):
        m = jnp.max(gs, axis=0, keepdims=True)
        gi = jnp.min(jnp.where(gs == m, gio, ng), axis=0, keepdims=True)
        hit = gio == gi
        gsel = jnp.where(hit, 1.0, gsel)
        gs = jnp.where(hit, neg, gs)
    gfull = jnp.broadcast_to(gsel.reshape(ng, 1, tt), (ng, eg, tt)).reshape(ne, tt)
    selm = jnp.where(gfull > 0.0, sel, neg)

    eio = lax.broadcasted_iota(jnp.int32, (ne, tt), 0)
    chosen = jnp.zeros((ne, tt), F32)
    idxs, ws = [], []
    for _ in range(TOP_K):
        m = jnp.max(selm, axis=0, keepdims=True)
        ei = jnp.min(jnp.where(selm == m, eio, ne), axis=0, keepdims=True)
        hit = eio == ei
        ws.append(jnp.sum(jnp.where(hit, sc, 0.0), axis=0, keepdims=True))
        idxs.append(ei)
        chosen = jnp.where(hit, 1.0, chosen)
        selm = jnp.where(hit, neg, selm)
    wsum = ws[0]
    for k in range(1, TOP_K):
        wsum = wsum + ws[k]

    tot = jnp.sum(chosen, axis=1, keepdims=True)

    @pl.when(phase == 1)
    def _():
        off = base[...][:, :1] + carry[...][:, :1]
        excl = jnp.dot(chosen.astype(BF16), tri_ref[...], preferred_element_type=F32) + off
        for k in range(TOP_K):
            idx_out[k:k + 1, :] = idxs[k]
            w_out[k:k + 1, :] = ws[k] / wsum * ROUTED_SCALE
            dk = jnp.sum(jnp.where(eio == idxs[k], excl, 0.0), axis=0, keepdims=True)
            dest_out[k:k + 1, :] = dk.astype(jnp.int32)

    carry[...] = carry[...] + tot


def _route(logits_t, bias_col, n_blk):
    ne, t = logits_t.shape
    tt = min(ROUTE_TT, t)
    n_info = -(-n_blk // 128) * 128
    tri = jnp.asarray(np.triu(np.ones((tt, tt), np.float32), 1), BF16)
    ltri = jnp.asarray(np.tril(np.ones((ne, ne), np.float32), -1), BF16)
    return pl.pallas_call(
        functools.partial(_route_kernel, tt=tt, n_info=n_info),
        out_shape=(
            jax.ShapeDtypeStruct((TOP_K, t), jnp.int32),
            jax.ShapeDtypeStruct((TOP_K, t), F32),
            jax.ShapeDtypeStruct((TOP_K, t), jnp.int32),
            jax.ShapeDtypeStruct((8, n_info), jnp.int32),
        ),
        grid=(2, t // tt),
        in_specs=[
            pl.BlockSpec((ne, tt), lambda p, i: (0, i)),
            pl.BlockSpec((ne, 1), lambda p, i: (0, 0)),
            pl.BlockSpec((tt, tt), lambda p, i: (0, 0)),
            pl.BlockSpec((ne, ne), lambda p, i: (0, 0)),
        ],
        out_specs=(
            pl.BlockSpec((TOP_K, tt), lambda p, i: (0, i * p)),
            pl.BlockSpec((TOP_K, tt), lambda p, i: (0, i * p)),
            pl.BlockSpec((TOP_K, tt), lambda p, i: (0, i * p)),
            pl.BlockSpec((8, n_info), lambda p, i: (0, 0)),
        ),
        scratch_shapes=[pltpu.VMEM((ne, 128), F32), pltpu.VMEM((ne, 128), F32)],
        compiler_params=_cparams(("arbitrary", "arbitrary")),
        name="route",
    )(logits_t, bias_col, tri, ltri)


def _dispatch_kernel(info_ref, dest_ref, u_hbm, xs_hbm, stage, zbuf, lsem, ssem, zsem, *, tm, n_blk):
    i = pl.program_id(0)
    n = pl.num_programs(0)
    slot = i % 2
    n_used = info_ref[1, 0]

    def load(tile, dst_slot):
        return pltpu.make_async_copy(u_hbm.at[pl.ds(tile * tm, tm)], stage.at[dst_slot],
                                     lsem.at[dst_slot])

    def zero_block(blk):
        return pltpu.make_async_copy(zbuf, xs_hbm.at[pl.ds(blk * MOE_BLK, MOE_BLK)], zsem)

    @pl.when(i == 0)
    def _():
        load(0, 0).start()
        zbuf[...] = jnp.zeros_like(zbuf)
        for e in range(N_EXPERTS):
            @pl.when(info_ref[3, e] > 0)
            def _():
                zero_block(info_ref[2, e] - 1).start()

        def zero_unused(j, carry):
            zero_block(j).start()
            return carry
        lax.fori_loop(n_used, n_blk, zero_unused, 0)
        for e in range(N_EXPERTS):
            @pl.when(info_ref[3, e] > 0)
            def _():
                zero_block(0).wait()

        def wait_unused(j, carry):
            zero_block(0).wait()
            return carry
        lax.fori_loop(n_used, n_blk, wait_unused, 0)

    def scatter_wait(src_slot):
        pltpu.make_async_copy(stage.at[src_slot], xs_hbm.at[pl.ds(0, tm)], ssem.at[src_slot]).wait()

    @pl.when(i >= 1)
    def _():
        for _ in range(TOP_K):
            scatter_wait(1 - slot)

    @pl.when(i + 1 < n)
    def _():
        load(i + 1, 1 - slot).start()

    load(i, slot).wait()
    for k in range(TOP_K):
        for r in range(tm):
            pltpu.make_async_copy(stage.at[slot, r], xs_hbm.at[dest_ref[0, k, r]],
                                  ssem.at[slot]).start(priority=r % 2)

    @pl.when(i == n - 1)
    def _():
        for _ in range(TOP_K):
            scatter_wait(slot)


def _dispatch(info, dest3, u3, n_blk):
    t, nt, _ = u3.shape
    tm = dest3.shape[2]
    return pl.pallas_call(
        functools.partial(_dispatch_kernel, tm=tm, n_blk=n_blk),
        out_shape=jax.ShapeDtypeStruct((n_blk * MOE_BLK, nt, 128), BF16),
        grid_spec=pltpu.PrefetchScalarGridSpec(
            num_scalar_prefetch=1,
            grid=(t // tm,),
            in_specs=[
                pl.BlockSpec((1, TOP_K, tm), lambda i, info: (i, 0, 0), memory_space=pltpu.SMEM),
                pl.BlockSpec(memory_space=pl.ANY),
            ],
            out_specs=pl.BlockSpec(memory_space=pl.ANY),
            scratch_shapes=[
                pltpu.VMEM((2, tm, nt, 128), BF16),
                pltpu.VMEM((MOE_BLK, nt, 128), BF16),
                pltpu.SemaphoreType.DMA((2,)),
                pltpu.SemaphoreType.DMA((2,)),
                pltpu.SemaphoreType.DMA(()),
            ],
        ),
        compiler_params=_cparams(("arbitrary",)),
        name="moe_dispatch",
    )(info, dest3, u3)


def _gmm_kernel(info_ref, xs_ref, wg_hbm, wu_hbm, wd_hbm, y_ref,
                wg_f, wu_f, wd_f, wsem, wg_s, wu_s, wd_s, nswap):
    i = pl.program_id(0)
    n_used = info_ref[1, 0]
    e = info_ref[0, i]
    prev = info_ref[0, jnp.maximum(i - 1, 0)]
    fresh = (i == 0) | (e != prev)

    def fetch(expert, slot):
        return (pltpu.make_async_copy(wg_hbm.at[expert], wg_f.at[slot], wsem.at[slot]),
                pltpu.make_async_copy(wu_hbm.at[expert], wu_f.at[slot], wsem.at[slot]),
                pltpu.make_async_copy(wd_hbm.at[expert], wd_f.at[slot], wsem.at[slot]))

    @pl.when(i == 0)
    def _():
        nswap[0] = 0
        for c in fetch(e, 0):
            c.start()

    @pl.when(fresh & (i < n_used))
    def _():
        slot = nswap[0] % 2
        nswap[0] = nswap[0] + 1
        for c in fetch(e, slot):
            c.wait()
        nxt = info_ref[2, e]

        @pl.when(nxt < n_used)
        def _():
            for c in fetch(info_ref[0, nxt], 1 - slot):
                c.start(priority=1)

        wg_s[...] = wg_f[slot].astype(BF16)
        wu_s[...] = wu_f[slot].astype(BF16)
        wd_s[...] = wd_f[slot].astype(BF16)

    @pl.when(i < n_used)
    def _():
        xb = _tiles_to_rows(xs_ref[...])
        a = jnp.dot(xb, wg_s[...], preferred_element_type=F32)
        u = jnp.dot(xb, wu_s[...], preferred_element_type=F32)
        hid = (a * _sigmoid(a) * u).astype(BF16)
        y = jnp.dot(hid, wd_s[...], preferred_element_type=F32)
        y_ref[...] = _rows_to_tiles(y.astype(BF16))

    @pl.when(i >= n_used)
    def _():
        y_ref[...] = jnp.zeros_like(y_ref)


def _gmm(info, xs3, w_gate, w_up, w_down):
    p, nt, _ = xs3.shape
    d = nt * 128
    n_blk = p // MOE_BLK
    de = w_gate.shape[-1]
    return pl.pallas_call(
        _gmm_kernel,
        out_shape=jax.ShapeDtypeStruct((p, nt, 128), BF16),
        grid_spec=pltpu.PrefetchScalarGridSpec(
            num_scalar_prefetch=1,
            grid=(n_blk,),
            in_specs=[
                pl.BlockSpec((MOE_BLK, nt, 128),
                             lambda i, info: (jnp.minimum(i, info[1, 0] - 1), 0, 0)),
                pl.BlockSpec(memory_space=pl.ANY),
                pl.BlockSpec(memory_space=pl.ANY),
                pl.BlockSpec(memory_space=pl.ANY),
            ],
            out_specs=pl.BlockSpec((MOE_BLK, nt, 128), lambda i, info: (i, 0, 0)),
            scratch_shapes=[
                pltpu.VMEM((2, d, de), F32),
                pltpu.VMEM((2, d, de), F32),
                pltpu.VMEM((2, de, d), F32),
                pltpu.SemaphoreType.DMA((2,)),
                pltpu.VMEM((d, de), BF16),
                pltpu.VMEM((d, de), BF16),
                pltpu.VMEM((de, d), BF16),
                pltpu.SMEM((1,), jnp.int32),
            ],
        ),
        compiler_params=_cparams(("arbitrary",)),
        name="moe_gmm",
    )(info, xs3, w_gate, w_up, w_down)


def _final_kernel(cur_ref, nxt_ref, u_ref, h_ref, gt_ref, wt_ref, wsg_ref, wsu_ref, wsd_ref, y_hbm,
                  o_ref, ybuf_a, ybuf_b, sem, *, tm):
    i = pl.program_id(0)
    n = pl.num_programs(0)
    n_rows = TOP_K * tm

    def issue(idx_ref, tile, buf, sem_i):
        for k in range(TOP_K):
            for r in range(tm):
                pltpu.make_async_copy(y_hbm.at[idx_ref[0, tile * TOP_K + k, r]], buf.at[k * tm + r],
                                      sem.at[sem_i]).start(priority=r % 2)

    def gather_wait(buf, sem_i):
        pltpu.make_async_copy(y_hbm.at[pl.ds(0, n_rows)], buf, sem.at[sem_i]).wait()

    eye = (lax.broadcasted_iota(jnp.int32, (tm, tm), 0)
           == lax.broadcasted_iota(jnp.int32, (tm, tm), 1))

    def compute(tile, buf):
        rows = slice(tile * tm, (tile + 1) * tm)
        ub = _tiles_to_rows(u_ref[rows])
        a = jnp.dot(ub, wsg_ref[...], preferred_element_type=F32)
        b = jnp.dot(ub, wsu_ref[...], preferred_element_type=F32)
        hid = (a * _sigmoid(a) * b).astype(BF16)
        acc = jnp.dot(hid, wsd_ref[...], preferred_element_type=F32)
        wt = wt_ref[rows, :]
        wcat = jnp.concatenate([jnp.where(eye, wt[:, k:k + 1], 0.0) for k in range(TOP_K)],
                               axis=1)
        whi, wlo = _split_bf16(wcat)
        ycat = _tiles_to_rows(buf[...])
        acc = acc + jnp.dot(whi, ycat, preferred_element_type=F32)
        acc = acc + jnp.dot(wlo, ycat, preferred_element_type=F32)
        o_ref[rows, :] = h_ref[rows, :] + gt_ref[0] * acc

    @pl.when(i == 0)
    def _():
        issue(cur_ref, 0, ybuf_a, 0)

    gather_wait(ybuf_a, 0)
    issue(cur_ref, 1, ybuf_b, 1)
    compute(0, ybuf_a)
    gather_wait(ybuf_b, 1)
    issue(nxt_ref, 0, ybuf_a, 0)
    compute(1, ybuf_b)

    @pl.when(i == n - 1)
    def _():
        gather_wait(ybuf_a, 0)


def _final(dest3, u2, h1, mod3, wt, wsg, wsu, wsd, y, seq):
    t, d = h1.shape
    tm = dest3.shape[2]
    n = t // (2 * tm)
    per_b = seq // (2 * tm)
    dest_pairs = dest3.reshape(n, 2 * TOP_K, tm)
    full = lambda shape: pl.BlockSpec(shape, lambda i: tuple(0 for _ in shape))
    return pl.pallas_call(
        functools.partial(_final_kernel, tm=tm),
        out_shape=jax.ShapeDtypeStruct((t, d), F32),
        grid=(n,),
        in_specs=[
            pl.BlockSpec((1, 2 * TOP_K, tm), lambda i: (i, 0, 0), memory_space=pltpu.SMEM),
            pl.BlockSpec((1, 2 * TOP_K, tm), lambda i: (jnp.minimum(i + 1, n - 1), 0, 0),
                         memory_space=pltpu.SMEM),
            pl.BlockSpec((2 * tm, d // 128, 128), lambda i: (i, 0, 0)),
            pl.BlockSpec((2 * tm, d), lambda i: (i, 0)),
            pl.BlockSpec((1, 1, d), lambda i: (i // per_b, 0, 5)),
            pl.BlockSpec((2 * tm, TOP_K), lambda i: (i, 0)),
            full(wsg.shape),
            full(wsu.shape),
            full(wsd.shape),
            pl.BlockSpec(memory_space=pl.ANY),
        ],
        out_specs=pl.BlockSpec((2 * tm, d), lambda i: (i, 0)),
        scratch_shapes=[
            pltpu.VMEM((TOP_K * tm, d // 128, 128), BF16),
            pltpu.VMEM((TOP_K * tm, d // 128, 128), BF16),
            pltpu.SemaphoreType.DMA((2,)),
        ],
        compiler_params=_cparams(("arbitrary",)),
        name="final",
    )(dest_pairs, dest_pairs, u2, h1, mod3, wt, wsg, wsu, wsd, y)


def _layer(h2, mod3, positions, lb, p, batch, seq):
    t, d = h2.shape

    w_in_t = p["w_in"].T
    proj, cq, ckv, kr = _inproj(h2, p["norm_mix_g"][None, :], mod3,
                                w_in_t[:COL_CQ].astype(BF16), w_in_t[COL_CQ:].astype(BF16), seq)

    o_hg = _hgrn(proj, lb[None, :], p["hg_out_g"][None, :], batch, seq)

    wuq = p["mla_w_uq"].reshape(MLA_Q_RANK, MLA_HEADS, MLA_QK)
    wuq_p = jnp.pad(wuq, ((0, 0), (0, 0), (0, MLA_QK_PAD - MLA_QK))).reshape(
        MLA_Q_RANK, MLA_HEADS * MLA_QK_PAD).astype(BF16)
    wukv = p["mla_w_ukv"].astype(BF16)
    gq = jnp.pad(p["mla_q_norm_g"], (0, MLA_QK_PAD - MLA_QK))[None, :]
    gkn = p["mla_k_norm_g"][None, :MLA_NOPE]
    gkr = jnp.pad(p["mla_k_norm_g"][MLA_NOPE:], (0, 128 - MLA_ROPE))[None, :]
    half = MLA_ROPE // 2
    inv_freq = ROPE_THETA ** (-jnp.arange(half, dtype=F32) / half)
    zeros64 = jnp.zeros((128 - MLA_ROPE,), F32)
    invf = jnp.concatenate([inv_freq, inv_freq, zeros64])[None, :]
    sgn = jnp.concatenate([-jnp.ones((half,), F32), jnp.ones((half,), F32), zeros64])[None, :]
    cm = jnp.concatenate([jnp.ones((MLA_ROPE,), F32), zeros64])[None, :]
    q, k, v = _mla_prep(cq, ckv, kr, positions.reshape(t, 1), p["mla_q_a_g"][None, :],
                        p["mla_kv_a_g"][None, :], wuq_p, wukv, gq, gkn, gkr, invf, sgn, cm,
                        batch, seq)
    o_mla = _attention(q, k, v)

    wr_t = p["w_router"].T
    wr_hi = wr_t.astype(BF16)
    wr_lo = (wr_t - wr_hi.astype(F32)).astype(BF16)
    h1, u2, logits_t = _outproj(o_hg, o_mla, h2, mod3, p["mla_out_g"][None, :],
                                p["norm_ffn_g"][None, :], p["w_out"].astype(BF16),
                                wr_hi, wr_lo, seq)

    n_blk = (t * TOP_K) // MOE_BLK + N_EXPERTS
    _, top_w, dest, info = _route(logits_t, p["router_bias"][:, None], n_blk)

    tm = min(MOE_TM, seq)
    dest3 = dest.reshape(TOP_K, t // tm, tm).transpose(1, 0, 2)
    xs = _dispatch(info, dest3, u2, n_blk)
    y = _gmm(info, xs, p["w_gate"], p["w_up"], p["w_down"])
    return _final(dest3, u2, h1, mod3, top_w.T, p["ws_gate"].astype(BF16),
                  p["ws_up"].astype(BF16), p["ws_down"].astype(BF16), y, seq)


def kernel(x, c, positions, w_ada, b_ada, norm_mix_g, norm_ffn_g, w_in, hg_lb_logits, hg_out_g,
           mla_q_a_g, mla_w_uq, mla_kv_a_g, mla_w_ukv, mla_q_norm_g, mla_k_norm_g, mla_out_g,
           w_out, w_router, router_bias, w_gate, w_up, w_down, ws_gate, ws_up, ws_down):
    batch, seq, d = x.shape
    depth = w_ada.shape[0]
    lb_all = jnp.cumsum(jax.nn.softmax(hg_lb_logits.astype(F32), axis=0), axis=0)
    c8 = jnp.pad(c, ((0, 8 - batch), (0, 0)))
    h2 = x.reshape(batch * seq, d)
    for l in range(depth):
        mod = _adaln(c8, w_ada[l], b_ada[l][None, :])[:batch]
        mod3 = mod.reshape(batch, 1, 6 * d)
        p = dict(norm_mix_g=norm_mix_g[l], norm_ffn_g=norm_ffn_g[l], w_in=w_in[l],
                 hg_out_g=hg_out_g[l], mla_q_a_g=mla_q_a_g[l], mla_w_uq=mla_w_uq[l],
                 mla_kv_a_g=mla_kv_a_g[l], mla_w_ukv=mla_w_ukv[l], mla_q_norm_g=mla_q_norm_g[l],
                 mla_k_norm_g=mla_k_norm_g[l], mla_out_g=mla_out_g[l], w_out=w_out[l],
                 w_router=w_router[l], router_bias=router_bias[l], w_gate=w_gate[l],
                 w_up=w_up[l], w_down=w_down[l], ws_gate=ws_gate[l], ws_up=ws_up[l],
                 ws_down=ws_down[l])
        h2 = _layer(h2, mod3, positions, lb_all[l], p, batch, seq)
    return h2.reshape(batch, seq, d)
```

```python
import functools

import numpy as np
import jax
import jax.numpy as jnp
from jax import lax
from jax.experimental import pallas as pl
from jax.experimental.pallas import tpu as pltpu

F32 = jnp.float32
BF16 = jnp.bfloat16

D_MODEL = 2048
EPS = 1e-6

HG_HEADS = 8
HG_DK = 128
HG_DV = 128
HG_WIDTH = HG_HEADS * HG_DV
HG_CHUNK = 128
HG_ROW_LEVEL = 2
HG_HEADS_PER_STEP = 2
HG_UNROLL = 16

MLA_HEADS = 8
MLA_Q_RANK = 512
MLA_KV_RANK = 256
MLA_NOPE = 128
MLA_ROPE = 64
MLA_QK = MLA_NOPE + MLA_ROPE
MLA_QK_PAD = 256
MLA_V = 128
MLA_WIDTH = MLA_HEADS * MLA_V
ROPE_THETA = 10000.0

COL_CQ = 4096

N_EXPERTS = 64
TOP_K = 8
N_GROUPS = 8
TOPK_GROUPS = 4
EXPERTS_PER_GROUP = N_EXPERTS // N_GROUPS
D_EXPERT = 512
ROUTED_SCALE = 2.5
MOE_BLK = 256

VMEM_LIMIT = 56 * 1024 * 1024

ADALN_TN = 1024
INPROJ_TM = 1024
INPROJ_TN = 1024
MLA_PREP_TM = 256
ATTN_TQ = 256
OUTPROJ_TM = 512
OUTPROJ_SUB = 2
ROUTE_TT = 512
DISPATCH_TM = 256
COMBINE_TM = 128

NT_DIMS = (((1,), (1,)), ((), ()))
TN_DIMS = (((0,), (0,)), ((), ()))


def _cparams(sem):
    return pltpu.CompilerParams(dimension_semantics=sem, vmem_limit_bytes=VMEM_LIMIT)


def _sigmoid(x):
    return 1.0 / (1.0 + jnp.exp(-x))


def _split_bf16(x):
    hi = x.astype(BF16)
    lo = (x - hi.astype(F32)).astype(BF16)
    return hi, lo


def _rows_to_tiles(x):
    n = x.shape[1] // 128
    return jnp.swapaxes(jnp.stack([x[:, s * 128:(s + 1) * 128] for s in range(n)], axis=0), 0, 1)


def _tiles_to_rows(x3):
    xt = jnp.swapaxes(x3, 0, 1)
    return jnp.concatenate([xt[s] for s in range(x3.shape[1])], axis=1)


def _adaln_kernel(c_ref, w_ref, b_ref, o_ref):
    c = c_ref[...]
    cond = c * _sigmoid(c)
    hi, lo = _split_bf16(cond)
    lhs = jnp.concatenate([hi, lo], axis=0)
    r = jnp.dot(lhs, w_ref[...].astype(BF16), preferred_element_type=F32)
    o_ref[...] = r[:8] + r[8:] + b_ref[...]


def _adaln(c8, w, b):
    d, n = w.shape
    tn = ADALN_TN
    return pl.pallas_call(
        _adaln_kernel,
        out_shape=jax.ShapeDtypeStruct((8, n), F32),
        grid=(n // tn,),
        in_specs=[
            pl.BlockSpec((8, d), lambda j: (0, 0)),
            pl.BlockSpec((d, tn), lambda j: (0, j)),
            pl.BlockSpec((1, tn), lambda j: (0, j)),
        ],
        out_specs=pl.BlockSpec((8, tn), lambda j: (0, j)),
        compiler_params=_cparams(("arbitrary",)),
        name="adaln",
    )(c8, w, b)


def _inproj_kernel(x_ref, g_ref, sc_ref, sh_ref, whg_ref, wml_ref, hg_out, cq_out, ckv_out, kr_out,
                   u_scr, *, tm, n_hg):
    j = pl.program_id(1)

    @pl.when(j == 0)
    def _():
        g = g_ref[...]
        sc = 1.0 + sc_ref[0]
        sh = sh_ref[0]

        def body(r, carry):
            rows = pl.ds(pl.multiple_of(r * 128, 128), 128)
            x = x_ref[rows, :]
            ms = jnp.mean(x * x, axis=-1, keepdims=True)
            u = x * lax.rsqrt(ms + EPS) * g * sc + sh
            u_scr[rows, :] = u.astype(BF16)
            return carry

        lax.fori_loop(0, tm // 128, body, 0)

    @pl.when(j < n_hg)
    def _():
        hg_out[...] = lax.dot_general(u_scr[...], whg_ref[...], NT_DIMS,
                                      preferred_element_type=F32)

    @pl.when(j == n_hg)
    def _():
        r = lax.dot_general(u_scr[...], wml_ref[...], NT_DIMS, preferred_element_type=F32)
        cq_out[...] = r[:, :MLA_Q_RANK]
        ckv_out[...] = r[:, MLA_Q_RANK:MLA_Q_RANK + MLA_KV_RANK]
        kr_out[...] = r[:, MLA_Q_RANK + MLA_KV_RANK:]


def _inproj(x2, g, mod3, w_hg_t, w_ml_t, seq):
    t, d = x2.shape
    tm = min(INPROJ_TM, seq)
    tn = INPROJ_TN
    n_hg = w_hg_t.shape[0] // tn
    per_b = seq // tm
    last = lambda j: jnp.minimum(j, n_hg - 1)
    return pl.pallas_call(
        functools.partial(_inproj_kernel, tm=tm, n_hg=n_hg),
        out_shape=(
            jax.ShapeDtypeStruct((t, w_hg_t.shape[0]), F32),
            jax.ShapeDtypeStruct((t, MLA_Q_RANK), F32),
            jax.ShapeDtypeStruct((t, MLA_KV_RANK), F32),
            jax.ShapeDtypeStruct((t, MLA_ROPE), F32),
        ),
        grid=(t // tm, n_hg + 1),
        in_specs=[
            pl.BlockSpec((tm, d), lambda i, j: (i, 0)),
            pl.BlockSpec((1, d), lambda i, j: (0, 0)),
            pl.BlockSpec((1, 1, d), lambda i, j: (i // per_b, 0, 1)),
            pl.BlockSpec((1, 1, d), lambda i, j: (i // per_b, 0, 0)),
            pl.BlockSpec((tn, d), lambda i, j: (last(j), 0)),
            pl.BlockSpec(w_ml_t.shape, lambda i, j: (0, 0)),
        ],
        out_specs=(
            pl.BlockSpec((tm, tn), lambda i, j: (i, last(j))),
            pl.BlockSpec((tm, MLA_Q_RANK), lambda i, j: (i, 0)),
            pl.BlockSpec((tm, MLA_KV_RANK), lambda i, j: (i, 0)),
            pl.BlockSpec((tm, MLA_ROPE), lambda i, j: (i, 0)),
        ),
        scratch_shapes=[pltpu.VMEM((tm, d), BF16)],
        compiler_params=_cparams(("arbitrary", "arbitrary")),
        name="inproj",
    )(x2, g, mod3, mod3, w_hg_t, w_ml_t)


def _hgrn_consts():
    c = HG_CHUNK
    t = np.arange(c)[:, None]
    j = np.arange(c)[None, :]
    tril = (j <= t).astype(np.float32)
    mats = [tril]
    masks = [np.eye(c, dtype=np.float32)]
    s = c // 2
    while s >= 1:
        ref = (t // (2 * s)) * (2 * s) + s - 1
        if s < HG_ROW_LEVEL:
            mats.append(tril - (j <= ref).astype(np.float32))
        masks.append((((t // s) % 2 == 1) & ((j // s) == (t // s) - 1)).astype(np.float32))
        s //= 2
    w = np.concatenate(mats, 0)
    return np.concatenate([w, w], 1), np.stack(masks, 0)


def _hgrn_head(q, v, g, kk, xs, og, m_ref, st_ref, o_ref):
    c = HG_CHUNK
    n_lvl = m_ref.shape[0]
    b = xs[0:c]
    bl = b[c - 1:c, :] - b
    st = st_ref[...]
    vb = v.astype(BF16)
    o = lax.dot_general((q * jnp.exp(b)).astype(BF16), st.astype(BF16), NT_DIMS,
                        preferred_element_type=F32)
    attn = m_ref[0] * lax.dot_general(q.astype(BF16), kk.astype(BF16), NT_DIMS,
                                      preferred_element_type=F32)
    s = c // 2
    fine = 1
    for l in range(1, n_lvl):
        if s >= HG_ROW_LEVEL:
            b3 = b.reshape(c // (2 * s), 2 * s, HG_DK)
            x = (b3 - b3[:, s - 1:s, :]).reshape(c, HG_DK)
        else:
            x = xs[fine * c:(fine + 1) * c]
            fine += 1
        s //= 2
        e = jnp.exp(-jnp.abs(x))
        attn = attn + m_ref[l] * lax.dot_general(
            (q * e).astype(BF16), (kk * e).astype(BF16), NT_DIMS, preferred_element_type=F32)
    o = o + jnp.dot(attn.astype(BF16), vb, preferred_element_type=F32)
    khat = (kk * jnp.exp(bl)).astype(BF16)
    st_ref[...] = st * jnp.exp(b[c - 1:c, :]) + lax.dot_general(
        vb, khat, TN_DIMS, preferred_element_type=F32)
    ms = jnp.mean(o * o, axis=-1, keepdims=True)
    y = o * lax.rsqrt(ms + EPS) * og * (g * _sigmoid(g))
    o_ref[...] = y.astype(BF16)


def _hgrn_kernel(q_ref, f_ref, i_ref, g_ref, lb_ref, og_ref, w_ref, m_ref, o_ref, st_scr, *,
                 seq, nh):
    c = HG_CHUNK
    st_scr[...] = jnp.zeros_like(st_scr)
    lb = lb_ref[...]
    og = og_ref[...]

    def chunk(ci, carry):
        rows = pl.ds(pl.multiple_of(ci * c, c), c)
        f = lb + (1.0 - lb) * _sigmoid(f_ref[rows, :])
        hi, lo = _split_bf16(jnp.log(f))
        xs = jnp.dot(w_ref[...], jnp.concatenate([hi, lo], axis=0), preferred_element_type=F32)
        for hh in range(nh):
            lanes = slice(hh * HG_DK, (hh + 1) * HG_DK)
            _hgrn_head(q_ref[rows, lanes], i_ref[rows, lanes], g_ref[rows, lanes],
                       1.0 - f[:, lanes], xs[:, lanes], og, m_ref, st_scr.at[hh],
                       o_ref.at[rows, lanes])
        return carry

    lax.fori_loop(0, seq // c, chunk, 0, unroll=HG_UNROLL // nh)


def _hgrn(proj, lb, og, batch, seq):
    t = proj.shape[0]
    wm, mm = _hgrn_consts()
    wm = jnp.asarray(wm, BF16)
    mm = jnp.asarray(mm, F32)
    nh = HG_HEADS_PER_STEP
    hg = HG_HEADS // nh
    wd = nh * HG_DK
    return pl.pallas_call(
        functools.partial(_hgrn_kernel, seq=seq, nh=nh),
        out_shape=jax.ShapeDtypeStruct((t, HG_WIDTH), BF16),
        grid=(batch, hg),
        in_specs=[
            pl.BlockSpec((seq, wd), lambda b, i: (b, i)),
            pl.BlockSpec((seq, wd), lambda b, i: (b, hg + i)),
            pl.BlockSpec((seq, wd), lambda b, i: (b, 2 * hg + i)),
            pl.BlockSpec((seq, wd), lambda b, i: (b, 3 * hg + i)),
            pl.BlockSpec((1, wd), lambda b, i: (0, i)),
            pl.BlockSpec((1, HG_DV), lambda b, i: (0, 0)),
            pl.BlockSpec(wm.shape, lambda b, i: (0, 0)),
            pl.BlockSpec(mm.shape, lambda b, i: (0, 0, 0)),
        ],
        out_specs=pl.BlockSpec((seq, wd), lambda b, i: (b, i)),
        scratch_shapes=[pltpu.VMEM((nh, HG_DV, HG_DK), F32)],
        compiler_params=_cparams(("arbitrary", "arbitrary")),
        name="hgrn2",
    )(proj, proj, proj, proj, lb, og, wm, mm)


def _mla_prep_kernel(cq_ref, ckv_ref, kr_ref, pos_ref, qag_ref, kvag_ref, wuq_ref, wukv_ref,
                     gq_ref, gkn_ref, gkr_ref, invf_ref, sgn_ref, cm_ref,
                     q_out, k_out, v_out):
    cq = cq_ref[...]
    cqn = cq * lax.rsqrt(jnp.mean(cq * cq, axis=-1, keepdims=True) + EPS) * qag_ref[...]
    qf = jnp.dot(cqn.astype(BF16), wuq_ref[...], preferred_element_type=F32)
    ckv = ckv_ref[...]
    ckvn = ckv * lax.rsqrt(jnp.mean(ckv * ckv, axis=-1, keepdims=True) + EPS) * kvag_ref[...]
    kvf = jnp.dot(ckvn.astype(BF16), wukv_ref[...], preferred_element_type=F32)

    ang = pos_ref[...].astype(F32) * invf_ref[...]
    cos_t = jnp.cos(ang) * cm_ref[...]
    sin_t = jnp.sin(ang) * sgn_ref[...]

    def rope(r):
        return r * cos_t + (pltpu.roll(r, 96, 1) + pltpu.roll(r, 32, 1)) * sin_t

    kr64 = kr_ref[...]
    kr = jnp.concatenate([kr64, jnp.zeros_like(kr64)], axis=-1)
    kr_ss = jnp.sum(kr * kr, axis=-1, keepdims=True)
    krr = rope(kr * gkr_ref[...])
    gq = gq_ref[...]
    gkn = gkn_ref[...]
    scale = MLA_QK ** -0.5
    for h in range(MLA_HEADS):
        qh = qf[:, h * MLA_QK_PAD:(h + 1) * MLA_QK_PAD]
        inv = lax.rsqrt(jnp.sum(qh * qh, axis=-1, keepdims=True) / MLA_QK + EPS)
        qn = qh * inv * gq
        qo = jnp.concatenate([qn[:, :MLA_NOPE], rope(qn[:, MLA_NOPE:])], axis=-1) * scale
        q_out[0, h] = qo.astype(BF16)
        kn = kvf[:, h * 256:h * 256 + MLA_NOPE]
        inv = lax.rsqrt((jnp.sum(kn * kn, axis=-1, keepdims=True) + kr_ss) / MLA_QK + EPS)
        k_out[0, h] = jnp.concatenate([kn * inv * gkn, krr * inv], axis=-1).astype(BF16)
        v_out[0, h] = kvf[:, h * 256 + MLA_NOPE:(h + 1) * 256].astype(BF16)


def _mla_prep(cq, ckv, kr, pos2, qag, kvag, wuq_p, wukv, gq, gkn, gkr, invf, sgn, cm, batch, seq):
    tm = min(MLA_PREP_TM, seq)
    per_b = seq // tm
    hh = MLA_HEADS
    full = lambda shape: pl.BlockSpec(shape, lambda b, i: tuple(0 for _ in shape))
    return pl.pallas_call(
        _mla_prep_kernel,
        out_shape=(
            jax.ShapeDtypeStruct((batch, hh, seq, MLA_QK_PAD), BF16),
            jax.ShapeDtypeStruct((batch, hh, seq, MLA_QK_PAD), BF16),
            jax.ShapeDtypeStruct((batch, hh, seq, MLA_V), BF16),
        ),
        grid=(batch, per_b),
        in_specs=[
            pl.BlockSpec((tm, MLA_Q_RANK), lambda b, i: (b * per_b + i, 0)),
            pl.BlockSpec((tm, MLA_KV_RANK), lambda b, i: (b * per_b + i, 0)),
            pl.BlockSpec((tm, MLA_ROPE), lambda b, i: (b * per_b + i, 0)),
            pl.BlockSpec((tm, 1), lambda b, i: (b * per_b + i, 0)),
            full((1, MLA_Q_RANK)),
            full((1, MLA_KV_RANK)),
            full(wuq_p.shape),
            full(wukv.shape),
            full((1, MLA_QK_PAD)),
            full((1, 128)),
            full((1, 128)),
            full((1, 128)),
            full((1, 128)),
            full((1, 128)),
        ],
        out_specs=(
            pl.BlockSpec((1, hh, tm, MLA_QK_PAD), lambda b, i: (b, 0, i, 0)),
            pl.BlockSpec((1, hh, tm, MLA_QK_PAD), lambda b, i: (b, 0, i, 0)),
            pl.BlockSpec((1, hh, tm, MLA_V), lambda b, i: (b, 0, i, 0)),
        ),
        compiler_params=_cparams(("arbitrary", "arbitrary")),
        name="mla_prep",
    )(cq, ckv, kr, pos2, qag, kvag, wuq_p, wukv, gq, gkn, gkr, invf, sgn, cm)


def _attn_kernel(q_ref, k_ref, v_ref, o_ref, *, tq, seq):
    row = lax.broadcasted_iota(jnp.int32, (tq, tq), 0)
    col = lax.broadcasted_iota(jnp.int32, (tq, tq), 1)
    causal = col <= row
    for i in range(seq // tq):
        lo = i * tq
        q = q_ref[0, 0, lo:lo + tq, :]
        sd = lax.dot_general(q, k_ref[0, 0, lo:lo + tq, :], NT_DIMS, preferred_element_type=F32)
        sd = jnp.where(causal, sd, -jnp.inf)
        m = jnp.max(sd, axis=-1, keepdims=True)
        if i > 0:
            so = lax.dot_general(q, k_ref[0, 0, :lo, :], NT_DIMS, preferred_element_type=F32)
            m = jnp.maximum(m, jnp.max(so, axis=-1, keepdims=True))
        pd = jnp.exp(sd - m)
        l = jnp.sum(pd, axis=-1, keepdims=True)
        o = jnp.dot(pd.astype(BF16), v_ref[0, 0, lo:lo + tq, :], preferred_element_type=F32)
        if i > 0:
            po = jnp.exp(so - m)
            l = l + jnp.sum(po, axis=-1, keepdims=True)
            o = o + jnp.dot(po.astype(BF16), v_ref[0, 0, :lo, :], preferred_element_type=F32)
        o_ref[lo:lo + tq, :] = o / l


def _attention(q, k, v):
    batch, hh, seq, dq = q.shape
    tq = min(ATTN_TQ, seq)
    return pl.pallas_call(
        functools.partial(_attn_kernel, tq=tq, seq=seq),
        out_shape=jax.ShapeDtypeStruct((batch * seq, hh * MLA_V), F32),
        grid=(batch, hh),
        in_specs=[
            pl.BlockSpec((1, 1, seq, dq), lambda b, h: (b, h, 0, 0)),
            pl.BlockSpec((1, 1, seq, dq), lambda b, h: (b, h, 0, 0)),
            pl.BlockSpec((1, 1, seq, MLA_V), lambda b, h: (b, h, 0, 0)),
        ],
        out_specs=pl.BlockSpec((seq, MLA_V), lambda b, h: (b, h)),
        compiler_params=_cparams(("arbitrary", "arbitrary")),
        name="mla_attn",
    )(q, k, v)


def _outproj_kernel(ohg_ref, omla_ref, x_ref, gt_ref, sc_ref, sh_ref, og_ref, g2_ref,
                    wo_ref, wrh_ref, wrl_ref, h_out, u_out, lg_out, *, n_sub):
    sub = x_ref.shape[0] // n_sub
    for j in range(n_sub):
        rows = slice(j * sub, (j + 1) * sub)
        om = omla_ref[rows, :]
        omn = om * lax.rsqrt(jnp.mean(om * om, axis=-1, keepdims=True) + EPS) * og_ref[...]
        y = jnp.dot(ohg_ref[rows, :], wo_ref[:HG_WIDTH, :], preferred_element_type=F32)
        y = y + jnp.dot(omn.astype(BF16), wo_ref[HG_WIDTH:, :], preferred_element_type=F32)
        h1 = x_ref[rows, :] + gt_ref[0] * y
        h_out[rows, :] = h1
        u = h1 * lax.rsqrt(jnp.mean(h1 * h1, axis=-1, keepdims=True) + EPS) * g2_ref[...]
        u = u * (1.0 + sc_ref[0]) + sh_ref[0]
        uh, ul = _split_bf16(u)
        u_out[rows] = _rows_to_tiles(uh)
        lg = lax.dot_general(wrh_ref[...], uh, NT_DIMS, preferred_element_type=F32)
        lg = lg + lax.dot_general(wrl_ref[...], uh, NT_DIMS, preferred_element_type=F32)
        lg = lg + lax.dot_general(wrh_ref[...], ul, NT_DIMS, preferred_element_type=F32)
        lg_out[:, rows] = lg


def _outproj(ohg, omla, x2, mod3, og, g2, wo_bf, wr_hi, wr_lo, seq):
    t, d = x2.shape
    tm = min(OUTPROJ_TM, seq)
    per_b = seq // tm
    full = lambda shape: pl.BlockSpec(shape, lambda i: tuple(0 for _ in shape))
    modspec = lambda col: pl.BlockSpec((1, 1, d), lambda i: (i // per_b, 0, col))
    return pl.pallas_call(
        functools.partial(_outproj_kernel, n_sub=OUTPROJ_SUB),
        out_shape=(
            jax.ShapeDtypeStruct((t, d), F32),
            jax.ShapeDtypeStruct((t, d // 128, 128), BF16),
            jax.ShapeDtypeStruct((N_EXPERTS, t), F32),
        ),
        grid=(t // tm,),
        in_specs=[
            pl.BlockSpec((tm, HG_WIDTH), lambda i: (i, 0)),
            pl.BlockSpec((tm, MLA_WIDTH), lambda i: (i, 0)),
            pl.BlockSpec((tm, d), lambda i: (i, 0)),
            modspec(2),
            modspec(4),
            modspec(3),
            full((1, MLA_WIDTH)),
            full((1, d)),
            full(wo_bf.shape),
            full(wr_hi.shape),
            full(wr_lo.shape),
        ],
        out_specs=(
            pl.BlockSpec((tm, d), lambda i: (i, 0)),
            pl.BlockSpec((tm, d // 128, 128), lambda i: (i, 0, 0)),
            pl.BlockSpec((N_EXPERTS, tm), lambda i: (0, i)),
        ),
        compiler_params=_cparams(("arbitrary",)),
        name="outproj",
    )(ohg, omla, x2, mod3, mod3, mod3, og, g2, wo_bf, wr_hi, wr_lo)


def _route_kernel(lg_ref, bias_ref, tri_ref, ltri_ref, idx_out, w_out, dest_out, info_out,
                  carry, base, *, tt, n_info):
    phase = pl.program_id(0)
    step = pl.program_id(1)

    @pl.when((phase == 0) & (step == 0))
    def _():
        carry[...] = jnp.zeros_like(carry)

    @pl.when((phase == 1) & (step == 0))
    def _():
        cnt = carry[...]
        padded = jnp.floor((cnt + (MOE_BLK - 1)) / MOE_BLK) * MOE_BLK
        hi, lo = _split_bf16(padded)
        ps = jnp.dot(ltri_ref[...], jnp.concatenate([hi, lo], axis=1), preferred_element_type=F32)
        pad_start = ps[:, :128] + ps[:, 128:]
        base[...] = pad_start
        carry[...] = jnp.zeros_like(carry)
        pad_end = pad_start + padded
        reps = n_info // 128
        pe = jnp.concatenate([pad_end] * reps, axis=1)
        starts = (lax.broadcasted_iota(jnp.int32, (N_EXPERTS, n_info), 1) * MOE_BLK).astype(F32)
        blk_e = jnp.sum(jnp.where(pe <= starts, 1.0, 0.0), axis=0, keepdims=True)
        blk_e = jnp.minimum(blk_e, N_EXPERTS - 1.0)
        n_used = pe[N_EXPERTS - 1:N_EXPERTS, :] / MOE_BLK
        diag = (lax.broadcasted_iota(jnp.int32, (N_EXPERTS, n_info), 0)
                == lax.broadcasted_iota(jnp.int32, (N_EXPERTS, n_info), 1))
        end_blk = jnp.sum(jnp.where(diag, pe, 0.0), axis=0, keepdims=True) / MOE_BLK
        pd = jnp.concatenate([padded] * reps, axis=1)
        num_blk = jnp.sum(jnp.where(diag, pd, 0.0), axis=0, keepdims=True) / MOE_BLK
        info_out[...] = jnp.concatenate(
            [blk_e, n_used, end_blk, num_blk, jnp.zeros((4, n_info), F32)],
            axis=0).astype(jnp.int32)

    ne, ng, eg = N_EXPERTS, N_GROUPS, EXPERTS_PER_GROUP
    neg = -jnp.inf
    sc = _sigmoid(lg_ref[...])
    sel = sc + bias_ref[...]
    sel3 = sel.reshape(ng, eg, tt)
    sub = lax.broadcasted_iota(jnp.int32, (ng, eg, tt), 1)
    m1 = jnp.max(sel3, axis=1, keepdims=True)
    first = jnp.min(jnp.where(sel3 == m1, sub, eg), axis=1, keepdims=True)
    m2 = jnp.max(jnp.where(sub == first, neg, sel3), axis=1, keepdims=True)
    gs = (m1 + m2).reshape(ng, tt)

    gio = lax.broadcasted_iota(jnp.int32, (ng, tt), 0)
    gsel = jnp.zeros((ng, tt), F32)
    for _ in range(TOPK_GROUPS):
        m = jnp.max(gs, axis=0, keepdims=True)
        gi = jnp.min(jnp.where(gs == m, gio, ng), axis=0, keepdims=True)
        hit = gio == gi
        gsel = jnp.where(hit, 1.0, gsel)
        gs = jnp.where(hit, neg, gs)
    gfull = jnp.broadcast_to(gsel.reshape(ng, 1, tt), (ng, eg, tt)).reshape(ne, tt)
    selm = jnp.where(gfull > 0.0, sel, neg)

    eio = lax.broadcasted_iota(jnp.int32, (ne, tt), 0)
    chosen = jnp.zeros((ne, tt), F32)
    idxs, ws = [], []
    for _ in range(TOP_K):
        m = jnp.max(selm, axis=0, keepdims=True)
        ei = jnp.min(jnp.where(selm == m, eio, ne), axis=0, keepdims=True)
        hit = eio == ei
        ws.append(jnp.sum(jnp.where(hit, sc, 0.0), axis=0, keepdims=True))
        idxs.append(ei)
        chosen = jnp.where(hit, 1.0, chosen)
        selm = jnp.where(hit, neg, selm)
    wsum = ws[0]
    for k in range(1, TOP_K):
        wsum = wsum + ws[k]

    tot = jnp.sum(chosen, axis=1, keepdims=True)

    @pl.when(phase == 1)
    def _():
        off = base[...][:, :1] + carry[...][:, :1]
        excl = jnp.dot(chosen.astype(BF16), tri_ref[...], preferred_element_type=F32) + off
        for k in range(TOP_K):
            idx_out[k:k + 1, :] = idxs[k]
            w_out[k:k + 1, :] = ws[k] / wsum * ROUTED_SCALE
            dk = jnp.sum(jnp.where(eio == idxs[k], excl, 0.0), axis=0, keepdims=True)
            dest_out[k:k + 1, :] = dk.astype(jnp.int32)

    carry[...] = carry[...] + tot


def _route(logits_t, bias_col, n_blk):
    ne, t = logits_t.shape
    tt = min(ROUTE_TT, t)
    n_info = -(-n_blk // 128) * 128
    tri = jnp.asarray(np.triu(np.ones((tt, tt), np.float32), 1), BF16)
    ltri = jnp.asarray(np.tril(np.ones((ne, ne), np.float32), -1), BF16)
    return pl.pallas_call(
        functools.partial(_route_kernel, tt=tt, n_info=n_info),
        out_shape=(
            jax.ShapeDtypeStruct((TOP_K, t), jnp.int32),
            jax.ShapeDtypeStruct((TOP_K, t), F32),
            jax.ShapeDtypeStruct((TOP_K, t), jnp.int32),
            jax.ShapeDtypeStruct((8, n_info), jnp.int32),
        ),
        grid=(2, t // tt),
        in_specs=[
            pl.BlockSpec((ne, tt), lambda p, i: (0, i)),
            pl.BlockSpec((ne, 1), lambda p, i: (0, 0)),
            pl.BlockSpec((tt, tt), lambda p, i: (0, 0)),
            pl.BlockSpec((ne, ne), lambda p, i: (0, 0)),
        ],
        out_specs=(
            pl.BlockSpec((TOP_K, tt), lambda p, i: (0, i * p)),
            pl.BlockSpec((TOP_K, tt), lambda p, i: (0, i * p)),
            pl.BlockSpec((TOP_K, tt), lambda p, i: (0, i * p)),
            pl.BlockSpec((8, n_info), lambda p, i: (0, 0)),
        ),
        scratch_shapes=[pltpu.VMEM((ne, 128), F32), pltpu.VMEM((ne, 128), F32)],
        compiler_params=_cparams(("arbitrary", "arbitrary")),
        name="route",
    )(logits_t, bias_col, tri, ltri)


def _dispatch_kernel(info_ref, dest_ref, u_hbm, xs_hbm, stage, zbuf, lsem, ssem, zsem, *, tm, n_blk):
    i = pl.program_id(0)
    n = pl.num_programs(0)
    slot = i % 2
    n_used = info_ref[1, 0]

    def load(tile, dst_slot):
        return pltpu.make_async_copy(u_hbm.at[pl.ds(tile * tm, tm)], stage.at[dst_slot],
                                     lsem.at[dst_slot])

    def zero_block(blk):
        return pltpu.make_async_copy(zbuf, xs_hbm.at[pl.ds(blk * MOE_BLK, MOE_BLK)], zsem)

    @pl.when(i == 0)
    def _():
        load(0, 0).start()
        zbuf[...] = jnp.zeros_like(zbuf)
        for e in range(N_EXPERTS):
            @pl.when(info_ref[3, e] > 0)
            def _():
                zero_block(info_ref[2, e] - 1).start()

        def zero_unused(j, carry):
            zero_block(j).start()
            return carry
        lax.fori_loop(n_used, n_blk, zero_unused, 0)
        for e in range(N_EXPERTS):
            @pl.when(info_ref[3, e] > 0)
            def _():
                zero_block(0).wait()

        def wait_unused(j, carry):
            zero_block(0).wait()
            return carry
        lax.fori_loop(n_used, n_blk, wait_unused, 0)

    def scatter_wait(src_slot):
        pltpu.make_async_copy(stage.at[src_slot], xs_hbm.at[pl.ds(0, tm)], ssem.at[src_slot]).wait()

    @pl.when(i >= 1)
    def _():
        for _ in range(TOP_K):
            scatter_wait(1 - slot)

    @pl.when(i + 1 < n)
    def _():
        load(i + 1, 1 - slot).start()

    load(i, slot).wait()
    for k in range(TOP_K):
        for r in range(tm):
            pltpu.make_async_copy(stage.at[slot, r], xs_hbm.at[dest_ref[0, k, r]],
                                  ssem.at[slot]).start(priority=r % 2)

    @pl.when(i == n - 1)
    def _():
        for _ in range(TOP_K):
            scatter_wait(slot)


def _dispatch(info, dest3, u3, n_blk):
    t, nt, _ = u3.shape
    tm = dest3.shape[2]
    return pl.pallas_call(
        functools.partial(_dispatch_kernel, tm=tm, n_blk=n_blk),
        out_shape=jax.ShapeDtypeStruct((n_blk * MOE_BLK, nt, 128), BF16),
        grid_spec=pltpu.PrefetchScalarGridSpec(
            num_scalar_prefetch=1,
            grid=(t // tm,),
            in_specs=[
                pl.BlockSpec((1, TOP_K, tm), lambda i, info: (i, 0, 0), memory_space=pltpu.SMEM),
                pl.BlockSpec(memory_space=pl.ANY),
            ],
            out_specs=pl.BlockSpec(memory_space=pl.ANY),
            scratch_shapes=[
                pltpu.VMEM((2, tm, nt, 128), BF16),
                pltpu.VMEM((MOE_BLK, nt, 128), BF16),
                pltpu.SemaphoreType.DMA((2,)),
                pltpu.SemaphoreType.DMA((2,)),
                pltpu.SemaphoreType.DMA(()),
            ],
        ),
        compiler_params=_cparams(("arbitrary",)),
        name="moe_dispatch",
    )(info, dest3, u3)


def _gmm_kernel(info_ref, xs_ref, wg_hbm, wu_hbm, wd_hbm, y_ref,
                wg_f, wu_f, wd_f, wsem, wg_s, wu_s, wd_s, nswap):
    i = pl.program_id(0)
    n_used = info_ref[1, 0]
    e = info_ref[0, i]
    prev = info_ref[0, jnp.maximum(i - 1, 0)]
    fresh = (i == 0) | (e != prev)

    def fetch(expert, slot):
        return (pltpu.make_async_copy(wg_hbm.at[expert], wg_f.at[slot], wsem.at[slot]),
                pltpu.make_async_copy(wu_hbm.at[expert], wu_f.at[slot], wsem.at[slot]),
                pltpu.make_async_copy(wd_hbm.at[expert], wd_f.at[slot], wsem.at[slot]))

    @pl.when(i == 0)
    def _():
        nswap[0] = 0
        for c in fetch(e, 0):
            c.start()

    @pl.when(fresh & (i < n_used))
    def _():
        slot = nswap[0] % 2
        nswap[0] = nswap[0] + 1
        for c in fetch(e, slot):
            c.wait()
        nxt = info_ref[2, e]

        @pl.when(nxt < n_used)
        def _():
            for c in fetch(info_ref[0, nxt], 1 - slot):
                c.start(priority=1)

        wg_s[...] = wg_f[slot].astype(BF16)
        wu_s[...] = wu_f[slot].astype(BF16)
        wd_s[...] = wd_f[slot].astype(BF16)

    @pl.when(i < n_used)
    def _():
        xb = _tiles_to_rows(xs_ref[...])
        a = jnp.dot(xb, wg_s[...], preferred_element_type=F32)
        u = jnp.dot(xb, wu_s[...], preferred_element_type=F32)
        hid = (a * _sigmoid(a) * u).astype(BF16)
        y = jnp.dot(hid, wd_s[...], preferred_element_type=F32)
        y_ref[...] = _rows_to_tiles(y.astype(BF16))

    @pl.when(i >= n_used)
    def _():
        y_ref[...] = jnp.zeros_like(y_ref)


def _gmm(info, xs3, w_gate, w_up, w_down):
    p, nt, _ = xs3.shape
    d = nt * 128
    n_blk = p // MOE_BLK
    de = w_gate.shape[-1]
    return pl.pallas_call(
        _gmm_kernel,
        out_shape=jax.ShapeDtypeStruct((p, nt, 128), BF16),
        grid_spec=pltpu.PrefetchScalarGridSpec(
            num_scalar_prefetch=1,
            grid=(n_blk,),
            in_specs=[
                pl.BlockSpec((MOE_BLK, nt, 128),
                             lambda i, info: (jnp.minimum(i, info[1, 0] - 1), 0, 0)),
                pl.BlockSpec(memory_space=pl.ANY),
                pl.BlockSpec(memory_space=pl.ANY),
                pl.BlockSpec(memory_space=pl.ANY),
            ],
            out_specs=pl.BlockSpec((MOE_BLK, nt, 128), lambda i, info: (i, 0, 0)),
            scratch_shapes=[
                pltpu.VMEM((2, d, de), F32),
                pltpu.VMEM((2, d, de), F32),
                pltpu.VMEM((2, de, d), F32),
                pltpu.SemaphoreType.DMA((2,)),
                pltpu.VMEM((d, de), BF16),
                pltpu.VMEM((d, de), BF16),
                pltpu.VMEM((de, d), BF16),
                pltpu.SMEM((1,), jnp.int32),
            ],
        ),
        compiler_params=_cparams(("arbitrary",)),
        name="moe_gmm",
    )(info, xs3, w_gate, w_up, w_down)


def _final_kernel(cur_ref, nxt_ref, u_ref, h_ref, gt_ref, wt_ref, wsg_ref, wsu_ref, wsd_ref, y_hbm,
                  o_ref, ybuf_a, ybuf_b, sem, *, tm):
    i = pl.program_id(0)
    n = pl.num_programs(0)
    n_rows = TOP_K * tm

    def issue(idx_ref, tile, buf, sem_i):
        for k in range(TOP_K):
            for r in range(tm):
                pltpu.make_async_copy(y_hbm.at[idx_ref[0, tile * TOP_K + k, r]], buf.at[k * tm + r],
                                      sem.at[sem_i]).start(priority=r % 2)

    def gather_wait(buf, sem_i):
        pltpu.make_async_copy(y_hbm.at[pl.ds(0, n_rows)], buf, sem.at[sem_i]).wait()

    eye = (lax.broadcasted_iota(jnp.int32, (tm, tm), 0)
           == lax.broadcasted_iota(jnp.int32, (tm, tm), 1))

    def compute(tile, buf):
        rows = slice(tile * tm, (tile + 1) * tm)
        ub = _tiles_to_rows(u_ref[rows])
        a = jnp.dot(ub, wsg_ref[...], preferred_element_type=F32)
        b = jnp.dot(ub, wsu_ref[...], preferred_element_type=F32)
        hid = (a * _sigmoid(a) * b).astype(BF16)
        acc = jnp.dot(hid, wsd_ref[...], preferred_element_type=F32)
        wt = wt_ref[rows, :]
        wcat = jnp.concatenate([jnp.where(eye, wt[:, k:k + 1], 0.0) for k in range(TOP_K)],
                               axis=1)
        whi, wlo = _split_bf16(wcat)
        ycat = _tiles_to_rows(buf[...])
        acc = acc + jnp.dot(whi, ycat, preferred_element_type=F32)
        acc = acc + jnp.dot(wlo, ycat, preferred_element_type=F32)
        o_ref[rows, :] = h_ref[rows, :] + gt_ref[0] * acc

    @pl.when(i == 0)
    def _():
        issue(cur_ref, 0, ybuf_a, 0)

    gather_wait(ybuf_a, 0)
    issue(cur_ref, 1, ybuf_b, 1)
    compute(0, ybuf_a)
    gather_wait(ybuf_b, 1)
    issue(nxt_ref, 0, ybuf_a, 0)
    compute(1, ybuf_b)

    @pl.when(i == n - 1)
    def _():
        gather_wait(ybuf_a, 0)


def _final(dest3, u2, h1, mod3, wt, wsg, wsu, wsd, y, seq):
    t, d = h1.shape
    tm = dest3.shape[2]
    n = t // (2 * tm)
    per_b = seq // (2 * tm)
    dest_pairs = dest3.reshape(n, 2 * TOP_K, tm)
    full = lambda shape: pl.BlockSpec(shape, lambda i: tuple(0 for _ in shape))
    return pl.pallas_call(
        functools.partial(_final_kernel, tm=tm),
        out_shape=jax.ShapeDtypeStruct((t, d), F32),
        grid=(n,),
        in_specs=[
            pl.BlockSpec((1, 2 * TOP_K, tm), lambda i: (i, 0, 0), memory_space=pltpu.SMEM),
            pl.BlockSpec((1, 2 * TOP_K, tm), lambda i: (jnp.minimum(i + 1, n - 1), 0, 0),
                         memory_space=pltpu.SMEM),
            pl.BlockSpec((2 * tm, d // 128, 128), lambda i: (i, 0, 0)),
            pl.BlockSpec((2 * tm, d), lambda i: (i, 0)),
            pl.BlockSpec((1, 1, d), lambda i: (i // per_b, 0, 5)),
            pl.BlockSpec((2 * tm, TOP_K), lambda i: (i, 0)),
            full(wsg.shape),
            full(wsu.shape),
            full(wsd.shape),
            pl.BlockSpec(memory_space=pl.ANY),
        ],
        out_specs=pl.BlockSpec((2 * tm, d), lambda i: (i, 0)),
        scratch_shapes=[
            pltpu.VMEM((TOP_K * tm, d // 128, 128), BF16),
            pltpu.VMEM((TOP_K * tm, d // 128, 128), BF16),
            pltpu.SemaphoreType.DMA((2,)),
        ],
        compiler_params=_cparams(("arbitrary",)),
        name="final",
    )(dest_pairs, dest_pairs, u2, h1, mod3, wt, wsg, wsu, wsd, y)


def _layer(h2, mod3, positions, lb, p, batch, seq):
    t, d = h2.shape

    w_in_t = p["w_in"].T
    proj, cq, ckv, kr = _inproj(h2, p["norm_mix_g"][None, :], mod3,
                                w_in_t[:COL_CQ].astype(BF16), w_in_t[COL_CQ:].astype(BF16), seq)

    o_hg = _hgrn(proj, lb[None, :], p["hg_out_g"][None, :], batch, seq)

    wuq = p["mla_w_uq"].reshape(MLA_Q_RANK, MLA_HEADS, MLA_QK)
    wuq_p = jnp.pad(wuq, ((0, 0), (0, 0), (0, MLA_QK_PAD - MLA_QK))).reshape(
        MLA_Q_RANK, MLA_HEADS * MLA_QK_PAD).astype(BF16)
    wukv = p["mla_w_ukv"].astype(BF16)
    gq = jnp.pad(p["mla_q_norm_g"], (0, MLA_QK_PAD - MLA_QK))[None, :]
    gkn = p["mla_k_norm_g"][None, :MLA_NOPE]
    gkr = jnp.pad(p["mla_k_norm_g"][MLA_NOPE:], (0, 128 - MLA_ROPE))[None, :]
    half = MLA_ROPE // 2
    inv_freq = ROPE_THETA ** (-jnp.arange(half, dtype=F32) / half)
    zeros64 = jnp.zeros((128 - MLA_ROPE,), F32)
    invf = jnp.concatenate([inv_freq, inv_freq, zeros64])[None, :]
    sgn = jnp.concatenate([-jnp.ones((half,), F32), jnp.ones((half,), F32), zeros64])[None, :]
    cm = jnp.concatenate([jnp.ones((MLA_ROPE,), F32), zeros64])[None, :]
    q, k, v = _mla_prep(cq, ckv, kr, positions.reshape(t, 1), p["mla_q_a_g"][None, :],
                        p["mla_kv_a_g"][None, :], wuq_p, wukv, gq, gkn, gkr, invf, sgn, cm,
                        batch, seq)
    o_mla = _attention(q, k, v)

    wr_t = p["w_router"].T
    wr_hi = wr_t.astype(BF16)
    wr_lo = (wr_t - wr_hi.astype(F32)).astype(BF16)
    h1, u2, logits_t = _outproj(o_hg, o_mla, h2, mod3, p["mla_out_g"][None, :],
                                p["norm_ffn_g"][None, :], p["w_out"].astype(BF16),
                                wr_hi, wr_lo, seq)

    n_blk = (t * TOP_K) // MOE_BLK + N_EXPERTS
    _, top_w, dest, info = _route(logits_t, p["router_bias"][:, None], n_blk)

    def tiled(tm):
        tm = min(tm, seq)
        return dest.reshape(TOP_K, t // tm, tm).transpose(1, 0, 2)

    xs = _dispatch(info, tiled(DISPATCH_TM), u2, n_blk)
    y = _gmm(info, xs, p["w_gate"], p["w_up"], p["w_down"])
    return _final(tiled(COMBINE_TM), u2, h1, mod3, top_w.T, p["ws_gate"].astype(BF16),
                  p["ws_up"].astype(BF16), p["ws_down"].astype(BF16), y, seq)


def kernel(x, c, positions, w_ada, b_ada, norm_mix_g, norm_ffn_g, w_in, hg_lb_logits, hg_out_g,
           mla_q_a_g, mla_w_uq, mla_kv_a_g, mla_w_ukv, mla_q_norm_g, mla_k_norm_g, mla_out_g,
           w_out, w_router, router_bias, w_gate, w_up, w_down, ws_gate, ws_up, ws_down):
    batch, seq, d = x.shape
    depth = w_ada.shape[0]
    lb_all = jnp.cumsum(jax.nn.softmax(hg_lb_logits.astype(F32), axis=0), axis=0)
    c8 = jnp.pad(c, ((0, 8 - batch), (0, 0)))
    h2 = x.reshape(batch * seq, d)
    for l in range(depth):
        mod = _adaln(c8, w_ada[l], b_ada[l][None, :])[:batch]
        mod3 = mod.reshape(batch, 1, 6 * d)
        p = dict(norm_mix_g=norm_mix_g[l], norm_ffn_g=norm_ffn_g[l], w_in=w_in[l],
                 hg_out_g=hg_out_g[l], mla_q_a_g=mla_q_a_g[l], mla_w_uq=mla_w_uq[l],
                 mla_kv_a_g=mla_kv_a_g[l], mla_w_ukv=mla_w_ukv[l], mla_q_norm_g=mla_q_norm_g[l],
                 mla_k_norm_g=mla_k_norm_g[l], mla_out_g=mla_out_g[l], w_out=w_out[l],
                 w_router=w_router[l], router_bias=router_bias[l], w_gate=w_gate[l],
                 w_up=w_up[l], w_down=w_down[l], ws_gate=ws_gate[l], ws_up=ws_up[l],
                 ws_down=ws_down[l])
        h2 = _layer(h2, mod3, positions, lb_all[l], p, batch, seq)
    return h2.reshape(batch, seq, d)
```

```python
import functools

import numpy as np
import jax
import jax.numpy as jnp
from jax import lax
from jax.experimental import pallas as pl
from jax.experimental.pallas import tpu as pltpu

F32 = jnp.float32
BF16 = jnp.bfloat16

D_MODEL = 2048
EPS = 1e-6

HG_HEADS = 8
HG_DK = 128
HG_DV = 128
HG_WIDTH = HG_HEADS * HG_DV
HG_CHUNK = 128
HG_ROW_LEVEL = 2
HG_HEADS_PER_STEP = 2
HG_UNROLL = 16

MLA_HEADS = 8
MLA_Q_RANK = 512
MLA_KV_RANK = 256
MLA_NOPE = 128
MLA_ROPE = 64
MLA_QK = MLA_NOPE + MLA_ROPE
MLA_QK_PAD = 256
MLA_V = 128
MLA_WIDTH = MLA_HEADS * MLA_V
ROPE_THETA = 10000.0

COL_CQ = 4096

N_EXPERTS = 64
TOP_K = 8
N_GROUPS = 8
TOPK_GROUPS = 4
EXPERTS_PER_GROUP = N_EXPERTS // N_GROUPS
D_EXPERT = 512
ROUTED_SCALE = 2.5
MOE_BLK = 256

VMEM_LIMIT = 56 * 1024 * 1024

ADALN_TN = 1024
INPROJ_TM = 1024
INPROJ_TN = 1024
MLA_PREP_TM = 256
ATTN_TQ = 256
OUTPROJ_TM = 512
OUTPROJ_SUB = 2
ROUTE_TT = 512
DISPATCH_TM = 256
COMBINE_TM = 128

NT_DIMS = (((1,), (1,)), ((), ()))
TN_DIMS = (((0,), (0,)), ((), ()))


def _cparams(sem):
    return pltpu.CompilerParams(dimension_semantics=sem, vmem_limit_bytes=VMEM_LIMIT)


def _sigmoid(x):
    return 1.0 / (1.0 + jnp.exp(-x))


def _split_bf16(x):
    hi = x.astype(BF16)
    lo = (x - hi.astype(F32)).astype(BF16)
    return hi, lo


def _rows_to_tiles(x):
    n = x.shape[1] // 128
    return jnp.swapaxes(jnp.stack([x[:, s * 128:(s + 1) * 128] for s in range(n)], axis=0), 0, 1)


def _tiles_to_rows(x3):
    xt = jnp.swapaxes(x3, 0, 1)
    return jnp.concatenate([xt[s] for s in range(x3.shape[1])], axis=1)


def _adaln_kernel(c_ref, w_ref, b_ref, o_ref):
    c = c_ref[...]
    cond = c * _sigmoid(c)
    hi, lo = _split_bf16(cond)
    lhs = jnp.concatenate([hi, lo], axis=0)
    r = jnp.dot(lhs, w_ref[...].astype(BF16), preferred_element_type=F32)
    o_ref[...] = r[:8] + r[8:] + b_ref[...]


def _adaln(c8, w, b):
    d, n = w.shape
    tn = ADALN_TN
    return pl.pallas_call(
        _adaln_kernel,
        out_shape=jax.ShapeDtypeStruct((8, n), F32),
        grid=(n // tn,),
        in_specs=[
            pl.BlockSpec((8, d), lambda j: (0, 0)),
            pl.BlockSpec((d, tn), lambda j: (0, j)),
            pl.BlockSpec((1, tn), lambda j: (0, j)),
        ],
        out_specs=pl.BlockSpec((8, tn), lambda j: (0, j)),
        compiler_params=_cparams(("arbitrary",)),
        name="adaln",
    )(c8, w, b)


def _inproj_kernel(x_ref, g_ref, sc_ref, sh_ref, whg_ref, wml_ref, hg_out, cq_out, ckv_out, kr_out,
                   u_scr, *, tm, n_hg):
    j = pl.program_id(1)

    @pl.when(j == 0)
    def _():
        g = g_ref[...]
        sc = 1.0 + sc_ref[0]
        sh = sh_ref[0]

        def body(r, carry):
            rows = pl.ds(pl.multiple_of(r * 128, 128), 128)
            x = x_ref[rows, :]
            ms = jnp.mean(x * x, axis=-1, keepdims=True)
            u = x * lax.rsqrt(ms + EPS) * g * sc + sh
            u_scr[rows, :] = u.astype(BF16)
            return carry

        lax.fori_loop(0, tm // 128, body, 0)

    @pl.when(j < n_hg)
    def _():
        hg_out[...] = lax.dot_general(u_scr[...], whg_ref[...], NT_DIMS,
                                      preferred_element_type=F32)

    @pl.when(j == n_hg)
    def _():
        r = lax.dot_general(u_scr[...], wml_ref[...], NT_DIMS, preferred_element_type=F32)
        cq_out[...] = r[:, :MLA_Q_RANK]
        ckv_out[...] = r[:, MLA_Q_RANK:MLA_Q_RANK + MLA_KV_RANK]
        kr_out[...] = r[:, MLA_Q_RANK + MLA_KV_RANK:]


def _inproj(x2, g, mod3, w_hg_t, w_ml_t, seq):
    t, d = x2.shape
    tm = min(INPROJ_TM, seq)
    tn = INPROJ_TN
    n_hg = w_hg_t.shape[0] // tn
    per_b = seq // tm
    last = lambda j: jnp.minimum(j, n_hg - 1)
    return pl.pallas_call(
        functools.partial(_inproj_kernel, tm=tm, n_hg=n_hg),
        out_shape=(
            jax.ShapeDtypeStruct((t, w_hg_t.shape[0]), F32),
            jax.ShapeDtypeStruct((t, MLA_Q_RANK), F32),
            jax.ShapeDtypeStruct((t, MLA_KV_RANK), F32),
            jax.ShapeDtypeStruct((t, MLA_ROPE), F32),
        ),
        grid=(t // tm, n_hg + 1),
        in_specs=[
            pl.BlockSpec((tm, d), lambda i, j: (i, 0)),
            pl.BlockSpec((1, d), lambda i, j: (0, 0)),
            pl.BlockSpec((1, 1, d), lambda i, j: (i // per_b, 0, 1)),
            pl.BlockSpec((1, 1, d), lambda i, j: (i // per_b, 0, 0)),
            pl.BlockSpec((tn, d), lambda i, j: (last(j), 0)),
            pl.BlockSpec(w_ml_t.shape, lambda i, j: (0, 0)),
        ],
        out_specs=(
            pl.BlockSpec((tm, tn), lambda i, j: (i, last(j))),
            pl.BlockSpec((tm, MLA_Q_RANK), lambda i, j: (i, 0)),
            pl.BlockSpec((tm, MLA_KV_RANK), lambda i, j: (i, 0)),
            pl.BlockSpec((tm, MLA_ROPE), lambda i, j: (i, 0)),
        ),
        scratch_shapes=[pltpu.VMEM((tm, d), BF16)],
        compiler_params=_cparams(("arbitrary", "arbitrary")),
        name="inproj",
    )(x2, g, mod3, mod3, w_hg_t, w_ml_t)


def _hgrn_consts():
    c = HG_CHUNK
    t = np.arange(c)[:, None]
    j = np.arange(c)[None, :]
    tril = (j <= t).astype(np.float32)
    mats = [tril]
    masks = [np.eye(c, dtype=np.float32)]
    s = c // 2
    while s >= 1:
        ref = (t // (2 * s)) * (2 * s) + s - 1
        if s < HG_ROW_LEVEL:
            mats.append(tril - (j <= ref).astype(np.float32))
        masks.append((((t // s) % 2 == 1) & ((j // s) == (t // s) - 1)).astype(np.float32))
        s //= 2
    w = np.concatenate(mats, 0)
    return np.concatenate([w, w], 1), np.stack(masks, 0)


def _hgrn_head(q, v, g, kk, xs, og, m_ref, st_ref, o_ref):
    c = HG_CHUNK
    n_lvl = m_ref.shape[0]
    b = xs[0:c]
    bl = b[c - 1:c, :] - b
    st = st_ref[...]
    vb = v.astype(BF16)
    o = lax.dot_general((q * jnp.exp(b)).astype(BF16), st.astype(BF16), NT_DIMS,
                        preferred_element_type=F32)
    attn = m_ref[0] * lax.dot_general(q.astype(BF16), kk.astype(BF16), NT_DIMS,
                                      preferred_element_type=F32)
    s = c // 2
    fine = 1
    for l in range(1, n_lvl):
        if s >= HG_ROW_LEVEL:
            b3 = b.reshape(c // (2 * s), 2 * s, HG_DK)
            x = (b3 - b3[:, s - 1:s, :]).reshape(c, HG_DK)
        else:
            x = xs[fine * c:(fine + 1) * c]
            fine += 1
        s //= 2
        e = jnp.exp(-jnp.abs(x))
        attn = attn + m_ref[l] * lax.dot_general(
            (q * e).astype(BF16), (kk * e).astype(BF16), NT_DIMS, preferred_element_type=F32)
    o = o + jnp.dot(attn.astype(BF16), vb, preferred_element_type=F32)
    khat = (kk * jnp.exp(bl)).astype(BF16)
    st_ref[...] = st * jnp.exp(b[c - 1:c, :]) + lax.dot_general(
        vb, khat, TN_DIMS, preferred_element_type=F32)
    ms = jnp.mean(o * o, axis=-1, keepdims=True)
    y = o * lax.rsqrt(ms + EPS) * og * (g * _sigmoid(g))
    o_ref[...] = y.astype(BF16)


def _hgrn_kernel(q_ref, f_ref, i_ref, g_ref, lb_ref, og_ref, w_ref, m_ref, o_ref, st_scr, *,
                 seq, nh):
    c = HG_CHUNK
    st_scr[...] = jnp.zeros_like(st_scr)
    lb = lb_ref[...]
    og = og_ref[...]

    def chunk(ci, carry):
        rows = pl.ds(pl.multiple_of(ci * c, c), c)
        f = lb + (1.0 - lb) * _sigmoid(f_ref[rows, :])
        hi, lo = _split_bf16(jnp.log(f))
        xs = jnp.dot(w_ref[...], jnp.concatenate([hi, lo], axis=0), preferred_element_type=F32)
        for hh in range(nh):
            lanes = slice(hh * HG_DK, (hh + 1) * HG_DK)
            _hgrn_head(q_ref[rows, lanes], i_ref[rows, lanes], g_ref[rows, lanes],
                       1.0 - f[:, lanes], xs[:, lanes], og, m_ref, st_scr.at[hh],
                       o_ref.at[rows, lanes])
        return carry

    lax.fori_loop(0, seq // c, chunk, 0, unroll=HG_UNROLL // nh)


def _hgrn(proj, lb, og, batch, seq):
    t = proj.shape[0]
    wm, mm = _hgrn_consts()
    wm = jnp.asarray(wm, BF16)
    mm = jnp.asarray(mm, F32)
    nh = HG_HEADS_PER_STEP
    hg = HG_HEADS // nh
    wd = nh * HG_DK
    return pl.pallas_call(
        functools.partial(_hgrn_kernel, seq=seq, nh=nh),
        out_shape=jax.ShapeDtypeStruct((t, HG_WIDTH), BF16),
        grid=(batch, hg),
        in_specs=[
            pl.BlockSpec((seq, wd), lambda b, i: (b, i)),
            pl.BlockSpec((seq, wd), lambda b, i: (b, hg + i)),
            pl.BlockSpec((seq, wd), lambda b, i: (b, 2 * hg + i)),
            pl.BlockSpec((seq, wd), lambda b, i: (b, 3 * hg + i)),
            pl.BlockSpec((1, wd), lambda b, i: (0, i)),
            pl.BlockSpec((1, HG_DV), lambda b, i: (0, 0)),
            pl.BlockSpec(wm.shape, lambda b, i: (0, 0)),
            pl.BlockSpec(mm.shape, lambda b, i: (0, 0, 0)),
        ],
        out_specs=pl.BlockSpec((seq, wd), lambda b, i: (b, i)),
        scratch_shapes=[pltpu.VMEM((nh, HG_DV, HG_DK), F32)],
        compiler_params=_cparams(("arbitrary", "arbitrary")),
        name="hgrn2",
    )(proj, proj, proj, proj, lb, og, wm, mm)


def _mla_prep_kernel(cq_ref, ckv_ref, kr_ref, pos_ref, qag_ref, kvag_ref, wuq_ref, wukv_ref,
                     gq_ref, gkn_ref, gkr_ref, invf_ref, sgn_ref, cm_ref,
                     q_out, k_out, v_out):
    cq = cq_ref[...]
    cqn = cq * lax.rsqrt(jnp.mean(cq * cq, axis=-1, keepdims=True) + EPS) * qag_ref[...]
    qf = jnp.dot(cqn.astype(BF16), wuq_ref[...], preferred_element_type=F32)
    ckv = ckv_ref[...]
    ckvn = ckv * lax.rsqrt(jnp.mean(ckv * ckv, axis=-1, keepdims=True) + EPS) * kvag_ref[...]
    kvf = jnp.dot(ckvn.astype(BF16), wukv_ref[...], preferred_element_type=F32)

    ang = pos_ref[...].astype(F32) * invf_ref[...]
    cos_t = jnp.cos(ang) * cm_ref[...]
    sin_t = jnp.sin(ang) * sgn_ref[...]

    def rope(r):
        return r * cos_t + (pltpu.roll(r, 96, 1) + pltpu.roll(r, 32, 1)) * sin_t

    kr64 = kr_ref[...]
    kr = jnp.concatenate([kr64, jnp.zeros_like(kr64)], axis=-1)
    kr_ss = jnp.sum(kr * kr, axis=-1, keepdims=True)
    krr = rope(kr * gkr_ref[...])
    gq = gq_ref[...]
    gkn = gkn_ref[...]
    scale = MLA_QK ** -0.5
    for h in range(MLA_HEADS):
        qh = qf[:, h * MLA_QK_PAD:(h + 1) * MLA_QK_PAD]
        inv = lax.rsqrt(jnp.sum(qh * qh, axis=-1, keepdims=True) / MLA_QK + EPS)
        qn = qh * inv * gq
        qo = jnp.concatenate([qn[:, :MLA_NOPE], rope(qn[:, MLA_NOPE:])], axis=-1) * scale
        q_out[0, h] = qo.astype(BF16)
        kn = kvf[:, h * 256:h * 256 + MLA_NOPE]
        inv = lax.rsqrt((jnp.sum(kn * kn, axis=-1, keepdims=True) + kr_ss) / MLA_QK + EPS)
        k_out[0, h] = jnp.concatenate([kn * inv * gkn, krr * inv], axis=-1).astype(BF16)
        v_out[0, h] = kvf[:, h * 256 + MLA_NOPE:(h + 1) * 256].astype(BF16)


def _mla_prep(cq, ckv, kr, pos2, qag, kvag, wuq_p, wukv, gq, gkn, gkr, invf, sgn, cm, batch, seq):
    tm = min(MLA_PREP_TM, seq)
    per_b = seq // tm
    hh = MLA_HEADS
    full = lambda shape: pl.BlockSpec(shape, lambda b, i: tuple(0 for _ in shape))
    return pl.pallas_call(
        _mla_prep_kernel,
        out_shape=(
            jax.ShapeDtypeStruct((batch, hh, seq, MLA_QK_PAD), BF16),
            jax.ShapeDtypeStruct((batch, hh, seq, MLA_QK_PAD), BF16),
            jax.ShapeDtypeStruct((batch, hh, seq, MLA_V), BF16),
        ),
        grid=(batch, per_b),
        in_specs=[
            pl.BlockSpec((tm, MLA_Q_RANK), lambda b, i: (b * per_b + i, 0)),
            pl.BlockSpec((tm, MLA_KV_RANK), lambda b, i: (b * per_b + i, 0)),
            pl.BlockSpec((tm, MLA_ROPE), lambda b, i: (b * per_b + i, 0)),
            pl.BlockSpec((tm, 1), lambda b, i: (b * per_b + i, 0)),
            full((1, MLA_Q_RANK)),
            full((1, MLA_KV_RANK)),
            full(wuq_p.shape),
            full(wukv.shape),
            full((1, MLA_QK_PAD)),
            full((1, 128)),
            full((1, 128)),
            full((1, 128)),
            full((1, 128)),
            full((1, 128)),
        ],
        out_specs=(
            pl.BlockSpec((1, hh, tm, MLA_QK_PAD), lambda b, i: (b, 0, i, 0)),
            pl.BlockSpec((1, hh, tm, MLA_QK_PAD), lambda b, i: (b, 0, i, 0)),
            pl.BlockSpec((1, hh, tm, MLA_V), lambda b, i: (b, 0, i, 0)),
        ),
        compiler_params=_cparams(("arbitrary", "arbitrary")),
        name="mla_prep",
    )(cq, ckv, kr, pos2, qag, kvag, wuq_p, wukv, gq, gkn, gkr, invf, sgn, cm)


def _attn_kernel(q_ref, k_ref, v_ref, o_ref, *, tq, seq):
    row = lax.broadcasted_iota(jnp.int32, (tq, tq), 0)
    col = lax.broadcasted_iota(jnp.int32, (tq, tq), 1)
    causal = col <= row
    for i in range(seq // tq):
        lo = i * tq
        q = q_ref[0, 0, lo:lo + tq, :]
        sd = lax.dot_general(q, k_ref[0, 0, lo:lo + tq, :], NT_DIMS, preferred_element_type=F32)
        sd = jnp.where(causal, sd, -jnp.inf)
        m = jnp.max(sd, axis=-1, keepdims=True)
        if i > 0:
            so = lax.dot_general(q, k_ref[0, 0, :lo, :], NT_DIMS, preferred_element_type=F32)
            m = jnp.maximum(m, jnp.max(so, axis=-1, keepdims=True))
        pd = jnp.exp(sd - m)
        l = jnp.sum(pd, axis=-1, keepdims=True)
        o = jnp.dot(pd.astype(BF16), v_ref[0, 0, lo:lo + tq, :], preferred_element_type=F32)
        if i > 0:
            po = jnp.exp(so - m)
            l = l + jnp.sum(po, axis=-1, keepdims=True)
            o = o + jnp.dot(po.astype(BF16), v_ref[0, 0, :lo, :], preferred_element_type=F32)
        o_ref[lo:lo + tq, :] = o / l


def _attention(q, k, v):
    batch, hh, seq, dq = q.shape
    tq = min(ATTN_TQ, seq)
    return pl.pallas_call(
        functools.partial(_attn_kernel, tq=tq, seq=seq),
        out_shape=jax.ShapeDtypeStruct((batch * seq, hh * MLA_V), F32),
        grid=(batch, hh),
        in_specs=[
            pl.BlockSpec((1, 1, seq, dq), lambda b, h: (b, h, 0, 0)),
            pl.BlockSpec((1, 1, seq, dq), lambda b, h: (b, h, 0, 0)),
            pl.BlockSpec((1, 1, seq, MLA_V), lambda b, h: (b, h, 0, 0)),
        ],
        out_specs=pl.BlockSpec((seq, MLA_V), lambda b, h: (b, h)),
        compiler_params=_cparams(("arbitrary", "arbitrary")),
        name="mla_attn",
    )(q, k, v)


def _outproj_kernel(ohg_ref, omla_ref, x_ref, gt_ref, sc_ref, sh_ref, og_ref, g2_ref,
                    wo_ref, wrh_ref, wrl_ref, h_out, u_out, lg_out, *, n_sub):
    sub = x_ref.shape[0] // n_sub
    for j in range(n_sub):
        rows = slice(j * sub, (j + 1) * sub)
        om = omla_ref[rows, :]
        omn = om * lax.rsqrt(jnp.mean(om * om, axis=-1, keepdims=True) + EPS) * og_ref[...]
        y = jnp.dot(ohg_ref[rows, :], wo_ref[:HG_WIDTH, :], preferred_element_type=F32)
        y = y + jnp.dot(omn.astype(BF16), wo_ref[HG_WIDTH:, :], preferred_element_type=F32)
        h1 = x_ref[rows, :] + gt_ref[0] * y
        h_out[rows, :] = h1
        u = h1 * lax.rsqrt(jnp.mean(h1 * h1, axis=-1, keepdims=True) + EPS) * g2_ref[...]
        u = u * (1.0 + sc_ref[0]) + sh_ref[0]
        uh, ul = _split_bf16(u)
        u_out[rows] = _rows_to_tiles(uh)
        lg = lax.dot_general(wrh_ref[...], uh, NT_DIMS, preferred_element_type=F32)
        lg = lg + lax.dot_general(wrl_ref[...], uh, NT_DIMS, preferred_element_type=F32)
        lg = lg + lax.dot_general(wrh_ref[...], ul, NT_DIMS, preferred_element_type=F32)
        lg_out[:, rows] = lg


def _outproj(ohg, omla, x2, mod3, og, g2, wo_bf, wr_hi, wr_lo, seq):
    t, d = x2.shape
    tm = min(OUTPROJ_TM, seq)
    per_b = seq // tm
    full = lambda shape: pl.BlockSpec(shape, lambda i: tuple(0 for _ in shape))
    modspec = lambda col: pl.BlockSpec((1, 1, d), lambda i: (i // per_b, 0, col))
    return pl.pallas_call(
        functools.partial(_outproj_kernel, n_sub=OUTPROJ_SUB),
        out_shape=(
            jax.ShapeDtypeStruct((t, d), F32),
            jax.ShapeDtypeStruct((t, d // 128, 128), BF16),
            jax.ShapeDtypeStruct((N_EXPERTS, t), F32),
        ),
        grid=(t // tm,),
        in_specs=[
            pl.BlockSpec((tm, HG_WIDTH), lambda i: (i, 0)),
            pl.BlockSpec((tm, MLA_WIDTH), lambda i: (i, 0)),
            pl.BlockSpec((tm, d), lambda i: (i, 0)),
            modspec(2),
            modspec(4),
            modspec(3),
            full((1, MLA_WIDTH)),
            full((1, d)),
            full(wo_bf.shape),
            full(wr_hi.shape),
            full(wr_lo.shape),
        ],
        out_specs=(
            pl.BlockSpec((tm, d), lambda i: (i, 0)),
            pl.BlockSpec((tm, d // 128, 128), lambda i: (i, 0, 0)),
            pl.BlockSpec((N_EXPERTS, tm), lambda i: (0, i)),
        ),
        compiler_params=_cparams(("arbitrary",)),
        name="outproj",
    )(ohg, omla, x2, mod3, mod3, mod3, og, g2, wo_bf, wr_hi, wr_lo)


def _route_kernel(lg_ref, bias_ref, tri_ref, ltri_ref, idx_out, w_out, dest_out, info_out,
                  carry, base, *, tt, n_info):
    phase = pl.program_id(0)
    step = pl.program_id(1)

    @pl.when((phase == 0) & (step == 0))
    def _():
        carry[...] = jnp.zeros_like(carry)

    @pl.when((phase == 1) & (step == 0))
    def _():
        cnt = carry[...]
        padded = jnp.floor((cnt + (MOE_BLK - 1)) / MOE_BLK) * MOE_BLK
        hi, lo = _split_bf16(padded)
        ps = jnp.dot(ltri_ref[...], jnp.concatenate([hi, lo], axis=1), preferred_element_type=F32)
        pad_start = ps[:, :128] + ps[:, 128:]
        base[...] = pad_start
        carry[...] = jnp.zeros_like(carry)
        pad_end = pad_start + padded
        reps = n_info // 128
        pe = jnp.concatenate([pad_end] * reps, axis=1)
        starts = (lax.broadcasted_iota(jnp.int32, (N_EXPERTS, n_info), 1) * MOE_BLK).astype(F32)
        blk_e = jnp.sum(jnp.where(pe <= starts, 1.0, 0.0), axis=0, keepdims=True)
        blk_e = jnp.minimum(blk_e, N_EXPERTS - 1.0)
        n_used = pe[N_EXPERTS - 1:N_EXPERTS, :] / MOE_BLK
        diag = (lax.broadcasted_iota(jnp.int32, (N_EXPERTS, n_info), 0)
                == lax.broadcasted_iota(jnp.int32, (N_EXPERTS, n_info), 1))
        end_blk = jnp.sum(jnp.where(diag, pe, 0.0), axis=0, keepdims=True) / MOE_BLK
        pd = jnp.concatenate([padded] * reps, axis=1)
        num_blk = jnp.sum(jnp.where(diag, pd, 0.0), axis=0, keepdims=True) / MOE_BLK
        info_out[...] = jnp.concatenate(
            [blk_e, n_used, end_blk, num_blk, jnp.zeros((4, n_info), F32)],
            axis=0).astype(jnp.int32)

    ne, ng, eg = N_EXPERTS, N_GROUPS, EXPERTS_PER_GROUP
    neg = -jnp.inf
    sc = _sigmoid(lg_ref[...])
    sel = sc + bias_ref[...]
    sel3 = sel.reshape(ng, eg, tt)
    sub = lax.broadcasted_iota(jnp.int32, (ng, eg, tt), 1)
    m1 = jnp.max(sel3, axis=1, keepdims=True)
    first = jnp.min(jnp.where(sel3 == m1, sub, eg), axis=1, keepdims=True)
    m2 = jnp.max(jnp.where(sub == first, neg, sel3), axis=1, keepdims=True)
    gs = (m1 + m2).reshape(ng, tt)

    gio = lax.broadcasted_iota(jnp.int32, (ng, tt), 0)
    gsel = jnp.zeros((ng, tt), F32)
    for _ in range(TOPK_GROUPS):
        m = jnp.max(gs, axis=0, keepdims=True)
        gi = jnp.min(jnp.where(gs == m, gio, ng), axis=0, keepdims=True)
        hit = gio == gi
        gsel = jnp.where(hit, 1.0, gsel)
        gs = jnp.where(hit, neg, gs)
    gfull = jnp.broadcast_to(gsel.reshape(ng, 1, tt), (ng, eg, tt)).reshape(ne, tt)
    selm = jnp.where(gfull > 0.0, sel, neg)

    eio = lax.broadcasted_iota(jnp.int32, (ne, tt), 0)
    chosen = jnp.zeros((ne, tt), F32)
    idxs, ws = [], []
    for _ in range(TOP_K):
        m = jnp.max(selm, axis=0, keepdims=True)
        ei = jnp.min(jnp.where(selm == m, eio, ne), axis=0, keepdims=True)
        hit = eio == ei
        ws.append(jnp.sum(jnp.where(hit, sc, 0.0), axis=0, keepdims=True))
        idxs.append(ei)
        chosen = jnp.where(hit, 1.0, chosen)
        selm = jnp.where(hit, neg, selm)
    wsum = ws[0]
    for k in range(1, TOP_K):
        wsum = wsum + ws[k]

    tot = jnp.sum(chosen, axis=1, keepdims=True)

    @pl.when(phase == 1)
    def _():
        off = base[...][:, :1] + carry[...][:, :1]
        excl = jnp.dot(chosen.astype(BF16), tri_ref[...], preferred_element_type=F32) + off
        for k in range(TOP_K):
            idx_out[k:k + 1, :] = idxs[k]
            w_out[k:k + 1, :] = ws[k] / wsum * ROUTED_SCALE
            dk = jnp.sum(jnp.where(eio == idxs[k], excl, 0.0), axis=0, keepdims=True)
            dest_out[k:k + 1, :] = dk.astype(jnp.int32)

    carry[...] = carry[...] + tot


def _route(logits_t, bias_col, n_blk):
    ne, t = logits_t.shape
    tt = min(ROUTE_TT, t)
    n_info = -(-n_blk // 128) * 128
    tri = jnp.asarray(np.triu(np.ones((tt, tt), np.float32), 1), BF16)
    ltri = jnp.asarray(np.tril(np.ones((ne, ne), np.float32), -1), BF16)
    return pl.pallas_call(
        functools.partial(_route_kernel, tt=tt, n_info=n_info),
        out_shape=(
            jax.ShapeDtypeStruct((TOP_K, t), jnp.int32),
            jax.ShapeDtypeStruct((TOP_K, t), F32),
            jax.ShapeDtypeStruct((TOP_K, t), jnp.int32),
            jax.ShapeDtypeStruct((8, n_info), jnp.int32),
        ),
        grid=(2, t // tt),
        in_specs=[
            pl.BlockSpec((ne, tt), lambda p, i: (0, i)),
            pl.BlockSpec((ne, 1), lambda p, i: (0, 0)),
            pl.BlockSpec((tt, tt), lambda p, i: (0, 0)),
            pl.BlockSpec((ne, ne), lambda p, i: (0, 0)),
        ],
        out_specs=(
            pl.BlockSpec((TOP_K, tt), lambda p, i: (0, i * p)),
            pl.BlockSpec((TOP_K, tt), lambda p, i: (0, i * p)),
            pl.BlockSpec((TOP_K, tt), lambda p, i: (0, i * p)),
            pl.BlockSpec((8, n_info), lambda p, i: (0, 0)),
        ),
        scratch_shapes=[pltpu.VMEM((ne, 128), F32), pltpu.VMEM((ne, 128), F32)],
        compiler_params=_cparams(("arbitrary", "arbitrary")),
        name="route",
    )(logits_t, bias_col, tri, ltri)


def _dispatch_kernel(info_ref, dest_ref, u_hbm, xs_hbm, stage, zbuf, lsem, ssem, zsem, *, tm, n_blk):
    i = pl.program_id(0)
    n = pl.num_programs(0)
    slot = i % 2
    n_used = info_ref[1, 0]

    def load(tile, dst_slot):
        return pltpu.make_async_copy(u_hbm.at[pl.ds(tile * tm, tm)], stage.at[dst_slot],
                                     lsem.at[dst_slot])

    def zero_block(blk):
        return pltpu.make_async_copy(zbuf, xs_hbm.at[pl.ds(blk * MOE_BLK, MOE_BLK)], zsem)

    @pl.when(i == 0)
    def _():
        load(0, 0).start()
        zbuf[...] = jnp.zeros_like(zbuf)
        for e in range(N_EXPERTS):
            @pl.when(info_ref[3, e] > 0)
            def _():
                zero_block(info_ref[2, e] - 1).start()

        def zero_unused(j, carry):
            zero_block(j).start()
            return carry
        lax.fori_loop(n_used, n_blk, zero_unused, 0)
        for e in range(N_EXPERTS):
            @pl.when(info_ref[3, e] > 0)
            def _():
                zero_block(0).wait()

        def wait_unused(j, carry):
            zero_block(0).wait()
            return carry
        lax.fori_loop(n_used, n_blk, wait_unused, 0)

    def scatter_wait(src_slot):
        pltpu.make_async_copy(stage.at[src_slot], xs_hbm.at[pl.ds(0, tm)], ssem.at[src_slot]).wait()

    @pl.when(i >= 1)
    def _():
        for _ in range(TOP_K):
            scatter_wait(1 - slot)

    @pl.when(i + 1 < n)
    def _():
        load(i + 1, 1 - slot).start()

    load(i, slot).wait()
    for k in range(TOP_K):
        for r in range(tm):
            pltpu.make_async_copy(stage.at[slot, r], xs_hbm.at[dest_ref[0, k, r]],
                                  ssem.at[slot]).start(priority=r % 2)

    @pl.when(i == n - 1)
    def _():
        for _ in range(TOP_K):
            scatter_wait(slot)


def _dispatch(info, dest3, u3, n_blk):
    t, nt, _ = u3.shape
    tm = dest3.shape[2]
    return pl.pallas_call(
        functools.partial(_dispatch_kernel, tm=tm, n_blk=n_blk),
        out_shape=jax.ShapeDtypeStruct((n_blk * MOE_BLK, nt, 128), BF16),
        grid_spec=pltpu.PrefetchScalarGridSpec(
            num_scalar_prefetch=1,
            grid=(t // tm,),
            in_specs=[
                pl.BlockSpec((1, TOP_K, tm), lambda i, info: (i, 0, 0), memory_space=pltpu.SMEM),
                pl.BlockSpec(memory_space=pl.ANY),
            ],
            out_specs=pl.BlockSpec(memory_space=pl.ANY),
            scratch_shapes=[
                pltpu.VMEM((2, tm, nt, 128), BF16),
                pltpu.VMEM((MOE_BLK, nt, 128), BF16),
                pltpu.SemaphoreType.DMA((2,)),
                pltpu.SemaphoreType.DMA((2,)),
                pltpu.SemaphoreType.DMA(()),
            ],
        ),
        compiler_params=_cparams(("arbitrary",)),
        name="moe_dispatch",
    )(info, dest3, u3)


def _gmm_kernel(info_ref, xs_ref, wg_hbm, wu_hbm, wd_hbm, y_ref,
                wg_f, wu_f, wd_f, wsem, wg_s, wu_s, wd_s, nswap):
    i = pl.program_id(0)
    n_used = info_ref[1, 0]
    e = info_ref[0, i]
    prev = info_ref[0, jnp.maximum(i - 1, 0)]
    fresh = (i == 0) | (e != prev)

    def fetch(expert, slot):
        return (pltpu.make_async_copy(wg_hbm.at[expert], wg_f.at[slot], wsem.at[slot]),
                pltpu.make_async_copy(wu_hbm.at[expert], wu_f.at[slot], wsem.at[slot]),
                pltpu.make_async_copy(wd_hbm.at[expert], wd_f.at[slot], wsem.at[slot]))

    @pl.when(i == 0)
    def _():
        nswap[0] = 0
        for c in fetch(e, 0):
            c.start()

    @pl.when(fresh & (i < n_used))
    def _():
        slot = nswap[0] % 2
        nswap[0] = nswap[0] + 1
        for c in fetch(e, slot):
            c.wait()
        nxt = info_ref[2, e]

        @pl.when(nxt < n_used)
        def _():
            for c in fetch(info_ref[0, nxt], 1 - slot):
                c.start(priority=1)

        wg_s[...] = wg_f[slot].astype(BF16)
        wu_s[...] = wu_f[slot].astype(BF16)
        wd_s[...] = wd_f[slot].astype(BF16)

    @pl.when(i < n_used)
    def _():
        xb = _tiles_to_rows(xs_ref[...])
        a = jnp.dot(xb, wg_s[...], preferred_element_type=F32)
        u = jnp.dot(xb, wu_s[...], preferred_element_type=F32)
        hid = (a * _sigmoid(a) * u).astype(BF16)
        y = jnp.dot(hid, wd_s[...], preferred_element_type=F32)
        y_ref[...] = _rows_to_tiles(y.astype(BF16))

    @pl.when(i >= n_used)
    def _():
        y_ref[...] = jnp.zeros_like(y_ref)


def _gmm(info, xs3, w_gate, w_up, w_down):
    p, nt, _ = xs3.shape
    d = nt * 128
    n_blk = p // MOE_BLK
    de = w_gate.shape[-1]
    return pl.pallas_call(
        _gmm_kernel,
        out_shape=jax.ShapeDtypeStruct((p, nt, 128), BF16),
        grid_spec=pltpu.PrefetchScalarGridSpec(
            num_scalar_prefetch=1,
            grid=(n_blk,),
            in_specs=[
                pl.BlockSpec((MOE_BLK, nt, 128),
                             lambda i, info: (jnp.minimum(i, info[1, 0] - 1), 0, 0)),
                pl.BlockSpec(memory_space=pl.ANY),
                pl.BlockSpec(memory_space=pl.ANY),
                pl.BlockSpec(memory_space=pl.ANY),
            ],
            out_specs=pl.BlockSpec((MOE_BLK, nt, 128), lambda i, info: (i, 0, 0)),
            scratch_shapes=[
                pltpu.VMEM((2, d, de), F32),
                pltpu.VMEM((2, d, de), F32),
                pltpu.VMEM((2, de, d), F32),
                pltpu.SemaphoreType.DMA((2,)),
                pltpu.VMEM((d, de), BF16),
                pltpu.VMEM((d, de), BF16),
                pltpu.VMEM((de, d), BF16),
                pltpu.SMEM((1,), jnp.int32),
            ],
        ),
        compiler_params=_cparams(("arbitrary",)),
        name="moe_gmm",
    )(info, xs3, w_gate, w_up, w_down)


def _final_kernel(cur_ref, nxt_ref, u_ref, h_ref, gt_ref, wt_ref, wsg_ref, wsu_ref, wsd_ref, y_hbm,
                  o_ref, ybuf_a, ybuf_b, sem, *, tm):
    i = pl.program_id(0)
    n = pl.num_programs(0)
    n_rows = TOP_K * tm

    def issue(idx_ref, tile, buf, sem_i):
        for k in range(TOP_K):
            for r in range(tm):
                pltpu.make_async_copy(y_hbm.at[idx_ref[0, tile * TOP_K + k, r]], buf.at[k * tm + r],
                                      sem.at[sem_i]).start(priority=min(r % 3, 1))

    def gather_wait(buf, sem_i):
        pltpu.make_async_copy(y_hbm.at[pl.ds(0, n_rows)], buf, sem.at[sem_i]).wait()

    eye = (lax.broadcasted_iota(jnp.int32, (tm, tm), 0)
           == lax.broadcasted_iota(jnp.int32, (tm, tm), 1))

    def compute(tile, buf):
        rows = slice(tile * tm, (tile + 1) * tm)
        ub = _tiles_to_rows(u_ref[rows])
        a = jnp.dot(ub, wsg_ref[...], preferred_element_type=F32)
        b = jnp.dot(ub, wsu_ref[...], preferred_element_type=F32)
        hid = (a * _sigmoid(a) * b).astype(BF16)
        acc = jnp.dot(hid, wsd_ref[...], preferred_element_type=F32)
        wt = wt_ref[rows, :]
        wcat = jnp.concatenate([jnp.where(eye, wt[:, k:k + 1], 0.0) for k in range(TOP_K)],
                               axis=1)
        whi, wlo = _split_bf16(wcat)
        ycat = _tiles_to_rows(buf[...])
        acc = acc + jnp.dot(whi, ycat, preferred_element_type=F32)
        acc = acc + jnp.dot(wlo, ycat, preferred_element_type=F32)
        o_ref[rows, :] = h_ref[rows, :] + gt_ref[0] * acc

    @pl.when(i == 0)
    def _():
        issue(cur_ref, 0, ybuf_a, 0)

    gather_wait(ybuf_a, 0)
    issue(cur_ref, 1, ybuf_b, 1)
    compute(0, ybuf_a)
    gather_wait(ybuf_b, 1)
    issue(nxt_ref, 0, ybuf_a, 0)
    compute(1, ybuf_b)

    @pl.when(i == n - 1)
    def _():
        gather_wait(ybuf_a, 0)


def _final(dest3, u2, h1, mod3, wt, wsg, wsu, wsd, y, seq):
    t, d = h1.shape
    tm = dest3.shape[2]
    n = t // (2 * tm)
    per_b = seq // (2 * tm)
    dest_pairs = dest3.reshape(n, 2 * TOP_K, tm)
    full = lambda shape: pl.BlockSpec(shape, lambda i: tuple(0 for _ in shape))
    return pl.pallas_call(
        functools.partial(_final_kernel, tm=tm),
        out_shape=jax.ShapeDtypeStruct((t, d), F32),
        grid=(n,),
        in_specs=[
            pl.BlockSpec((1, 2 * TOP_K, tm), lambda i: (i, 0, 0), memory_space=pltpu.SMEM),
            pl.BlockSpec((1, 2 * TOP_K, tm), lambda i: (jnp.minimum(i + 1, n - 1), 0, 0),
                         memory_space=pltpu.SMEM),
            pl.BlockSpec((2 * tm, d // 128, 128), lambda i: (i, 0, 0)),
            pl.BlockSpec((2 * tm, d), lambda i: (i, 0)),
            pl.BlockSpec((1, 1, d), lambda i: (i // per_b, 0, 5)),
            pl.BlockSpec((2 * tm, TOP_K), lambda i: (i, 0)),
            full(wsg.shape),
            full(wsu.shape),
            full(wsd.shape),
            pl.BlockSpec(memory_space=pl.ANY),
        ],
        out_specs=pl.BlockSpec((2 * tm, d), lambda i: (i, 0)),
        scratch_shapes=[
            pltpu.VMEM((TOP_K * tm, d // 128, 128), BF16),
            pltpu.VMEM((TOP_K * tm, d // 128, 128), BF16),
            pltpu.SemaphoreType.DMA((2,)),
        ],
        compiler_params=_cparams(("arbitrary",)),
        name="final",
    )(dest_pairs, dest_pairs, u2, h1, mod3, wt, wsg, wsu, wsd, y)


def _layer(h2, mod3, positions, lb, p, batch, seq):
    t, d = h2.shape

    w_in_t = p["w_in"].T
    proj, cq, ckv, kr = _inproj(h2, p["norm_mix_g"][None, :], mod3,
                                w_in_t[:COL_CQ].astype(BF16), w_in_t[COL_CQ:].astype(BF16), seq)

    o_hg = _hgrn(proj, lb[None, :], p["hg_out_g"][None, :], batch, seq)

    wuq = p["mla_w_uq"].reshape(MLA_Q_RANK, MLA_HEADS, MLA_QK)
    wuq_p = jnp.pad(wuq, ((0, 0), (0, 0), (0, MLA_QK_PAD - MLA_QK))).reshape(
        MLA_Q_RANK, MLA_HEADS * MLA_QK_PAD).astype(BF16)
    wukv = p["mla_w_ukv"].astype(BF16)
    gq = jnp.pad(p["mla_q_norm_g"], (0, MLA_QK_PAD - MLA_QK))[None, :]
    gkn = p["mla_k_norm_g"][None, :MLA_NOPE]
    gkr = jnp.pad(p["mla_k_norm_g"][MLA_NOPE:], (0, 128 - MLA_ROPE))[None, :]
    half = MLA_ROPE // 2
    inv_freq = ROPE_THETA ** (-jnp.arange(half, dtype=F32) / half)
    zeros64 = jnp.zeros((128 - MLA_ROPE,), F32)
    invf = jnp.concatenate([inv_freq, inv_freq, zeros64])[None, :]
    sgn = jnp.concatenate([-jnp.ones((half,), F32), jnp.ones((half,), F32), zeros64])[None, :]
    cm = jnp.concatenate([jnp.ones((MLA_ROPE,), F32), zeros64])[None, :]
    q, k, v = _mla_prep(cq, ckv, kr, positions.reshape(t, 1), p["mla_q_a_g"][None, :],
                        p["mla_kv_a_g"][None, :], wuq_p, wukv, gq, gkn, gkr, invf, sgn, cm,
                        batch, seq)
    o_mla = _attention(q, k, v)

    wr_t = p["w_router"].T
    wr_hi = wr_t.astype(BF16)
    wr_lo = (wr_t - wr_hi.astype(F32)).astype(BF16)
    h1, u2, logits_t = _outproj(o_hg, o_mla, h2, mod3, p["mla_out_g"][None, :],
                                p["norm_ffn_g"][None, :], p["w_out"].astype(BF16),
                                wr_hi, wr_lo, seq)

    n_blk = (t * TOP_K) // MOE_BLK + N_EXPERTS
    _, top_w, dest, info = _route(logits_t, p["router_bias"][:, None], n_blk)

    def tiled(tm):
        tm = min(tm, seq)
        return dest.reshape(TOP_K, t // tm, tm).transpose(1, 0, 2)

    xs = _dispatch(info, tiled(DISPATCH_TM), u2, n_blk)
    y = _gmm(info, xs, p["w_gate"], p["w_up"], p["w_down"])
    return _final(tiled(COMBINE_TM), u2, h1, mod3, top_w.T, p["ws_gate"].astype(BF16),
                  p["ws_up"].astype(BF16), p["ws_down"].astype(BF16), y, seq)


def kernel(x, c, positions, w_ada, b_ada, norm_mix_g, norm_ffn_g, w_in, hg_lb_logits, hg_out_g,
           mla_q_a_g, mla_w_uq, mla_kv_a_g, mla_w_ukv, mla_q_norm_g, mla_k_norm_g, mla_out_g,
           w_out, w_router, router_bias, w_gate, w_up, w_down, ws_gate, ws_up, ws_down):
    batch, seq, d = x.shape
    depth = w_ada.shape[0]
    lb_all = jnp.cumsum(jax.nn.softmax(hg_lb_logits.astype(F32), axis=0), axis=0)
    c8 = jnp.pad(c, ((0, 8 - batch), (0, 0)))
    h2 = x.reshape(batch * seq, d)
    for l in range(depth):
        mod = _adaln(c8, w_ada[l], b_ada[l][None, :])[:batch]
        mod3 = mod.reshape(batch, 1, 6 * d)
        p = dict(norm_mix_g=norm_mix_g[l], norm_ffn_g=norm_ffn_g[l], w_in=w_in[l],
                 hg_out_g=hg_out_g[l], mla_q_a_g=mla_q_a_g[l], mla_w_uq=mla_w_uq[l],
                 mla_kv_a_g=mla_kv_a_g[l], mla_w_ukv=mla_w_ukv[l], mla_q_norm_g=mla_q_norm_g[l],
                 mla_k_norm_g=mla_k_norm_g[l], mla_out_g=mla_out_g[l], w_out=w_out[l],
                 w_router=w_router[l], router_bias=router_bias[l], w_gate=w_gate[l],
                 w_up=w_up[l], w_down=w_down[l], ws_gate=ws_gate[l], ws_up=ws_up[l],
                 ws_down=ws_down[l])
        h2 = _layer(h2, mod3, positions, lb_all[l], p, batch, seq)
    return h2.reshape(batch, seq, d)
```

```python
import functools

import numpy as np
import jax
import jax.numpy as jnp
from jax import lax
from jax.experimental import pallas as pl
from jax.experimental.pallas import tpu as pltpu

F32 = jnp.float32
BF16 = jnp.bfloat16

EPS = 1e-6

HG_HEADS = 8
HG_DK = 128
HG_DV = 128
HG_WIDTH = HG_HEADS * HG_DV
HG_CHUNK = 128
HG_ROW_LEVEL = 2
HG_HEADS_PER_STEP = 2
HG_UNROLL = 16

MLA_HEADS = 8
MLA_Q_RANK = 512
MLA_KV_RANK = 256
MLA_NOPE = 128
MLA_ROPE = 64
MLA_QK = MLA_NOPE + MLA_ROPE
MLA_QK_PAD = 256
MLA_V = 128
MLA_WIDTH = MLA_HEADS * MLA_V
ROPE_THETA = 10000.0

COL_CQ = 4096

N_EXPERTS = 64
TOP_K = 8
N_GROUPS = 8
TOPK_GROUPS = 4
EXPERTS_PER_GROUP = N_EXPERTS // N_GROUPS
ROUTED_SCALE = 2.5
MOE_BLK = 256

VMEM_LIMIT = 56 * 1024 * 1024

ADALN_TN = 1024
INPROJ_TM = 1024
INPROJ_TN = 1024
MLA_PREP_TM = 256
ATTN_TQ = 256
OUTPROJ_TM = 512
OUTPROJ_SUB = 2
ROUTE_TT = 512
DISPATCH_TM = 256
COMBINE_TM = 128

NT_DIMS = (((1,), (1,)), ((), ()))
TN_DIMS = (((0,), (0,)), ((), ()))


def _cparams(sem):
    return pltpu.CompilerParams(dimension_semantics=sem, vmem_limit_bytes=VMEM_LIMIT)


def _sigmoid(x):
    return 1.0 / (1.0 + jnp.exp(-x))


def _split_bf16(x):
    hi = x.astype(BF16)
    lo = (x - hi.astype(F32)).astype(BF16)
    return hi, lo


def _rows_to_tiles(x):
    n = x.shape[1] // 128
    return jnp.swapaxes(jnp.stack([x[:, s * 128:(s + 1) * 128] for s in range(n)], axis=0), 0, 1)


def _tiles_to_rows(x3):
    xt = jnp.swapaxes(x3, 0, 1)
    return jnp.concatenate([xt[s] for s in range(x3.shape[1])], axis=1)


def _adaln_kernel(c_ref, w_ref, b_ref, o_ref):
    c = c_ref[...]
    cond = c * _sigmoid(c)
    hi, lo = _split_bf16(cond)
    lhs = jnp.concatenate([hi, lo], axis=0)
    r = jnp.dot(lhs, w_ref[...].astype(BF16), preferred_element_type=F32)
    o_ref[...] = r[:8] + r[8:] + b_ref[...]


def _adaln(c8, w, b):
    d, n = w.shape
    tn = ADALN_TN
    return pl.pallas_call(
        _adaln_kernel,
        out_shape=jax.ShapeDtypeStruct((8, n), F32),
        grid=(n // tn,),
        in_specs=[
            pl.BlockSpec((8, d), lambda j: (0, 0)),
            pl.BlockSpec((d, tn), lambda j: (0, j)),
            pl.BlockSpec((1, tn), lambda j: (0, j)),
        ],
        out_specs=pl.BlockSpec((8, tn), lambda j: (0, j)),
        compiler_params=_cparams(("arbitrary",)),
        name="adaln",
    )(c8, w, b)


def _inproj_kernel(x_ref, g_ref, sc_ref, sh_ref, whg_ref, wml_ref, hg_out, cq_out, ckv_out, kr_out,
                   u_scr, *, tm, n_hg):
    j = pl.program_id(1)

    @pl.when(j == 0)
    def _():
        g = g_ref[...]
        sc = 1.0 + sc_ref[0]
        sh = sh_ref[0]

        def body(r, carry):
            rows = pl.ds(pl.multiple_of(r * 128, 128), 128)
            x = x_ref[rows, :]
            ms = jnp.mean(x * x, axis=-1, keepdims=True)
            u = x * lax.rsqrt(ms + EPS) * g * sc + sh
            u_scr[rows, :] = u.astype(BF16)
            return carry

        lax.fori_loop(0, tm // 128, body, 0)

    @pl.when(j < n_hg)
    def _():
        hg_out[...] = lax.dot_general(u_scr[...], whg_ref[...], NT_DIMS,
                                      preferred_element_type=F32)

    @pl.when(j == n_hg)
    def _():
        r = lax.dot_general(u_scr[...], wml_ref[...], NT_DIMS, preferred_element_type=F32)
        cq_out[...] = r[:, :MLA_Q_RANK]
        ckv_out[...] = r[:, MLA_Q_RANK:MLA_Q_RANK + MLA_KV_RANK]
        kr_out[...] = r[:, MLA_Q_RANK + MLA_KV_RANK:]


def _inproj(x2, g, mod3, w_hg_t, w_ml_t, seq):
    t, d = x2.shape
    tm = min(INPROJ_TM, seq)
    tn = INPROJ_TN
    n_hg = w_hg_t.shape[0] // tn
    per_b = seq // tm
    last = lambda j: jnp.minimum(j, n_hg - 1)
    return pl.pallas_call(
        functools.partial(_inproj_kernel, tm=tm, n_hg=n_hg),
        out_shape=(
            jax.ShapeDtypeStruct((t, w_hg_t.shape[0]), F32),
            jax.ShapeDtypeStruct((t, MLA_Q_RANK), F32),
            jax.ShapeDtypeStruct((t, MLA_KV_RANK), F32),
            jax.ShapeDtypeStruct((t, MLA_ROPE), F32),
        ),
        grid=(t // tm, n_hg + 1),
        in_specs=[
            pl.BlockSpec((tm, d), lambda i, j: (i, 0)),
            pl.BlockSpec((1, d), lambda i, j: (0, 0)),
            pl.BlockSpec((1, 1, d), lambda i, j: (i // per_b, 0, 1)),
            pl.BlockSpec((1, 1, d), lambda i, j: (i // per_b, 0, 0)),
            pl.BlockSpec((tn, d), lambda i, j: (last(j), 0)),
            pl.BlockSpec(w_ml_t.shape, lambda i, j: (0, 0)),
        ],
        out_specs=(
            pl.BlockSpec((tm, tn), lambda i, j: (i, last(j))),
            pl.BlockSpec((tm, MLA_Q_RANK), lambda i, j: (i, 0)),
            pl.BlockSpec((tm, MLA_KV_RANK), lambda i, j: (i, 0)),
            pl.BlockSpec((tm, MLA_ROPE), lambda i, j: (i, 0)),
        ),
        scratch_shapes=[pltpu.VMEM((tm, d), BF16)],
        compiler_params=_cparams(("arbitrary", "arbitrary")),
        name="inproj",
    )(x2, g, mod3, mod3, w_hg_t, w_ml_t)


def _hgrn_consts():
    c = HG_CHUNK
    t = np.arange(c)[:, None]
    j = np.arange(c)[None, :]
    tril = (j <= t).astype(np.float32)
    mats = [tril]
    masks = [np.eye(c, dtype=np.float32)]
    s = c // 2
    while s >= 1:
        ref = (t // (2 * s)) * (2 * s) + s - 1
        if s < HG_ROW_LEVEL:
            mats.append(tril - (j <= ref).astype(np.float32))
        masks.append((((t // s) % 2 == 1) & ((j // s) == (t // s) - 1)).astype(np.float32))
        s //= 2
    w = np.concatenate(mats, 0)
    return np.concatenate([w, w], 1), np.stack(masks, 0)


def _hgrn_head(q, v, g, kk, xs, og, m_ref, st_ref, o_ref):
    c = HG_CHUNK
    n_lvl = m_ref.shape[0]
    b = xs[0:c]
    bl = b[c - 1:c, :] - b
    st = st_ref[...]
    vb = v.astype(BF16)
    o = lax.dot_general((q * jnp.exp(b)).astype(BF16), st.astype(BF16), NT_DIMS,
                        preferred_element_type=F32)
    attn = m_ref[0] * lax.dot_general(q.astype(BF16), kk.astype(BF16), NT_DIMS,
                                      preferred_element_type=F32)
    s = c // 2
    fine = 1
    for l in range(1, n_lvl):
        if s >= HG_ROW_LEVEL:
            b3 = b.reshape(c // (2 * s), 2 * s, HG_DK)
            x = (b3 - b3[:, s - 1:s, :]).reshape(c, HG_DK)
        else:
            x = xs[fine * c:(fine + 1) * c]
            fine += 1
        s //= 2
        e = jnp.exp(-jnp.abs(x))
        attn = attn + m_ref[l] * lax.dot_general(
            (q * e).astype(BF16), (kk * e).astype(BF16), NT_DIMS, preferred_element_type=F32)
    o = o + jnp.dot(attn.astype(BF16), vb, preferred_element_type=F32)
    khat = (kk * jnp.exp(bl)).astype(BF16)
    st_ref[...] = st * jnp.exp(b[c - 1:c, :]) + lax.dot_general(
        vb, khat, TN_DIMS, preferred_element_type=F32)
    ms = jnp.mean(o * o, axis=-1, keepdims=True)
    y = o * lax.rsqrt(ms + EPS) * og * (g * _sigmoid(g))
    o_ref[...] = y.astype(BF16)


def _hgrn_kernel(q_ref, f_ref, i_ref, g_ref, lb_ref, og_ref, w_ref, m_ref, o_ref, st_scr, *,
                 seq, nh):
    c = HG_CHUNK
    st_scr[...] = jnp.zeros_like(st_scr)
    lb = lb_ref[...]
    og = og_ref[...]

    def chunk(ci, carry):
        rows = pl.ds(pl.multiple_of(ci * c, c), c)
        f = lb + (1.0 - lb) * _sigmoid(f_ref[rows, :])
        hi, lo = _split_bf16(jnp.log(f))
        xs = jnp.dot(w_ref[...], jnp.concatenate([hi, lo], axis=0), preferred_element_type=F32)
        for hh in range(nh):
            lanes = slice(hh * HG_DK, (hh + 1) * HG_DK)
            _hgrn_head(q_ref[rows, lanes], i_ref[rows, lanes], g_ref[rows, lanes],
                       1.0 - f[:, lanes], xs[:, lanes], og, m_ref, st_scr.at[hh],
                       o_ref.at[rows, lanes])
        return carry

    lax.fori_loop(0, seq // c, chunk, 0, unroll=HG_UNROLL // nh)


def _hgrn(proj, lb, og, batch, seq):
    t = proj.shape[0]
    wm, mm = _hgrn_consts()
    wm = jnp.asarray(wm, BF16)
    mm = jnp.asarray(mm, F32)
    nh = HG_HEADS_PER_STEP
    hg = HG_HEADS // nh
    wd = nh * HG_DK
    return pl.pallas_call(
        functools.partial(_hgrn_kernel, seq=seq, nh=nh),
        out_shape=jax.ShapeDtypeStruct((t, HG_WIDTH), BF16),
        grid=(batch, hg),
        in_specs=[
            pl.BlockSpec((seq, wd), lambda b, i: (b, i)),
            pl.BlockSpec((seq, wd), lambda b, i: (b, hg + i)),
            pl.BlockSpec((seq, wd), lambda b, i: (b, 2 * hg + i)),
            pl.BlockSpec((seq, wd), lambda b, i: (b, 3 * hg + i)),
            pl.BlockSpec((1, wd), lambda b, i: (0, i)),
            pl.BlockSpec((1, HG_DV), lambda b, i: (0, 0)),
            pl.BlockSpec(wm.shape, lambda b, i: (0, 0)),
            pl.BlockSpec(mm.shape, lambda b, i: (0, 0, 0)),
        ],
        out_specs=pl.BlockSpec((seq, wd), lambda b, i: (b, i)),
        scratch_shapes=[pltpu.VMEM((nh, HG_DV, HG_DK), F32)],
        compiler_params=_cparams(("arbitrary", "arbitrary")),
        name="hgrn2",
    )(proj, proj, proj, proj, lb, og, wm, mm)


def _mla_prep_kernel(cq_ref, ckv_ref, kr_ref, pos_ref, qag_ref, kvag_ref, wuq_ref, wukv_ref,
                     gq_ref, gkn_ref, gkr_ref, invf_ref, sgn_ref, cm_ref,
                     q_out, k_out, v_out):
    cq = cq_ref[...]
    cqn = cq * lax.rsqrt(jnp.mean(cq * cq, axis=-1, keepdims=True) + EPS) * qag_ref[...]
    qf = jnp.dot(cqn.astype(BF16), wuq_ref[...], preferred_element_type=F32)
    ckv = ckv_ref[...]
    ckvn = ckv * lax.rsqrt(jnp.mean(ckv * ckv, axis=-1, keepdims=True) + EPS) * kvag_ref[...]
    kvf = jnp.dot(ckvn.astype(BF16), wukv_ref[...], preferred_element_type=F32)

    ang = pos_ref[...].astype(F32) * invf_ref[...]
    cos_t = jnp.cos(ang) * cm_ref[...]
    sin_t = jnp.sin(ang) * sgn_ref[...]

    def rope(r):
        return r * cos_t + (pltpu.roll(r, 96, 1) + pltpu.roll(r, 32, 1)) * sin_t

    kr64 = kr_ref[...]
    kr = jnp.concatenate([kr64, jnp.zeros_like(kr64)], axis=-1)
    kr_ss = jnp.sum(kr * kr, axis=-1, keepdims=True)
    krr = rope(kr * gkr_ref[...])
    gq = gq_ref[...]
    gkn = gkn_ref[...]
    scale = MLA_QK ** -0.5
    for h in range(MLA_HEADS):
        qh = qf[:, h * MLA_QK_PAD:(h + 1) * MLA_QK_PAD]
        inv = lax.rsqrt(jnp.sum(qh * qh, axis=-1, keepdims=True) / MLA_QK + EPS)
        qn = qh * inv * gq
        qo = jnp.concatenate([qn[:, :MLA_NOPE], rope(qn[:, MLA_NOPE:])], axis=-1) * scale
        q_out[0, h] = qo.astype(BF16)
        kn = kvf[:, h * 256:h * 256 + MLA_NOPE]
        inv = lax.rsqrt((jnp.sum(kn * kn, axis=-1, keepdims=True) + kr_ss) / MLA_QK + EPS)
        k_out[0, h] = jnp.concatenate([kn * inv * gkn, krr * inv], axis=-1).astype(BF16)
        v_out[0, h] = kvf[:, h * 256 + MLA_NOPE:(h + 1) * 256].astype(BF16)


def _mla_prep(cq, ckv, kr, pos2, qag, kvag, wuq_p, wukv, gq, gkn, gkr, invf, sgn, cm, batch, seq):
    tm = min(MLA_PREP_TM, seq)
    per_b = seq // tm
    hh = MLA_HEADS
    full = lambda shape: pl.BlockSpec(shape, lambda b, i: tuple(0 for _ in shape))
    return pl.pallas_call(
        _mla_prep_kernel,
        out_shape=(
            jax.ShapeDtypeStruct((batch, hh, seq, MLA_QK_PAD), BF16),
            jax.ShapeDtypeStruct((batch, hh, seq, MLA_QK_PAD), BF16),
            jax.ShapeDtypeStruct((batch, hh, seq, MLA_V), BF16),
        ),
        grid=(batch, per_b),
        in_specs=[
            pl.BlockSpec((tm, MLA_Q_RANK), lambda b, i: (b * per_b + i, 0)),
            pl.BlockSpec((tm, MLA_KV_RANK), lambda b, i: (b * per_b + i, 0)),
            pl.BlockSpec((tm, MLA_ROPE), lambda b, i: (b * per_b + i, 0)),
            pl.BlockSpec((tm, 1), lambda b, i: (b * per_b + i, 0)),
            full((1, MLA_Q_RANK)),
            full((1, MLA_KV_RANK)),
            full(wuq_p.shape),
            full(wukv.shape),
            full((1, MLA_QK_PAD)),
            full((1, 128)),
            full((1, 128)),
            full((1, 128)),
            full((1, 128)),
            full((1, 128)),
        ],
        out_specs=(
            pl.BlockSpec((1, hh, tm, MLA_QK_PAD), lambda b, i: (b, 0, i, 0)),
            pl.BlockSpec((1, hh, tm, MLA_QK_PAD), lambda b, i: (b, 0, i, 0)),
            pl.BlockSpec((1, hh, tm, MLA_V), lambda b, i: (b, 0, i, 0)),
        ),
        compiler_params=_cparams(("arbitrary", "arbitrary")),
        name="mla_prep",
    )(cq, ckv, kr, pos2, qag, kvag, wuq_p, wukv, gq, gkn, gkr, invf, sgn, cm)


def _attn_kernel(q_ref, k_ref, v_ref, o_ref, *, tq, seq):
    row = lax.broadcasted_iota(jnp.int32, (tq, tq), 0)
    col = lax.broadcasted_iota(jnp.int32, (tq, tq), 1)
    causal = col <= row
    for i in range(seq // tq):
        lo = i * tq
        q = q_ref[0, 0, lo:lo + tq, :]
        sd = lax.dot_general(q, k_ref[0, 0, lo:lo + tq, :], NT_DIMS, preferred_element_type=F32)
        sd = jnp.where(causal, sd, -jnp.inf)
        m = jnp.max(sd, axis=-1, keepdims=True)
        if i > 0:
            so = lax.dot_general(q, k_ref[0, 0, :lo, :], NT_DIMS, preferred_element_type=F32)
            m = jnp.maximum(m, jnp.max(so, axis=-1, keepdims=True))
        pd = jnp.exp(sd - m)
        l = jnp.sum(pd, axis=-1, keepdims=True)
        o = jnp.dot(pd.astype(BF16), v_ref[0, 0, lo:lo + tq, :], preferred_element_type=F32)
        if i > 0:
            po = jnp.exp(so - m)
            l = l + jnp.sum(po, axis=-1, keepdims=True)
            o = o + jnp.dot(po.astype(BF16), v_ref[0, 0, :lo, :], preferred_element_type=F32)
        o_ref[lo:lo + tq, :] = o / l


def _attention(q, k, v):
    batch, hh, seq, dq = q.shape
    tq = min(ATTN_TQ, seq)
    return pl.pallas_call(
        functools.partial(_attn_kernel, tq=tq, seq=seq),
        out_shape=jax.ShapeDtypeStruct((batch * seq, hh * MLA_V), F32),
        grid=(batch, hh),
        in_specs=[
            pl.BlockSpec((1, 1, seq, dq), lambda b, h: (b, h, 0, 0)),
            pl.BlockSpec((1, 1, seq, dq), lambda b, h: (b, h, 0, 0)),
            pl.BlockSpec((1, 1, seq, MLA_V), lambda b, h: (b, h, 0, 0)),
        ],
        out_specs=pl.BlockSpec((seq, MLA_V), lambda b, h: (b, h)),
        compiler_params=_cparams(("arbitrary", "arbitrary")),
        name="mla_attn",
    )(q, k, v)


def _outproj_kernel(ohg_ref, omla_ref, x_ref, gt_ref, sc_ref, sh_ref, og_ref, g2_ref,
                    wo_ref, wrh_ref, wrl_ref, h_out, u_out, lg_out, *, n_sub):
    sub = x_ref.shape[0] // n_sub
    for j in range(n_sub):
        rows = slice(j * sub, (j + 1) * sub)
        om = omla_ref[rows, :]
        omn = om * lax.rsqrt(jnp.mean(om * om, axis=-1, keepdims=True) + EPS) * og_ref[...]
        y = jnp.dot(ohg_ref[rows, :], wo_ref[:HG_WIDTH, :], preferred_element_type=F32)
        y = y + jnp.dot(omn.astype(BF16), wo_ref[HG_WIDTH:, :], preferred_element_type=F32)
        h1 = x_ref[rows, :] + gt_ref[0] * y
        h_out[rows, :] = h1
        u = h1 * lax.rsqrt(jnp.mean(h1 * h1, axis=-1, keepdims=True) + EPS) * g2_ref[...]
        u = u * (1.0 + sc_ref[0]) + sh_ref[0]
        uh, ul = _split_bf16(u)
        u_out[rows] = _rows_to_tiles(uh)
        lg = lax.dot_general(wrh_ref[...], uh, NT_DIMS, preferred_element_type=F32)
        lg = lg + lax.dot_general(wrl_ref[...], uh, NT_DIMS, preferred_element_type=F32)
        lg = lg + lax.dot_general(wrh_ref[...], ul, NT_DIMS, preferred_element_type=F32)
        lg_out[:, rows] = lg


def _outproj(ohg, omla, x2, mod3, og, g2, wo_bf, wr_hi, wr_lo, seq):
    t, d = x2.shape
    tm = min(OUTPROJ_TM, seq)
    per_b = seq // tm
    full = lambda shape: pl.BlockSpec(shape, lambda i: tuple(0 for _ in shape))
    modspec = lambda col: pl.BlockSpec((1, 1, d), lambda i: (i // per_b, 0, col))
    return pl.pallas_call(
        functools.partial(_outproj_kernel, n_sub=OUTPROJ_SUB),
        out_shape=(
            jax.ShapeDtypeStruct((t, d), F32),
            jax.ShapeDtypeStruct((t, d // 128, 128), BF16),
            jax.ShapeDtypeStruct((N_EXPERTS, t), F32),
        ),
        grid=(t // tm,),
        in_specs=[
            pl.BlockSpec((tm, HG_WIDTH), lambda i: (i, 0)),
            pl.BlockSpec((tm, MLA_WIDTH), lambda i: (i, 0)),
            pl.BlockSpec((tm, d), lambda i: (i, 0)),
            modspec(2),
            modspec(4),
            modspec(3),
            full((1, MLA_WIDTH)),
            full((1, d)),
            full(wo_bf.shape),
            full(wr_hi.shape),
            full(wr_lo.shape),
        ],
        out_specs=(
            pl.BlockSpec((tm, d), lambda i: (i, 0)),
            pl.BlockSpec((tm, d // 128, 128), lambda i: (i, 0, 0)),
            pl.BlockSpec((N_EXPERTS, tm), lambda i: (0, i)),
        ),
        compiler_params=_cparams(("arbitrary",)),
        name="outproj",
    )(ohg, omla, x2, mod3, mod3, mod3, og, g2, wo_bf, wr_hi, wr_lo)


def _route_kernel(lg_ref, bias_ref, tri_ref, ltri_ref, idx_out, w_out, dest_out, info_out,
                  carry, base, *, tt, n_info):
    phase = pl.program_id(0)
    step = pl.program_id(1)

    @pl.when((phase == 0) & (step == 0))
    def _():
        carry[...] = jnp.zeros_like(carry)

    @pl.when((phase == 1) & (step == 0))
    def _():
        cnt = carry[...]
        padded = jnp.floor((cnt + (MOE_BLK - 1)) / MOE_BLK) * MOE_BLK
        hi, lo = _split_bf16(padded)
        ps = jnp.dot(ltri_ref[...], jnp.concatenate([hi, lo], axis=1), preferred_element_type=F32)
        pad_start = ps[:, :128] + ps[:, 128:]
        base[...] = pad_start
        carry[...] = jnp.zeros_like(carry)
        pad_end = pad_start + padded
        reps = n_info // 128
        pe = jnp.concatenate([pad_end] * reps, axis=1)
        starts = (lax.broadcasted_iota(jnp.int32, (N_EXPERTS, n_info), 1) * MOE_BLK).astype(F32)
        blk_e = jnp.sum(jnp.where(pe <= starts, 1.0, 0.0), axis=0, keepdims=True)
        blk_e = jnp.minimum(blk_e, N_EXPERTS - 1.0)
        n_used = pe[N_EXPERTS - 1:N_EXPERTS, :] / MOE_BLK
        diag = (lax.broadcasted_iota(jnp.int32, (N_EXPERTS, n_info), 0)
                == lax.broadcasted_iota(jnp.int32, (N_EXPERTS, n_info), 1))
        end_blk = jnp.sum(jnp.where(diag, pe, 0.0), axis=0, keepdims=True) / MOE_BLK
        pd = jnp.concatenate([padded] * reps, axis=1)
        num_blk = jnp.sum(jnp.where(diag, pd, 0.0), axis=0, keepdims=True) / MOE_BLK
        info_out[...] = jnp.concatenate(
            [blk_e, n_used, end_blk, num_blk, jnp.zeros((4, n_info), F32)],
            axis=0).astype(jnp.int32)

    ne, ng, eg = N_EXPERTS, N_GROUPS, EXPERTS_PER_GROUP
    neg = -jnp.inf
    sc = _sigmoid(lg_ref[...])
    sel = sc + bias_ref[...]
    sel3 = sel.reshape(ng, eg, tt)
    sub = lax.broadcasted_iota(jnp.int32, (ng, eg, tt), 1)
    m1 = jnp.max(sel3, axis=1, keepdims=True)
    first = jnp.min(jnp.where(sel3 == m1, sub, eg), axis=1, keepdims=True)
    m2 = jnp.max(jnp.where(sub == first, neg, sel3), axis=1, keepdims=True)
    gs = (m1 + m2).reshape(ng, tt)

    gio = lax.broadcasted_iota(jnp.int32, (ng, tt), 0)
    gsel = jnp.zeros((ng, tt), F32)
    for _ in range(TOPK_GROUPS):
        m = jnp.max(gs, axis=0, keepdims=True)
        gi = jnp.min(jnp.where(gs == m, gio, ng), axis=0, keepdims=True)
        hit = gio == gi
        gsel = jnp.where(hit, 1.0, gsel)
        gs = jnp.where(hit, neg, gs)
    gfull = jnp.broadcast_to(gsel.reshape(ng, 1, tt), (ng, eg, tt)).reshape(ne, tt)
    selm = jnp.where(gfull > 0.0, sel, neg)

    eio = lax.broadcasted_iota(jnp.int32, (ne, tt), 0)
    chosen = jnp.zeros((ne, tt), F32)
    idxs, ws = [], []
    for _ in range(TOP_K):
        m = jnp.max(selm, axis=0, keepdims=True)
        ei = jnp.min(jnp.where(selm == m, eio, ne), axis=0, keepdims=True)
        hit = eio == ei
        ws.append(jnp.sum(jnp.where(hit, sc, 0.0), axis=0, keepdims=True))
        idxs.append(ei)
        chosen = jnp.where(hit, 1.0, chosen)
        selm = jnp.where(hit, neg, selm)
    wsum = ws[0]
    for k in range(1, TOP_K):
        wsum = wsum + ws[k]

    tot = jnp.sum(chosen, axis=1, keepdims=True)

    @pl.when(phase == 1)
    def _():
        off = base[...][:, :1] + carry[...][:, :1]
        excl = jnp.dot(chosen.astype(BF16), tri_ref[...], preferred_element_type=F32) + off
        for k in range(TOP_K):
            idx_out[k:k + 1, :] = idxs[k]
            w_out[k:k + 1, :] = ws[k] / wsum * ROUTED_SCALE
            dk = jnp.sum(jnp.where(eio == idxs[k], excl, 0.0), axis=0, keepdims=True)
            dest_out[k:k + 1, :] = dk.astype(jnp.int32)

    carry[...] = carry[...] + tot


def _route(logits_t, bias_col, n_blk):
    ne, t = logits_t.shape
    tt = min(ROUTE_TT, t)
    n_info = -(-n_blk // 128) * 128
    tri = jnp.asarray(np.triu(np.ones((tt, tt), np.float32), 1), BF16)
    ltri = jnp.asarray(np.tril(np.ones((ne, ne), np.float32), -1), BF16)
    return pl.pallas_call(
        functools.partial(_route_kernel, tt=tt, n_info=n_info),
        out_shape=(
            jax.ShapeDtypeStruct((TOP_K, t), jnp.int32),
            jax.ShapeDtypeStruct((TOP_K, t), F32),
            jax.ShapeDtypeStruct((TOP_K, t), jnp.int32),
            jax.ShapeDtypeStruct((8, n_info), jnp.int32),
        ),
        grid=(2, t // tt),
        in_specs=[
            pl.BlockSpec((ne, tt), lambda p, i: (0, i)),
            pl.BlockSpec((ne, 1), lambda p, i: (0, 0)),
            pl.BlockSpec((tt, tt), lambda p, i: (0, 0)),
            pl.BlockSpec((ne, ne), lambda p, i: (0, 0)),
        ],
        out_specs=(
            pl.BlockSpec((TOP_K, tt), lambda p, i: (0, i * p)),
            pl.BlockSpec((TOP_K, tt), lambda p, i: (0, i * p)),
            pl.BlockSpec((TOP_K, tt), lambda p, i: (0, i * p)),
            pl.BlockSpec((8, n_info), lambda p, i: (0, 0)),
        ),
        scratch_shapes=[pltpu.VMEM((ne, 128), F32), pltpu.VMEM((ne, 128), F32)],
        compiler_params=_cparams(("arbitrary", "arbitrary")),
        name="route",
    )(logits_t, bias_col, tri, ltri)


def _dispatch_kernel(info_ref, dest_ref, u_hbm, xs_hbm, stage, zbuf, lsem, ssem, zsem, *, tm, n_blk):
    i = pl.program_id(0)
    n = pl.num_programs(0)
    slot = i % 2
    n_used = info_ref[1, 0]

    def load(tile, dst_slot):
        return pltpu.make_async_copy(u_hbm.at[pl.ds(tile * tm, tm)], stage.at[dst_slot],
                                     lsem.at[dst_slot])

    def zero_block(blk):
        return pltpu.make_async_copy(zbuf, xs_hbm.at[pl.ds(blk * MOE_BLK, MOE_BLK)], zsem)

    @pl.when(i == 0)
    def _():
        load(0, 0).start()
        zbuf[...] = jnp.zeros_like(zbuf)
        for e in range(N_EXPERTS):
            @pl.when(info_ref[3, e] > 0)
            def _():
                zero_block(info_ref[2, e] - 1).start()

        def zero_unused(j, carry):
            zero_block(j).start()
            return carry
        lax.fori_loop(n_used, n_blk, zero_unused, 0)
        for e in range(N_EXPERTS):
            @pl.when(info_ref[3, e] > 0)
            def _():
                zero_block(0).wait()

        def wait_unused(j, carry):
            zero_block(0).wait()
            return carry
        lax.fori_loop(n_used, n_blk, wait_unused, 0)

    def scatter_wait(src_slot):
        pltpu.make_async_copy(stage.at[src_slot], xs_hbm.at[pl.ds(0, tm)], ssem.at[src_slot]).wait()

    @pl.when(i >= 1)
    def _():
        for _ in range(TOP_K):
            scatter_wait(1 - slot)

    @pl.when(i + 1 < n)
    def _():
        load(i + 1, 1 - slot).start()

    load(i, slot).wait()
    for k in range(TOP_K):
        for r in range(tm):
            pltpu.make_async_copy(stage.at[slot, r], xs_hbm.at[dest_ref[0, k, r]],
                                  ssem.at[slot]).start(priority=r % 2)

    @pl.when(i == n - 1)
    def _():
        for _ in range(TOP_K):
            scatter_wait(slot)


def _dispatch(info, dest3, u3, n_blk):
    t, nt, _ = u3.shape
    tm = dest3.shape[2]
    return pl.pallas_call(
        functools.partial(_dispatch_kernel, tm=tm, n_blk=n_blk),
        out_shape=jax.ShapeDtypeStruct((n_blk * MOE_BLK, nt, 128), BF16),
        grid_spec=pltpu.PrefetchScalarGridSpec(
            num_scalar_prefetch=1,
            grid=(t // tm,),
            in_specs=[
                pl.BlockSpec((1, TOP_K, tm), lambda i, info: (i, 0, 0), memory_space=pltpu.SMEM),
                pl.BlockSpec(memory_space=pl.ANY),
            ],
            out_specs=pl.BlockSpec(memory_space=pl.ANY),
            scratch_shapes=[
                pltpu.VMEM((2, tm, nt, 128), BF16),
                pltpu.VMEM((MOE_BLK, nt, 128), BF16),
                pltpu.SemaphoreType.DMA((2,)),
                pltpu.SemaphoreType.DMA((2,)),
                pltpu.SemaphoreType.DMA(()),
            ],
        ),
        compiler_params=_cparams(("arbitrary",)),
        name="moe_dispatch",
    )(info, dest3, u3)


def _gmm_kernel(info_ref, xs_ref, wg_hbm, wu_hbm, wd_hbm, y_ref,
                wg_f, wu_f, wd_f, wsem, wg_s, wu_s, wd_s, nswap):
    i = pl.program_id(0)
    n_used = info_ref[1, 0]
    e = info_ref[0, i]
    prev = info_ref[0, jnp.maximum(i - 1, 0)]
    fresh = (i == 0) | (e != prev)

    def fetch(expert, slot):
        return (pltpu.make_async_copy(wg_hbm.at[expert], wg_f.at[slot], wsem.at[slot]),
                pltpu.make_async_copy(wu_hbm.at[expert], wu_f.at[slot], wsem.at[slot]),
                pltpu.make_async_copy(wd_hbm.at[expert], wd_f.at[slot], wsem.at[slot]))

    @pl.when(i == 0)
    def _():
        nswap[0] = 0
        for c in fetch(e, 0):
            c.start()

    @pl.when(fresh & (i < n_used))
    def _():
        slot = nswap[0] % 2
        nswap[0] = nswap[0] + 1
        for c in fetch(e, slot):
            c.wait()
        nxt = info_ref[2, e]

        @pl.when(nxt < n_used)
        def _():
            for c in fetch(info_ref[0, nxt], 1 - slot):
                c.start(priority=1)

        wg_s[...] = wg_f[slot].astype(BF16)
        wu_s[...] = wu_f[slot].astype(BF16)
        wd_s[...] = wd_f[slot].astype(BF16)

    @pl.when(i < n_used)
    def _():
        xb = _tiles_to_rows(xs_ref[...])
        a = jnp.dot(xb, wg_s[...], preferred_element_type=F32)
        u = jnp.dot(xb, wu_s[...], preferred_element_type=F32)
        hid = (a * _sigmoid(a) * u).astype(BF16)
        y = jnp.dot(hid, wd_s[...], preferred_element_type=F32)
        y_ref[...] = _rows_to_tiles(y.astype(BF16))

    @pl.when(i >= n_used)
    def _():
        y_ref[...] = jnp.zeros_like(y_ref)


def _gmm(info, xs3, w_gate, w_up, w_down):
    p, nt, _ = xs3.shape
    d = nt * 128
    n_blk = p // MOE_BLK
    de = w_gate.shape[-1]
    return pl.pallas_call(
        _gmm_kernel,
        out_shape=jax.ShapeDtypeStruct((p, nt, 128), BF16),
        grid_spec=pltpu.PrefetchScalarGridSpec(
            num_scalar_prefetch=1,
            grid=(n_blk,),
            in_specs=[
                pl.BlockSpec((MOE_BLK, nt, 128),
                             lambda i, info: (jnp.minimum(i, info[1, 0] - 1), 0, 0)),
                pl.BlockSpec(memory_space=pl.ANY),
                pl.BlockSpec(memory_space=pl.ANY),
                pl.BlockSpec(memory_space=pl.ANY),
            ],
            out_specs=pl.BlockSpec((MOE_BLK, nt, 128), lambda i, info: (i, 0, 0)),
            scratch_shapes=[
                pltpu.VMEM((2, d, de), F32),
                pltpu.VMEM((2, d, de), F32),
                pltpu.VMEM((2, de, d), F32),
                pltpu.SemaphoreType.DMA((2,)),
                pltpu.VMEM((d, de), BF16),
                pltpu.VMEM((d, de), BF16),
                pltpu.VMEM((de, d), BF16),
                pltpu.SMEM((1,), jnp.int32),
            ],
        ),
        compiler_params=_cparams(("arbitrary",)),
        name="moe_gmm",
    )(info, xs3, w_gate, w_up, w_down)


def _final_kernel(cur_ref, nxt_ref, u_ref, h_ref, gt_ref, wt_ref, wsg_ref, wsu_ref, wsd_ref, y_hbm,
                  o_ref, ybuf_a, ybuf_b, sem, *, tm):
    i = pl.program_id(0)
    n = pl.num_programs(0)
    n_rows = TOP_K * tm

    def issue(idx_ref, tile, buf, sem_i):
        for k in range(TOP_K):
            for r in range(tm):
                pltpu.make_async_copy(y_hbm.at[idx_ref[0, tile * TOP_K + k, r]], buf.at[k * tm + r],
                                      sem.at[sem_i]).start(priority=r % 2)

    def gather_wait(buf, sem_i):
        pltpu.make_async_copy(y_hbm.at[pl.ds(0, n_rows)], buf, sem.at[sem_i]).wait()

    eye = (lax.broadcasted_iota(jnp.int32, (tm, tm), 0)
           == lax.broadcasted_iota(jnp.int32, (tm, tm), 1))

    def compute(tile, buf):
        rows = slice(tile * tm, (tile + 1) * tm)
        ub = _tiles_to_rows(u_ref[rows])
        a = jnp.dot(ub, wsg_ref[...], preferred_element_type=F32)
        b = jnp.dot(ub, wsu_ref[...], preferred_element_type=F32)
        hid = (a * _sigmoid(a) * b).astype(BF16)
        acc = jnp.dot(hid, wsd_ref[...], preferred_element_type=F32)
        wt = wt_ref[rows, :]
        wcat = jnp.concatenate([jnp.where(eye, wt[:, k:k + 1], 0.0) for k in range(TOP_K)],
                               axis=1)
        whi, wlo = _split_bf16(wcat)
        ycat = _tiles_to_rows(buf[...])
        acc = acc + jnp.dot(whi, ycat, preferred_element_type=F32)
        acc = acc + jnp.dot(wlo, ycat, preferred_element_type=F32)
        o_ref[rows, :] = h_ref[rows, :] + gt_ref[0] * acc

    @pl.when(i == 0)
    def _():
        issue(cur_ref, 0, ybuf_a, 0)

    gather_wait(ybuf_a, 0)
    issue(cur_ref, 1, ybuf_b, 1)
    compute(0, ybuf_a)
    gather_wait(ybuf_b, 1)
    issue(nxt_ref, 0, ybuf_a, 0)
    compute(1, ybuf_b)

    @pl.when(i == n - 1)
    def _():
        gather_wait(ybuf_a, 0)


def _final(dest3, u2, h1, mod3, wt, wsg, wsu, wsd, y, seq):
    t, d = h1.shape
    tm = dest3.shape[2]
    n = t // (2 * tm)
    per_b = seq // (2 * tm)
    dest_pairs = dest3.reshape(n, 2 * TOP_K, tm)
    full = lambda shape: pl.BlockSpec(shape, lambda i: tuple(0 for _ in shape))
    return pl.pallas_call(
        functools.partial(_final_kernel, tm=tm),
        out_shape=jax.ShapeDtypeStruct((t, d), F32),
        grid=(n,),
        in_specs=[
            pl.BlockSpec((1, 2 * TOP_K, tm), lambda i: (i, 0, 0), memory_space=pltpu.SMEM),
            pl.BlockSpec((1, 2 * TOP_K, tm), lambda i: (jnp.minimum(i + 1, n - 1), 0, 0),
                         memory_space=pltpu.SMEM),
            pl.BlockSpec((2 * tm, d // 128, 128), lambda i: (i, 0, 0)),
            pl.BlockSpec((2 * tm, d), lambda i: (i, 0)),
            pl.BlockSpec((1, 1, d), lambda i: (i // per_b, 0, 5)),
            pl.BlockSpec((2 * tm, TOP_K), lambda i: (i, 0)),
            full(wsg.shape),
            full(wsu.shape),
            full(wsd.shape),
            pl.BlockSpec(memory_space=pl.ANY),
        ],
        out_specs=pl.BlockSpec((2 * tm, d), lambda i: (i, 0)),
        scratch_shapes=[
            pltpu.VMEM((TOP_K * tm, d // 128, 128), BF16),
            pltpu.VMEM((TOP_K * tm, d // 128, 128), BF16),
            pltpu.SemaphoreType.DMA((2,)),
        ],
        compiler_params=_cparams(("arbitrary",)),
        name="final",
    )(dest_pairs, dest_pairs, u2, h1, mod3, wt, wsg, wsu, wsd, y)


def _layer(h2, mod3, positions, lb, p, batch, seq):
    t, d = h2.shape

    w_in_t = p["w_in"].T
    proj, cq, ckv, kr = _inproj(h2, p["norm_mix_g"][None, :], mod3,
                                w_in_t[:COL_CQ].astype(BF16), w_in_t[COL_CQ:].astype(BF16), seq)

    o_hg = _hgrn(proj, lb[None, :], p["hg_out_g"][None, :], batch, seq)

    wuq = p["mla_w_uq"].reshape(MLA_Q_RANK, MLA_HEADS, MLA_QK)
    wuq_p = jnp.pad(wuq, ((0, 0), (0, 0), (0, MLA_QK_PAD - MLA_QK))).reshape(
        MLA_Q_RANK, MLA_HEADS * MLA_QK_PAD).astype(BF16)
    wukv = p["mla_w_ukv"].astype(BF16)
    gq = jnp.pad(p["mla_q_norm_g"], (0, MLA_QK_PAD - MLA_QK))[None, :]
    gkn = p["mla_k_norm_g"][None, :MLA_NOPE]
    gkr = jnp.pad(p["mla_k_norm_g"][MLA_NOPE:], (0, 128 - MLA_ROPE))[None, :]
    half = MLA_ROPE // 2
    inv_freq = ROPE_THETA ** (-jnp.arange(half, dtype=F32) / half)
    zeros64 = jnp.zeros((128 - MLA_ROPE,), F32)
    invf = jnp.concatenate([inv_freq, inv_freq, zeros64])[None, :]
    sgn = jnp.concatenate([-jnp.ones((half,), F32), jnp.ones((half,), F32), zeros64])[None, :]
    cm = jnp.concatenate([jnp.ones((MLA_ROPE,), F32), zeros64])[None, :]
    q, k, v = _mla_prep(cq, ckv, kr, positions.reshape(t, 1), p["mla_q_a_g"][None, :],
                        p["mla_kv_a_g"][None, :], wuq_p, wukv, gq, gkn, gkr, invf, sgn, cm,
                        batch, seq)
    o_mla = _attention(q, k, v)

    wr_t = p["w_router"].T
    wr_hi = wr_t.astype(BF16)
    wr_lo = (wr_t - wr_hi.astype(F32)).astype(BF16)
    h1, u2, logits_t = _outproj(o_hg, o_mla, h2, mod3, p["mla_out_g"][None, :],
                                p["norm_ffn_g"][None, :], p["w_out"].astype(BF16),
                                wr_hi, wr_lo, seq)

    n_blk = (t * TOP_K) // MOE_BLK + N_EXPERTS
    _, top_w, dest, info = _route(logits_t, p["router_bias"][:, None], n_blk)

    def tiled(tm):
        tm = min(tm, seq)
        return dest.reshape(TOP_K, t // tm, tm).transpose(1, 0, 2)

    xs = _dispatch(info, tiled(DISPATCH_TM), u2, n_blk)
    y = _gmm(info, xs, p["w_gate"], p["w_up"], p["w_down"])
    return _final(tiled(COMBINE_TM), u2, h1, mod3, top_w.T, p["ws_gate"].astype(BF16),
                  p["ws_up"].astype(BF16), p["ws_down"].astype(BF16), y, seq)


def kernel(x, c, positions, w_ada, b_ada, norm_mix_g, norm_ffn_g, w_in, hg_lb_logits, hg_out_g,
           mla_q_a_g, mla_w_uq, mla_kv_a_g, mla_w_ukv, mla_q_norm_g, mla_k_norm_g, mla_out_g,
           w_out, w_router, router_bias, w_gate, w_up, w_down, ws_gate, ws_up, ws_down):
    batch, seq, d = x.shape
    depth = w_ada.shape[0]
    lb_all = jnp.cumsum(jax.nn.softmax(hg_lb_logits.astype(F32), axis=0), axis=0)
    c8 = jnp.pad(c, ((0, 8 - batch), (0, 0)))
    h2 = x.reshape(batch * seq, d)
    for l in range(depth):
        mod = _adaln(c8, w_ada[l], b_ada[l][None, :])[:batch]
        mod3 = mod.reshape(batch, 1, 6 * d)
        p = dict(norm_mix_g=norm_mix_g[l], norm_ffn_g=norm_ffn_g[l], w_in=w_in[l],
                 hg_out_g=hg_out_g[l], mla_q_a_g=mla_q_a_g[l], mla_w_uq=mla_w_uq[l],
                 mla_kv_a_g=mla_kv_a_g[l], mla_w_ukv=mla_w_ukv[l], mla_q_norm_g=mla_q_norm_g[l],
                 mla_k_norm_g=mla_k_norm_g[l], mla_out_g=mla_out_g[l], w_out=w_out[l],
                 w_router=w_router[l], router_bias=router_bias[l], w_gate=w_gate[l],
                 w_up=w_up[l], w_down=w_down[l], ws_gate=ws_gate[l], ws_up=ws_up[l],
                 ws_down=ws_down[l])
        h2 = _layer(h2, mod3, positions, lb_all[l], p, batch, seq)
    return h2.reshape(batch, seq, d)
```

```python
import functools

import numpy as np
import jax
import jax.numpy as jnp
from jax import lax
from jax.experimental import pallas as pl
from jax.experimental.pallas import tpu as pltpu

F32 = jnp.float32
BF16 = jnp.bfloat16

EPS = 1e-6

HG_HEADS = 8
HG_DK = 128
HG_DV = 128
HG_WIDTH = HG_HEADS * HG_DV
HG_CHUNK = 128
HG_ROW_LEVEL = 2
HG_HEADS_PER_STEP = 2
HG_UNROLL = 16

MLA_HEADS = 8
MLA_Q_RANK = 512
MLA_KV_RANK = 256
MLA_NOPE = 128
MLA_ROPE = 64
MLA_QK = MLA_NOPE + MLA_ROPE
MLA_QK_PAD = 256
MLA_V = 128
MLA_WIDTH = MLA_HEADS * MLA_V
ROPE_THETA = 10000.0

COL_CQ = 4096

N_EXPERTS = 64
TOP_K = 8
N_GROUPS = 8
TOPK_GROUPS = 4
EXPERTS_PER_GROUP = N_EXPERTS // N_GROUPS
ROUTED_SCALE = 2.5
MOE_BLK = 256

VMEM_LIMIT = 56 * 1024 * 1024

ADALN_TN = 1024
INPROJ_TM = 1024
INPROJ_TN = 1024
MLA_PREP_TM = 512
ATTN_TQ = 256
OUTPROJ_TM = 512
OUTPROJ_SUB = 2
ROUTE_TT = 512
DISPATCH_TM = 512
COMBINE_TM = 128

NT_DIMS = (((1,), (1,)), ((), ()))
TN_DIMS = (((0,), (0,)), ((), ()))


def _cparams(sem):
    return pltpu.CompilerParams(dimension_semantics=sem, vmem_limit_bytes=VMEM_LIMIT)


def _sigmoid(x):
    return 1.0 / (1.0 + jnp.exp(-x))


def _split_bf16(x):
    hi = x.astype(BF16)
    lo = (x - hi.astype(F32)).astype(BF16)
    return hi, lo


def _rows_to_tiles(x):
    n = x.shape[1] // 128
    return jnp.swapaxes(jnp.stack([x[:, s * 128:(s + 1) * 128] for s in range(n)], axis=0), 0, 1)


def _tiles_to_rows(x3):
    xt = jnp.swapaxes(x3, 0, 1)
    return jnp.concatenate([xt[s] for s in range(x3.shape[1])], axis=1)


def _adaln_kernel(c_ref, w_ref, b_ref, o_ref):
    c = c_ref[...]
    cond = c * _sigmoid(c)
    hi, lo = _split_bf16(cond)
    lhs = jnp.concatenate([hi, lo], axis=0)
    r = jnp.dot(lhs, w_ref[...].astype(BF16), preferred_element_type=F32)
    o_ref[...] = r[:8] + r[8:] + b_ref[...]


def _adaln(c8, w, b):
    d, n = w.shape
    tn = ADALN_TN
    return pl.pallas_call(
        _adaln_kernel,
        out_shape=jax.ShapeDtypeStruct((8, n), F32),
        grid=(n // tn,),
        in_specs=[
            pl.BlockSpec((8, d), lambda j: (0, 0)),
            pl.BlockSpec((d, tn), lambda j: (0, j)),
            pl.BlockSpec((1, tn), lambda j: (0, j)),
        ],
        out_specs=pl.BlockSpec((8, tn), lambda j: (0, j)),
        compiler_params=_cparams(("arbitrary",)),
        name="adaln",
    )(c8, w, b)


def _inproj_kernel(x_ref, g_ref, sc_ref, sh_ref, whg_ref, wml_ref, hg_out, cq_out, ckv_out, kr_out,
                   u_scr, *, tm, n_hg):
    j = pl.program_id(1)

    @pl.when(j == 0)
    def _():
        g = g_ref[...]
        sc = 1.0 + sc_ref[0]
        sh = sh_ref[0]

        def body(r, carry):
            rows = pl.ds(pl.multiple_of(r * 128, 128), 128)
            x = x_ref[rows, :]
            ms = jnp.mean(x * x, axis=-1, keepdims=True)
            u = x * lax.rsqrt(ms + EPS) * g * sc + sh
            u_scr[rows, :] = u.astype(BF16)
            return carry

        lax.fori_loop(0, tm // 128, body, 0)

    @pl.when(j < n_hg)
    def _():
        hg_out[...] = lax.dot_general(u_scr[...], whg_ref[...], NT_DIMS,
                                      preferred_element_type=F32)

    @pl.when(j == n_hg)
    def _():
        r = lax.dot_general(u_scr[...], wml_ref[...], NT_DIMS, preferred_element_type=F32)
        cq_out[...] = r[:, :MLA_Q_RANK]
        ckv_out[...] = r[:, MLA_Q_RANK:MLA_Q_RANK + MLA_KV_RANK]
        kr_out[...] = r[:, MLA_Q_RANK + MLA_KV_RANK:]


def _inproj(x2, g, mod3, w_hg_t, w_ml_t, seq):
    t, d = x2.shape
    tm = min(INPROJ_TM, seq)
    tn = INPROJ_TN
    n_hg = w_hg_t.shape[0] // tn
    per_b = seq // tm
    last = lambda j: jnp.minimum(j, n_hg - 1)
    return pl.pallas_call(
        functools.partial(_inproj_kernel, tm=tm, n_hg=n_hg),
        out_shape=(
            jax.ShapeDtypeStruct((t, w_hg_t.shape[0]), F32),
            jax.ShapeDtypeStruct((t, MLA_Q_RANK), F32),
            jax.ShapeDtypeStruct((t, MLA_KV_RANK), F32),
            jax.ShapeDtypeStruct((t, MLA_ROPE), F32),
        ),
        grid=(t // tm, n_hg + 1),
        in_specs=[
            pl.BlockSpec((tm, d), lambda i, j: (i, 0)),
            pl.BlockSpec((1, d), lambda i, j: (0, 0)),
            pl.BlockSpec((1, 1, d), lambda i, j: (i // per_b, 0, 1)),
            pl.BlockSpec((1, 1, d), lambda i, j: (i // per_b, 0, 0)),
            pl.BlockSpec((tn, d), lambda i, j: (last(j), 0)),
            pl.BlockSpec(w_ml_t.shape, lambda i, j: (0, 0)),
        ],
        out_specs=(
            pl.BlockSpec((tm, tn), lambda i, j: (i, last(j))),
            pl.BlockSpec((tm, MLA_Q_RANK), lambda i, j: (i, 0)),
            pl.BlockSpec((tm, MLA_KV_RANK), lambda i, j: (i, 0)),
            pl.BlockSpec((tm, MLA_ROPE), lambda i, j: (i, 0)),
        ),
        scratch_shapes=[pltpu.VMEM((tm, d), BF16)],
        compiler_params=_cparams(("arbitrary", "arbitrary")),
        name="inproj",
    )(x2, g, mod3, mod3, w_hg_t, w_ml_t)


def _hgrn_consts():
    c = HG_CHUNK
    t = np.arange(c)[:, None]
    j = np.arange(c)[None, :]
    tril = (j <= t).astype(np.float32)
    mats = [tril]
    masks = [np.eye(c, dtype=np.float32)]
    s = c // 2
    while s >= 1:
        ref = (t // (2 * s)) * (2 * s) + s - 1
        if s < HG_ROW_LEVEL:
            mats.append(tril - (j <= ref).astype(np.float32))
        masks.append((((t // s) % 2 == 1) & ((j // s) == (t // s) - 1)).astype(np.float32))
        s //= 2
    w = np.concatenate(mats, 0)
    return np.concatenate([w, w], 1), np.stack(masks, 0)


def _hgrn_head(q, v, g, kk, xs, og, m_ref, st_ref, o_ref):
    c = HG_CHUNK
    n_lvl = m_ref.shape[0]
    b = xs[0:c]
    bl = b[c - 1:c, :] - b
    st = st_ref[...]
    vb = v.astype(BF16)
    o = lax.dot_general((q * jnp.exp(b)).astype(BF16), st.astype(BF16), NT_DIMS,
                        preferred_element_type=F32)
    attn = m_ref[0] * lax.dot_general(q.astype(BF16), kk.astype(BF16), NT_DIMS,
                                      preferred_element_type=F32)
    s = c // 2
    fine = 1
    for l in range(1, n_lvl):
        if s >= HG_ROW_LEVEL:
            b3 = b.reshape(c // (2 * s), 2 * s, HG_DK)
            x = (b3 - b3[:, s - 1:s, :]).reshape(c, HG_DK)
        else:
            x = xs[fine * c:(fine + 1) * c]
            fine += 1
        s //= 2
        e = jnp.exp(-jnp.abs(x))
        attn = attn + m_ref[l] * lax.dot_general(
            (q * e).astype(BF16), (kk * e).astype(BF16), NT_DIMS, preferred_element_type=F32)
    o = o + jnp.dot(attn.astype(BF16), vb, preferred_element_type=F32)
    khat = (kk * jnp.exp(bl)).astype(BF16)
    st_ref[...] = st * jnp.exp(b[c - 1:c, :]) + lax.dot_general(
        vb, khat, TN_DIMS, preferred_element_type=F32)
    ms = jnp.mean(o * o, axis=-1, keepdims=True)
    y = o * lax.rsqrt(ms + EPS) * og * (g * _sigmoid(g))
    o_ref[...] = y.astype(BF16)


def _hgrn_kernel(q_ref, f_ref, i_ref, g_ref, lb_ref, og_ref, w_ref, m_ref, o_ref, st_scr, *,
                 seq, nh):
    c = HG_CHUNK
    st_scr[...] = jnp.zeros_like(st_scr)
    lb = lb_ref[...]
    og = og_ref[...]

    def chunk(ci, carry):
        rows = pl.ds(pl.multiple_of(ci * c, c), c)
        f = lb + (1.0 - lb) * _sigmoid(f_ref[rows, :])
        hi, lo = _split_bf16(jnp.log(f))
        xs = jnp.dot(w_ref[...], jnp.concatenate([hi, lo], axis=0), preferred_element_type=F32)
        for hh in range(nh):
            lanes = slice(hh * HG_DK, (hh + 1) * HG_DK)
            _hgrn_head(q_ref[rows, lanes], i_ref[rows, lanes], g_ref[rows, lanes],
                       1.0 - f[:, lanes], xs[:, lanes], og, m_ref, st_scr.at[hh],
                       o_ref.at[rows, lanes])
        return carry

    lax.fori_loop(0, seq // c, chunk, 0, unroll=HG_UNROLL // nh)


def _hgrn(proj, lb, og, batch, seq):
    t = proj.shape[0]
    wm, mm = _hgrn_consts()
    wm = jnp.asarray(wm, BF16)
    mm = jnp.asarray(mm, F32)
    nh = HG_HEADS_PER_STEP
    hg = HG_HEADS // nh
    wd = nh * HG_DK
    return pl.pallas_call(
        functools.partial(_hgrn_kernel, seq=seq, nh=nh),
        out_shape=jax.ShapeDtypeStruct((t, HG_WIDTH), BF16),
        grid=(batch, hg),
        in_specs=[
            pl.BlockSpec((seq, wd), lambda b, i: (b, i)),
            pl.BlockSpec((seq, wd), lambda b, i: (b, hg + i)),
            pl.BlockSpec((seq, wd), lambda b, i: (b, 2 * hg + i)),
            pl.BlockSpec((seq, wd), lambda b, i: (b, 3 * hg + i)),
            pl.BlockSpec((1, wd), lambda b, i: (0, i)),
            pl.BlockSpec((1, HG_DV), lambda b, i: (0, 0)),
            pl.BlockSpec(wm.shape, lambda b, i: (0, 0)),
            pl.BlockSpec(mm.shape, lambda b, i: (0, 0, 0)),
        ],
        out_specs=pl.BlockSpec((seq, wd), lambda b, i: (b, i)),
        scratch_shapes=[pltpu.VMEM((nh, HG_DV, HG_DK), F32)],
        compiler_params=_cparams(("arbitrary", "arbitrary")),
        name="hgrn2",
    )(proj, proj, proj, proj, lb, og, wm, mm)


def _mla_prep_kernel(cq_ref, ckv_ref, kr_ref, pos_ref, qag_ref, kvag_ref, wuq_ref, wukv_ref,
                     gq_ref, gkn_ref, gkr_ref, invf_ref, sgn_ref, cm_ref,
                     q_out, k_out, v_out):
    cq = cq_ref[...]
    cqn = cq * lax.rsqrt(jnp.mean(cq * cq, axis=-1, keepdims=True) + EPS) * qag_ref[...]
    qf = jnp.dot(cqn.astype(BF16), wuq_ref[...], preferred_element_type=F32)
    ckv = ckv_ref[...]
    ckvn = ckv * lax.rsqrt(jnp.mean(ckv * ckv, axis=-1, keepdims=True) + EPS) * kvag_ref[...]
    kvf = jnp.dot(ckvn.astype(BF16), wukv_ref[...], preferred_element_type=F32)

    ang = pos_ref[...].astype(F32) * invf_ref[...]
    cos_t = jnp.cos(ang) * cm_ref[...]
    sin_t = jnp.sin(ang) * sgn_ref[...]

    def rope(r):
        return r * cos_t + (pltpu.roll(r, 96, 1) + pltpu.roll(r, 32, 1)) * sin_t

    kr64 = kr_ref[...]
    kr = jnp.concatenate([kr64, jnp.zeros_like(kr64)], axis=-1)
    kr_ss = jnp.sum(kr * kr, axis=-1, keepdims=True)
    krr = rope(kr * gkr_ref[...])
    gq = gq_ref[...]
    gkn = gkn_ref[...]
    scale = MLA_QK ** -0.5
    for h in range(MLA_HEADS):
        qh = qf[:, h * MLA_QK_PAD:(h + 1) * MLA_QK_PAD]
        inv = lax.rsqrt(jnp.sum(qh * qh, axis=-1, keepdims=True) / MLA_QK + EPS)
        qn = qh * inv * gq
        qo = jnp.concatenate([qn[:, :MLA_NOPE], rope(qn[:, MLA_NOPE:])], axis=-1) * scale
        q_out[0, h] = qo.astype(BF16)
        kn = kvf[:, h * 256:h * 256 + MLA_NOPE]
        inv = lax.rsqrt((jnp.sum(kn * kn, axis=-1, keepdims=True) + kr_ss) / MLA_QK + EPS)
        k_out[0, h] = jnp.concatenate([kn * inv * gkn, krr * inv], axis=-1).astype(BF16)
        v_out[0, h] = kvf[:, h * 256 + MLA_NOPE:(h + 1) * 256].astype(BF16)


def _mla_prep(cq, ckv, kr, pos2, qag, kvag, wuq_p, wukv, gq, gkn, gkr, invf, sgn, cm, batch, seq):
    tm = min(MLA_PREP_TM, seq)
    per_b = seq // tm
    hh = MLA_HEADS
    full = lambda shape: pl.BlockSpec(shape, lambda b, i: tuple(0 for _ in shape))
    return pl.pallas_call(
        _mla_prep_kernel,
        out_shape=(
            jax.ShapeDtypeStruct((batch, hh, seq, MLA_QK_PAD), BF16),
            jax.ShapeDtypeStruct((batch, hh, seq, MLA_QK_PAD), BF16),
            jax.ShapeDtypeStruct((batch, hh, seq, MLA_V), BF16),
        ),
        grid=(batch, per_b),
        in_specs=[
            pl.BlockSpec((tm, MLA_Q_RANK), lambda b, i: (b * per_b + i, 0)),
            pl.BlockSpec((tm, MLA_KV_RANK), lambda b, i: (b * per_b + i, 0)),
            pl.BlockSpec((tm, MLA_ROPE), lambda b, i: (b * per_b + i, 0)),
            pl.BlockSpec((tm, 1), lambda b, i: (b * per_b + i, 0)),
            full((1, MLA_Q_RANK)),
            full((1, MLA_KV_RANK)),
            full(wuq_p.shape),
            full(wukv.shape),
            full((1, MLA_QK_PAD)),
            full((1, 128)),
            full((1, 128)),
            full((1, 128)),
            full((1, 128)),
            full((1, 128)),
        ],
        out_specs=(
            pl.BlockSpec((1, hh, tm, MLA_QK_PAD), lambda b, i: (b, 0, i, 0)),
            pl.BlockSpec((1, hh, tm, MLA_QK_PAD), lambda b, i: (b, 0, i, 0)),
            pl.BlockSpec((1, hh, tm, MLA_V), lambda b, i: (b, 0, i, 0)),
        ),
        compiler_params=_cparams(("arbitrary", "arbitrary")),
        name="mla_prep",
    )(cq, ckv, kr, pos2, qag, kvag, wuq_p, wukv, gq, gkn, gkr, invf, sgn, cm)


def _attn_kernel(q_ref, k_ref, v_ref, o_ref, *, tq, seq):
    row = lax.broadcasted_iota(jnp.int32, (tq, tq), 0)
    col = lax.broadcasted_iota(jnp.int32, (tq, tq), 1)
    causal = col <= row
    for i in range(seq // tq):
        lo = i * tq
        q = q_ref[0, 0, lo:lo + tq, :]
        sd = lax.dot_general(q, k_ref[0, 0, lo:lo + tq, :], NT_DIMS, preferred_element_type=F32)
        sd = jnp.where(causal, sd, -jnp.inf)
        m = jnp.max(sd, axis=-1, keepdims=True)
        if i > 0:
            so = lax.dot_general(q, k_ref[0, 0, :lo, :], NT_DIMS, preferred_element_type=F32)
            m = jnp.maximum(m, jnp.max(so, axis=-1, keepdims=True))
        pd = jnp.exp(sd - m)
        l = jnp.sum(pd, axis=-1, keepdims=True)
        o = jnp.dot(pd.astype(BF16), v_ref[0, 0, lo:lo + tq, :], preferred_element_type=F32)
        if i > 0:
            po = jnp.exp(so - m)
            l = l + jnp.sum(po, axis=-1, keepdims=True)
            o = o + jnp.dot(po.astype(BF16), v_ref[0, 0, :lo, :], preferred_element_type=F32)
        o_ref[lo:lo + tq, :] = o / l


def _attention(q, k, v):
    batch, hh, seq, dq = q.shape
    tq = min(ATTN_TQ, seq)
    return pl.pallas_call(
        functools.partial(_attn_kernel, tq=tq, seq=seq),
        out_shape=jax.ShapeDtypeStruct((batch * seq, hh * MLA_V), F32),
        grid=(batch, hh),
        in_specs=[
            pl.BlockSpec((1, 1, seq, dq), lambda b, h: (b, h, 0, 0)),
            pl.BlockSpec((1, 1, seq, dq), lambda b, h: (b, h, 0, 0)),
            pl.BlockSpec((1, 1, seq, MLA_V), lambda b, h: (b, h, 0, 0)),
        ],
        out_specs=pl.BlockSpec((seq, MLA_V), lambda b, h: (b, h)),
        compiler_params=_cparams(("arbitrary", "arbitrary")),
        name="mla_attn",
    )(q, k, v)


def _outproj_kernel(ohg_ref, omla_ref, x_ref, gt_ref, sc_ref, sh_ref, og_ref, g2_ref,
                    wo_ref, wrh_ref, wrl_ref, h_out, u_out, lg_out, *, n_sub):
    sub = x_ref.shape[0] // n_sub
    for j in range(n_sub):
        rows = slice(j * sub, (j + 1) * sub)
        om = omla_ref[rows, :]
        omn = om * lax.rsqrt(jnp.mean(om * om, axis=-1, keepdims=True) + EPS) * og_ref[...]
        y = jnp.dot(ohg_ref[rows, :], wo_ref[:HG_WIDTH, :], preferred_element_type=F32)
        y = y + jnp.dot(omn.astype(BF16), wo_ref[HG_WIDTH:, :], preferred_element_type=F32)
        h1 = x_ref[rows, :] + gt_ref[0] * y
        h_out[rows, :] = h1
        u = h1 * lax.rsqrt(jnp.mean(h1 * h1, axis=-1, keepdims=True) + EPS) * g2_ref[...]
        u = u * (1.0 + sc_ref[0]) + sh_ref[0]
        uh, ul = _split_bf16(u)
        u_out[rows] = _rows_to_tiles(uh)
        lg = lax.dot_general(wrh_ref[...], uh, NT_DIMS, preferred_element_type=F32)
        lg = lg + lax.dot_general(wrl_ref[...], uh, NT_DIMS, preferred_element_type=F32)
        lg = lg + lax.dot_general(wrh_ref[...], ul, NT_DIMS, preferred_element_type=F32)
        lg_out[:, rows] = lg


def _outproj(ohg, omla, x2, mod3, og, g2, wo_bf, wr_hi, wr_lo, seq):
    t, d = x2.shape
    tm = min(OUTPROJ_TM, seq)
    per_b = seq // tm
    full = lambda shape: pl.BlockSpec(shape, lambda i: tuple(0 for _ in shape))
    modspec = lambda col: pl.BlockSpec((1, 1, d), lambda i: (i // per_b, 0, col))
    return pl.pallas_call(
        functools.partial(_outproj_kernel, n_sub=OUTPROJ_SUB),
        out_shape=(
            jax.ShapeDtypeStruct((t, d), F32),
            jax.ShapeDtypeStruct((t, d // 128, 128), BF16),
            jax.ShapeDtypeStruct((N_EXPERTS, t), F32),
        ),
        grid=(t // tm,),
        in_specs=[
            pl.BlockSpec((tm, HG_WIDTH), lambda i: (i, 0)),
            pl.BlockSpec((tm, MLA_WIDTH), lambda i: (i, 0)),
            pl.BlockSpec((tm, d), lambda i: (i, 0)),
            modspec(2),
            modspec(4),
            modspec(3),
            full((1, MLA_WIDTH)),
            full((1, d)),
            full(wo_bf.shape),
            full(wr_hi.shape),
            full(wr_lo.shape),
        ],
        out_specs=(
            pl.BlockSpec((tm, d), lambda i: (i, 0)),
            pl.BlockSpec((tm, d // 128, 128), lambda i: (i, 0, 0)),
            pl.BlockSpec((N_EXPERTS, tm), lambda i: (0, i)),
        ),
        compiler_params=_cparams(("arbitrary",)),
        name="outproj",
    )(ohg, omla, x2, mod3, mod3, mod3, og, g2, wo_bf, wr_hi, wr_lo)


def _route_kernel(lg_ref, bias_ref, tri_ref, ltri_ref, idx_out, w_out, dest_out, info_out,
                  carry, base, *, tt, n_info):
    phase = pl.program_id(0)
    step = pl.program_id(1)

    @pl.when((phase == 0) & (step == 0))
    def _():
        carry[...] = jnp.zeros_like(carry)

    @pl.when((phase == 1) & (step == 0))
    def _():
        cnt = carry[...]
        padded = jnp.floor((cnt + (MOE_BLK - 1)) / MOE_BLK) * MOE_BLK
        hi, lo = _split_bf16(padded)
        ps = jnp.dot(ltri_ref[...], jnp.concatenate([hi, lo], axis=1), preferred_element_type=F32)
        pad_start = ps[:, :128] + ps[:, 128:]
        base[...] = pad_start
        carry[...] = jnp.zeros_like(carry)
        pad_end = pad_start + padded
        reps = n_info // 128
        pe = jnp.concatenate([pad_end] * reps, axis=1)
        starts = (lax.broadcasted_iota(jnp.int32, (N_EXPERTS, n_info), 1) * MOE_BLK).astype(F32)
        blk_e = jnp.sum(jnp.where(pe <= starts, 1.0, 0.0), axis=0, keepdims=True)
        blk_e = jnp.minimum(blk_e, N_EXPERTS - 1.0)
        n_used = pe[N_EXPERTS - 1:N_EXPERTS, :] / MOE_BLK
        diag = (lax.broadcasted_iota(jnp.int32, (N_EXPERTS, n_info), 0)
                == lax.broadcasted_iota(jnp.int32, (N_EXPERTS, n_info), 1))
        end_blk = jnp.sum(jnp.where(diag, pe, 0.0), axis=0, keepdims=True) / MOE_BLK
        pd = jnp.concatenate([padded] * reps, axis=1)
        num_blk = jnp.sum(jnp.where(diag, pd, 0.0), axis=0, keepdims=True) / MOE_BLK
        info_out[...] = jnp.concatenate(
            [blk_e, n_used, end_blk, num_blk, jnp.zeros((4, n_info), F32)],
            axis=0).astype(jnp.int32)

    ne, ng, eg = N_EXPERTS, N_GROUPS, EXPERTS_PER_GROUP
    neg = -jnp.inf
    sc = _sigmoid(lg_ref[...])
    sel = sc + bias_ref[...]
    sel3 = sel.reshape(ng, eg, tt)
    sub = lax.broadcasted_iota(jnp.int32, (ng, eg, tt), 1)
    m1 = jnp.max(sel3, axis=1, keepdims=True)
    first = jnp.min(jnp.where(sel3 == m1, sub, eg), axis=1, keepdims=True)
    m2 = jnp.max(jnp.where(sub == first, neg, sel3), axis=1, keepdims=True)
    gs = (m1 + m2).reshape(ng, tt)

    gio = lax.broadcasted_iota(jnp.int32, (ng, tt), 0)
    gsel = jnp.zeros((ng, tt), F32)
    for _ in range(TOPK_GROUPS):
        m = jnp.max(gs, axis=0, keepdims=True)
        gi = jnp.min(jnp.where(gs == m, gio, ng), axis=0, keepdims=True)
        hit = gio == gi
        gsel = jnp.where(hit, 1.0, gsel)
        gs = jnp.where(hit, neg, gs)
    gfull = jnp.broadcast_to(gsel.reshape(ng, 1, tt), (ng, eg, tt)).reshape(ne, tt)
    selm = jnp.where(gfull > 0.0, sel, neg)

    eio = lax.broadcasted_iota(jnp.int32, (ne, tt), 0)
    chosen = jnp.zeros((ne, tt), F32)
    idxs, ws = [], []
    for _ in range(TOP_K):
        m = jnp.max(selm, axis=0, keepdims=True)
        ei = jnp.min(jnp.where(selm == m, eio, ne), axis=0, keepdims=True)
        hit = eio == ei
        ws.append(jnp.sum(jnp.where(hit, sc, 0.0), axis=0, keepdims=True))
        idxs.append(ei)
        chosen = jnp.where(hit, 1.0, chosen)
        selm = jnp.where(hit, neg, selm)
    wsum = ws[0]
    for k in range(1, TOP_K):
        wsum = wsum + ws[k]

    tot = jnp.sum(chosen, axis=1, keepdims=True)

    @pl.when(phase == 1)
    def _():
        off = base[...][:, :1] + carry[...][:, :1]
        excl = jnp.dot(chosen.astype(BF16), tri_ref[...], preferred_element_type=F32) + off
        for k in range(TOP_K):
            idx_out[k:k + 1, :] = idxs[k]
            w_out[k:k + 1, :] = ws[k] / wsum * ROUTED_SCALE
            dk = jnp.sum(jnp.where(eio == idxs[k], excl, 0.0), axis=0, keepdims=True)
            dest_out[k:k + 1, :] = dk.astype(jnp.int32)

    carry[...] = carry[...] + tot


def _route(logits_t, bias_col, n_blk):
    ne, t = logits_t.shape
    tt = min(ROUTE_TT, t)
    n_info = -(-n_blk // 128) * 128
    tri = jnp.asarray(np.triu(np.ones((tt, tt), np.float32), 1), BF16)
    ltri = jnp.asarray(np.tril(np.ones((ne, ne), np.float32), -1), BF16)
    return pl.pallas_call(
        functools.partial(_route_kernel, tt=tt, n_info=n_info),
        out_shape=(
            jax.ShapeDtypeStruct((TOP_K, t), jnp.int32),
            jax.ShapeDtypeStruct((TOP_K, t), F32),
            jax.ShapeDtypeStruct((TOP_K, t), jnp.int32),
            jax.ShapeDtypeStruct((8, n_info), jnp.int32),
        ),
        grid=(2, t // tt),
        in_specs=[
            pl.BlockSpec((ne, tt), lambda p, i: (0, i)),
            pl.BlockSpec((ne, 1), lambda p, i: (0, 0)),
            pl.BlockSpec((tt, tt), lambda p, i: (0, 0)),
            pl.BlockSpec((ne, ne), lambda p, i: (0, 0)),
        ],
        out_specs=(
            pl.BlockSpec((TOP_K, tt), lambda p, i: (0, i * p)),
            pl.BlockSpec((TOP_K, tt), lambda p, i: (0, i * p)),
            pl.BlockSpec((TOP_K, tt), lambda p, i: (0, i * p)),
            pl.BlockSpec((8, n_info), lambda p, i: (0, 0)),
        ),
        scratch_shapes=[pltpu.VMEM((ne, 128), F32), pltpu.VMEM((ne, 128), F32)],
        compiler_params=_cparams(("arbitrary", "arbitrary")),
        name="route",
    )(logits_t, bias_col, tri, ltri)


def _dispatch_kernel(info_ref, dest_ref, u_hbm, xs_hbm, stage, zbuf, lsem, ssem, zsem, *, tm, n_blk):
    i = pl.program_id(0)
    n = pl.num_programs(0)
    slot = i % 2
    n_used = info_ref[1, 0]

    def load(tile, dst_slot):
        return pltpu.make_async_copy(u_hbm.at[pl.ds(tile * tm, tm)], stage.at[dst_slot],
                                     lsem.at[dst_slot])

    def zero_block(blk):
        return pltpu.make_async_copy(zbuf, xs_hbm.at[pl.ds(blk * MOE_BLK, MOE_BLK)], zsem)

    @pl.when(i == 0)
    def _():
        load(0, 0).start()
        zbuf[...] = jnp.zeros_like(zbuf)
        for e in range(N_EXPERTS):
            @pl.when(info_ref[3, e] > 0)
            def _():
                zero_block(info_ref[2, e] - 1).start()

        def zero_unused(j, carry):
            zero_block(j).start()
            return carry
        lax.fori_loop(n_used, n_blk, zero_unused, 0)
        for e in range(N_EXPERTS):
            @pl.when(info_ref[3, e] > 0)
            def _():
                zero_block(0).wait()

        def wait_unused(j, carry):
            zero_block(0).wait()
            return carry
        lax.fori_loop(n_used, n_blk, wait_unused, 0)

    def scatter_wait(src_slot):
        pltpu.make_async_copy(stage.at[src_slot], xs_hbm.at[pl.ds(0, tm)], ssem.at[src_slot]).wait()

    @pl.when(i >= 1)
    def _():
        for _ in range(TOP_K):
            scatter_wait(1 - slot)

    @pl.when(i + 1 < n)
    def _():
        load(i + 1, 1 - slot).start()

    load(i, slot).wait()
    for k in range(TOP_K):
        for r in range(tm):
            pltpu.make_async_copy(stage.at[slot, r], xs_hbm.at[dest_ref[0, k, r]],
                                  ssem.at[slot]).start(priority=r % 2)

    @pl.when(i == n - 1)
    def _():
        for _ in range(TOP_K):
            scatter_wait(slot)


def _dispatch(info, dest3, u3, n_blk):
    t, nt, _ = u3.shape
    tm = dest3.shape[2]
    return pl.pallas_call(
        functools.partial(_dispatch_kernel, tm=tm, n_blk=n_blk),
        out_shape=jax.ShapeDtypeStruct((n_blk * MOE_BLK, nt, 128), BF16),
        grid_spec=pltpu.PrefetchScalarGridSpec(
            num_scalar_prefetch=1,
            grid=(t // tm,),
            in_specs=[
                pl.BlockSpec((1, TOP_K, tm), lambda i, info: (i, 0, 0), memory_space=pltpu.SMEM),
                pl.BlockSpec(memory_space=pl.ANY),
            ],
            out_specs=pl.BlockSpec(memory_space=pl.ANY),
            scratch_shapes=[
                pltpu.VMEM((2, tm, nt, 128), BF16),
                pltpu.VMEM((MOE_BLK, nt, 128), BF16),
                pltpu.SemaphoreType.DMA((2,)),
                pltpu.SemaphoreType.DMA((2,)),
                pltpu.SemaphoreType.DMA(()),
            ],
        ),
        compiler_params=_cparams(("arbitrary",)),
        name="moe_dispatch",
    )(info, dest3, u3)


def _gmm_kernel(info_ref, xs_ref, wg_hbm, wu_hbm, wd_hbm, y_ref,
                wg_f, wu_f, wd_f, wsem, wg_s, wu_s, wd_s, nswap):
    i = pl.program_id(0)
    n_used = info_ref[1, 0]
    e = info_ref[0, i]
    prev = info_ref[0, jnp.maximum(i - 1, 0)]
    fresh = (i == 0) | (e != prev)

    def fetch(expert, slot):
        return (pltpu.make_async_copy(wg_hbm.at[expert], wg_f.at[slot], wsem.at[slot]),
                pltpu.make_async_copy(wu_hbm.at[expert], wu_f.at[slot], wsem.at[slot]),
                pltpu.make_async_copy(wd_hbm.at[expert], wd_f.at[slot], wsem.at[slot]))

    @pl.when(i == 0)
    def _():
        nswap[0] = 0
        for c in fetch(e, 0):
            c.start()

    @pl.when(fresh & (i < n_used))
    def _():
        slot = nswap[0] % 2
        nswap[0] = nswap[0] + 1
        for c in fetch(e, slot):
            c.wait()
        nxt = info_ref[2, e]

        @pl.when(nxt < n_used)
        def _():
            for c in fetch(info_ref[0, nxt], 1 - slot):
                c.start(priority=1)

        wg_s[...] = wg_f[slot].astype(BF16)
        wu_s[...] = wu_f[slot].astype(BF16)
        wd_s[...] = wd_f[slot].astype(BF16)

    @pl.when(i < n_used)
    def _():
        xb = _tiles_to_rows(xs_ref[...])
        a = jnp.dot(xb, wg_s[...], preferred_element_type=F32)
        u = jnp.dot(xb, wu_s[...], preferred_element_type=F32)
        hid = (a * _sigmoid(a) * u).astype(BF16)
        y = jnp.dot(hid, wd_s[...], preferred_element_type=F32)
        y_ref[...] = _rows_to_tiles(y.astype(BF16))

    @pl.when(i >= n_used)
    def _():
        y_ref[...] = jnp.zeros_like(y_ref)


def _gmm(info, xs3, w_gate, w_up, w_down):
    p, nt, _ = xs3.shape
    d = nt * 128
    n_blk = p // MOE_BLK
    de = w_gate.shape[-1]
    return pl.pallas_call(
        _gmm_kernel,
        out_shape=jax.ShapeDtypeStruct((p, nt, 128), BF16),
        grid_spec=pltpu.PrefetchScalarGridSpec(
            num_scalar_prefetch=1,
            grid=(n_blk,),
            in_specs=[
                pl.BlockSpec((MOE_BLK, nt, 128),
                             lambda i, info: (jnp.minimum(i, info[1, 0] - 1), 0, 0)),
                pl.BlockSpec(memory_space=pl.ANY),
                pl.BlockSpec(memory_space=pl.ANY),
                pl.BlockSpec(memory_space=pl.ANY),
            ],
            out_specs=pl.BlockSpec((MOE_BLK, nt, 128), lambda i, info: (i, 0, 0)),
            scratch_shapes=[
                pltpu.VMEM((2, d, de), F32),
                pltpu.VMEM((2, d, de), F32),
                pltpu.VMEM((2, de, d), F32),
                pltpu.SemaphoreType.DMA((2,)),
                pltpu.VMEM((d, de), BF16),
                pltpu.VMEM((d, de), BF16),
                pltpu.VMEM((de, d), BF16),
                pltpu.SMEM((1,), jnp.int32),
            ],
        ),
        compiler_params=_cparams(("arbitrary",)),
        name="moe_gmm",
    )(info, xs3, w_gate, w_up, w_down)


def _final_kernel(cur_ref, nxt_ref, u_ref, h_ref, gt_ref, wt_ref, wsg_ref, wsu_ref, wsd_ref, y_hbm,
                  o_ref, ybuf_a, ybuf_b, sem, *, tm):
    i = pl.program_id(0)
    n = pl.num_programs(0)
    n_rows = TOP_K * tm

    def issue(idx_ref, tile, buf, sem_i):
        for k in range(TOP_K):
            for r in range(tm):
                pltpu.make_async_copy(y_hbm.at[idx_ref[0, tile * TOP_K + k, r]], buf.at[k * tm + r],
                                      sem.at[sem_i]).start(priority=r % 2)

    def gather_wait(buf, sem_i):
        pltpu.make_async_copy(y_hbm.at[pl.ds(0, n_rows)], buf, sem.at[sem_i]).wait()

    eye = (lax.broadcasted_iota(jnp.int32, (tm, tm), 0)
           == lax.broadcasted_iota(jnp.int32, (tm, tm), 1))

    def compute(tile, buf):
        rows = slice(tile * tm, (tile + 1) * tm)
        ub = _tiles_to_rows(u_ref[rows])
        a = jnp.dot(ub, wsg_ref[...], preferred_element_type=F32)
        b = jnp.dot(ub, wsu_ref[...], preferred_element_type=F32)
        hid = (a * _sigmoid(a) * b).astype(BF16)
        acc = jnp.dot(hid, wsd_ref[...], preferred_element_type=F32)
        wt = wt_ref[rows, :]
        wcat = jnp.concatenate([jnp.where(eye, wt[:, k:k + 1], 0.0) for k in range(TOP_K)],
                               axis=1)
        whi, wlo = _split_bf16(wcat)
        ycat = _tiles_to_rows(buf[...])
        acc = acc + jnp.dot(whi, ycat, preferred_element_type=F32)
        acc = acc + jnp.dot(wlo, ycat, preferred_element_type=F32)
        o_ref[rows, :] = h_ref[rows, :] + gt_ref[0] * acc

    @pl.when(i == 0)
    def _():
        issue(cur_ref, 0, ybuf_a, 0)

    gather_wait(ybuf_a, 0)
    issue(cur_ref, 1, ybuf_b, 1)
    compute(0, ybuf_a)
    gather_wait(ybuf_b, 1)
    issue(nxt_ref, 0, ybuf_a, 0)
    compute(1, ybuf_b)

    @pl.when(i == n - 1)
    def _():
        gather_wait(ybuf_a, 0)


def _final(dest3, u2, h1, mod3, wt, wsg, wsu, wsd, y, seq):
    t, d = h1.shape
    tm = dest3.shape[2]
    n = t // (2 * tm)
    per_b = seq // (2 * tm)
    dest_pairs = dest3.reshape(n, 2 * TOP_K, tm)
    full = lambda shape: pl.BlockSpec(shape, lambda i: tuple(0 for _ in shape))
    return pl.pallas_call(
        functools.partial(_final_kernel, tm=tm),
        out_shape=jax.ShapeDtypeStruct((t, d), F32),
        grid=(n,),
        in_specs=[
            pl.BlockSpec((1, 2 * TOP_K, tm), lambda i: (i, 0, 0), memory_space=pltpu.SMEM),
            pl.BlockSpec((1, 2 * TOP_K, tm), lambda i: (jnp.minimum(i + 1, n - 1), 0, 0),
                         memory_space=pltpu.SMEM),
            pl.BlockSpec((2 * tm, d // 128, 128), lambda i: (i, 0, 0)),
            pl.BlockSpec((2 * tm, d), lambda i: (i, 0)),
            pl.BlockSpec((1, 1, d), lambda i: (i // per_b, 0, 5)),
            pl.BlockSpec((2 * tm, TOP_K), lambda i: (i, 0)),
            full(wsg.shape),
            full(wsu.shape),
            full(wsd.shape),
            pl.BlockSpec(memory_space=pl.ANY),
        ],
        out_specs=pl.BlockSpec((2 * tm, d), lambda i: (i, 0)),
        scratch_shapes=[
            pltpu.VMEM((TOP_K * tm, d // 128, 128), BF16),
            pltpu.VMEM((TOP_K * tm, d // 128, 128), BF16),
            pltpu.SemaphoreType.DMA((2,)),
        ],
        compiler_params=_cparams(("arbitrary",)),
        name="final",
    )(dest_pairs, dest_pairs, u2, h1, mod3, wt, wsg, wsu, wsd, y)


def _layer(h2, mod3, positions, lb, p, batch, seq):
    t, d = h2.shape

    w_in_t = p["w_in"].T
    proj, cq, ckv, kr = _inproj(h2, p["norm_mix_g"][None, :], mod3,
                                w_in_t[:COL_CQ].astype(BF16), w_in_t[COL_CQ:].astype(BF16), seq)

    o_hg = _hgrn(proj, lb[None, :], p["hg_out_g"][None, :], batch, seq)

    wuq = p["mla_w_uq"].reshape(MLA_Q_RANK, MLA_HEADS, MLA_QK)
    wuq_p = jnp.pad(wuq, ((0, 0), (0, 0), (0, MLA_QK_PAD - MLA_QK))).reshape(
        MLA_Q_RANK, MLA_HEADS * MLA_QK_PAD).astype(BF16)
    wukv = p["mla_w_ukv"].astype(BF16)
    gq = jnp.pad(p["mla_q_norm_g"], (0, MLA_QK_PAD - MLA_QK))[None, :]
    gkn = p["mla_k_norm_g"][None, :MLA_NOPE]
    gkr = jnp.pad(p["mla_k_norm_g"][MLA_NOPE:], (0, 128 - MLA_ROPE))[None, :]
    half = MLA_ROPE // 2
    inv_freq = ROPE_THETA ** (-jnp.arange(half, dtype=F32) / half)
    zeros64 = jnp.zeros((128 - MLA_ROPE,), F32)
    invf = jnp.concatenate([inv_freq, inv_freq, zeros64])[None, :]
    sgn = jnp.concatenate([-jnp.ones((half,), F32), jnp.ones((half,), F32), zeros64])[None, :]
    cm = jnp.concatenate([jnp.ones((MLA_ROPE,), F32), zeros64])[None, :]
    q, k, v = _mla_prep(cq, ckv, kr, positions.reshape(t, 1), p["mla_q_a_g"][None, :],
                        p["mla_kv_a_g"][None, :], wuq_p, wukv, gq, gkn, gkr, invf, sgn, cm,
                        batch, seq)
    o_mla = _attention(q, k, v)

    wr_t = p["w_router"].T
    wr_hi = wr_t.astype(BF16)
    wr_lo = (wr_t - wr_hi.astype(F32)).astype(BF16)
    h1, u2, logits_t = _outproj(o_hg, o_mla, h2, mod3, p["mla_out_g"][None, :],
                                p["norm_ffn_g"][None, :], p["w_out"].astype(BF16),
                                wr_hi, wr_lo, seq)

    n_blk = (t * TOP_K) // MOE_BLK + N_EXPERTS
    _, top_w, dest, info = _route(logits_t, p["router_bias"][:, None], n_blk)

    def tiled(tm):
        tm = min(tm, seq)
        return dest.reshape(TOP_K, t // tm, tm).transpose(1, 0, 2)

    xs = _dispatch(info, tiled(DISPATCH_TM), u2, n_blk)
    y = _gmm(info, xs, p["w_gate"], p["w_up"], p["w_down"])
    return _final(tiled(COMBINE_TM), u2, h1, mod3, top_w.T, p["ws_gate"].astype(BF16),
                  p["ws_up"].astype(BF16), p["ws_down"].astype(BF16), y, seq)


def kernel(x, c, positions, w_ada, b_ada, norm_mix_g, norm_ffn_g, w_in, hg_lb_logits, hg_out_g,
           mla_q_a_g, mla_w_uq, mla_kv_a_g, mla_w_ukv, mla_q_norm_g, mla_k_norm_g, mla_out_g,
           w_out, w_router, router_bias, w_gate, w_up, w_down, ws_gate, ws_up, ws_down):
    batch, seq, d = x.shape
    depth = w_ada.shape[0]
    lb_all = jnp.cumsum(jax.nn.softmax(hg_lb_logits.astype(F32), axis=0), axis=0)
    c8 = jnp.pad(c, ((0, 8 - batch), (0, 0)))
    h2 = x.reshape(batch * seq, d)
    for l in range(depth):
        mod = _adaln(c8, w_ada[l], b_ada[l][None, :])[:batch]
        mod3 = mod.reshape(batch, 1, 6 * d)
        p = dict(norm_mix_g=norm_mix_g[l], norm_ffn_g=norm_ffn_g[l], w_in=w_in[l],
                 hg_out_g=hg_out_g[l], mla_q_a_g=mla_q_a_g[l], mla_w_uq=mla_w_uq[l],
                 mla_kv_a_g=mla_kv_a_g[l], mla_w_ukv=mla_w_ukv[l], mla_q_norm_g=mla_q_norm_g[l],
                 mla_k_norm_g=mla_k_norm_g[l], mla_out_g=mla_out_g[l], w_out=w_out[l],
                 w_router=w_router[l], router_bias=router_bias[l], w_gate=w_gate[l],
                 w_up=w_up[l], w_down=w_down[l], ws_gate=ws_gate[l], ws_up=ws_up[l],
                 ws_down=ws_down[l])
        h2 = _layer(h2, mod3, positions, lb_all[l], p, batch, seq)
    return h2.reshape(batch, seq, d)
```

```python
import functools

import numpy as np
import jax
import jax.numpy as jnp
from jax import lax
from jax.experimental import pallas as pl
from jax.experimental.pallas import tpu as pltpu

F32 = jnp.float32
BF16 = jnp.bfloat16

EPS = 1e-6

HG_HEADS = 8
HG_DK = 128
HG_DV = 128
HG_WIDTH = HG_HEADS * HG_DV
HG_CHUNK = 128
HG_ROW_LEVEL = 2
HG_HEADS_PER_STEP = 2
HG_UNROLL = 16

MLA_HEADS = 8
MLA_Q_RANK = 512
MLA_KV_RANK = 256
MLA_NOPE = 128
MLA_ROPE = 64
MLA_QK = MLA_NOPE + MLA_ROPE
MLA_QK_PAD = 256
MLA_V = 128
MLA_WIDTH = MLA_HEADS * MLA_V
ROPE_THETA = 10000.0

COL_CQ = 4096

N_EXPERTS = 64
TOP_K = 8
N_GROUPS = 8
TOPK_GROUPS = 4
EXPERTS_PER_GROUP = N_EXPERTS // N_GROUPS
ROUTED_SCALE = 2.5
MOE_BLK = 256

VMEM_LIMIT = 56 * 1024 * 1024

ADALN_TN = 1024
INPROJ_TM = 1024
INPROJ_TN = 1024
MLA_PREP_TM = 512
ATTN_TQ = 256
OUTPROJ_TM = 512
OUTPROJ_SUB = 2
ROUTE_TT = 512
DISPATCH_TM = 512
COMBINE_TM = 128

NT_DIMS = (((1,), (1,)), ((), ()))
TN_DIMS = (((0,), (0,)), ((), ()))


def _cparams(sem):
    return pltpu.CompilerParams(dimension_semantics=sem, vmem_limit_bytes=VMEM_LIMIT)


def _sigmoid(x):
    return 1.0 / (1.0 + jnp.exp(-x))


def _split_bf16(x):
    hi = x.astype(BF16)
    lo = (x - hi.astype(F32)).astype(BF16)
    return hi, lo


def _rows_to_tiles(x):
    n = x.shape[1] // 128
    return jnp.swapaxes(jnp.stack([x[:, s * 128:(s + 1) * 128] for s in range(n)], axis=0), 0, 1)


def _tiles_to_rows(x3):
    xt = jnp.swapaxes(x3, 0, 1)
    return jnp.concatenate([xt[s] for s in range(x3.shape[1])], axis=1)


def _adaln_kernel(c_ref, w_ref, b_ref, o_ref):
    c = c_ref[...]
    cond = c * _sigmoid(c)
    hi, lo = _split_bf16(cond)
    lhs = jnp.concatenate([hi, lo], axis=0)
    r = jnp.dot(lhs, w_ref[...].astype(BF16), preferred_element_type=F32)
    o_ref[...] = r[:8] + r[8:] + b_ref[...]


def _adaln(c8, w, b):
    d, n = w.shape
    tn = ADALN_TN
    return pl.pallas_call(
        _adaln_kernel,
        out_shape=jax.ShapeDtypeStruct((8, n), F32),
        grid=(n // tn,),
        in_specs=[
            pl.BlockSpec((8, d), lambda j: (0, 0)),
            pl.BlockSpec((d, tn), lambda j: (0, j)),
            pl.BlockSpec((1, tn), lambda j: (0, j)),
        ],
        out_specs=pl.BlockSpec((8, tn), lambda j: (0, j)),
        compiler_params=_cparams(("arbitrary",)),
        name="adaln",
    )(c8, w, b)


def _inproj_kernel(x_ref, g_ref, sc_ref, sh_ref, whg_ref, wml_ref, hg_out, cq_out, ckv_out, kr_out,
                   u_scr, *, tm, n_hg):
    j = pl.program_id(1)

    @pl.when(j == 0)
    def _():
        g = g_ref[...]
        sc = 1.0 + sc_ref[0]
        sh = sh_ref[0]

        def body(r, carry):
            rows = pl.ds(pl.multiple_of(r * 128, 128), 128)
            x = x_ref[rows, :]
            ms = jnp.mean(x * x, axis=-1, keepdims=True)
            u = x * lax.rsqrt(ms + EPS) * g * sc + sh
            u_scr[rows, :] = u.astype(BF16)
            return carry

        lax.fori_loop(0, tm // 128, body, 0)

    @pl.when(j < n_hg)
    def _():
        hg_out[...] = lax.dot_general(u_scr[...], whg_ref[...], NT_DIMS,
                                      preferred_element_type=F32)

    @pl.when(j == n_hg)
    def _():
        r = lax.dot_general(u_scr[...], wml_ref[...], NT_DIMS, preferred_element_type=F32)
        cq_out[...] = r[:, :MLA_Q_RANK]
        ckv_out[...] = r[:, MLA_Q_RANK:MLA_Q_RANK + MLA_KV_RANK]
        kr_out[...] = r[:, MLA_Q_RANK + MLA_KV_RANK:]


def _inproj(x2, g, mod3, w_hg_t, w_ml_t, seq):
    t, d = x2.shape
    tm = min(INPROJ_TM, seq)
    tn = INPROJ_TN
    n_hg = w_hg_t.shape[0] // tn
    per_b = seq // tm
    last = lambda j: jnp.minimum(j, n_hg - 1)
    return pl.pallas_call(
        functools.partial(_inproj_kernel, tm=tm, n_hg=n_hg),
        out_shape=(
            jax.ShapeDtypeStruct((t, w_hg_t.shape[0]), F32),
            jax.ShapeDtypeStruct((t, MLA_Q_RANK), F32),
            jax.ShapeDtypeStruct((t, MLA_KV_RANK), F32),
            jax.ShapeDtypeStruct((t, MLA_ROPE), F32),
        ),
        grid=(t // tm, n_hg + 1),
        in_specs=[
            pl.BlockSpec((tm, d), lambda i, j: (i, 0)),
            pl.BlockSpec((1, d), lambda i, j: (0, 0)),
            pl.BlockSpec((1, 1, d), lambda i, j: (i // per_b, 0, 1)),
            pl.BlockSpec((1, 1, d), lambda i, j: (i // per_b, 0, 0)),
            pl.BlockSpec((tn, d), lambda i, j: (last(j), 0)),
            pl.BlockSpec(w_ml_t.shape, lambda i, j: (0, 0)),
        ],
        out_specs=(
            pl.BlockSpec((tm, tn), lambda i, j: (i, last(j))),
            pl.BlockSpec((tm, MLA_Q_RANK), lambda i, j: (i, 0)),
            pl.BlockSpec((tm, MLA_KV_RANK), lambda i, j: (i, 0)),
            pl.BlockSpec((tm, MLA_ROPE), lambda i, j: (i, 0)),
        ),
        scratch_shapes=[pltpu.VMEM((tm, d), BF16)],
        compiler_params=_cparams(("arbitrary", "arbitrary")),
        name="inproj",
    )(x2, g, mod3, mod3, w_hg_t, w_ml_t)


def _hgrn_consts():
    c = HG_CHUNK
    t = np.arange(c)[:, None]
    j = np.arange(c)[None, :]
    tril = (j <= t).astype(np.float32)
    mats = [tril]
    masks = [np.eye(c, dtype=np.float32)]
    s = c // 2
    while s >= 1:
        ref = (t // (2 * s)) * (2 * s) + s - 1
        if s < HG_ROW_LEVEL:
            mats.append(tril - (j <= ref).astype(np.float32))
        masks.append((((t // s) % 2 == 1) & ((j // s) == (t // s) - 1)).astype(np.float32))
        s //= 2
    w = np.concatenate(mats, 0)
    return np.concatenate([w, w], 1), np.stack(masks, 0)


def _hgrn_head(q, v, g, kk, xs, og, m_ref, st_ref, o_ref):
    c = HG_CHUNK
    n_lvl = m_ref.shape[0]
    b = xs[0:c]
    bl = b[c - 1:c, :] - b
    st = st_ref[...]
    vb = v.astype(BF16)
    o = lax.dot_general((q * jnp.exp(b)).astype(BF16), st.astype(BF16), NT_DIMS,
                        preferred_element_type=F32)
    attn = m_ref[0] * lax.dot_general(q.astype(BF16), kk.astype(BF16), NT_DIMS,
                                      preferred_element_type=F32)
    s = c // 2
    fine = 1
    for l in range(1, n_lvl):
        if s >= HG_ROW_LEVEL:
            b3 = b.reshape(c // (2 * s), 2 * s, HG_DK)
            x = (b3 - b3[:, s - 1:s, :]).reshape(c, HG_DK)
        else:
            x = xs[fine * c:(fine + 1) * c]
            fine += 1
        s //= 2
        e = jnp.exp(-jnp.abs(x))
        attn = attn + m_ref[l] * lax.dot_general(
            (q * e).astype(BF16), (kk * e).astype(BF16), NT_DIMS, preferred_element_type=F32)
    o = o + jnp.dot(attn.astype(BF16), vb, preferred_element_type=F32)
    khat = (kk * jnp.exp(bl)).astype(BF16)
    st_ref[...] = st * jnp.exp(b[c - 1:c, :]) + lax.dot_general(
        vb, khat, TN_DIMS, preferred_element_type=F32)
    ms = jnp.mean(o * o, axis=-1, keepdims=True)
    y = o * lax.rsqrt(ms + EPS) * og * (g * _sigmoid(g))
    o_ref[...] = y.astype(BF16)


def _hgrn_kernel(q_ref, f_ref, i_ref, g_ref, lb_ref, og_ref, w_ref, m_ref, o_ref, st_scr, *,
                 seq, nh):
    c = HG_CHUNK
    st_scr[...] = jnp.zeros_like(st_scr)
    lb = lb_ref[...]
    og = og_ref[...]

    def chunk(ci, carry):
        rows = pl.ds(pl.multiple_of(ci * c, c), c)
        f = lb + (1.0 - lb) * _sigmoid(f_ref[rows, :])
        hi, lo = _split_bf16(jnp.log(f))
        xs = jnp.dot(w_ref[...], jnp.concatenate([hi, lo], axis=0), preferred_element_type=F32)
        for hh in range(nh):
            lanes = slice(hh * HG_DK, (hh + 1) * HG_DK)
            _hgrn_head(q_ref[rows, lanes], i_ref[rows, lanes], g_ref[rows, lanes],
                       1.0 - f[:, lanes], xs[:, lanes], og, m_ref, st_scr.at[hh],
                       o_ref.at[rows, lanes])
        return carry

    lax.fori_loop(0, seq // c, chunk, 0, unroll=HG_UNROLL // nh)


def _hgrn(proj, lb, og, batch, seq):
    t = proj.shape[0]
    wm, mm = _hgrn_consts()
    wm = jnp.asarray(wm, BF16)
    mm = jnp.asarray(mm, F32)
    nh = HG_HEADS_PER_STEP
    hg = HG_HEADS // nh
    wd = nh * HG_DK
    return pl.pallas_call(
        functools.partial(_hgrn_kernel, seq=seq, nh=nh),
        out_shape=jax.ShapeDtypeStruct((t, HG_WIDTH), BF16),
        grid=(batch, hg),
        in_specs=[
            pl.BlockSpec((seq, wd), lambda b, i: (b, i)),
            pl.BlockSpec((seq, wd), lambda b, i: (b, hg + i)),
            pl.BlockSpec((seq, wd), lambda b, i: (b, 2 * hg + i)),
            pl.BlockSpec((seq, wd), lambda b, i: (b, 3 * hg + i)),
            pl.BlockSpec((1, wd), lambda b, i: (0, i)),
            pl.BlockSpec((1, HG_DV), lambda b, i: (0, 0)),
            pl.BlockSpec(wm.shape, lambda b, i: (0, 0)),
            pl.BlockSpec(mm.shape, lambda b, i: (0, 0, 0)),
        ],
        out_specs=pl.BlockSpec((seq, wd), lambda b, i: (b, i)),
        scratch_shapes=[pltpu.VMEM((nh, HG_DV, HG_DK), F32)],
        compiler_params=_cparams(("arbitrary", "arbitrary")),
        name="hgrn2",
    )(proj, proj, proj, proj, lb, og, wm, mm)


def _mla_prep_kernel(cq_ref, ckv_ref, kr_ref, pos_ref, qag_ref, kvag_ref, wuq_ref, wukv_ref,
                     gq_ref, gkn_ref, gkr_ref, invf_ref, sgn_ref, cm_ref,
                     q_out, k_out, v_out):
    cq = cq_ref[...]
    cqn = cq * lax.rsqrt(jnp.mean(cq * cq, axis=-1, keepdims=True) + EPS) * qag_ref[...]
    qf = jnp.dot(cqn.astype(BF16), wuq_ref[...], preferred_element_type=F32)
    ckv = ckv_ref[...]
    ckvn = ckv * lax.rsqrt(jnp.mean(ckv * ckv, axis=-1, keepdims=True) + EPS) * kvag_ref[...]
    kvf = jnp.dot(ckvn.astype(BF16), wukv_ref[...], preferred_element_type=F32)

    ang = pos_ref[...].astype(F32) * invf_ref[...]
    cos_t = jnp.cos(ang) * cm_ref[...]
    sin_t = jnp.sin(ang) * sgn_ref[...]

    def rope(r):
        return r * cos_t + (pltpu.roll(r, 96, 1) + pltpu.roll(r, 32, 1)) * sin_t

    kr64 = kr_ref[...]
    kr = jnp.concatenate([kr64, jnp.zeros_like(kr64)], axis=-1)
    kr_ss = jnp.sum(kr * kr, axis=-1, keepdims=True)
    krr = rope(kr * gkr_ref[...])
    gq = gq_ref[...]
    gkn = gkn_ref[...]
    scale = MLA_QK ** -0.5
    for h in range(MLA_HEADS):
        qh = qf[:, h * MLA_QK_PAD:(h + 1) * MLA_QK_PAD]
        inv = lax.rsqrt(jnp.sum(qh * qh, axis=-1, keepdims=True) / MLA_QK + EPS)
        qn = qh * inv * gq
        qo = jnp.concatenate([qn[:, :MLA_NOPE], rope(qn[:, MLA_NOPE:])], axis=-1) * scale
        q_out[0, h] = qo.astype(BF16)
        kn = kvf[:, h * 256:h * 256 + MLA_NOPE]
        inv = lax.rsqrt((jnp.sum(kn * kn, axis=-1, keepdims=True) + kr_ss) / MLA_QK + EPS)
        k_out[0, h] = jnp.concatenate([kn * inv * gkn, krr * inv], axis=-1).astype(BF16)
        v_out[0, h] = kvf[:, h * 256 + MLA_NOPE:(h + 1) * 256].astype(BF16)


def _mla_prep(cq, ckv, kr, pos2, qag, kvag, wuq_p, wukv, gq, gkn, gkr, invf, sgn, cm, batch, seq):
    tm = min(MLA_PREP_TM, seq)
    per_b = seq // tm
    hh = MLA_HEADS
    full = lambda shape: pl.BlockSpec(shape, lambda b, i: tuple(0 for _ in shape))
    return pl.pallas_call(
        _mla_prep_kernel,
        out_shape=(
            jax.ShapeDtypeStruct((batch, hh, seq, MLA_QK_PAD), BF16),
            jax.ShapeDtypeStruct((batch, hh, seq, MLA_QK_PAD), BF16),
            jax.ShapeDtypeStruct((batch, hh, seq, MLA_V), BF16),
        ),
        grid=(batch, per_b),
        in_specs=[
            pl.BlockSpec((tm, MLA_Q_RANK), lambda b, i: (b * per_b + i, 0)),
            pl.BlockSpec((tm, MLA_KV_RANK), lambda b, i: (b * per_b + i, 0)),
            pl.BlockSpec((tm, MLA_ROPE), lambda b, i: (b * per_b + i, 0)),
            pl.BlockSpec((tm, 1), lambda b, i: (b * per_b + i, 0)),
            full((1, MLA_Q_RANK)),
            full((1, MLA_KV_RANK)),
            full(wuq_p.shape),
            full(wukv.shape),
            full((1, MLA_QK_PAD)),
            full((1, 128)),
            full((1, 128)),
            full((1, 128)),
            full((1, 128)),
            full((1, 128)),
        ],
        out_specs=(
            pl.BlockSpec((1, hh, tm, MLA_QK_PAD), lambda b, i: (b, 0, i, 0)),
            pl.BlockSpec((1, hh, tm, MLA_QK_PAD), lambda b, i: (b, 0, i, 0)),
            pl.BlockSpec((1, hh, tm, MLA_V), lambda b, i: (b, 0, i, 0)),
        ),
        compiler_params=_cparams(("arbitrary", "arbitrary")),
        name="mla_prep",
    )(cq, ckv, kr, pos2, qag, kvag, wuq_p, wukv, gq, gkn, gkr, invf, sgn, cm)


def _attn_kernel(q_ref, k_ref, v_ref, o_ref, *, tq, seq):
    row = lax.broadcasted_iota(jnp.int32, (tq, tq), 0)
    col = lax.broadcasted_iota(jnp.int32, (tq, tq), 1)
    causal = col <= row
    for i in range(seq // tq):
        lo = i * tq
        q = q_ref[0, 0, lo:lo + tq, :]
        sd = lax.dot_general(q, k_ref[0, 0, lo:lo + tq, :], NT_DIMS, preferred_element_type=F32)
        sd = jnp.where(causal, sd, -jnp.inf)
        m = jnp.max(sd, axis=-1, keepdims=True)
        if i > 0:
            so = lax.dot_general(q, k_ref[0, 0, :lo, :], NT_DIMS, preferred_element_type=F32)
            m = jnp.maximum(m, jnp.max(so, axis=-1, keepdims=True))
        pd = jnp.exp(sd - m)
        l = jnp.sum(pd, axis=-1, keepdims=True)
        o = jnp.dot(pd.astype(BF16), v_ref[0, 0, lo:lo + tq, :], preferred_element_type=F32)
        if i > 0:
            po = jnp.exp(so - m)
            l = l + jnp.sum(po, axis=-1, keepdims=True)
            o = o + jnp.dot(po.astype(BF16), v_ref[0, 0, :lo, :], preferred_element_type=F32)
        o_ref[lo:lo + tq, :] = o / l


def _attention(q, k, v):
    batch, hh, seq, dq = q.shape
    tq = min(ATTN_TQ, seq)
    return pl.pallas_call(
        functools.partial(_attn_kernel, tq=tq, seq=seq),
        out_shape=jax.ShapeDtypeStruct((batch * seq, hh * MLA_V), F32),
        grid=(batch, hh),
        in_specs=[
            pl.BlockSpec((1, 1, seq, dq), lambda b, h: (b, h, 0, 0)),
            pl.BlockSpec((1, 1, seq, dq), lambda b, h: (b, h, 0, 0)),
            pl.BlockSpec((1, 1, seq, MLA_V), lambda b, h: (b, h, 0, 0)),
        ],
        out_specs=pl.BlockSpec((seq, MLA_V), lambda b, h: (b, h)),
        compiler_params=_cparams(("arbitrary", "arbitrary")),
        name="mla_attn",
    )(q, k, v)


def _outproj_kernel(ohg_ref, omla_ref, x_ref, gt_ref, sc_ref, sh_ref, og_ref, g2_ref,
                    wo_ref, wrh_ref, wrl_ref, h_out, u_out, lg_out, *, n_sub):
    sub = x_ref.shape[0] // n_sub
    for j in range(n_sub):
        rows = slice(j * sub, (j + 1) * sub)
        om = omla_ref[rows, :]
        omn = om * lax.rsqrt(jnp.mean(om * om, axis=-1, keepdims=True) + EPS) * og_ref[...]
        y = jnp.dot(ohg_ref[rows, :], wo_ref[:HG_WIDTH, :], preferred_element_type=F32)
        y = y + jnp.dot(omn.astype(BF16), wo_ref[HG_WIDTH:, :], preferred_element_type=F32)
        h1 = x_ref[rows, :] + gt_ref[0] * y
        h_out[rows, :] = h1
        u = h1 * lax.rsqrt(jnp.mean(h1 * h1, axis=-1, keepdims=True) + EPS) * g2_ref[...]
        u = u * (1.0 + sc_ref[0]) + sh_ref[0]
        uh, ul = _split_bf16(u)
        u_out[rows] = _rows_to_tiles(uh)
        lg = lax.dot_general(wrh_ref[...], uh, NT_DIMS, preferred_element_type=F32)
        lg = lg + lax.dot_general(wrl_ref[...], uh, NT_DIMS, preferred_element_type=F32)
        lg = lg + lax.dot_general(wrh_ref[...], ul, NT_DIMS, preferred_element_type=F32)
        lg_out[:, rows] = lg


def _outproj(ohg, omla, x2, mod3, og, g2, wo_bf, wr_hi, wr_lo, seq):
    t, d = x2.shape
    tm = min(OUTPROJ_TM, seq)
    per_b = seq // tm
    full = lambda shape: pl.BlockSpec(shape, lambda i: tuple(0 for _ in shape))
    modspec = lambda col: pl.BlockSpec((1, 1, d), lambda i: (i // per_b, 0, col))
    return pl.pallas_call(
        functools.partial(_outproj_kernel, n_sub=OUTPROJ_SUB),
        out_shape=(
            jax.ShapeDtypeStruct((t, d), F32),
            jax.ShapeDtypeStruct((t, d // 128, 128), BF16),
            jax.ShapeDtypeStruct((N_EXPERTS, t), F32),
        ),
        grid=(t // tm,),
        in_specs=[
            pl.BlockSpec((tm, HG_WIDTH), lambda i: (i, 0)),
            pl.BlockSpec((tm, MLA_WIDTH), lambda i: (i, 0)),
            pl.BlockSpec((tm, d), lambda i: (i, 0)),
            modspec(2),
            modspec(4),
            modspec(3),
            full((1, MLA_WIDTH)),
            full((1, d)),
            full(wo_bf.shape),
            full(wr_hi.shape),
            full(wr_lo.shape),
        ],
        out_specs=(
            pl.BlockSpec((tm, d), lambda i: (i, 0)),
            pl.BlockSpec((tm, d // 128, 128), lambda i: (i, 0, 0)),
            pl.BlockSpec((N_EXPERTS, tm), lambda i: (0, i)),
        ),
        compiler_params=_cparams(("arbitrary",)),
        name="outproj",
    )(ohg, omla, x2, mod3, mod3, mod3, og, g2, wo_bf, wr_hi, wr_lo)


def _route_kernel(lg_ref, bias_ref, tri_ref, ltri_ref, idx_out, w_out, dest_out, info_out,
                  carry, base, *, tt, n_info):
    phase = pl.program_id(0)
    step = pl.program_id(1)

    @pl.when((phase == 0) & (step == 0))
    def _():
        carry[...] = jnp.zeros_like(carry)

    @pl.when((phase == 1) & (step == 0))
    def _():
        cnt = carry[...]
        padded = jnp.floor((cnt + (MOE_BLK - 1)) / MOE_BLK) * MOE_BLK
        hi, lo = _split_bf16(padded)
        ps = jnp.dot(ltri_ref[...], jnp.concatenate([hi, lo], axis=1), preferred_element_type=F32)
        pad_start = ps[:, :128] + ps[:, 128:]
        base[...] = pad_start
        carry[...] = jnp.zeros_like(carry)
        pad_end = pad_start + padded
        reps = n_info // 128
        pe = jnp.concatenate([pad_end] * reps, axis=1)
        starts = (lax.broadcasted_iota(jnp.int32, (N_EXPERTS, n_info), 1) * MOE_BLK).astype(F32)
        blk_e = jnp.sum(jnp.where(pe <= starts, 1.0, 0.0), axis=0, keepdims=True)
        blk_e = jnp.minimum(blk_e, N_EXPERTS - 1.0)
        n_used = pe[N_EXPERTS - 1:N_EXPERTS, :] / MOE_BLK
        diag = (lax.broadcasted_iota(jnp.int32, (N_EXPERTS, n_info), 0)
                == lax.broadcasted_iota(jnp.int32, (N_EXPERTS, n_info), 1))
        end_blk = jnp.sum(jnp.where(diag, pe, 0.0), axis=0, keepdims=True) / MOE_BLK
        pd = jnp.concatenate([padded] * reps, axis=1)
        num_blk = jnp.sum(jnp.where(diag, pd, 0.0), axis=0, keepdims=True) / MOE_BLK
        info_out[...] = jnp.concatenate(
            [blk_e, n_used, end_blk, num_blk, jnp.zeros((4, n_info), F32)],
            axis=0).astype(jnp.int32)

    ne, ng, eg = N_EXPERTS, N_GROUPS, EXPERTS_PER_GROUP
    neg = -jnp.inf
    sc = _sigmoid(lg_ref[...])
    sel = sc + bias_ref[...]
    sel3 = sel.reshape(ng, eg, tt)
    sub = lax.broadcasted_iota(jnp.int32, (ng, eg, tt), 1)
    m1 = jnp.max(sel3, axis=1, keepdims=True)
    first = jnp.min(jnp.where(sel3 == m1, sub, eg), axis=1, keepdims=True)
    m2 = jnp.max(jnp.where(sub == first, neg, sel3), axis=1, keepdims=True)
    gs = (m1 + m2).reshape(ng, tt)

    gio = lax.broadcasted_iota(jnp.int32, (ng, tt), 0)
    gsel = jnp.zeros((ng, tt), F32)
    for _ in range(TOPK_GROUPS):
        m = jnp.max(gs, axis=0, keepdims=True)
        gi = jnp.min(jnp.where(gs == m, gio, ng), axis=0, keepdims=True)
        hit = gio == gi
        gsel = jnp.where(hit, 1.0, gsel)
        gs = jnp.where(hit, neg, gs)
    gfull = jnp.broadcast_to(gsel.reshape(ng, 1, tt), (ng, eg, tt)).reshape(ne, tt)
    selm = jnp.where(gfull > 0.0, sel, neg)

    eio = lax.broadcasted_iota(jnp.int32, (ne, tt), 0)
    chosen = jnp.zeros((ne, tt), F32)
    idxs, ws = [], []
    for _ in range(TOP_K):
        m = jnp.max(selm, axis=0, keepdims=True)
        ei = jnp.min(jnp.where(selm == m, eio, ne), axis=0, keepdims=True)
        hit = eio == ei
        ws.append(jnp.sum(jnp.where(hit, sc, 0.0), axis=0, keepdims=True))
        idxs.append(ei)
        chosen = jnp.where(hit, 1.0, chosen)
        selm = jnp.where(hit, neg, selm)
    wsum = ws[0]
    for k in range(1, TOP_K):
        wsum = wsum + ws[k]

    tot = jnp.sum(chosen, axis=1, keepdims=True)

    @pl.when(phase == 1)
    def _():
        off = base[...][:, :1] + carry[...][:, :1]
        excl = jnp.dot(chosen.astype(BF16), tri_ref[...], preferred_element_type=F32) + off
        for k in range(TOP_K):
            idx_out[k:k + 1, :] = idxs[k]
            w_out[k:k + 1, :] = ws[k] / wsum * ROUTED_SCALE
            dk = jnp.sum(jnp.where(eio == idxs[k], excl, 0.0), axis=0, keepdims=True)
            dest_out[k:k + 1, :] = dk.astype(jnp.int32)

    carry[...] = carry[...] + tot


def _route(logits_t, bias_col, n_blk):
    ne, t = logits_t.shape
    tt = min(ROUTE_TT, t)
    n_info = -(-n_blk // 128) * 128
    tri = jnp.asarray(np.triu(np.ones((tt, tt), np.float32), 1), BF16)
    ltri = jnp.asarray(np.tril(np.ones((ne, ne), np.float32), -1), BF16)
    return pl.pallas_call(
        functools.partial(_route_kernel, tt=tt, n_info=n_info),
        out_shape=(
            jax.ShapeDtypeStruct((TOP_K, t), jnp.int32),
            jax.ShapeDtypeStruct((TOP_K, t), F32),
            jax.ShapeDtypeStruct((TOP_K, t), jnp.int32),
            jax.ShapeDtypeStruct((8, n_info), jnp.int32),
        ),
        grid=(2, t // tt),
        in_specs=[
            pl.BlockSpec((ne, tt), lambda p, i: (0, i)),
            pl.BlockSpec((ne, 1), lambda p, i: (0, 0)),
            pl.BlockSpec((tt, tt), lambda p, i: (0, 0)),
            pl.BlockSpec((ne, ne), lambda p, i: (0, 0)),
        ],
        out_specs=(
            pl.BlockSpec((TOP_K, tt), lambda p, i: (0, i * p)),
            pl.BlockSpec((TOP_K, tt), lambda p, i: (0, i * p)),
            pl.BlockSpec((TOP_K, tt), lambda p, i: (0, i * p)),
            pl.BlockSpec((8, n_info), lambda p, i: (0, 0)),
        ),
        scratch_shapes=[pltpu.VMEM((ne, 128), F32), pltpu.VMEM((ne, 128), F32)],
        compiler_params=_cparams(("arbitrary", "arbitrary")),
        name="route",
    )(logits_t, bias_col, tri, ltri)


def _dispatch_kernel(info_ref, dest_ref, u_hbm, xs_hbm, stage, zbuf, lsem, ssem, zsem, *, tm, n_blk):
    i = pl.program_id(0)
    n = pl.num_programs(0)
    slot = i % 2
    n_used = info_ref[1, 0]

    def load(tile, dst_slot):
        return pltpu.make_async_copy(u_hbm.at[pl.ds(tile * tm, tm)], stage.at[dst_slot],
                                     lsem.at[dst_slot])

    def zero_block(blk, which):
        return pltpu.make_async_copy(zbuf, xs_hbm.at[pl.ds(blk * MOE_BLK, MOE_BLK)],
                                     zsem.at[which])

    @pl.when(i == 0)
    def _():
        load(0, 0).start()
        zbuf[...] = jnp.zeros_like(zbuf)
        for e in range(N_EXPERTS):
            @pl.when(info_ref[3, e] > 0)
            def _():
                zero_block(info_ref[2, e] - 1, 0).start()

        def zero_unused(j, carry):
            zero_block(j, 1).start()
            return carry
        lax.fori_loop(n_used, n_blk, zero_unused, 0)
        for e in range(N_EXPERTS):
            @pl.when(info_ref[3, e] > 0)
            def _():
                zero_block(0, 0).wait()

    @pl.when(i == n - 1)
    def _():
        def wait_unused(j, carry):
            zero_block(0, 1).wait()
            return carry
        lax.fori_loop(n_used, n_blk, wait_unused, 0)

    def scatter_wait(src_slot):
        pltpu.make_async_copy(stage.at[src_slot], xs_hbm.at[pl.ds(0, tm)], ssem.at[src_slot]).wait()

    @pl.when(i >= 1)
    def _():
        for _ in range(TOP_K):
            scatter_wait(1 - slot)

    @pl.when(i + 1 < n)
    def _():
        load(i + 1, 1 - slot).start()

    load(i, slot).wait()
    for k in range(TOP_K):
        for r in range(tm):
            pltpu.make_async_copy(stage.at[slot, r], xs_hbm.at[dest_ref[0, k, r]],
                                  ssem.at[slot]).start(priority=r % 2)

    @pl.when(i == n - 1)
    def _():
        for _ in range(TOP_K):
            scatter_wait(slot)


def _dispatch(info, dest3, u3, n_blk):
    t, nt, _ = u3.shape
    tm = dest3.shape[2]
    return pl.pallas_call(
        functools.partial(_dispatch_kernel, tm=tm, n_blk=n_blk),
        out_shape=jax.ShapeDtypeStruct((n_blk * MOE_BLK, nt, 128), BF16),
        grid_spec=pltpu.PrefetchScalarGridSpec(
            num_scalar_prefetch=1,
            grid=(t // tm,),
            in_specs=[
                pl.BlockSpec((1, TOP_K, tm), lambda i, info: (i, 0, 0), memory_space=pltpu.SMEM),
                pl.BlockSpec(memory_space=pl.ANY),
            ],
            out_specs=pl.BlockSpec(memory_space=pl.ANY),
            scratch_shapes=[
                pltpu.VMEM((2, tm, nt, 128), BF16),
                pltpu.VMEM((MOE_BLK, nt, 128), BF16),
                pltpu.SemaphoreType.DMA((2,)),
                pltpu.SemaphoreType.DMA((2,)),
                pltpu.SemaphoreType.DMA((2,)),
            ],
        ),
        compiler_params=_cparams(("arbitrary",)),
        name="moe_dispatch",
    )(info, dest3, u3)


def _gmm_kernel(info_ref, xs_ref, wg_hbm, wu_hbm, wd_hbm, y_ref,
                wg_f, wu_f, wd_f, wsem, wg_s, wu_s, wd_s, nswap):
    i = pl.program_id(0)
    n_used = info_ref[1, 0]
    e = info_ref[0, i]
    prev = info_ref[0, jnp.maximum(i - 1, 0)]
    fresh = (i == 0) | (e != prev)

    def fetch(expert, slot):
        return (pltpu.make_async_copy(wg_hbm.at[expert], wg_f.at[slot], wsem.at[slot]),
                pltpu.make_async_copy(wu_hbm.at[expert], wu_f.at[slot], wsem.at[slot]),
                pltpu.make_async_copy(wd_hbm.at[expert], wd_f.at[slot], wsem.at[slot]))

    @pl.when(i == 0)
    def _():
        nswap[0] = 0
        for c in fetch(e, 0):
            c.start()

    @pl.when(fresh & (i < n_used))
    def _():
        slot = nswap[0] % 2
        nswap[0] = nswap[0] + 1
        for c in fetch(e, slot):
            c.wait()
        nxt = info_ref[2, e]

        @pl.when(nxt < n_used)
        def _():
            for c in fetch(info_ref[0, nxt], 1 - slot):
                c.start(priority=1)

        wg_s[...] = wg_f[slot].astype(BF16)
        wu_s[...] = wu_f[slot].astype(BF16)
        wd_s[...] = wd_f[slot].astype(BF16)

    @pl.when(i < n_used)
    def _():
        xb = _tiles_to_rows(xs_ref[...])
        a = jnp.dot(xb, wg_s[...], preferred_element_type=F32)
        u = jnp.dot(xb, wu_s[...], preferred_element_type=F32)
        hid = (a * _sigmoid(a) * u).astype(BF16)
        y = jnp.dot(hid, wd_s[...], preferred_element_type=F32)
        y_ref[...] = _rows_to_tiles(y.astype(BF16))

    @pl.when(i >= n_used)
    def _():
        y_ref[...] = jnp.zeros_like(y_ref)


def _gmm(info, xs3, w_gate, w_up, w_down):
    p, nt, _ = xs3.shape
    d = nt * 128
    n_blk = p // MOE_BLK
    de = w_gate.shape[-1]
    return pl.pallas_call(
        _gmm_kernel,
        out_shape=jax.ShapeDtypeStruct((p, nt, 128), BF16),
        grid_spec=pltpu.PrefetchScalarGridSpec(
            num_scalar_prefetch=1,
            grid=(n_blk,),
            in_specs=[
                pl.BlockSpec((MOE_BLK, nt, 128),
                             lambda i, info: (jnp.minimum(i, info[1, 0] - 1), 0, 0)),
                pl.BlockSpec(memory_space=pl.ANY),
                pl.BlockSpec(memory_space=pl.ANY),
                pl.BlockSpec(memory_space=pl.ANY),
            ],
            out_specs=pl.BlockSpec((MOE_BLK, nt, 128), lambda i, info: (i, 0, 0)),
            scratch_shapes=[
                pltpu.VMEM((2, d, de), F32),
                pltpu.VMEM((2, d, de), F32),
                pltpu.VMEM((2, de, d), F32),
                pltpu.SemaphoreType.DMA((2,)),
                pltpu.VMEM((d, de), BF16),
                pltpu.VMEM((d, de), BF16),
                pltpu.VMEM((de, d), BF16),
                pltpu.SMEM((1,), jnp.int32),
            ],
        ),
        compiler_params=_cparams(("arbitrary",)),
        name="moe_gmm",
    )(info, xs3, w_gate, w_up, w_down)


def _final_kernel(cur_ref, nxt_ref, u_ref, h_ref, gt_ref, wt_ref, wsg_ref, wsu_ref, wsd_ref, y_hbm,
                  o_ref, ybuf_a, ybuf_b, sem, *, tm):
    i = pl.program_id(0)
    n = pl.num_programs(0)
    n_rows = TOP_K * tm

    def issue(idx_ref, tile, buf, sem_i):
        for k in range(TOP_K):
            for r in range(tm):
                pltpu.make_async_copy(y_hbm.at[idx_ref[0, tile * TOP_K + k, r]], buf.at[k * tm + r],
                                      sem.at[sem_i]).start(priority=r % 2)

    def gather_wait(buf, sem_i):
        pltpu.make_async_copy(y_hbm.at[pl.ds(0, n_rows)], buf, sem.at[sem_i]).wait()

    eye = (lax.broadcasted_iota(jnp.int32, (tm, tm), 0)
           == lax.broadcasted_iota(jnp.int32, (tm, tm), 1))

    def compute(tile, buf):
        rows = slice(tile * tm, (tile + 1) * tm)
        ub = _tiles_to_rows(u_ref[rows])
        a = jnp.dot(ub, wsg_ref[...], preferred_element_type=F32)
        b = jnp.dot(ub, wsu_ref[...], preferred_element_type=F32)
        hid = (a * _sigmoid(a) * b).astype(BF16)
        acc = jnp.dot(hid, wsd_ref[...], preferred_element_type=F32)
        wt = wt_ref[rows, :]
        wcat = jnp.concatenate([jnp.where(eye, wt[:, k:k + 1], 0.0) for k in range(TOP_K)],
                               axis=1)
        whi, wlo = _split_bf16(wcat)
        ycat = _tiles_to_rows(buf[...])
        acc = acc + jnp.dot(whi, ycat, preferred_element_type=F32)
        acc = acc + jnp.dot(wlo, ycat, preferred_element_type=F32)
        o_ref[rows, :] = h_ref[rows, :] + gt_ref[0] * acc

    @pl.when(i == 0)
    def _():
        issue(cur_ref, 0, ybuf_a, 0)

    gather_wait(ybuf_a, 0)
    issue(cur_ref, 1, ybuf_b, 1)
    compute(0, ybuf_a)
    gather_wait(ybuf_b, 1)
    issue(nxt_ref, 0, ybuf_a, 0)
    compute(1, ybuf_b)

    @pl.when(i == n - 1)
    def _():
        gather_wait(ybuf_a, 0)


def _final(dest3, u2, h1, mod3, wt, wsg, wsu, wsd, y, seq):
    t, d = h1.shape
    tm = dest3.shape[2]
    n = t // (2 * tm)
    per_b = seq // (2 * tm)
    dest_pairs = dest3.reshape(n, 2 * TOP_K, tm)
    full = lambda shape: pl.BlockSpec(shape, lambda i: tuple(0 for _ in shape))
    return pl.pallas_call(
        functools.partial(_final_kernel, tm=tm),
        out_shape=jax.ShapeDtypeStruct((t, d), F32),
        grid=(n,),
        in_specs=[
            pl.BlockSpec((1, 2 * TOP_K, tm), lambda i: (i, 0, 0), memory_space=pltpu.SMEM),
            pl.BlockSpec((1, 2 * TOP_K, tm), lambda i: (jnp.minimum(i + 1, n - 1), 0, 0),
                         memory_space=pltpu.SMEM),
            pl.BlockSpec((2 * tm, d // 128, 128), lambda i: (i, 0, 0)),
            pl.BlockSpec((2 * tm, d), lambda i: (i, 0)),
            pl.BlockSpec((1, 1, d), lambda i: (i // per_b, 0, 5)),
            pl.BlockSpec((2 * tm, TOP_K), lambda i: (i, 0)),
            full(wsg.shape),
            full(wsu.shape),
            full(wsd.shape),
            pl.BlockSpec(memory_space=pl.ANY),
        ],
        out_specs=pl.BlockSpec((2 * tm, d), lambda i: (i, 0)),
        scratch_shapes=[
            pltpu.VMEM((TOP_K * tm, d // 128, 128), BF16),
            pltpu.VMEM((TOP_K * tm, d // 128, 128), BF16),
            pltpu.SemaphoreType.DMA((2,)),
        ],
        compiler_params=_cparams(("arbitrary",)),
        name="final",
    )(dest_pairs, dest_pairs, u2, h1, mod3, wt, wsg, wsu, wsd, y)


def _layer(h2, mod3, positions, lb, p, batch, seq):
    t, d = h2.shape

    w_in_t = p["w_in"].T
    proj, cq, ckv, kr = _inproj(h2, p["norm_mix_g"][None, :], mod3,
                                w_in_t[:COL_CQ].astype(BF16), w_in_t[COL_CQ:].astype(BF16), seq)

    o_hg = _hgrn(proj, lb[None, :], p["hg_out_g"][None, :], batch, seq)

    wuq = p["mla_w_uq"].reshape(MLA_Q_RANK, MLA_HEADS, MLA_QK)
    wuq_p = jnp.pad(wuq, ((0, 0), (0, 0), (0, MLA_QK_PAD - MLA_QK))).reshape(
        MLA_Q_RANK, MLA_HEADS * MLA_QK_PAD).astype(BF16)
    wukv = p["mla_w_ukv"].astype(BF16)
    gq = jnp.pad(p["mla_q_norm_g"], (0, MLA_QK_PAD - MLA_QK))[None, :]
    gkn = p["mla_k_norm_g"][None, :MLA_NOPE]
    gkr = jnp.pad(p["mla_k_norm_g"][MLA_NOPE:], (0, 128 - MLA_ROPE))[None, :]
    half = MLA_ROPE // 2
    inv_freq = ROPE_THETA ** (-jnp.arange(half, dtype=F32) / half)
    zeros64 = jnp.zeros((128 - MLA_ROPE,), F32)
    invf = jnp.concatenate([inv_freq, inv_freq, zeros64])[None, :]
    sgn = jnp.concatenate([-jnp.ones((half,), F32), jnp.ones((half,), F32), zeros64])[None, :]
    cm = jnp.concatenate([jnp.ones((MLA_ROPE,), F32), zeros64])[None, :]
    q, k, v = _mla_prep(cq, ckv, kr, positions.reshape(t, 1), p["mla_q_a_g"][None, :],
                        p["mla_kv_a_g"][None, :], wuq_p, wukv, gq, gkn, gkr, invf, sgn, cm,
                        batch, seq)
    o_mla = _attention(q, k, v)

    wr_t = p["w_router"].T
    wr_hi = wr_t.astype(BF16)
    wr_lo = (wr_t - wr_hi.astype(F32)).astype(BF16)
    h1, u2, logits_t = _outproj(o_hg, o_mla, h2, mod3, p["mla_out_g"][None, :],
                                p["norm_ffn_g"][None, :], p["w_out"].astype(BF16),
                                wr_hi, wr_lo, seq)

    n_blk = (t * TOP_K) // MOE_BLK + N_EXPERTS
    _, top_w, dest, info = _route(logits_t, p["router_bias"][:, None], n_blk)

    def tiled(tm):
        tm = min(tm, seq)
        return dest.reshape(TOP_K, t // tm, tm).transpose(1, 0, 2)

    xs = _dispatch(info, tiled(DISPATCH_TM), u2, n_blk)
    y = _gmm(info, xs, p["w_gate"], p["w_up"], p["w_down"])
    return _final(tiled(COMBINE_TM), u2, h1, mod3, top_w.T, p["ws_gate"].astype(BF16),
                  p["ws_up"].astype(BF16), p["ws_down"].astype(BF16), y, seq)


def kernel(x, c, positions, w_ada, b_ada, norm_mix_g, norm_ffn_g, w_in, hg_lb_logits, hg_out_g,
           mla_q_a_g, mla_w_uq, mla_kv_a_g, mla_w_ukv, mla_q_norm_g, mla_k_norm_g, mla_out_g,
           w_out, w_router, router_bias, w_gate, w_up, w_down, ws_gate, ws_up, ws_down):
    batch, seq, d = x.shape
    depth = w_ada.shape[0]
    lb_all = jnp.cumsum(jax.nn.softmax(hg_lb_logits.astype(F32), axis=0), axis=0)
    c8 = jnp.pad(c, ((0, 8 - batch), (0, 0)))
    h2 = x.reshape(batch * seq, d)
    for l in range(depth):
        mod = _adaln(c8, w_ada[l], b_ada[l][None, :])[:batch]
        mod3 = mod.reshape(batch, 1, 6 * d)
        p = dict(norm_mix_g=norm_mix_g[l], norm_ffn_g=norm_ffn_g[l], w_in=w_in[l],
                 hg_out_g=hg_out_g[l], mla_q_a_g=mla_q_a_g[l], mla_w_uq=mla_w_uq[l],
                 mla_kv_a_g=mla_kv_a_g[l], mla_w_ukv=mla_w_ukv[l], mla_q_norm_g=mla_q_norm_g[l],
                 mla_k_norm_g=mla_k_norm_g[l], mla_out_g=mla_out_g[l], w_out=w_out[l],
                 w_router=w_router[l], router_bias=router_bias[l], w_gate=w_gate[l],
                 w_up=w_up[l], w_down=w_down[l], ws_gate=ws_gate[l], ws_up=ws_up[l],
                 ws_down=ws_down[l])
        h2 = _layer(h2, mod3, positions, lb_all[l], p, batch, seq)
    return h2.reshape(batch, seq, d)
```
